```python
import math
import functools
import jax
import jax.numpy as jnp
from jax import lax
import numpy as np

D_MODEL = 2048
BATCH = 4
SEQ = 2048
DEPTH = 2
DEC_BATCH = 128
DEC_SEQ = 4
PAST_LEN = 2048
PAGE_SIZE = 128

N_HEADS = 16
N_KV_HEADS = 4
GROUP = N_HEADS // N_KV_HEADS
HEAD_DIM = 64
CMP_STRIDE = 16
CMP_LEN = 2 * CMP_STRIDE
SEL_LEN = 64
N_SEL = 16
WINDOW = 512
Q_BLOCK = 128
FORCE_BONUS = 1e3
N_BUCKETS = 32
MAX_DISTANCE = 128
D_RNN = 1024
N_RNN_BLOCKS = 16
RNN_BLOCK = D_RNN // N_RNN_BLOCKS
CONV_WIDTH = 4
LRU_C = 8.0
D_FF = ((8 * D_MODEL + 3 * 256 - 1) // (3 * 256)) * 256
ALPHA = (2.0 * DEPTH) ** 0.25
BETA = (8.0 * DEPTH) ** -0.25
Q_W = N_HEADS * HEAD_DIM
KV_W = 2 * N_KV_HEADS * HEAD_DIM
IN_SPLITS = (Q_W, KV_W, KV_W, KV_W, 3 * N_HEADS, D_RNN, D_RNN, 2 * D_MODEL)
N_IN = Q_W + 3 * KV_W + 3 * N_HEADS + 2 * D_RNN + 2 * D_MODEL

kernel_name = 'nsa_rglru_hybrid_step'


def layer_norm(x, g, b, eps=1e-5):
    xf = x.astype(jnp.float32)
    mu = xf.mean(-1, keepdims=True)
    var = jnp.square(xf - mu).mean(-1, keepdims=True)
    return ((xf - mu) * lax.rsqrt(var + eps)).astype(x.dtype) * g + b


def masked_softmax(s, mask):
    s = jnp.where(mask, s.astype(jnp.float32), -1e30)
    p = jnp.exp(s - jnp.max(s, -1, keepdims=True)) * mask
    return p / jnp.maximum(p.sum(-1, keepdims=True), 1e-30)


def t5_bucket(dist):
    n = jnp.maximum(dist, 0)
    exact = N_BUCKETS // 2
    log_ratio = jnp.log(jnp.maximum(n, 1).astype(jnp.float32) / exact) / math.log(MAX_DISTANCE / exact)
    large = jnp.minimum(exact + (log_ratio * (N_BUCKETS - exact)).astype(jnp.int32), N_BUCKETS - 1)
    return jnp.where(n < exact, n, large)


def cmp_to_sel(n_cmp, n_sel):
    j = np.arange(n_cmp)[:, None]
    s = np.arange(n_sel)[None, :]
    lo = np.maximum(j * CMP_STRIDE, s * SEL_LEN)
    hi = np.minimum(j * CMP_STRIDE + CMP_LEN, (s + 1) * SEL_LEN)
    return jnp.asarray(np.maximum(hi - lo, 0) / CMP_LEN, dtype=jnp.float32)


def compress(rows, pos_emb, w1, w2):
    B, T = rows.shape[0], rows.shape[1]
    r = rows.reshape(B, T // CMP_STRIDE, CMP_STRIDE, N_KV_HEADS, HEAD_DIM)
    w1h = w1.reshape(2, CMP_STRIDE, HEAD_DIM, HEAD_DIM)
    pe = pos_emb.reshape(2, CMP_STRIDE, 1, HEAD_DIM)
    lead = jnp.einsum('bnlgd,lde->bnge', r + pe[0], w1h[0])
    tail = jnp.einsum('bnlgd,lde->bnge', r + pe[1], w1h[1])
    return jnp.einsum('bnge,ef->bngf', jax.nn.silu(lead[:, :-1] + tail[:, 1:]), w2)


def nsa_attend(q, qpos, gates, kc, vc, k_sel, v_sel, kw, vw, kwpos, rel_bias):
    B, Q = q.shape[0], q.shape[1]
    n_cmp, n_sel = kc.shape[1], k_sel.shape[1]
    scale = HEAD_DIM ** -0.5
    tbl = rel_bias.reshape(N_BUCKETS, N_KV_HEADS, GROUP)
    c_end = jnp.arange(n_cmp) * CMP_STRIDE + (CMP_LEN - 1)
    dist_c = qpos[:, None] - c_end[None, :]
    bias_c = tbl[t5_bucket(dist_c)].transpose(0, 2, 3, 1)
    s_c = jnp.einsum('bqgrd,bngd->bqgrn', q, kc) * scale + bias_c[None]
    p_c = masked_softmax(s_c, (dist_c >= 0)[None, :, None, None, :])
    o_c = jnp.einsum('bqgrn,bngd->bqgrd', p_c.astype(vc.dtype), vc)
    imp = jnp.einsum('bqgn,ns->bqgs', p_c.sum(3), cmp_to_sel(n_cmp, n_sel))
    blk = jnp.arange(n_sel)[None, :]
    cur = (qpos // SEL_LEN)[:, None]
    forced = (blk == 0) | (blk == cur) | (blk == cur - 1)
    imp = jnp.where(forced[None, :, None, :], imp + FORCE_BONUS, imp)
    imp = jnp.where((blk <= cur)[None, :, None, :], imp, -1e30)
    top_val, top_idx = lax.top_k(imp, min(N_SEL, n_sel))
    top_ok = top_val > -1e29
    b_i = jnp.arange(B)[:, None, None, None]
    g_i = jnp.arange(N_KV_HEADS)[None, None, :, None]
    k_s = k_sel.transpose(0, 3, 1, 2, 4)[b_i, g_i, top_idx]
    v_s = v_sel.transpose(0, 3, 1, 2, 4)[b_i, g_i, top_idx]
    pos_s = top_idx[..., None] * SEL_LEN + jnp.arange(SEL_LEN)
    dist_s = qpos[None, :, None, None, None] - pos_s
    mask_s = (dist_s >= 0) & top_ok[..., None]
    bias_s = tbl[t5_bucket(dist_s), g_i[..., None]]
    s_s = jnp.einsum('bqgrd,bqgnld->bqgrnl', q, k_s) * scale + bias_s.transpose(0, 1, 2, 5, 3, 4)
    n_keys = s_s.shape[-2] * SEL_LEN
    p_s = masked_softmax(s_s.reshape(B, Q, N_KV_HEADS, GROUP, n_keys), mask_s.reshape(B, Q, N_KV_HEADS, 1, n_keys))
    o_s = jnp.einsum('bqgrm,bqgmd->bqgrd', p_s.astype(v_s.dtype), v_s.reshape(B, Q, N_KV_HEADS, n_keys, HEAD_DIM))
    dist_w = qpos[:, None] - kwpos[None, :]
    mask_w = (dist_w >= 0) & (dist_w < WINDOW) & (kwpos >= 0)[None, :]
    bias_w = tbl[t5_bucket(dist_w)].transpose(0, 2, 3, 1)
    s_w = jnp.einsum('bqgrd,bkgd->bqgrk', q, kw) * scale + bias_w[None]
    p_w = masked_softmax(s_w, mask_w[None, :, None, None, :])
    o_w = jnp.einsum('bqgrk,bkgd->bqgrd', p_w.astype(vw.dtype), vw)
    o = gates[..., 0:1] * o_c + gates[..., 1:2] * o_s + gates[..., 2:3] * o_w
    return o.reshape(B, Q, N_HEADS * HEAD_DIM)


def nsa_prompt(q, kvc, kvs, kvw, gates, cmp_pos, cmp_w1, cmp_w2, rel_bias):
    B, T = q.shape[0], q.shape[1]
    kc = compress(kvc[:, :, 0], cmp_pos[0], cmp_w1[0], cmp_w2[0])
    vc = compress(kvc[:, :, 1], cmp_pos[1], cmp_w1[1], cmp_w2[1])
    blocks = kvs.reshape(B, T // SEL_LEN, SEL_LEN, 2, N_KV_HEADS, HEAD_DIM)
    kw_pad = jnp.pad(kvw, ((0, 0), (WINDOW, 0), (0, 0), (0, 0), (0, 0)))

    def one_block(i):
        q0 = i * Q_BLOCK
        qb = lax.dynamic_slice_in_dim(q, q0, Q_BLOCK, axis=1)
        gb = lax.dynamic_slice_in_dim(gates, q0, Q_BLOCK, axis=1)
        wb = lax.dynamic_slice_in_dim(kw_pad, q0, WINDOW + Q_BLOCK, axis=1)
        qpos = q0 + jnp.arange(Q_BLOCK)
        kwpos = q0 - WINDOW + jnp.arange(WINDOW + Q_BLOCK)
        return nsa_attend(qb, qpos, gb, kc, vc, blocks[:, :, :, 0], blocks[:, :, :, 1],
                          wb[:, :, 0], wb[:, :, 1], kwpos, rel_bias)

    out = lax.map(one_block, jnp.arange(T // Q_BLOCK))
    return out.transpose(1, 0, 2, 3).reshape(B, T, N_HEADS * HEAD_DIM)


def nsa_sample(q, kvc, kvs, kvw, gates, cache_c, cache_s, win_buf, page_table, cmp_pos, cmp_w1, cmp_w2, rel_bias):
    B, S = q.shape[0], q.shape[1]
    past = page_table.shape[1] * PAGE_SIZE

    def full_rows(cache, new, mult):
        rows = jnp.concatenate([cache[page_table].reshape(B, past, 2, N_KV_HEADS, HEAD_DIM), new.astype(cache.dtype)], axis=1)
        pad = -(past + S) % mult
        return jnp.pad(rows, ((0, 0), (0, pad), (0, 0), (0, 0), (0, 0)))

    rows_c = full_rows(cache_c, kvc, CMP_STRIDE)
    kc = compress(rows_c[:, :, 0], cmp_pos[0], cmp_w1[0], cmp_w2[0])
    vc = compress(rows_c[:, :, 1], cmp_pos[1], cmp_w1[1], cmp_w2[1])
    rows_s = full_rows(cache_s, kvs, SEL_LEN)
    blocks = rows_s.reshape(B, -1, SEL_LEN, 2, N_KV_HEADS, HEAD_DIM)
    n_buf = win_buf.shape[1]
    rows_w = jnp.concatenate([win_buf, kvw.astype(win_buf.dtype)], axis=1)
    qpos = past + jnp.arange(S)
    kwpos = past - n_buf + jnp.arange(n_buf + S)
    return nsa_attend(q, qpos, gates, kc, vc, blocks[:, :, :, 0], blocks[:, :, :, 1],
                      rows_w[:, :, 0], rows_w[:, :, 1], kwpos, rel_bias)


def causal_conv(x, buf, w, b):
    T = x.shape[1]
    xp = jnp.concatenate([buf.astype(x.dtype), x], axis=1)
    y = xp[:, CONV_WIDTH - 1:] * w[CONV_WIDTH - 1] + b
    for k in range(CONV_WIDTH - 1):
        y = y + xp[:, k:k + T] * w[k]
    return y, xp[:, T:]


def rg_lru(x, h0, w_a, b_a, w_i, b_i, lam):
    B, T = x.shape[0], x.shape[1]
    xb = x.reshape(B, T, N_RNN_BLOCKS, RNN_BLOCK)
    r = jax.nn.sigmoid(jnp.einsum('btnd,nde->btne', xb, w_a).reshape(B, T, D_RNN) + b_a)
    i = jax.nn.sigmoid(jnp.einsum('btnd,nde->btne', xb, w_i).reshape(B, T, D_RNN) + b_i)
    log_a = -LRU_C * r.astype(jnp.float32) * jax.nn.softplus(-lam.astype(jnp.float32))
    u = jnp.sqrt(-jnp.expm1(2.0 * log_a)) * (i * x).astype(jnp.float32)

    def step(h, inp):
        a_t, u_t = inp
        h = a_t * h + u_t
        return h, h

    h_last, hs = lax.scan(step, h0.astype(jnp.float32), (jnp.exp(log_a).swapaxes(0, 1), u.swapaxes(0, 1)))
    return hs.swapaxes(0, 1).astype(x.dtype), h_last.astype(x.dtype)


def mixer(h, attend, conv_buf, h0, lp):
    B, T = h.shape[0], h.shape[1]
    proj = h @ lp['w_in'] + lp['b_in']
    cuts = [int(v) for v in np.cumsum(IN_SPLITS)[:-1]]
    q, kvc, kvs, kvw, nsa_g, r_x, r_y, merge_g = jnp.split(proj, cuts, axis=-1)
    kv_shape = (B, T, 2, N_KV_HEADS, HEAD_DIM)
    kvc = kvc.reshape(kv_shape)
    kvs = kvs.reshape(kv_shape)
    kvw = kvw.reshape(kv_shape)
    gates = jax.nn.sigmoid(nsa_g).reshape(B, T, N_KV_HEADS, GROUP, 3)
    o_attn = attend(q.reshape(B, T, N_KV_HEADS, GROUP, HEAD_DIM), kvc, kvs, kvw, gates) @ lp['w_attn_o']
    xc, conv_new = causal_conv(r_x, conv_buf, lp['conv_w'], lp['conv_b'])
    hs, h_last = rg_lru(xc, h0, lp['lru_wa'], lp['lru_ba'], lp['lru_wi'], lp['lru_bi'], lp['lru_lambda'])
    o_rnn = (hs * jax.nn.gelu(r_y)) @ lp['w_rnn_o']
    g_attn, g_rnn = jnp.split(jax.nn.sigmoid(merge_g), 2, axis=-1)
    out = (g_attn * o_attn + g_rnn * o_rnn) @ lp['w_out']
    return out, (kvc, kvs, kvw, conv_new, h_last)


def trunk_layer(x, c, attend, conv_buf, h0, lp):
    ada = jax.nn.silu(c) @ lp['w_ada'] + lp['b_ada']
    sh1, sc1, g1, sh2, sc2, g2 = [a[:, None, :] for a in jnp.split(ada, 6, axis=-1)]
    mix, states = mixer(x * (1 + sc1) + sh1, attend, conv_buf, h0, lp)
    x = layer_norm(ALPHA * x + (1 + g1) * mix, lp['ln_g'][0], lp['ln_b'][0])
    gt, up = jnp.split((x * (1 + sc2) + sh2) @ lp['w_ffn_in'], 2, axis=-1)
    f = (jax.nn.silu(gt) * up) @ lp['w_ffn_out']
    x = layer_norm(ALPHA * x + (1 + g2) * f, lp['ln_g'][1], lp['ln_b'][1])
    return x, states


def setup_inputs(seed: int = 0) -> dict:
    key = jax.random.key(seed)
    ks = jax.random.split(key, 32)
    f32 = jnp.float32

    def nrm(k, shape, scale):
        return jax.random.normal(k, shape, f32) * scale

    n_pages = PAST_LEN // PAGE_SIZE
    n_used = DEC_BATCH * n_pages
    n_phys = n_used + max(1, n_used // 4)
    win_rows = min(WINDOW, PAST_LEN)
    kv_paged = (DEPTH, n_phys, PAGE_SIZE, 2, N_KV_HEADS, HEAD_DIM)
    page_table = jax.random.permutation(ks[9], n_phys)[:n_used].reshape(DEC_BATCH, n_pages).astype(jnp.int32)
    u = jax.random.uniform(ks[25], (DEPTH, D_RNN), f32, 0.9, 0.999)
    a_base = u ** (1.0 / LRU_C)
    lru_lambda = jnp.log(a_base) - jnp.log1p(-a_base)
    return {
        'x_prompt': nrm(ks[0], (BATCH, SEQ, D_MODEL), 1.0),
        'x_sample': nrm(ks[1], (DEC_BATCH, DEC_SEQ, D_MODEL), 1.0),
        'c_prompt': nrm(ks[2], (BATCH, D_MODEL), 1.0),
        'c_sample': nrm(ks[3], (DEC_BATCH, D_MODEL), 1.0),
        'cache_cmp_kv': nrm(ks[4], kv_paged, 1.0),
        'cache_slc_kv': nrm(ks[5], kv_paged, 1.0),
        'cache_win_kv': nrm(ks[6], (DEPTH, DEC_BATCH, win_rows, 2, N_KV_HEADS, HEAD_DIM), 1.0),
        'state_conv': nrm(ks[7], (DEPTH, DEC_BATCH, CONV_WIDTH - 1, D_RNN), 1.0),
        'state_rnn_h': nrm(ks[8], (DEPTH, DEC_BATCH, D_RNN), 0.5),
        'page_table': page_table,
        'rel_bias': nrm(ks[10], (N_BUCKETS, N_HEADS), 0.5),
        'w_ada': nrm(ks[11], (DEPTH, D_MODEL, 6 * D_MODEL), 0.2 * D_MODEL ** -0.5),
        'b_ada': nrm(ks[12], (DEPTH, 6 * D_MODEL), 0.01),
        'w_in': nrm(ks[13], (DEPTH, D_MODEL, N_IN), D_MODEL ** -0.5),
        'b_in': nrm(ks[14], (DEPTH, N_IN), 0.01),
        'cmp_pos': nrm(ks[15], (DEPTH, 2, CMP_LEN, HEAD_DIM), 0.1),
        'cmp_w1': nrm(ks[16], (DEPTH, 2, CMP_LEN, HEAD_DIM, HEAD_DIM), (CMP_LEN * HEAD_DIM) ** -0.5),
        'cmp_w2': nrm(ks[17], (DEPTH, 2, HEAD_DIM, HEAD_DIM), HEAD_DIM ** -0.5),
        'w_attn_o': nrm(ks[18], (DEPTH, Q_W, D_MODEL), Q_W ** -0.5),
        'conv_w': nrm(ks[19], (DEPTH, CONV_WIDTH, D_RNN), CONV_WIDTH ** -0.5),
        'conv_b': nrm(ks[20], (DEPTH, D_RNN), 0.01),
        'lru_wa': nrm(ks[21], (DEPTH, N_RNN_BLOCKS, RNN_BLOCK, RNN_BLOCK), RNN_BLOCK ** -0.5),
        'lru_ba': nrm(ks[22], (DEPTH, D_RNN), 0.01),
        'lru_wi': nrm(ks[23], (DEPTH, N_RNN_BLOCKS, RNN_BLOCK, RNN_BLOCK), RNN_BLOCK ** -0.5),
        'lru_bi': nrm(ks[24], (DEPTH, D_RNN), 0.01),
        'lru_lambda': lru_lambda,
        'w_rnn_o': nrm(ks[26], (DEPTH, D_RNN, D_MODEL), D_RNN ** -0.5),
        'w_out': nrm(ks[27], (DEPTH, D_MODEL, D_MODEL), BETA * D_MODEL ** -0.5),
        'w_ffn_in': nrm(ks[28], (DEPTH, D_MODEL, 2 * D_FF), D_MODEL ** -0.5),
        'w_ffn_out': nrm(ks[29], (DEPTH, D_FF, D_MODEL), BETA * D_FF ** -0.5),
        'ln_g': 1.0 + nrm(ks[30], (DEPTH, 2, D_MODEL), 0.01),
        'ln_b': nrm(ks[31], (DEPTH, 2, D_MODEL), 0.01),
    }


def reference(x_prompt, x_sample, c_prompt, c_sample, cache_cmp_kv, cache_slc_kv, cache_win_kv, state_conv,
              state_rnn_h, page_table, rel_bias, w_ada, b_ada, w_in, b_in, cmp_pos, cmp_w1, cmp_w2, w_attn_o,
              conv_w, conv_b, lru_wa, lru_ba, lru_wi, lru_bi, lru_lambda, w_rnn_o, w_out, w_ffn_in, w_ffn_out,
              ln_g, ln_b):
    y_p, y_s = x_prompt, x_sample
    st_p, st_s = [], []
    n_keep = min(WINDOW, x_prompt.shape[1])
    for l in range(DEPTH):
        lp = {'w_ada': w_ada[l], 'b_ada': b_ada[l], 'w_in': w_in[l], 'b_in': b_in[l], 'w_attn_o': w_attn_o[l],
              'conv_w': conv_w[l], 'conv_b': conv_b[l], 'lru_wa': lru_wa[l], 'lru_ba': lru_ba[l],
              'lru_wi': lru_wi[l], 'lru_bi': lru_bi[l], 'lru_lambda': lru_lambda[l], 'w_rnn_o': w_rnn_o[l],
              'w_out': w_out[l], 'w_ffn_in': w_ffn_in[l], 'w_ffn_out': w_ffn_out[l], 'ln_g': ln_g[l],
              'ln_b': ln_b[l]}
        attend_p = functools.partial(nsa_prompt, cmp_pos=cmp_pos[l], cmp_w1=cmp_w1[l], cmp_w2=cmp_w2[l],
                                     rel_bias=rel_bias)
        attend_s = functools.partial(nsa_sample, cache_c=cache_cmp_kv[l], cache_s=cache_slc_kv[l],
                                     win_buf=cache_win_kv[l], page_table=page_table, cmp_pos=cmp_pos[l],
                                     cmp_w1=cmp_w1[l], cmp_w2=cmp_w2[l], rel_bias=rel_bias)
        zero_buf = jnp.zeros((x_prompt.shape[0], CONV_WIDTH - 1, D_RNN), x_prompt.dtype)
        zero_h = jnp.zeros((x_prompt.shape[0], D_RNN), jnp.float32)
        y_p, (kc, ksl, kw, cv, hh) = trunk_layer(y_p, c_prompt, attend_p, zero_buf, zero_h, lp)
        st_p.append((kc, ksl, kw[:, kw.shape[1] - n_keep:], cv, hh))
        y_s, sts = trunk_layer(y_s, c_sample, attend_s, state_conv[l], state_rnn_h[l], lp)
        st_s.append(sts)
    new_cmp_kv_prompt = jnp.stack([s[0] for s in st_p])
    new_slc_kv_prompt = jnp.stack([s[1] for s in st_p])
    new_win_kv_prompt = jnp.stack([s[2] for s in st_p])
    new_conv_prompt = jnp.stack([s[3] for s in st_p])
    new_rnn_h_prompt = jnp.stack([s[4] for s in st_p])
    new_cmp_kv_sample = jnp.stack([s[0] for s in st_s])
    new_slc_kv_sample = jnp.stack([s[1] for s in st_s])
    new_win_kv_sample = jnp.stack([s[2] for s in st_s])
    new_conv_sample = jnp.stack([s[3] for s in st_s])
    new_rnn_h_sample = jnp.stack([s[4] for s in st_s])
    return (y_p, y_s, new_cmp_kv_prompt, new_slc_kv_prompt, new_win_kv_prompt, new_conv_prompt, new_rnn_h_prompt,
            new_cmp_kv_sample, new_slc_kv_sample, new_win_kv_sample, new_conv_sample, new_rnn_h_sample)
```

```python
import functools
import math

import numpy as np
import jax
import jax.numpy as jnp
from jax import lax
from jax.experimental import pallas as pl
from jax.experimental.pallas import tpu as pltpu

f32 = jnp.float32
bf16 = jnp.bfloat16
i32 = jnp.int32

D_MODEL = 2048
N_HEADS = 16
N_KV = 4
GROUP = 4
HEAD_DIM = 64
CMP_STRIDE = 16
CMP_LEN = 32
SEL_LEN = 64
N_SEL = 16
WINDOW = 512
PAGE = 128
FORCE_BONUS = 1e3
N_BUCKETS = 32
MAX_DISTANCE = 128
D_RNN = 1024
N_RNN_BLOCKS = 16
RNN_BLOCK = 64
CONV_W = 4
LRU_C = 8.0
D_FF = 5632
DEPTH = 2
ALPHA = (2.0 * DEPTH) ** 0.25
SCALE = HEAD_DIM ** -0.5
NEG = -1e30

C_MG = 0
C_Q = 4096
C_RX = 5120
C_RY = 6144
C_KVC = 7168
C_KVS = 7680
C_KVW = 8192
C_NG = 8704
N_PACK = 8832

VMEM_LIMIT = 56 * 1024 * 1024

TQ = 128
TK = 256


def _cparams(sem):
    return pltpu.CompilerParams(dimension_semantics=sem, vmem_limit_bytes=VMEM_LIMIT)


def _sigmoid(x):
    return 1.0 / (1.0 + jnp.exp(-x))


def _silu(x):
    return x * _sigmoid(x)


def _layer_norm(z, g, b):
    mu = jnp.mean(z, axis=-1, keepdims=True)
    zc = z - mu
    var = jnp.mean(zc * zc, axis=-1, keepdims=True)
    return zc * lax.rsqrt(var + 1e-5) * g + b


def _dot(a, b):
    return jnp.dot(a, b, preferred_element_type=f32)


def _dot_nt(a, b):
    return lax.dot_general(a, b, (((1,), (1,)), ((), ())), preferred_element_type=f32)


def _split3(x):
    hi = x.astype(bf16)
    r1 = x - hi.astype(f32)
    mid = r1.astype(bf16)
    lo = (r1 - mid.astype(f32)).astype(bf16)
    return hi, mid, lo


def _ada_kernel(c_ref, w_ref, b_ref, o_ref):
    h = _silu(c_ref[...]).astype(bf16)
    o_ref[...] = _dot(h, w_ref[...].astype(bf16)) + b_ref[...]


def _ada(c_all, w, b):
    m = c_all.shape[0]
    n = w.shape[1]
    tn = 1024
    return pl.pallas_call(
        _ada_kernel,
        grid=(n // tn,),
        in_specs=[pl.BlockSpec((m, D_MODEL), lambda j: (0, 0)),
                  pl.BlockSpec((D_MODEL, tn), lambda j: (0, j)),
                  pl.BlockSpec((1, tn), lambda j: (0, j))],
        out_specs=pl.BlockSpec((m, tn), lambda j: (0, j)),
        out_shape=jax.ShapeDtypeStruct((m, n), f32),
        compiler_params=_cparams(("arbitrary",)),
        name="ada",
    )(c_all, w, b.reshape(1, n))


def _bias_lookup_kernel(tbl_ref, bk_ref, o_ref):
    bk = bk_ref[...]
    for h in range(N_HEADS):
        acc = jnp.zeros(bk.shape, f32)
        for k in range(N_BUCKETS):
            acc = jnp.where(bk == k, tbl_ref[k, h], acc)
        o_ref[h] = acc


def _bias_lookup(tbl, buckets):
    n = buckets.shape[0]
    tr = 256
    return pl.pallas_call(
        _bias_lookup_kernel,
        grid=(n // tr,),
        in_specs=[pl.BlockSpec(memory_space=pltpu.SMEM),
                  pl.BlockSpec((tr, 128), lambda i: (i, 0))],
        out_specs=pl.BlockSpec((N_HEADS, tr, 128), lambda i: (0, i, 0)),
        out_shape=jax.ShapeDtypeStruct((N_HEADS, n, 128), f32),
        compiler_params=_cparams(("arbitrary",)),
        name="bias_lookup",
    )(tbl, buckets)


def _t5_bucket_np(dist):
    n = np.maximum(dist, 0)
    exact = N_BUCKETS // 2
    ratio = np.maximum(n, 1).astype(np.float32) / np.float32(exact)
    log_ratio = np.log(ratio).astype(np.float32) / np.float32(math.log(MAX_DISTANCE / exact))
    large = np.minimum(exact + (log_ratio * np.float32(N_BUCKETS - exact)).astype(np.int32), N_BUCKETS - 1)
    return np.where(n < exact, n, large).astype(np.int32)


def _inproj_kernel(x_ref, sc_ref, sh_ref, w_ref, b_ref, o_ref, h_scr):
    @pl.when(pl.program_id(1) == 0)
    def _():
        h_scr[...] = (x_ref[...] * (1.0 + sc_ref[...]) + sh_ref[...]).astype(bf16)

    o_ref[...] = _dot(h_scr[...], w_ref[...]) + b_ref[...]


def _inproj(x, sc, sh, w, b, tm):
    m = x.shape[0]
    n = w.shape[1]
    tn = 384
    mr = sc.shape[1]
    tpm = (m // tm) // sc.shape[0]
    mod_spec = pl.BlockSpec((None, mr, D_MODEL), lambda i, j: (i // tpm, 0, 0))
    return pl.pallas_call(
        _inproj_kernel,
        grid=(m // tm, n // tn),
        in_specs=[pl.BlockSpec((tm, D_MODEL), lambda i, j: (i, 0)),
                  mod_spec, mod_spec,
                  pl.BlockSpec((D_MODEL, tn), lambda i, j: (0, j)),
                  pl.BlockSpec((1, tn), lambda i, j: (0, j))],
        out_specs=pl.BlockSpec((tm, tn), lambda i, j: (i, j)),
        out_shape=jax.ShapeDtypeStruct((m, n), f32),
        scratch_shapes=[pltpu.VMEM((tm, D_MODEL), bf16)],
        compiler_params=_cparams(("parallel", "arbitrary")),
        name="inproj",
    )(x, sc, sh, w, b)


def _compress_chunks(piece, n_rows, kv, wl_ref, wt_ref, pe_ref, w2_ref):
    lead = jnp.zeros((n_rows, 256), f32)
    tail = jnp.zeros((n_rows, 256), f32)
    for l in range(CMP_STRIDE):
        x = piece(l)
        lead = lead + _dot((x + pe_ref[kv, 0, l:l + 1, :]).astype(bf16), wl_ref[kv, l])
        tail = tail + _dot((x + pe_ref[kv, 1, l:l + 1, :]).astype(bf16), wt_ref[kv, l])
    z = lead + pltpu.roll(tail, n_rows - 1, 0)
    return _dot(_silu(z).astype(bf16), w2_ref[kv])


def _compress_prompt_kernel(x_ref, wl_ref, wt_ref, pe_ref, w2_ref, kc_ref, vc_ref):
    n_chunk = x_ref.shape[0]
    for kv, dst in enumerate((kc_ref, vc_ref)):
        piece = lambda l: x_ref[:, l * 512 + kv * 256:l * 512 + (kv + 1) * 256]
        out = _compress_chunks(piece, n_chunk, kv, wl_ref, wt_ref, pe_ref, w2_ref)
        row = lax.broadcasted_iota(i32, out.shape, 0)
        dst[...] = jnp.where(row < n_chunk - 1, out, 0.0)


def _compress_prompt(rows, wl, wt, pe, w2):
    nb, n_chunk, width = rows.shape
    full = lambda a: pl.BlockSpec(a.shape, lambda b: (0,) * a.ndim)
    return pl.pallas_call(
        _compress_prompt_kernel,
        grid=(nb,),
        in_specs=[pl.BlockSpec((None, n_chunk, width), lambda b: (b, 0, 0)),
                  full(wl), full(wt), full(pe), full(w2)],
        out_specs=[pl.BlockSpec((None, n_chunk, 256), lambda b: (b, 0, 0))] * 2,
        out_shape=[jax.ShapeDtypeStruct((nb, n_chunk, 256), f32)] * 2,
        compiler_params=_cparams(("arbitrary",)),
        name="compress_prompt",
    )(rows, wl, wt, pe, w2)


def _topk_mask_rows(imp, sidx, n_keep):
    n_rows = imp.shape[0]
    rank = jnp.zeros(imp.shape, i32)
    for j in range(n_rows):
        row = imp[j:j + 1, :]
        beats = jnp.where(row > imp, 1, jnp.where(row == imp, jnp.where(sidx > j, 1, 0), 0))
        rank = rank + beats
    return jnp.where(rank < n_keep, jnp.where(imp > -1e29, 1.0, 0.0), 0.0)


def _nsa_prompt_kernel(q_ref, ng_ref, ks_ref, vs_ref, kw_ref, vw_ref, kc_ref, vc_ref,
                       bc_ref, bt_ref, ct_ref, e_ref, o_ref, selk_scr):
    i = pl.program_id(1)
    q0 = i * TQ
    rows = GROUP * TQ
    n_sel_blk = ks_ref.shape[0] // SEL_LEN

    d0 = (lax.broadcasted_iota(i32, (rows, TK), 0) & (TQ - 1)) - lax.broadcasted_iota(i32, (rows, TK), 1)
    dist_c = (q0 + (lax.broadcasted_iota(i32, (rows, 128), 0) & (TQ - 1))
              - CMP_STRIDE * lax.broadcasted_iota(i32, (rows, 128), 1) - (CMP_LEN - 1))
    mask_c = dist_c >= 0
    sig = _sigmoid(ng_ref[...])

    sidx = lax.broadcasted_iota(i32, (n_sel_blk, TQ), 0)
    cur = (q0 + lax.broadcasted_iota(i32, (n_sel_blk, TQ), 1)) // SEL_LEN
    forced = jnp.where(sidx == 0, 1, jnp.where(sidx == cur, 1, jnp.where(sidx == cur - 1, 1, 0)))

    def branch(qs, g, k_ref, v_ref, kt_lo, kt_hi, window):
        def body(kt, carry):
            m, l, acc = carry
            k0 = pl.multiple_of(kt * TK, TK)
            k = k_ref[pl.ds(k0, TK), g * 64:(g + 1) * 64].astype(bf16)
            v = v_ref[pl.ds(k0, TK), g * 64:(g + 1) * 64].astype(bf16)
            delta = q0 - k0
            va = jnp.clip(delta, 0, 256) // 128
            vb = jnp.clip(delta - 128, 0, 256) // 128
            bias = jnp.concatenate([bt_ref[g * 3 + va], bt_ref[g * 3 + vb]], axis=1)
            s = _dot_nt(qs, k) + bias
            dist = d0 + delta
            if window:
                s = jnp.where(dist >= 0, jnp.where(dist < WINDOW, s, NEG), NEG)
            else:
                t = selk_scr[:, pl.ds(k0, TK)]
                sel4 = jnp.concatenate([t, t, t, t], axis=0)
                s = jnp.where(dist >= 0, jnp.where(sel4 > 0.5, s, NEG), NEG)
            m_new = jnp.maximum(m, jnp.max(s, axis=-1, keepdims=True))
            alpha = jnp.exp(m - m_new)
            p = jnp.where(s > -1e29, jnp.exp(s - m_new), 0.0)
            l = alpha * l + jnp.sum(p, axis=-1, keepdims=True)
            acc = alpha * acc + _dot(p.astype(bf16), v)
            return m_new, l, acc

        init = (jnp.full((rows, 1), NEG, f32), jnp.zeros((rows, 1), f32), jnp.zeros((rows, HEAD_DIM), f32))
        m, l, acc = lax.fori_loop(kt_lo, kt_hi + 1, body, init)
        return acc / jnp.maximum(l, 1e-30)

    for g in range(N_KV):
        qg = q_ref[:, g * 256:(g + 1) * 256]
        qs = (jnp.concatenate([qg[:, r * 64:(r + 1) * 64] for r in range(GROUP)], axis=0) * SCALE).astype(bf16)

        kcg = kc_ref[:, g * 64:(g + 1) * 64].astype(bf16)
        vcg = vc_ref[:, g * 64:(g + 1) * 64].astype(bf16)
        bias_c = jnp.concatenate([bc_ref[g * GROUP + r] for r in range(GROUP)], axis=0)
        s = jnp.where(mask_c, _dot_nt(qs, kcg) + bias_c, NEG)
        p = jnp.where(mask_c, jnp.exp(s - jnp.max(s, axis=-1, keepdims=True)), 0.0)
        p = p / jnp.maximum(jnp.sum(p, axis=-1, keepdims=True), 1e-30)
        o_c = _dot(p.astype(bf16), vcg)

        p_sum = p[0:TQ] + p[TQ:2 * TQ] + p[2 * TQ:3 * TQ] + p[3 * TQ:4 * TQ]
        ct = ct_ref[...]
        imp_t = sum(_dot_nt(ct, piece) for piece in _split3(p_sum))[0:n_sel_blk]
        imp_t = jnp.where(forced > 0, imp_t + FORCE_BONUS, imp_t)
        imp_t = jnp.where(sidx <= cur, imp_t, NEG)
        sel_t = _topk_mask_rows(imp_t, sidx, N_SEL)
        sel_pad = jnp.concatenate([sel_t, jnp.zeros((128 - n_sel_blk, TQ), f32)], axis=0)
        sel_q = sel_pad.T.astype(bf16)
        selk_scr[...] = _dot(sel_q, e_ref[...])

        o_s = branch(qs, g, ks_ref, vs_ref, 0, (q0 + TQ - 1) // TK, False)
        o_w = branch(qs, g, kw_ref, vw_ref, jnp.maximum(q0 - (WINDOW - 1), 0) // TK, (q0 + TQ - 1) // TK, True)

        gate = lambda c: jnp.concatenate(
            [sig[:, g * 12 + r * 3 + c:g * 12 + r * 3 + c + 1] for r in range(GROUP)], axis=0)
        o = gate(0) * o_c + gate(1) * o_s + gate(2) * o_w
        o_ref[:, g * 256:(g + 1) * 256] = jnp.concatenate(
            [o[r * TQ:(r + 1) * TQ] for r in range(GROUP)], axis=1).astype(bf16)


def _nsa_prompt(proj, kc, vc, bias_c, bias_t, ct, e, nb, t):
    nq = t // TQ
    full = lambda a: pl.BlockSpec(a.shape, lambda b, i: (0,) * a.ndim)
    kv = lambda c: pl.BlockSpec((t, 256), lambda b, i: (b, c // 256))
    return pl.pallas_call(
        _nsa_prompt_kernel,
        grid=(nb, nq),
        in_specs=[pl.BlockSpec((TQ, 1024), lambda b, i: (b * nq + i, C_Q // 1024)),
                  pl.BlockSpec((TQ, 128), lambda b, i: (b * nq + i, C_NG // 128)),
                  kv(C_KVS), kv(C_KVS + 256), kv(C_KVW), kv(C_KVW + 256),
                  pl.BlockSpec((None, t // CMP_STRIDE, 256), lambda b, i: (b, 0, 0)),
                  pl.BlockSpec((None, t // CMP_STRIDE, 256), lambda b, i: (b, 0, 0)),
                  pl.BlockSpec((N_HEADS, TQ, 128), lambda b, i: (0, i, 0)),
                  full(bias_t), full(ct), full(e)],
        out_specs=pl.BlockSpec((TQ, 1024), lambda b, i: (b * nq + i, 0)),
        out_shape=jax.ShapeDtypeStruct((nb * t, 1024), bf16),
        scratch_shapes=[pltpu.VMEM((TQ, t), f32)],
        compiler_params=_cparams(("parallel", "arbitrary")),
        name="nsa_prompt",
    )(proj, proj, proj, proj, proj, proj, kc, vc, bias_c, bias_t, ct, e)


def _lru_gates(xc, ry, wa_ref, ba_ref, wi_ref, bi_ref, lam_ref):
    xb = xc.astype(bf16)
    ra = jnp.concatenate([_dot(xb[:, c * 256:(c + 1) * 256], wa_ref[c]) for c in range(4)], axis=1)
    ri = jnp.concatenate([_dot(xb[:, c * 256:(c + 1) * 256], wi_ref[c]) for c in range(4)], axis=1)
    r = _sigmoid(ra + ba_ref[...])
    ig = _sigmoid(ri + bi_ref[...])
    nl = -lam_ref[...]
    softplus = jnp.maximum(nl, 0.0) + jnp.log1p(jnp.exp(-jnp.abs(nl)))
    log_a = -LRU_C * r * softplus
    a = jnp.exp(log_a)
    u = jnp.sqrt(jnp.tanh(-log_a) * (a * a + 1.0)) * (ig * xc)
    return a, u, jax.nn.gelu(ry)


def _rglru_prompt_kernel(rx_ref, ry_ref, cw_ref, cb_ref, wa_ref, ba_ref, wi_ref, bi_ref, lam_ref,
                         o_ref, hl_ref, xp_scr, a_scr, u_scr, hs_scr, h_scr):
    tt = pl.program_id(1)
    tr = rx_ref.shape[0]

    @pl.when(tt == 0)
    def _():
        xp_scr[0:8, :] = jnp.zeros((8, D_RNN), f32)
        h_scr[...] = jnp.zeros((1, D_RNN), f32)

    x = rx_ref[...]
    xp_scr[8:8 + tr, :] = x
    y = x * cw_ref[CONV_W - 1:CONV_W, :] + cb_ref[...]
    for k in range(CONV_W - 1):
        y = y + xp_scr[5 + k:5 + k + tr, :] * cw_ref[k:k + 1, :]
    xp_scr[0:8, :] = xp_scr[tr:tr + 8, :]

    a, u, gate = _lru_gates(y, ry_ref[...], wa_ref, ba_ref, wi_ref, bi_ref, lam_ref)
    a_scr[...] = a
    u_scr[...] = u

    def step(t, h):
        h = a_scr[pl.ds(t, 1), :] * h + u_scr[pl.ds(t, 1), :]
        hs_scr[pl.ds(t, 1), :] = h
        return h

    h = lax.fori_loop(0, tr, step, h_scr[...], unroll=8)
    h_scr[...] = h
    hl_ref[...] = h
    o_ref[...] = (hs_scr[...] * gate).astype(bf16)


def _rglru_prompt(proj, nb, t, cw, cb, wa, ba, wi, bi, lam):
    tr = 256
    nt = t // tr
    full = lambda a: pl.BlockSpec(a.shape, lambda b, i: (0,) * a.ndim)
    return pl.pallas_call(
        _rglru_prompt_kernel,
        grid=(nb, nt),
        in_specs=[pl.BlockSpec((tr, D_RNN), lambda b, i: (b * nt + i, C_RX // D_RNN)),
                  pl.BlockSpec((tr, D_RNN), lambda b, i: (b * nt + i, C_RY // D_RNN)),
                  full(cw), full(cb), full(wa), full(ba), full(wi), full(bi), full(lam)],
        out_specs=[pl.BlockSpec((tr, D_RNN), lambda b, i: (b * nt + i, 0)),
                   pl.BlockSpec((None, 1, D_RNN), lambda b, i: (b, 0, 0))],
        out_shape=[jax.ShapeDtypeStruct((nb * t, D_RNN), bf16),
                   jax.ShapeDtypeStruct((nb, 1, D_RNN), f32)],
        scratch_shapes=[pltpu.VMEM((tr + 8, D_RNN), f32), pltpu.VMEM((tr, D_RNN), f32),
                        pltpu.VMEM((tr, D_RNN), f32), pltpu.VMEM((tr, D_RNN), f32),
                        pltpu.VMEM((1, D_RNN), f32)],
        compiler_params=_cparams(("parallel", "arbitrary")),
        name="rglru_prompt",
    )(proj, proj, cw, cb, wa, ba, wi, bi, lam)


def _rglru_sample_kernel(rx_ref, ry_ref, cbuf_ref, h0_ref, cw_ref, cb_ref, wa_ref, ba_ref, wi_ref, bi_ref,
                         lam_ref, o_ref, hl_ref):
    n_t = rx_ref.shape[0]
    xp = [cbuf_ref[k] for k in range(CONV_W - 1)] + [rx_ref[s] for s in range(n_t)]
    h = h0_ref[...]
    for s in range(n_t):
        y = xp[s + CONV_W - 1] * cw_ref[CONV_W - 1:CONV_W, :] + cb_ref[...]
        for k in range(CONV_W - 1):
            y = y + xp[s + k] * cw_ref[k:k + 1, :]
        a, u, gate = _lru_gates(y, ry_ref[s], wa_ref, ba_ref, wi_ref, bi_ref, lam_ref)
        h = a * h + u
        o_ref[s] = (h * gate).astype(bf16)
    hl_ref[...] = h


def _rglru_sample(rx, ry, cbuf, h0, cw, cb, wa, ba, wi, bi, lam):
    n_t, nb, _ = rx.shape
    return pl.pallas_call(
        _rglru_sample_kernel,
        out_shape=[jax.ShapeDtypeStruct((n_t, nb, D_RNN), bf16),
                   jax.ShapeDtypeStruct((nb, D_RNN), f32)],
        compiler_params=pltpu.CompilerParams(vmem_limit_bytes=VMEM_LIMIT),
        name="rglru_sample",
    )(rx, ry, cbuf, h0, cw, cb, wa, ba, wi, bi, lam)


def _merge_kernel(oa_ref, orn_ref, ga_ref, gr_ref, wa_ref, wr_ref, o_ref):
    a1 = _dot(oa_ref[...], wa_ref[...])
    a2 = _dot(orn_ref[...], wr_ref[...])
    o_ref[...] = (_sigmoid(ga_ref[...]) * a1 + _sigmoid(gr_ref[...]) * a2).astype(bf16)


def _merge(o_attn, o_rnn, proj, w_ao, w_ro, tm):
    m = o_attn.shape[0]
    tn = 512
    nn = D_MODEL // tn
    return pl.pallas_call(
        _merge_kernel,
        grid=(m // tm, nn),
        in_specs=[pl.BlockSpec((tm, 1024), lambda i, j: (i, 0)),
                  pl.BlockSpec((tm, 1024), lambda i, j: (i, 0)),
                  pl.BlockSpec((tm, tn), lambda i, j: (i, j)),
                  pl.BlockSpec((tm, tn), lambda i, j: (i, nn + j)),
                  pl.BlockSpec((1024, tn), lambda i, j: (0, j)),
                  pl.BlockSpec((1024, tn), lambda i, j: (0, j))],
        out_specs=pl.BlockSpec((tm, tn), lambda i, j: (i, j)),
        out_shape=jax.ShapeDtypeStruct((m, D_MODEL), bf16),
        compiler_params=_cparams(("parallel", "arbitrary")),
        name="merge",
    )(o_attn, o_rnn, proj, proj, w_ao, w_ro)


def _outproj_kernel(a_ref, x_ref, gt_ref, w_ref, lg_ref, lb_ref, o_ref):
    mix = _dot(a_ref[...], w_ref[...])
    z = ALPHA * x_ref[...] + (1.0 + gt_ref[...]) * mix
    o_ref[...] = _layer_norm(z, lg_ref[...], lb_ref[...])


def _outproj(a, x, gate, w, lg, lb, tm):
    m = a.shape[0]
    mr = gate.shape[1]
    tpm = (m // tm) // gate.shape[0]
    return pl.pallas_call(
        _outproj_kernel,
        grid=(m // tm,),
        in_specs=[pl.BlockSpec((tm, D_MODEL), lambda i: (i, 0)),
                  pl.BlockSpec((tm, D_MODEL), lambda i: (i, 0)),
                  pl.BlockSpec((None, mr, D_MODEL), lambda i: (i // tpm, 0, 0)),
                  pl.BlockSpec((D_MODEL, D_MODEL), lambda i: (0, 0)),
                  pl.BlockSpec((1, D_MODEL), lambda i: (0, 0)),
                  pl.BlockSpec((1, D_MODEL), lambda i: (0, 0))],
        out_specs=pl.BlockSpec((tm, D_MODEL), lambda i: (i, 0)),
        out_shape=jax.ShapeDtypeStruct((m, D_MODEL), f32),
        compiler_params=_cparams(("parallel",)),
        name="outproj_ln",
    )(a, x, gate, w, lg, lb)


def _ffn_kernel(x_ref, sc_ref, sh_ref, gt_ref, wg_ref, wu_ref, wo_ref, lg_ref, lb_ref, o_ref, h_scr, acc_scr):
    j = pl.program_id(1)

    @pl.when(j == 0)
    def _():
        h_scr[...] = (x_ref[...] * (1.0 + sc_ref[...]) + sh_ref[...]).astype(bf16)
        acc_scr[...] = jnp.zeros(acc_scr.shape, f32)

    h = h_scr[...]
    act = (_silu(_dot(h, wg_ref[...])) * _dot(h, wu_ref[...])).astype(bf16)
    acc_scr[...] += _dot(act, wo_ref[...])

    @pl.when(j == pl.num_programs(1) - 1)
    def _():
        z = ALPHA * x_ref[...] + (1.0 + gt_ref[...]) * acc_scr[...]
        o_ref[...] = _layer_norm(z, lg_ref[...], lb_ref[...])


def _ffn(x, sc, sh, gate, w_in, w_out, lg, lb, tm):
    m = x.shape[0]
    tf = 512
    nf = D_FF // tf
    mr = sc.shape[1]
    tpm = (m // tm) // sc.shape[0]
    mod_spec = pl.BlockSpec((None, mr, D_MODEL), lambda i, j: (i // tpm, 0, 0))
    return pl.pallas_call(
        _ffn_kernel,
        grid=(m // tm, nf),
        in_specs=[pl.BlockSpec((tm, D_MODEL), lambda i, j: (i, 0)),
                  mod_spec, mod_spec, mod_spec,
                  pl.BlockSpec((D_MODEL, tf), lambda i, j: (0, j)),
                  pl.BlockSpec((D_MODEL, tf), lambda i, j: (0, nf + j)),
                  pl.BlockSpec((tf, D_MODEL), lambda i, j: (j, 0)),
                  pl.BlockSpec((1, D_MODEL), lambda i, j: (0, 0)),
                  pl.BlockSpec((1, D_MODEL), lambda i, j: (0, 0))],
        out_specs=pl.BlockSpec((tm, D_MODEL), lambda i, j: (i, 0)),
        out_shape=jax.ShapeDtypeStruct((m, D_MODEL), f32),
        scratch_shapes=[pltpu.VMEM((tm, D_MODEL), bf16), pltpu.VMEM((tm, D_MODEL), f32)],
        compiler_params=_cparams(("parallel", "arbitrary")),
        name="ffn",
    )(x, sc, sh, gate, w_in, w_in, w_out, lg, lb)


N_PAGES = 16
S_TILES = N_PAGES + 1
W_TILES = WINDOW // 128 + 1


def _nsa_sample_kernel(pt_ref, qrow_ref, ng_ref, newc_ref, news_ref, neww_ref, *rest):
    cc = rest[0:N_PAGES]
    cs = rest[N_PAGES:2 * N_PAGES]
    (win_ref, wl_ref, wt_ref, pe_ref, w2_ref, bc_ref, mc_ref, bs_ref, ms_ref, bw_ref, mw_ref,
     smat_ref, smt_ref, c_ref, e_ref, dsel_ref, o_ref, new_scr) = rest[2 * N_PAGES:]
    n_new = news_ref.shape[0]
    n_chunk = N_PAGES * (PAGE // CMP_STRIDE)
    row8 = lax.broadcasted_iota(i32, (8, 256), 0)

    comp = []
    for kv in range(2):
        def piece(l):
            cols = slice(l * 512 + kv * 256, l * 512 + (kv + 1) * 256)
            extra = jnp.where(row8 == 0, jnp.broadcast_to(newc_ref[:, cols], (8, 256)), 0.0)
            return jnp.concatenate([cc[p][:, cols] for p in range(N_PAGES)] + [extra], axis=0)

        out = _compress_chunks(piece, n_chunk + 8, kv, wl_ref, wt_ref, pe_ref, w2_ref)
        comp.append(out[0:n_chunk].astype(bf16))
    kc, vc = comp

    qs = (qrow_ref[...] * SCALE).astype(bf16)
    dsel = dsel_ref[...]

    def diag(x):
        y = x * dsel
        return y[:, 0:64] + y[:, 64:128] + y[:, 128:192] + y[:, 192:256]

    mc = mc_ref[...]
    s = jnp.where(mc > 0.5, _dot_nt(qs, kc) + bc_ref[...], NEG)
    p = jnp.exp(s - jnp.max(s, axis=-1, keepdims=True)) * mc
    p = p / jnp.maximum(jnp.sum(p, axis=-1, keepdims=True), 1e-30)
    o_c = diag(_dot(p.astype(bf16), vc))

    smat = smat_ref[...]
    p_sum = sum(_dot(smat, piece) for piece in _split3(p))
    cmat = c_ref[...]
    imp = sum(_dot(piece, cmat) for piece in _split3(p_sum))
    n_blk = (N_PAGES * PAGE + n_new + SEL_LEN - 1) // SEL_LEN
    cur = (N_PAGES * PAGE) // SEL_LEN
    sidx = lax.broadcasted_iota(i32, imp.shape, 1)
    forced = jnp.where(sidx == 0, 1, jnp.where(sidx == cur, 1, jnp.where(sidx == cur - 1, 1, 0)))
    imp = jnp.where(forced > 0, imp + FORCE_BONUS, imp)
    imp = jnp.where(sidx <= cur, imp, NEG)
    rank = jnp.zeros(imp.shape, i32)
    for j in range(n_blk):
        col = imp[:, j:j + 1]
        rank = rank + jnp.where(col > imp, 1, jnp.where(col == imp, jnp.where(sidx > j, 1, 0), 0))
    sel = jnp.where(rank < N_SEL, jnp.where(imp > -1e29, jnp.where(sidx < n_blk, 1.0, 0.0), 0.0), 0.0)
    sel64 = _dot(smt_ref[...], sel.astype(bf16)).astype(bf16)
    selk = _dot(sel64, e_ref[...])

    def attend(k_tiles, v_tiles, bias, mask):
        s = jnp.concatenate([_dot_nt(qs, kt) for kt in k_tiles], axis=1) + bias
        s = jnp.where(mask > 0.5, s, NEG)
        p = jnp.exp(s - jnp.max(s, axis=-1, keepdims=True)) * mask
        p = (p / jnp.maximum(jnp.sum(p, axis=-1, keepdims=True), 1e-30)).astype(bf16)
        acc = jnp.zeros((64, 256), f32)
        for t, vt in enumerate(v_tiles):
            acc = acc + _dot(p[:, t * 128:(t + 1) * 128], vt)
        return diag(acc)

    def new_tile(ref, half):
        new_scr[...] = jnp.zeros(new_scr.shape, f32)
        new_scr[0:n_new, :] = ref[:, half * 256:(half + 1) * 256]
        return new_scr[...].astype(bf16)

    k_tiles = [cs[p][:, 0:256].astype(bf16) for p in range(N_PAGES)] + [new_tile(news_ref, 0)]
    v_tiles = [cs[p][:, 256:512].astype(bf16) for p in range(N_PAGES)] + [new_tile(news_ref, 1)]
    o_s = attend(k_tiles, v_tiles, bs_ref[...], ms_ref[...] * selk)

    nw = WINDOW // 128
    k_tiles = [win_ref[t * 128:(t + 1) * 128, 0:256].astype(bf16) for t in range(nw)] + [new_tile(neww_ref, 0)]
    v_tiles = [win_ref[t * 128:(t + 1) * 128, 256:512].astype(bf16) for t in range(nw)] + [new_tile(neww_ref, 1)]
    o_w = attend(k_tiles, v_tiles, bw_ref[...], mw_ref[...])

    sig = _sigmoid(ng_ref[...])
    o_ref[...] = sig[:, 0:1] * o_c + sig[:, 1:2] * o_s + sig[:, 2:3] * o_w


def _nsa_sample(page_table, qrow, ng, newc, proj3, cache_c, cache_s, win, layer, n_phys,
                wl, wt, pe, w2, consts):
    nb = qrow.shape[0]
    n_new = proj3.shape[1]
    full = lambda a: pl.BlockSpec(a.shape, lambda b, pt: (0,) * a.ndim)
    new_spec = lambda c: pl.BlockSpec((None, n_new, 512), lambda b, pt: (b, 0, c // 512))

    def page_spec(p, shape):
        return pl.BlockSpec((None,) + shape, lambda b, pt: (layer * n_phys + pt[b, p], 0, 0))

    in_specs = ([pl.BlockSpec((None, 64, 256), lambda b, pt: (b, 0, 0)),
                 pl.BlockSpec((None, 64, 3), lambda b, pt: (b, 0, 0)),
                 pl.BlockSpec((None, 1, CMP_STRIDE * 512), lambda b, pt: (b, 0, 0)),
                 new_spec(C_KVS), new_spec(C_KVW)]
                + [page_spec(p, (PAGE // CMP_STRIDE, CMP_STRIDE * 512)) for p in range(N_PAGES)]
                + [page_spec(p, (PAGE, 512)) for p in range(N_PAGES)]
                + [pl.BlockSpec((None, WINDOW, 512), lambda b, pt: (layer * nb + b, 0, 0)),
                   full(wl), full(wt), full(pe), full(w2)]
                + [full(c) for c in consts])
    grid_spec = pltpu.PrefetchScalarGridSpec(
        num_scalar_prefetch=1,
        grid=(nb,),
        in_specs=in_specs,
        out_specs=pl.BlockSpec((None, 64, 64), lambda b, pt: (b, 0, 0)),
        scratch_shapes=[pltpu.VMEM((128, 256), f32)],
    )
    return pl.pallas_call(
        _nsa_sample_kernel,
        grid_spec=grid_spec,
        out_shape=jax.ShapeDtypeStruct((nb, 64, 64), f32),
        compiler_params=_cparams(("arbitrary",)),
        name="nsa_sample",
    )(page_table, qrow, ng, newc, proj3, proj3, *([cache_c] * N_PAGES), *([cache_s] * N_PAGES), win,
      wl, wt, pe, w2, *consts)


def _static_tables(t, past, n_new):
    ar = np.arange
    tiles = np.stack([d + ar(128)[:, None] - ar(128)[None, :] for d in (0, 128, 256)])
    cmp_p = ar(t)[:, None] - (CMP_STRIDE * ar(128)[None, :] + CMP_LEN - 1)
    qpos = past + ar(n_new)
    cmp_s = qpos[:, None] - (CMP_STRIDE * ar(128)[None, :] + CMP_LEN - 1)
    kpos_s = np.concatenate([ar(past), past + ar(128)])
    slc_s = qpos[:, None] - kpos_s[None, :]
    kpos_w = np.concatenate([past - WINDOW + ar(WINDOW), past + ar(128)])
    win_s = qpos[:, None] - kpos_w[None, :]
    parts = [tiles.reshape(-1, 128), cmp_p, cmp_s.reshape(-1, 128), slc_s.reshape(-1, 128), win_s.reshape(-1, 128)]
    sizes = [p.shape[0] for p in parts]
    flat = np.concatenate(parts, axis=0)
    pad = (-flat.shape[0]) % 256
    flat = np.concatenate([flat, np.zeros((pad, 128), flat.dtype)], axis=0)
    buckets = _t5_bucket_np(flat)
    real_s = np.concatenate([np.ones(past, bool), ar(128) < n_new])
    real_w = np.concatenate([np.ones(WINDOW, bool), ar(128) < n_new])
    mask_c = (cmp_s >= 0)
    mask_s = (slc_s >= 0) & real_s[None, :]
    mask_w = (win_s >= 0) & (win_s < WINDOW) & real_w[None, :]
    rep = lambda mk: np.tile(mk[None].astype(np.float32), (N_HEADS, 1, 1)).reshape(N_HEADS * n_new, -1)
    return buckets, sizes, rep(mask_c), rep(mask_s), rep(mask_w)


def _cmp_to_sel_np(n_cmp, n_sel):
    j = np.arange(n_cmp)[:, None]
    s = np.arange(n_sel)[None, :]
    lo = np.maximum(j * CMP_STRIDE, s * SEL_LEN)
    hi = np.minimum(j * CMP_STRIDE + CMP_LEN, (s + 1) * SEL_LEN)
    return (np.maximum(hi - lo, 0) / CMP_LEN).astype(np.float32)


def _kron4(w):
    eye = jnp.eye(N_KV, dtype=w.dtype)
    out = jnp.einsum('gh,...de->...gdhe', eye, w)
    return out.reshape(w.shape[:-2] + (256, 256))


def _block_diag_rnn(w):
    w4 = w.reshape(4, 4, RNN_BLOCK, RNN_BLOCK)
    eye = jnp.eye(4, dtype=w.dtype)
    return jnp.einsum('jk,cjde->cjdke', eye, w4).reshape(4, 256, 256)


def _pack_in(w, b):
    cuts = np.cumsum([1024, 512, 512, 512, 48, 1024, 1024, 4096])[:-1]
    q, kvc, kvs, kvw, ng, rx, ry, mg = jnp.split(w, cuts, axis=-1)
    bq, bkvc, bkvs, bkvw, bng, brx, bry, bmg = jnp.split(b, cuts, axis=-1)
    zw = jnp.zeros((w.shape[0], 128 - 48), w.dtype)
    zb = jnp.zeros((128 - 48,), b.dtype)
    wp = jnp.concatenate([mg, q, rx, ry, kvc, kvs, kvw, ng, zw], axis=-1)
    bp = jnp.concatenate([bmg, bq, brx, bry, bkvc, bkvs, bkvw, bng, zb], axis=-1)
    return wp.astype(bf16), bp.reshape(1, N_PACK)


def kernel(x_prompt, x_sample, c_prompt, c_sample, cache_cmp_kv, cache_slc_kv, cache_win_kv, state_conv, state_rnn_h, page_table, rel_bias, w_ada, b_ada, w_in, b_in, cmp_pos, cmp_w1, cmp_w2, w_attn_o, conv_w, conv_b, lru_wa, lru_ba, lru_wi, lru_bi, lru_lambda, w_rnn_o, w_out, w_ffn_in, w_ffn_out, ln_g, ln_b):
    nb, t, _ = x_prompt.shape
    ns, n_new, _ = x_sample.shape
    n_phys = cache_cmp_kv.shape[1]
    past = page_table.shape[1] * PAGE
    assert page_table.shape[1] == N_PAGES and cache_win_kv.shape[2] == WINDOW and t % TK == 0
    mp, ms = nb * t, ns * n_new

    buckets, sizes, mask_c, mask_s, mask_w = _static_tables(t, past, n_new)
    bias_all = _bias_lookup(rel_bias, jnp.asarray(buckets))
    offs = np.cumsum([0] + sizes)
    seg = lambda k: bias_all[:, offs[k]:offs[k + 1]]
    bias_t = seg(0).reshape(N_KV, GROUP, 3, 128, 128).transpose(0, 2, 1, 3, 4).reshape(N_KV * 3, GROUP * 128, 128)
    bias_cp = seg(1)
    bias_cs = seg(2).reshape(N_HEADS * n_new, 128)
    bias_ss = seg(3).reshape(N_HEADS * n_new, past + 128)
    bias_ws = seg(4).reshape(N_HEADS * n_new, WINDOW + 128)

    n_sel_p = t // SEL_LEN
    ct_p = np.zeros((128, 128), np.float32)
    ct_p[:n_sel_p, :t // CMP_STRIDE - 1] = _cmp_to_sel_np(t // CMP_STRIDE - 1, n_sel_p).T
    e_p = (np.arange(128)[:, None] == (np.arange(t)[None, :] // SEL_LEN)).astype(np.float32)
    n_cmp_s = (past + n_new + CMP_STRIDE - 1) // CMP_STRIDE - 1
    n_sel_s = (past + n_new + SEL_LEN - 1) // SEL_LEN
    c_s = np.zeros((128, 128), np.float32)
    c_s[:n_cmp_s, :n_sel_s] = _cmp_to_sel_np(n_cmp_s, n_sel_s)
    e_s = (np.arange(128)[:, None] == (np.arange(past + 128)[None, :] // SEL_LEN)).astype(np.float32)
    hq = np.arange(N_HEADS * n_new)
    smat = ((hq[None, :] // (GROUP * n_new)) * n_new + hq[None, :] % n_new
            == np.arange(N_KV * n_new)[:, None]).astype(np.float32)
    dsel = (hq[:, None] // (GROUP * n_new) == np.arange(256)[None, :] // HEAD_DIM).astype(np.float32)
    consts_s = (bias_cs, jnp.asarray(mask_c), bias_ss, jnp.asarray(mask_s), bias_ws, jnp.asarray(mask_w),
                jnp.asarray(smat, bf16), jnp.asarray(smat.T, bf16), jnp.asarray(c_s, bf16),
                jnp.asarray(e_s, bf16), jnp.asarray(dsel))

    c_all = jnp.concatenate([c_prompt, c_sample, jnp.zeros((-(nb + ns) % 8, D_MODEL), f32)], axis=0)
    cache_c2 = cache_cmp_kv.reshape(DEPTH * n_phys, PAGE // CMP_STRIDE, CMP_STRIDE * 512)
    cache_s2 = cache_slc_kv.reshape(DEPTH * n_phys, PAGE, 512)
    win2 = cache_win_kv.reshape(DEPTH * ns, WINDOW, 512)

    y_p = x_prompt.reshape(mp, D_MODEL)
    y_s = x_sample.reshape(ms, D_MODEL)
    st_p, st_s = [], []
    for l in range(DEPTH):
        w_in_p, b_in_p = _pack_in(w_in[l], b_in[l])
        w1 = cmp_w1[l].reshape(2, 2, CMP_STRIDE, HEAD_DIM, HEAD_DIM)
        wl = _kron4(w1[:, 0]).astype(bf16)
        wt = _kron4(w1[:, 1]).astype(bf16)
        w2k = _kron4(cmp_w2[l]).astype(bf16)
        pe = jnp.tile(cmp_pos[l].reshape(2, 2, CMP_STRIDE, HEAD_DIM), (1, 1, 1, N_KV))
        wa_bd = _block_diag_rnn(lru_wa[l]).astype(bf16)
        wi_bd = _block_diag_rnn(lru_wi[l]).astype(bf16)
        row = lambda v: v.reshape(1, -1)
        rnn_w = (conv_w[l], row(conv_b[l]), wa_bd, row(lru_ba[l]), wi_bd, row(lru_bi[l]), row(lru_lambda[l]))
        w_ao, w_ro, w_o = w_attn_o[l].astype(bf16), w_rnn_o[l].astype(bf16), w_out[l].astype(bf16)
        w_f1, w_f2 = w_ffn_in[l].astype(bf16), w_ffn_out[l].astype(bf16)
        lg0, lb0, lg1, lb1 = row(ln_g[l, 0]), row(ln_b[l, 0]), row(ln_g[l, 1]), row(ln_b[l, 1])

        ada = _ada(c_all, w_ada[l], b_ada[l])
        mods_p = [a.reshape(nb, 1, D_MODEL) for a in jnp.split(ada[:nb], 6, axis=-1)]
        mods_s = [jnp.repeat(a, n_new, axis=0).reshape(1, ms, D_MODEL) for a in jnp.split(ada[nb:nb + ns], 6, axis=-1)]

        sh1, sc1, g1, sh2, sc2, g2 = mods_p
        proj = _inproj(y_p, sc1, sh1, w_in_p, b_in_p, tm=1024)
        rows_c = proj[:, C_KVC:C_KVC + 512].reshape(nb, t // CMP_STRIDE, CMP_STRIDE * 512)
        kc, vc = _compress_prompt(rows_c, wl, wt, pe, w2k)
        o_attn = _nsa_prompt(proj, kc, vc, bias_cp, bias_t, jnp.asarray(ct_p, bf16), jnp.asarray(e_p, bf16), nb, t)
        o_rnn, h_last = _rglru_prompt(proj, nb, t, *rnn_w)
        merged = _merge(o_attn, o_rnn, proj, w_ao, w_ro, tm=1024)
        x1 = _outproj(merged, y_p, g1, w_o, lg0, lb0, tm=256)
        y_p = _ffn(x1, sc2, sh2, g2, w_f1, w_f2, lg1, lb1, tm=512)
        kv = lambda c: proj[:, c:c + 512].reshape(nb, t, 2, N_KV, HEAD_DIM)
        st_p.append((kv(C_KVC), kv(C_KVS), kv(C_KVW)[:, t - min(WINDOW, t):],
                     proj[:, C_RX:C_RX + D_RNN].reshape(nb, t, D_RNN)[:, t - (CONV_W - 1):],
                     h_last.reshape(nb, D_RNN)))

        sh1, sc1, g1, sh2, sc2, g2 = mods_s
        proj = _inproj(y_s, sc1, sh1, w_in_p, b_in_p, tm=ms)
        proj3 = proj.reshape(ns, n_new, N_PACK)
        q5 = proj3[:, :, C_Q:C_Q + 1024].reshape(ns, n_new, N_KV, GROUP, HEAD_DIM).transpose(0, 2, 3, 1, 4)
        qrow = (q5[:, :, :, :, None, :] * jnp.eye(N_KV, dtype=f32)[None, :, None, None, :, None]
                ).reshape(ns, N_HEADS * n_new, 256)
        ng = proj3[:, :, C_NG:C_NG + 48].reshape(ns, n_new, N_HEADS, 3).transpose(0, 2, 1, 3).reshape(
            ns, N_HEADS * n_new, 3)
        newc = jnp.pad(proj3[:, :, C_KVC:C_KVC + 512], ((0, 0), (0, CMP_STRIDE - n_new), (0, 0))).reshape(
            ns, 1, CMP_STRIDE * 512)
        o64 = _nsa_sample(page_table, qrow, ng, newc, proj3, cache_c2, cache_s2, win2, l, n_phys,
                          wl, wt, pe, w2k, consts_s)
        o_attn = o64.reshape(ns, N_HEADS, n_new, HEAD_DIM).transpose(0, 2, 1, 3).reshape(ms, 1024).astype(bf16)
        tmaj = lambda c: proj3[:, :, c:c + D_RNN].transpose(1, 0, 2)
        o_rnn_t, h_last = _rglru_sample(tmaj(C_RX), tmaj(C_RY), state_conv[l].transpose(1, 0, 2),
                                        state_rnn_h[l], *rnn_w)
        o_rnn = o_rnn_t.transpose(1, 0, 2).reshape(ms, D_RNN)
        merged = _merge(o_attn, o_rnn, proj, w_ao, w_ro, tm=ms)
        x1 = _outproj(merged, y_s, g1, w_o, lg0, lb0, tm=ms)
        y_s = _ffn(x1, sc2, sh2, g2, w_f1, w_f2, lg1, lb1, tm=ms)
        kv = lambda c: proj3[:, :, c:c + 512].reshape(ns, n_new, 2, N_KV, HEAD_DIM)
        xp = jnp.concatenate([state_conv[l], proj3[:, :, C_RX:C_RX + D_RNN]], axis=1)
        st_s.append((kv(C_KVC), kv(C_KVS), kv(C_KVW), xp[:, n_new:], h_last))

    stack = lambda sts, k: jnp.stack([s[k] for s in sts])
    return (y_p.reshape(nb, t, D_MODEL), y_s.reshape(ns, n_new, D_MODEL),
            stack(st_p, 0), stack(st_p, 1), stack(st_p, 2), stack(st_p, 3), stack(st_p, 4),
            stack(st_s, 0), stack(st_s, 1), stack(st_s, 2), stack(st_s, 3), stack(st_s, 4))
```

```python
import math

import numpy as np
import jax
import jax.numpy as jnp
from jax import lax
from jax.experimental import pallas as pl
from jax.experimental.pallas import tpu as pltpu

f32 = jnp.float32
bf16 = jnp.bfloat16
i32 = jnp.int32

D_MODEL = 2048
N_HEADS = 16
N_KV = 4
GROUP = 4
HEAD_DIM = 64
CMP_STRIDE = 16
CMP_LEN = 32
SEL_LEN = 64
N_SEL = 16
WINDOW = 512
PAGE = 128
FORCE_BONUS = 1e3
N_BUCKETS = 32
MAX_DISTANCE = 128
D_RNN = 1024
RNN_BLOCK = 64
CONV_W = 4
LRU_C = 8.0
D_FF = 5632
DEPTH = 2
ALPHA = (2.0 * DEPTH) ** 0.25
SCALE = HEAD_DIM ** -0.5
NEG = -1e30
KV_W = 2 * N_KV * HEAD_DIM

C_MG = 0
C_Q = 4096
C_RX = 5120
C_RY = 6144
C_KVC = 7168
C_KVS = 7680
C_KVW = 8192
C_NG = 8704
N_PACK = 9216

VMEM_LIMIT = 56 * 1024 * 1024

TQ = 128
TK = 256
N_PAGES = 16


def _cparams(sem):
    return pltpu.CompilerParams(dimension_semantics=sem, vmem_limit_bytes=VMEM_LIMIT)


def _sigmoid(x):
    return 1.0 / (1.0 + jnp.exp(-x))


def _silu(x):
    return x * _sigmoid(x)


def _layer_norm(z, g, b):
    mu = jnp.mean(z, axis=-1, keepdims=True)
    zc = z - mu
    var = jnp.mean(zc * zc, axis=-1, keepdims=True)
    return zc * lax.rsqrt(var + 1e-5) * g + b


def _dot(a, b):
    return jnp.dot(a, b, preferred_element_type=f32)


def _dot_nt(a, b):
    return lax.dot_general(a, b, (((1,), (1,)), ((), ())), preferred_element_type=f32)


def _split3(x):
    hi = x.astype(bf16)
    r1 = x - hi.astype(f32)
    mid = r1.astype(bf16)
    lo = (r1 - mid.astype(f32)).astype(bf16)
    return hi, mid, lo


def _rows(mod, tm):
    mr = mod.shape[0]
    if mr == 1 or mr == tm:
        return mod
    return jnp.concatenate([mod] * (tm // mr), axis=0)


def _mod_spec(mod, k, tiles_per_block, ngrid):
    mr = mod.shape[1]
    if ngrid == 1:
        return pl.BlockSpec((None, mr, D_MODEL), lambda i: (i // tiles_per_block, 0, k))
    return pl.BlockSpec((None, mr, D_MODEL), lambda i, j: (i // tiles_per_block, 0, k))


def _ada_kernel(c_ref, w_ref, b_ref, o_ref):
    h = _silu(c_ref[...]).astype(bf16)
    o_ref[...] = _dot(h, w_ref[...].astype(bf16)) + b_ref[...]


def _ada(c_all, w, b, layer):
    m = c_all.shape[0]
    n = w.shape[2]
    tn = 1024
    return pl.pallas_call(
        _ada_kernel,
        grid=(n // tn,),
        in_specs=[pl.BlockSpec((m, D_MODEL), lambda j: (0, 0)),
                  pl.BlockSpec((None, D_MODEL, tn), lambda j: (layer, 0, j)),
                  pl.BlockSpec((None, 1, tn), lambda j: (layer, 0, j))],
        out_specs=pl.BlockSpec((m, tn), lambda j: (0, j)),
        out_shape=jax.ShapeDtypeStruct((m, n), f32),
        compiler_params=_cparams(("arbitrary",)),
        name="ada",
    )(c_all, w, b)


def _bias_lookup_kernel(tbl_ref, bk_ref, o_ref):
    bk = bk_ref[...]
    for h in range(N_HEADS):
        acc = jnp.zeros(bk.shape, f32)
        for k in range(N_BUCKETS):
            acc = jnp.where(bk == k, tbl_ref[k, h], acc)
        o_ref[h] = acc


def _bias_lookup(tbl, buckets):
    n = buckets.shape[0]
    tr = 256
    return pl.pallas_call(
        _bias_lookup_kernel,
        grid=(n // tr,),
        in_specs=[pl.BlockSpec(memory_space=pltpu.SMEM),
                  pl.BlockSpec((tr, 128), lambda i: (i, 0))],
        out_specs=pl.BlockSpec((N_HEADS, tr, 128), lambda i: (0, i, 0)),
        out_shape=jax.ShapeDtypeStruct((N_HEADS, n, 128), f32),
        compiler_params=_cparams(("arbitrary",)),
        name="bias_lookup",
    )(tbl, buckets)


def _t5_bucket_np(dist):
    n = np.maximum(dist, 0)
    exact = N_BUCKETS // 2
    ratio = np.maximum(n, 1).astype(np.float32) / np.float32(exact)
    log_ratio = np.log(ratio).astype(np.float32) / np.float32(math.log(MAX_DISTANCE / exact))
    large = np.minimum(exact + (log_ratio * np.float32(N_BUCKETS - exact)).astype(np.int32), N_BUCKETS - 1)
    return np.where(n < exact, n, large).astype(np.int32)


def _inproj_kernel(x_ref, sc_ref, sh_ref, w_ref, b_ref, o_ref, h_scr):
    @pl.when(pl.program_id(1) == 0)
    def _():
        tm = x_ref.shape[0]
        h_scr[...] = (x_ref[...] * (1.0 + _rows(sc_ref[...], tm)) + _rows(sh_ref[...], tm)).astype(bf16)

    o_ref[...] = _dot(h_scr[...], w_ref[...]) + b_ref[...]


def _inproj(x, mod, w, b, layer, tm):
    m = x.shape[0]
    n = w.shape[2]
    tn = 512
    tpb = (m // tm) // mod.shape[0]
    return pl.pallas_call(
        _inproj_kernel,
        grid=(m // tm, n // tn),
        in_specs=[pl.BlockSpec((tm, D_MODEL), lambda i, j: (i, 0)),
                  _mod_spec(mod, 1, tpb, 2), _mod_spec(mod, 0, tpb, 2),
                  pl.BlockSpec((None, D_MODEL, tn), lambda i, j: (layer, 0, j)),
                  pl.BlockSpec((None, 1, tn), lambda i, j: (layer, 0, j))],
        out_specs=pl.BlockSpec((tm, tn), lambda i, j: (i, j)),
        out_shape=jax.ShapeDtypeStruct((m, n), f32),
        scratch_shapes=[pltpu.VMEM((tm, D_MODEL), bf16)],
        compiler_params=_cparams(("parallel", "arbitrary")),
        name="inproj",
    )(x, mod, mod, w, b)


def _cmp_bias_kernel(pe_ref, wlt_ref, o_ref):
    for kv in range(2):
        acc = jnp.zeros((8, 512), f32)
        for l in range(CMP_STRIDE):
            w = wlt_ref[kv, l]
            lead = _dot(jnp.broadcast_to(pe_ref[kv, 0, l:l + 1, :], (8, 256)).astype(bf16), w[:, 0:256])
            tail = _dot(jnp.broadcast_to(pe_ref[kv, 1, l:l + 1, :], (8, 256)).astype(bf16), w[:, 256:512])
            acc = acc + jnp.concatenate([lead, tail], axis=1)
        o_ref[kv] = acc


def _cmp_bias(pe, wlt):
    return pl.pallas_call(
        _cmp_bias_kernel,
        out_shape=jax.ShapeDtypeStruct((2, 8, 512), f32),
        compiler_params=pltpu.CompilerParams(vmem_limit_bytes=VMEM_LIMIT),
        name="cmp_bias",
    )(pe, wlt)


def _compress_chunks(piece, n_rows, kv, wlt_ref, cb_ref, w2_ref):
    acc = jnp.zeros((n_rows, 512), f32)
    for l in range(CMP_STRIDE):
        acc = acc + _dot(piece(l), wlt_ref[kv, l])
    acc = acc + cb_ref[kv, 0:1, :]
    z = acc[:, 0:256] + pltpu.roll(acc[:, 256:512], n_rows - 1, 0)
    return _dot(_silu(z).astype(bf16), w2_ref[kv])


def _compress_prompt_kernel(x_ref, wlt_ref, cb_ref, w2_ref, kc_ref, vc_ref):
    n_chunk = x_ref.shape[0]
    for kv, dst in enumerate((kc_ref, vc_ref)):
        piece = lambda l: x_ref[:, l * KV_W + kv * 256:l * KV_W + (kv + 1) * 256].astype(bf16)
        out = _compress_chunks(piece, n_chunk, kv, wlt_ref, cb_ref, w2_ref)
        row = lax.broadcasted_iota(i32, out.shape, 0)
        dst[...] = jnp.where(row < n_chunk - 1, out, 0.0)


def _compress_prompt(rows, wlt, cb, w2, layer):
    nb, n_chunk, width = rows.shape
    lsel = lambda a: pl.BlockSpec((None,) + a.shape[1:], lambda b: (layer,) + (0,) * (a.ndim - 1))
    return pl.pallas_call(
        _compress_prompt_kernel,
        grid=(nb,),
        in_specs=[pl.BlockSpec((None, n_chunk, width), lambda b: (b, 0, 0)),
                  lsel(wlt), pl.BlockSpec(cb.shape, lambda b: (0, 0, 0)), lsel(w2)],
        out_specs=[pl.BlockSpec((None, n_chunk, 256), lambda b: (b, 0, 0))] * 2,
        out_shape=[jax.ShapeDtypeStruct((nb, n_chunk, 256), f32)] * 2,
        compiler_params=_cparams(("arbitrary",)),
        name="compress_prompt",
    )(rows, wlt, cb, w2)


def _topk_mask_rows(imp, sidx, n_keep):
    n_rows = imp.shape[0]
    rank = jnp.zeros(imp.shape, i32)
    for j in range(n_rows):
        row = imp[j:j + 1, :]
        beats = jnp.where(row > imp, 1, jnp.where(row == imp, jnp.where(sidx > j, 1, 0), 0))
        rank = rank + beats
    return jnp.where(rank < n_keep, jnp.where(imp > -1e29, 1.0, 0.0), 0.0)


def _nsa_prompt_kernel(q_ref, ng_ref, ks_ref, vs_ref, kw_ref, vw_ref, kc_ref, vc_ref,
                       bc_ref, bt_ref, ct_ref, e_ref, o_ref, selk_scr, s_scr, mx_scr, ls_scr):
    i = pl.program_id(1)
    q0 = i * TQ
    rows = GROUP * TQ
    n_sel_blk = ks_ref.shape[0] // SEL_LEN

    d0 = lax.broadcasted_iota(i32, (TQ, TK), 0) - lax.broadcasted_iota(i32, (TQ, TK), 1)
    dist_c = (q0 + (lax.broadcasted_iota(i32, (rows, 128), 0) & (TQ - 1))
              - CMP_STRIDE * lax.broadcasted_iota(i32, (rows, 128), 1) - (CMP_LEN - 1))
    mask_c = dist_c >= 0
    sig = _sigmoid(ng_ref[...])

    sidx = lax.broadcasted_iota(i32, (n_sel_blk, TQ), 0)
    cur = (q0 + lax.broadcasted_iota(i32, (n_sel_blk, TQ), 1)) // SEL_LEN
    forced = jnp.where(sidx == 0, 1, jnp.where(sidx == cur, 1, jnp.where(sidx == cur - 1, 1, 0)))

    def branch(qs, g, k_ref, v_ref, kt_lo, kt_hi, window):
        def scores(kt, carry):
            k0 = pl.multiple_of(kt * TK, TK)
            k = k_ref[pl.ds(k0, TK), g * 64:(g + 1) * 64].astype(bf16)
            delta = q0 - k0
            dist = d0 + delta
            if window:
                addm = jnp.where(dist >= 0, jnp.where(dist < WINDOW, 0.0, NEG), NEG)
            else:
                addm = jnp.where(dist >= 0, selk_scr[:, pl.ds(k0, TK)], NEG)
            va = jnp.clip(delta, 0, 256) // 128
            vb = jnp.clip(delta - 128, 0, 256) // 128
            bias = jnp.concatenate([bt_ref[g * 3 + va], bt_ref[g * 3 + vb]], axis=1)
            s = _dot_nt(qs, k) + bias + jnp.concatenate([addm] * GROUP, axis=0)
            s_scr[:, pl.ds(k0, TK)] = s
            mx_scr[...] = jnp.maximum(mx_scr[...], jnp.maximum(s[:, 0:128], s[:, 128:256]))
            return carry

        mx_scr[...] = jnp.full(mx_scr.shape, NEG, f32)
        lax.fori_loop(kt_lo, kt_hi + 1, scores, 0)
        m = jnp.max(mx_scr[...], axis=-1, keepdims=True)
        mx_scr[...] = jnp.broadcast_to(m, mx_scr.shape)
        ls_scr[...] = jnp.zeros(ls_scr.shape, f32)

        def values(kt, acc):
            k0 = pl.multiple_of(kt * TK, TK)
            v = v_ref[pl.ds(k0, TK), g * 64:(g + 1) * 64].astype(bf16)
            mb = mx_scr[...]
            p = jnp.exp(s_scr[:, pl.ds(k0, TK)] - jnp.concatenate([mb, mb], axis=1))
            ls_scr[...] += p[:, 0:128] + p[:, 128:256]
            return acc + _dot(p.astype(bf16), v)

        acc = lax.fori_loop(kt_lo, kt_hi + 1, values, jnp.zeros((rows, HEAD_DIM), f32))
        return acc / jnp.maximum(jnp.sum(ls_scr[...], axis=-1, keepdims=True), 1e-30)

    for g in range(N_KV):
        qg = q_ref[:, g * 256:(g + 1) * 256]
        qs = (jnp.concatenate([qg[:, r * 64:(r + 1) * 64] for r in range(GROUP)], axis=0) * SCALE).astype(bf16)

        kcg = kc_ref[:, g * 64:(g + 1) * 64].astype(bf16)
        vcg = vc_ref[:, g * 64:(g + 1) * 64].astype(bf16)
        bias_c = jnp.concatenate([bc_ref[g * GROUP + r] for r in range(GROUP)], axis=0)
        s = jnp.where(mask_c, _dot_nt(qs, kcg) + bias_c, NEG)
        p = jnp.where(mask_c, jnp.exp(s - jnp.max(s, axis=-1, keepdims=True)), 0.0)
        p = p / jnp.maximum(jnp.sum(p, axis=-1, keepdims=True), 1e-30)
        o_c = _dot(p.astype(bf16), vcg)

        p_sum = p[0:TQ] + p[TQ:2 * TQ] + p[2 * TQ:3 * TQ] + p[3 * TQ:4 * TQ]
        ct = ct_ref[...]
        imp_t = sum(_dot_nt(ct, piece) for piece in _split3(p_sum))[0:n_sel_blk]
        imp_t = jnp.where(forced > 0, imp_t + FORCE_BONUS, imp_t)
        imp_t = jnp.where(sidx <= cur, imp_t, NEG)
        sel_t = _topk_mask_rows(imp_t, sidx, N_SEL)
        sel_pad = jnp.concatenate([sel_t, jnp.zeros((128 - n_sel_blk, TQ), f32)], axis=0)
        sel_q = sel_pad.T.astype(bf16)
        selk_scr[...] = jnp.where(_dot(sel_q, e_ref[...]) > 0.5, 0.0, NEG)

        kt_hi = (q0 + TQ - 1) // TK
        o_s = branch(qs, g, ks_ref, vs_ref, 0, kt_hi, False)
        o_w = branch(qs, g, kw_ref, vw_ref, jnp.maximum(q0 - (WINDOW - 1), 0) // TK, kt_hi, True)

        gate = lambda c: jnp.concatenate(
            [sig[:, g * 12 + r * 3 + c:g * 12 + r * 3 + c + 1] for r in range(GROUP)], axis=0)
        o = gate(0) * o_c + gate(1) * o_s + gate(2) * o_w
        o_ref[:, g * 256:(g + 1) * 256] = jnp.concatenate(
            [o[r * TQ:(r + 1) * TQ] for r in range(GROUP)], axis=1).astype(bf16)


def _nsa_prompt(proj, kc, vc, bias_c, bias_t, ct, e, nb, t):
    nq = t // TQ
    full = lambda a: pl.BlockSpec(a.shape, lambda b, i: (0,) * a.ndim)
    kv = lambda c: pl.BlockSpec((t, 256), lambda b, i: (b, c // 256))
    return pl.pallas_call(
        _nsa_prompt_kernel,
        grid=(nb, nq),
        in_specs=[pl.BlockSpec((TQ, 1024), lambda b, i: (b * nq + i, C_Q // 1024)),
                  pl.BlockSpec((TQ, 128), lambda b, i: (b * nq + i, C_NG // 128)),
                  kv(C_KVS), kv(C_KVS + 256), kv(C_KVW), kv(C_KVW + 256),
                  pl.BlockSpec((None, t // CMP_STRIDE, 256), lambda b, i: (b, 0, 0)),
                  pl.BlockSpec((None, t // CMP_STRIDE, 256), lambda b, i: (b, 0, 0)),
                  pl.BlockSpec((N_HEADS, TQ, 128), lambda b, i: (0, i, 0)),
                  full(bias_t), full(ct), full(e)],
        out_specs=pl.BlockSpec((TQ, 1024), lambda b, i: (b * nq + i, 0)),
        out_shape=jax.ShapeDtypeStruct((nb * t, 1024), bf16),
        scratch_shapes=[pltpu.VMEM((TQ, t), f32), pltpu.VMEM((GROUP * TQ, t), f32),
                        pltpu.VMEM((GROUP * TQ, 128), f32), pltpu.VMEM((GROUP * TQ, 128), f32)],
        compiler_params=_cparams(("parallel", "arbitrary")),
        name="nsa_prompt",
    )(proj, proj, proj, proj, proj, proj, kc, vc, bias_c, bias_t, ct, e)


def _lru_gates(xc, ry, wa_ref, ba_ref, wi_ref, bi_ref, lam_ref):
    xb = xc.astype(bf16)
    ra = jnp.concatenate([_dot(xb[:, c * 256:(c + 1) * 256], wa_ref[c]) for c in range(4)], axis=1)
    ri = jnp.concatenate([_dot(xb[:, c * 256:(c + 1) * 256], wi_ref[c]) for c in range(4)], axis=1)
    r = _sigmoid(ra + ba_ref[...])
    ig = _sigmoid(ri + bi_ref[...])
    nl = -lam_ref[...]
    softplus = jnp.maximum(nl, 0.0) + jnp.log1p(jnp.exp(-jnp.abs(nl)))
    log_a = -LRU_C * r * softplus
    a = jnp.exp(log_a)
    u = jnp.sqrt(jnp.tanh(-log_a) * (a * a + 1.0)) * (ig * xc)
    return a, u, jax.nn.gelu(ry)


def _rglru_prompt_kernel(rx_ref, ry_ref, cw_ref, cb_ref, wa_ref, ba_ref, wi_ref, bi_ref, lam_ref,
                         o_ref, hl_ref, xp_scr, a_scr, u_scr, hs_scr, h_scr):
    tt = pl.program_id(1)
    tr = rx_ref.shape[0]

    @pl.when(tt == 0)
    def _():
        xp_scr[0:8, :] = jnp.zeros((8, D_RNN), f32)
        h_scr[...] = jnp.zeros((1, D_RNN), f32)

    x = rx_ref[...]
    xp_scr[8:8 + tr, :] = x
    y = x * cw_ref[CONV_W - 1:CONV_W, :] + cb_ref[...]
    for k in range(CONV_W - 1):
        y = y + xp_scr[5 + k:5 + k + tr, :] * cw_ref[k:k + 1, :]
    xp_scr[0:8, :] = xp_scr[tr:tr + 8, :]

    a, u, gate = _lru_gates(y, ry_ref[...], wa_ref, ba_ref, wi_ref, bi_ref, lam_ref)
    a_scr[...] = a
    u_scr[...] = u

    def step(t, h):
        h = a_scr[pl.ds(t, 1), :] * h + u_scr[pl.ds(t, 1), :]
        hs_scr[pl.ds(t, 1), :] = h
        return h

    h = lax.fori_loop(0, tr, step, h_scr[...], unroll=8)
    h_scr[...] = h
    hl_ref[...] = h
    o_ref[...] = (hs_scr[...] * gate).astype(bf16)


def _rnn_specs(rnn_w, layer, ngrid):
    zeros = lambda n: (0,) * n
    if ngrid == 2:
        return [pl.BlockSpec((None,) + a.shape[1:], lambda b, i, n=a.ndim - 1: (layer,) + zeros(n)) for a in rnn_w]
    return [pl.BlockSpec((None,) + a.shape[1:], lambda i, n=a.ndim - 1: (layer,) + zeros(n)) for a in rnn_w]


def _rglru_prompt(proj, nb, t, rnn_w, layer):
    tr = 256
    nt = t // tr
    return pl.pallas_call(
        _rglru_prompt_kernel,
        grid=(nb, nt),
        in_specs=[pl.BlockSpec((tr, D_RNN), lambda b, i: (b * nt + i, C_RX // D_RNN)),
                  pl.BlockSpec((tr, D_RNN), lambda b, i: (b * nt + i, C_RY // D_RNN))]
        + _rnn_specs(rnn_w, layer, 2),
        out_specs=[pl.BlockSpec((tr, D_RNN), lambda b, i: (b * nt + i, 0)),
                   pl.BlockSpec((None, 1, D_RNN), lambda b, i: (b, 0, 0))],
        out_shape=[jax.ShapeDtypeStruct((nb * t, D_RNN), bf16),
                   jax.ShapeDtypeStruct((nb, 1, D_RNN), f32)],
        scratch_shapes=[pltpu.VMEM((tr + 8, D_RNN), f32), pltpu.VMEM((tr, D_RNN), f32),
                        pltpu.VMEM((tr, D_RNN), f32), pltpu.VMEM((tr, D_RNN), f32),
                        pltpu.VMEM((1, D_RNN), f32)],
        compiler_params=_cparams(("parallel", "arbitrary")),
        name="rglru_prompt",
    )(proj, proj, *rnn_w)


def _rglru_sample_kernel(rx_ref, ry_ref, cbuf_ref, h0_ref, cw_ref, cb_ref, wa_ref, ba_ref, wi_ref, bi_ref,
                         lam_ref, o_ref, hl_ref):
    ns = h0_ref.shape[0]
    n_t = rx_ref.shape[0] // ns
    xp = [cbuf_ref[k] for k in range(CONV_W - 1)] + [rx_ref[s * ns:(s + 1) * ns, :] for s in range(n_t)]
    h = h0_ref[...]
    for s in range(n_t):
        y = xp[s + CONV_W - 1] * cw_ref[CONV_W - 1:CONV_W, :] + cb_ref[...]
        for k in range(CONV_W - 1):
            y = y + xp[s + k] * cw_ref[k:k + 1, :]
        a, u, gate = _lru_gates(y, ry_ref[s * ns:(s + 1) * ns, :], wa_ref, ba_ref, wi_ref, bi_ref, lam_ref)
        h = a * h + u
        o_ref[s * ns:(s + 1) * ns, :] = (h * gate).astype(bf16)
    hl_ref[...] = h


def _rglru_sample(proj, cbuf, h0, rnn_w, layer):
    ms = proj.shape[0]
    ns = h0.shape[1]
    return pl.pallas_call(
        _rglru_sample_kernel,
        grid=(1,),
        in_specs=[pl.BlockSpec((ms, D_RNN), lambda i: (0, C_RX // D_RNN)),
                  pl.BlockSpec((ms, D_RNN), lambda i: (0, C_RY // D_RNN)),
                  pl.BlockSpec((None, CONV_W - 1, ns, D_RNN), lambda i: (layer, 0, 0, 0)),
                  pl.BlockSpec((None, ns, D_RNN), lambda i: (layer, 0, 0))]
        + _rnn_specs(rnn_w, layer, 1),
        out_specs=[pl.BlockSpec((ms, D_RNN), lambda i: (0, 0)),
                   pl.BlockSpec((ns, D_RNN), lambda i: (0, 0))],
        out_shape=[jax.ShapeDtypeStruct((ms, D_RNN), bf16),
                   jax.ShapeDtypeStruct((ns, D_RNN), f32)],
        compiler_params=_cparams(("arbitrary",)),
        name="rglru_sample",
    )(proj, proj, cbuf, h0, *rnn_w)


def _merge_kernel(oa_ref, orn_ref, ga_ref, gr_ref, wa_ref, wr_ref, o_ref):
    a1 = _dot(oa_ref[...], wa_ref[...])
    a2 = _dot(orn_ref[...], wr_ref[...])
    o_ref[...] = (_sigmoid(ga_ref[...]) * a1 + _sigmoid(gr_ref[...]) * a2).astype(bf16)


def _merge(o_attn, o_rnn, proj, w_ao, w_ro, layer, tm):
    m = o_attn.shape[0]
    tn = 512
    nn = D_MODEL // tn
    return pl.pallas_call(
        _merge_kernel,
        grid=(m // tm, nn),
        in_specs=[pl.BlockSpec((tm, 1024), lambda i, j: (i, 0)),
                  pl.BlockSpec((tm, 1024), lambda i, j: (i, 0)),
                  pl.BlockSpec((tm, tn), lambda i, j: (i, j)),
                  pl.BlockSpec((tm, tn), lambda i, j: (i, nn + j)),
                  pl.BlockSpec((None, 1024, tn), lambda i, j: (layer, 0, j)),
                  pl.BlockSpec((None, 1024, tn), lambda i, j: (layer, 0, j))],
        out_specs=pl.BlockSpec((tm, tn), lambda i, j: (i, j)),
        out_shape=jax.ShapeDtypeStruct((m, D_MODEL), bf16),
        compiler_params=_cparams(("parallel", "arbitrary")),
        name="merge",
    )(o_attn, o_rnn, proj, proj, w_ao, w_ro)


def _outproj_kernel(a_ref, x_ref, gt_ref, w_ref, lg_ref, lb_ref, o_ref):
    mix = _dot(a_ref[...], w_ref[...])
    z = ALPHA * x_ref[...] + (1.0 + _rows(gt_ref[...], x_ref.shape[0])) * mix
    o_ref[...] = _layer_norm(z, lg_ref[...], lb_ref[...])


def _outproj(a, x, mod, w, ln_g, ln_b, layer, tm):
    m = a.shape[0]
    tpb = (m // tm) // mod.shape[0]
    ln_spec = pl.BlockSpec((None, None, 1, D_MODEL), lambda i: (layer, 0, 0, 0))
    return pl.pallas_call(
        _outproj_kernel,
        grid=(m // tm,),
        in_specs=[pl.BlockSpec((tm, D_MODEL), lambda i: (i, 0)),
                  pl.BlockSpec((tm, D_MODEL), lambda i: (i, 0)),
                  _mod_spec(mod, 2, tpb, 1),
                  pl.BlockSpec((None, D_MODEL, D_MODEL), lambda i: (layer, 0, 0)),
                  ln_spec, ln_spec],
        out_specs=pl.BlockSpec((tm, D_MODEL), lambda i: (i, 0)),
        out_shape=jax.ShapeDtypeStruct((m, D_MODEL), f32),
        compiler_params=_cparams(("parallel",)),
        name="outproj_ln",
    )(a, x, mod, w, ln_g, ln_b)


def _ffn_kernel(x_ref, sc_ref, sh_ref, gt_ref, wg_ref, wu_ref, wo_ref, lg_ref, lb_ref, o_ref, h_scr, acc_scr):
    j = pl.program_id(1)
    tm = x_ref.shape[0]

    @pl.when(j == 0)
    def _():
        h_scr[...] = (x_ref[...] * (1.0 + _rows(sc_ref[...], tm)) + _rows(sh_ref[...], tm)).astype(bf16)
        acc_scr[...] = jnp.zeros(acc_scr.shape, f32)

    h = h_scr[...]
    act = (_silu(_dot(h, wg_ref[...])) * _dot(h, wu_ref[...])).astype(bf16)
    acc_scr[...] += _dot(act, wo_ref[...])

    @pl.when(j == pl.num_programs(1) - 1)
    def _():
        z = ALPHA * x_ref[...] + (1.0 + _rows(gt_ref[...], tm)) * acc_scr[...]
        o_ref[...] = _layer_norm(z, lg_ref[...], lb_ref[...])


def _ffn(x, mod, w_in, w_out, ln_g, ln_b, layer, tm):
    m = x.shape[0]
    tf = 512
    nf = D_FF // tf
    tpb = (m // tm) // mod.shape[0]
    ln_spec = pl.BlockSpec((None, None, 1, D_MODEL), lambda i, j: (layer, 1, 0, 0))
    return pl.pallas_call(
        _ffn_kernel,
        grid=(m // tm, nf),
        in_specs=[pl.BlockSpec((tm, D_MODEL), lambda i, j: (i, 0)),
                  _mod_spec(mod, 4, tpb, 2), _mod_spec(mod, 3, tpb, 2), _mod_spec(mod, 5, tpb, 2),
                  pl.BlockSpec((None, D_MODEL, tf), lambda i, j: (layer, 0, j)),
                  pl.BlockSpec((None, D_MODEL, tf), lambda i, j: (layer, 0, nf + j)),
                  pl.BlockSpec((None, tf, D_MODEL), lambda i, j: (layer, j, 0)),
                  ln_spec, ln_spec],
        out_specs=pl.BlockSpec((tm, D_MODEL), lambda i, j: (i, 0)),
        out_shape=jax.ShapeDtypeStruct((m, D_MODEL), f32),
        scratch_shapes=[pltpu.VMEM((tm, D_MODEL), bf16), pltpu.VMEM((tm, D_MODEL), f32)],
        compiler_params=_cparams(("parallel", "arbitrary")),
        name="ffn",
    )(x, mod, mod, mod, w_in, w_in, w_out, ln_g, ln_b)


def _nsa_sample_kernel(pt_ref, qrow_ref, ng_ref, newc_ref, news_ref, neww_ref, *rest):
    cc = rest[0:N_PAGES]
    cs = rest[N_PAGES:2 * N_PAGES]
    (win_ref, wlt_ref, cb_ref, w2_ref, selp_ref, bc_ref, mc_ref, bs_ref, ms_ref, bw_ref, mw_ref,
     smat_ref, smt_ref, c_ref, e_ref, dsel_ref, o_ref) = rest[2 * N_PAGES:]
    n_new = newc_ref.shape[0]
    n_chunk = N_PAGES * (PAGE // CMP_STRIDE)
    row8 = lax.broadcasted_iota(i32, (8, 256), 0)
    selp = selp_ref[...]

    comp = []
    for kv in range(2):
        perm = [_dot_nt(selp, cc[p][kv * 256:(kv + 1) * 256, :].astype(bf16)) for p in range(N_PAGES)]

        def piece(l):
            if l < n_new:
                extra = jnp.where(row8 == 0, jnp.broadcast_to(newc_ref[l:l + 1, kv * 256:(kv + 1) * 256], (8, 256)), 0.0)
            else:
                extra = jnp.zeros((8, 256), f32)
            return jnp.concatenate([y[l * 8:(l + 1) * 8, :] for y in perm] + [extra], axis=0).astype(bf16)

        out = _compress_chunks(piece, n_chunk + 8, kv, wlt_ref, cb_ref, w2_ref)
        comp.append(out[0:n_chunk].astype(bf16))
    kc, vc = comp

    qs = (qrow_ref[...] * SCALE).astype(bf16)
    dsel = dsel_ref[...]

    def diag(x):
        y = x * dsel
        return y[:, 0:64] + y[:, 64:128] + y[:, 128:192] + y[:, 192:256]

    mc = mc_ref[...]
    s = jnp.where(mc > 0.5, _dot_nt(qs, kc) + bc_ref[...], NEG)
    p = jnp.exp(s - jnp.max(s, axis=-1, keepdims=True)) * mc
    p = p / jnp.maximum(jnp.sum(p, axis=-1, keepdims=True), 1e-30)
    o_c = diag(_dot(p.astype(bf16), vc))

    smat = smat_ref[...]
    p_sum = sum(_dot(smat, piece_) for piece_ in _split3(p))
    cmat = c_ref[...]
    imp = sum(_dot(piece_, cmat) for piece_ in _split3(p_sum))
    n_blk = (N_PAGES * PAGE + n_new + SEL_LEN - 1) // SEL_LEN
    cur = (N_PAGES * PAGE) // SEL_LEN
    sidx = lax.broadcasted_iota(i32, imp.shape, 1)
    forced = jnp.where(sidx == 0, 1, jnp.where(sidx == cur, 1, jnp.where(sidx == cur - 1, 1, 0)))
    imp = jnp.where(forced > 0, imp + FORCE_BONUS, imp)
    imp = jnp.where(sidx <= cur, imp, NEG)
    rank = jnp.zeros(imp.shape, i32)
    for j in range(n_blk):
        col = imp[:, j:j + 1]
        rank = rank + jnp.where(col > imp, 1, jnp.where(col == imp, jnp.where(sidx > j, 1, 0), 0))
    sel = jnp.where(rank < N_SEL, jnp.where(imp > -1e29, jnp.where(sidx < n_blk, 1.0, 0.0), 0.0), 0.0)
    sel64 = _dot(smt_ref[...], sel.astype(bf16)).astype(bf16)
    selk = _dot(sel64, e_ref[...])

    def attend(kt_tiles, vt_tiles, bias, mask):
        s = jnp.concatenate([_dot(qs, kt) for kt in kt_tiles], axis=1) + bias
        s = jnp.where(mask > 0.5, s, NEG)
        p = jnp.exp(s - jnp.max(s, axis=-1, keepdims=True)) * mask
        p = (p / jnp.maximum(jnp.sum(p, axis=-1, keepdims=True), 1e-30)).astype(bf16)
        acc = jnp.zeros((64, 256), f32)
        off = 0
        for vt in vt_tiles:
            w = vt.shape[1]
            acc = acc + _dot_nt(p[:, off:off + w], vt)
            off += w
        return diag(acc)

    k_tiles = [cs[p][0:256, :].astype(bf16) for p in range(N_PAGES)] + [news_ref[0:256, :].astype(bf16)]
    v_tiles = [cs[p][256:512, :].astype(bf16) for p in range(N_PAGES)] + [news_ref[256:512, :].astype(bf16)]
    o_s = attend(k_tiles, v_tiles, bs_ref[...], ms_ref[...] * selk)

    k_tiles = [win_ref[0:256, :].astype(bf16), neww_ref[0:256, :].astype(bf16)]
    v_tiles = [win_ref[256:512, :].astype(bf16), neww_ref[256:512, :].astype(bf16)]
    o_w = attend(k_tiles, v_tiles, bw_ref[...], mw_ref[...])

    sig = _sigmoid(ng_ref[...])
    o_ref[...] = sig[:, 0:1] * o_c + sig[:, 1:2] * o_s + sig[:, 2:3] * o_w


def _nsa_sample(page_table, qrow, ng, newc, news_t, neww_t, cache_c, cache_s, win, layer, n_phys,
                wlt, cb, w2, consts):
    nb = qrow.shape[0]
    n_new = newc.shape[1]
    full = lambda a: pl.BlockSpec(a.shape, lambda b, pt: (0,) * a.ndim)
    lsel = lambda a: pl.BlockSpec((None,) + a.shape[1:], lambda b, pt: (layer,) + (0,) * (a.ndim - 1))
    per_b = lambda shape: pl.BlockSpec((None,) + shape, lambda b, pt: (b, 0, 0))

    def page_spec(p):
        return pl.BlockSpec((None, KV_W, PAGE), lambda b, pt: (layer * n_phys + pt[b, p], 0, 0))

    in_specs = ([per_b((64, 256)), per_b((64, 3)), per_b((n_new, KV_W)), per_b((KV_W, 128)), per_b((KV_W, 128))]
                + [page_spec(p) for p in range(N_PAGES)] * 2
                + [pl.BlockSpec((None, KV_W, WINDOW), lambda b, pt: (layer * nb + b, 0, 0)),
                   lsel(wlt), full(cb), lsel(w2)]
                + [full(c) for c in consts])
    grid_spec = pltpu.PrefetchScalarGridSpec(
        num_scalar_prefetch=1,
        grid=(nb,),
        in_specs=in_specs,
        out_specs=pl.BlockSpec((None, 64, 64), lambda b, pt: (b, 0, 0)),
    )
    return pl.pallas_call(
        _nsa_sample_kernel,
        grid_spec=grid_spec,
        out_shape=jax.ShapeDtypeStruct((nb, 64, 64), f32),
        compiler_params=_cparams(("arbitrary",)),
        name="nsa_sample",
    )(page_table, qrow, ng, newc, news_t, neww_t, *([cache_c] * N_PAGES), *([cache_s] * N_PAGES), win,
      wlt, cb, w2, *consts)


def _static_tables(t, past, n_new):
    ar = np.arange
    tiles = np.stack([d + ar(128)[:, None] - ar(128)[None, :] for d in (0, 128, 256)])
    cmp_p = ar(t)[:, None] - (CMP_STRIDE * ar(128)[None, :] + CMP_LEN - 1)
    qpos = past + ar(n_new)
    cmp_s = qpos[:, None] - (CMP_STRIDE * ar(128)[None, :] + CMP_LEN - 1)
    kpos_s = np.concatenate([ar(past), past + ar(128)])
    slc_s = qpos[:, None] - kpos_s[None, :]
    kpos_w = np.concatenate([past - WINDOW + ar(WINDOW), past + ar(128)])
    win_s = qpos[:, None] - kpos_w[None, :]
    parts = [tiles.reshape(-1, 128), cmp_p, cmp_s.reshape(-1, 128), slc_s.reshape(-1, 128), win_s.reshape(-1, 128)]
    sizes = [p.shape[0] for p in parts]
    flat = np.concatenate(parts, axis=0)
    pad = (-flat.shape[0]) % 256
    flat = np.concatenate([flat, np.zeros((pad, 128), flat.dtype)], axis=0)
    buckets = _t5_bucket_np(flat)
    real_s = np.concatenate([np.ones(past, bool), ar(128) < n_new])
    real_w = np.concatenate([np.ones(WINDOW, bool), ar(128) < n_new])
    mask_c = (cmp_s >= 0)
    mask_s = (slc_s >= 0) & real_s[None, :]
    mask_w = (win_s >= 0) & (win_s < WINDOW) & real_w[None, :]
    rep = lambda mk: np.tile(mk[None].astype(np.float32), (N_HEADS, 1, 1)).reshape(N_HEADS * n_new, -1)
    return buckets, sizes, rep(mask_c), rep(mask_s), rep(mask_w)


def _cmp_to_sel_np(n_cmp, n_sel):
    j = np.arange(n_cmp)[:, None]
    s = np.arange(n_sel)[None, :]
    lo = np.maximum(j * CMP_STRIDE, s * SEL_LEN)
    hi = np.minimum(j * CMP_STRIDE + CMP_LEN, (s + 1) * SEL_LEN)
    return (np.maximum(hi - lo, 0) / CMP_LEN).astype(np.float32)


def _kron4(w):
    eye = jnp.eye(N_KV, dtype=w.dtype)
    out = jnp.einsum('gh,...de->...gdhe', eye, w)
    return out.reshape(w.shape[:-2] + (256, 256))


def _block_diag_rnn(w):
    w4 = w.reshape(DEPTH, 4, 4, RNN_BLOCK, RNN_BLOCK)
    eye = jnp.eye(4, dtype=w.dtype)
    return jnp.einsum('jk,zcjde->zcjdke', eye, w4).reshape(DEPTH, 4, 256, 256)


def _pack_in(w, b):
    cuts = np.cumsum([1024, 512, 512, 512, 48, 1024, 1024, 4096])[:-1]
    q, kvc, kvs, kvw, ng, rx, ry, mg = jnp.split(w, cuts, axis=-1)
    bq, bkvc, bkvs, bkvw, bng, brx, bry, bmg = jnp.split(b, cuts, axis=-1)
    n_pad = N_PACK - (C_NG + 48)
    zw = jnp.zeros(w.shape[:-1] + (n_pad,), w.dtype)
    zb = jnp.zeros(b.shape[:-1] + (n_pad,), b.dtype)
    wp = jnp.concatenate([mg, q, rx, ry, kvc, kvs, kvw, ng, zw], axis=-1)
    bp = jnp.concatenate([bmg, bq, brx, bry, bkvc, bkvs, bkvw, bng, zb], axis=-1)
    return wp.astype(bf16), bp.reshape(DEPTH, 1, N_PACK)


def _feature_major(cache):
    d, n, rows = cache.shape[:3]
    return cache.transpose(0, 1, 3, 4, 5, 2).reshape(d * n, KV_W, rows)


def kernel(x_prompt, x_sample, c_prompt, c_sample, cache_cmp_kv, cache_slc_kv, cache_win_kv, state_conv, state_rnn_h, page_table, rel_bias, w_ada, b_ada, w_in, b_in, cmp_pos, cmp_w1, cmp_w2, w_attn_o, conv_w, conv_b, lru_wa, lru_ba, lru_wi, lru_bi, lru_lambda, w_rnn_o, w_out, w_ffn_in, w_ffn_out, ln_g, ln_b):
    nb, t, _ = x_prompt.shape
    ns, n_new, _ = x_sample.shape
    n_phys = cache_cmp_kv.shape[1]
    past = page_table.shape[1] * PAGE
    assert page_table.shape[1] == N_PAGES and cache_win_kv.shape[2] == WINDOW and t % TK == 0
    assert n_new <= CMP_STRIDE and ns % 8 == 0
    mp, ms = nb * t, ns * n_new

    buckets, sizes, mask_c, mask_s, mask_w = _static_tables(t, past, n_new)
    bias_all = _bias_lookup(rel_bias, jnp.asarray(buckets))
    offs = np.cumsum([0] + sizes)
    seg = lambda k: bias_all[:, offs[k]:offs[k + 1]]
    bias_t = seg(0).reshape(N_KV, GROUP, 3, 128, 128).transpose(0, 2, 1, 3, 4).reshape(N_KV * 3, GROUP * 128, 128)
    bias_cp = seg(1)
    bias_cs = seg(2).reshape(N_HEADS * n_new, 128)
    bias_ss = seg(3).reshape(N_HEADS * n_new, past + 128)
    bias_ws = seg(4).reshape(N_HEADS * n_new, WINDOW + 128)

    n_sel_p = t // SEL_LEN
    ct_p = np.zeros((128, 128), np.float32)
    ct_p[:n_sel_p, :t // CMP_STRIDE - 1] = _cmp_to_sel_np(t // CMP_STRIDE - 1, n_sel_p).T
    e_p = (np.arange(128)[:, None] == (np.arange(t)[None, :] // SEL_LEN)).astype(np.float32)
    n_cmp_s = (past + n_new + CMP_STRIDE - 1) // CMP_STRIDE - 1
    n_sel_s = (past + n_new + SEL_LEN - 1) // SEL_LEN
    c_s = np.zeros((128, 128), np.float32)
    c_s[:n_cmp_s, :n_sel_s] = _cmp_to_sel_np(n_cmp_s, n_sel_s)
    e_s = (np.arange(128)[:, None] == (np.arange(past + 128)[None, :] // SEL_LEN)).astype(np.float32)
    hq = np.arange(N_HEADS * n_new)
    smat = ((hq[None, :] // (GROUP * n_new)) * n_new + hq[None, :] % n_new
            == np.arange(N_KV * n_new)[:, None]).astype(np.float32)
    dsel = (hq[:, None] // (GROUP * n_new) == np.arange(256)[None, :] // HEAD_DIM).astype(np.float32)
    pos = np.arange(PAGE)
    selp = ((pos % CMP_STRIDE) * (PAGE // CMP_STRIDE) + pos // CMP_STRIDE)[None, :] == np.arange(PAGE)[:, None]
    consts_s = (jnp.asarray(selp, bf16), bias_cs, jnp.asarray(mask_c), bias_ss, jnp.asarray(mask_s), bias_ws,
                jnp.asarray(mask_w), jnp.asarray(smat, bf16), jnp.asarray(smat.T, bf16), jnp.asarray(c_s, bf16),
                jnp.asarray(e_s, bf16), jnp.asarray(dsel))

    w_in_p, b_in_p = _pack_in(w_in, b_in)
    w1 = cmp_w1.reshape(DEPTH, 2, 2, CMP_STRIDE, HEAD_DIM, HEAD_DIM)
    wlt = jnp.concatenate([_kron4(w1[:, :, 0]), _kron4(w1[:, :, 1])], axis=-1).astype(bf16)
    w2k = _kron4(cmp_w2).astype(bf16)
    pe = jnp.tile(cmp_pos.reshape(DEPTH, 2, 2, CMP_STRIDE, HEAD_DIM), (1, 1, 1, 1, N_KV))
    rnn_w = (conv_w, conv_b.reshape(DEPTH, 1, D_RNN), _block_diag_rnn(lru_wa).astype(bf16),
             lru_ba.reshape(DEPTH, 1, D_RNN), _block_diag_rnn(lru_wi).astype(bf16),
             lru_bi.reshape(DEPTH, 1, D_RNN), lru_lambda.reshape(DEPTH, 1, D_RNN))
    w_ao, w_ro, w_o = w_attn_o.astype(bf16), w_rnn_o.astype(bf16), w_out.astype(bf16)
    w_f1, w_f2 = w_ffn_in.astype(bf16), w_ffn_out.astype(bf16)
    ln_g4, ln_b4 = ln_g.reshape(DEPTH, 2, 1, D_MODEL), ln_b.reshape(DEPTH, 2, 1, D_MODEL)
    b_ada3 = b_ada.reshape(DEPTH, 1, 6 * D_MODEL)

    n_cpad = -(nb + ns) % 8
    c_all = jnp.concatenate([c_prompt, c_sample, jnp.zeros((n_cpad, D_MODEL), f32)], axis=0)
    cache_ct, cache_st, win_t = _feature_major(cache_cmp_kv), _feature_major(cache_slc_kv), _feature_major(cache_win_kv)
    cbuf_t = state_conv.transpose(0, 2, 1, 3)

    y_p = x_prompt.reshape(mp, D_MODEL)
    y_s = x_sample.transpose(1, 0, 2).reshape(ms, D_MODEL)
    st_p, st_s = [], []
    for l in range(DEPTH):
        ada = _ada(c_all, w_ada, b_ada3, l)
        mod_p = ada[:nb].reshape(nb, 1, 6 * D_MODEL)
        mod_s = ada[nb:nb + ns].reshape(1, ns, 6 * D_MODEL)
        cb = _cmp_bias(pe[l], wlt[l])

        proj = _inproj(y_p, mod_p, w_in_p, b_in_p, l, tm=1024)
        rows_c = proj[:, C_KVC:C_KVC + KV_W].reshape(nb, t // CMP_STRIDE, CMP_STRIDE * KV_W)
        kc, vc = _compress_prompt(rows_c, wlt, cb, w2k, l)
        o_attn = _nsa_prompt(proj, kc, vc, bias_cp, bias_t, jnp.asarray(ct_p, bf16), jnp.asarray(e_p, bf16), nb, t)
        o_rnn, h_last = _rglru_prompt(proj, nb, t, rnn_w, l)
        merged = _merge(o_attn, o_rnn, proj, w_ao, w_ro, l, tm=1024)
        x1 = _outproj(merged, y_p, mod_p, w_o, ln_g4, ln_b4, l, tm=256)
        y_p = _ffn(x1, mod_p, w_f1, w_f2, ln_g4, ln_b4, l, tm=512)
        kv = lambda c: proj[:, c:c + KV_W].reshape(nb, t, 2, N_KV, HEAD_DIM)
        st_p.append((kv(C_KVC), kv(C_KVS), kv(C_KVW)[:, t - min(WINDOW, t):],
                     proj[:, C_RX:C_RX + D_RNN].reshape(nb, t, D_RNN)[:, t - (CONV_W - 1):],
                     h_last.reshape(nb, D_RNN)))

        proj = _inproj(y_s, mod_s, w_in_p, b_in_p, l, tm=ms)
        proj3 = proj.reshape(n_new, ns, N_PACK)
        q5 = proj3[:, :, C_Q:C_Q + 1024].reshape(n_new, ns, N_KV, GROUP, HEAD_DIM).transpose(1, 2, 3, 0, 4)
        qrow = (q5[:, :, :, :, None, :] * jnp.eye(N_KV, dtype=f32)[None, :, None, None, :, None]
                ).reshape(ns, N_HEADS * n_new, 256)
        ng = proj3[:, :, C_NG:C_NG + 48].reshape(n_new, ns, N_HEADS, 3).transpose(1, 2, 0, 3).reshape(
            ns, N_HEADS * n_new, 3)
        new_rows = lambda c: proj3[:, :, c:c + KV_W].transpose(1, 0, 2)
        new_t = lambda c: jnp.pad(proj3[:, :, c:c + KV_W].transpose(1, 2, 0), ((0, 0), (0, 0), (0, 128 - n_new)))
        o64 = _nsa_sample(page_table, qrow, ng, new_rows(C_KVC), new_t(C_KVS), new_t(C_KVW),
                          cache_ct, cache_st, win_t, l, n_phys, wlt, cb, w2k, consts_s)
        o_attn = o64.reshape(ns, N_HEADS, n_new, HEAD_DIM).transpose(2, 0, 1, 3).reshape(ms, 1024).astype(bf16)
        o_rnn, h_last = _rglru_sample(proj, cbuf_t, state_rnn_h, rnn_w, l)
        merged = _merge(o_attn, o_rnn, proj, w_ao, w_ro, l, tm=ms)
        x1 = _outproj(merged, y_s, mod_s, w_o, ln_g4, ln_b4, l, tm=ms)
        y_s = _ffn(x1, mod_s, w_f1, w_f2, ln_g4, ln_b4, l, tm=ms)
        kv = lambda c: new_rows(c).reshape(ns, n_new, 2, N_KV, HEAD_DIM)
        xp = jnp.concatenate([state_conv[l], proj3[:, :, C_RX:C_RX + D_RNN].transpose(1, 0, 2)], axis=1)
        st_s.append((kv(C_KVC), kv(C_KVS), kv(C_KVW), xp[:, n_new:], h_last))

    stack = lambda sts, k: jnp.stack([s[k] for s in sts])
    return (y_p.reshape(nb, t, D_MODEL), y_s.reshape(n_new, ns, D_MODEL).transpose(1, 0, 2),
            stack(st_p, 0), stack(st_p, 1), stack(st_p, 2), stack(st_p, 3), stack(st_p, 4),
            stack(st_s, 0), stack(st_s, 1), stack(st_s, 2), stack(st_s, 3), stack(st_s, 4))
```

```python
import math

import numpy as np
import jax
import jax.numpy as jnp
from jax import lax
from jax.experimental import pallas as pl
from jax.experimental.pallas import tpu as pltpu

f32 = jnp.float32
bf16 = jnp.bfloat16
i32 = jnp.int32

D_MODEL = 2048
N_HEADS = 16
N_KV = 4
GROUP = 4
HEAD_DIM = 64
CMP_STRIDE = 16
CMP_LEN = 32
SEL_LEN = 64
N_SEL = 16
WINDOW = 512
PAGE = 128
FORCE_BONUS = 1e3
N_BUCKETS = 32
MAX_DISTANCE = 128
D_RNN = 1024
RNN_BLOCK = 64
CONV_W = 4
LRU_C = 8.0
D_FF = 5632
DEPTH = 2
ALPHA = (2.0 * DEPTH) ** 0.25
SCALE = HEAD_DIM ** -0.5
NEG = -1e30
KV_W = 2 * N_KV * HEAD_DIM

C_MG = 0
C_Q = 4096
C_RX = 5120
C_RY = 6144
C_KVC = 7168
C_KVS = 7680
C_KVW = 8192
C_NG = 8704
N_PACK = 9216

VMEM_LIMIT = 56 * 1024 * 1024

TQ = 128
TK = 256
N_PAGES = 16


def _cparams(sem):
    return pltpu.CompilerParams(dimension_semantics=sem, vmem_limit_bytes=VMEM_LIMIT)


def _sigmoid(x):
    return 1.0 / (1.0 + jnp.exp(-x))


def _silu(x):
    return x * _sigmoid(x)


def _layer_norm(z, g, b):
    mu = jnp.mean(z, axis=-1, keepdims=True)
    zc = z - mu
    var = jnp.mean(zc * zc, axis=-1, keepdims=True)
    return zc * lax.rsqrt(var + 1e-5) * g + b


def _dot(a, b):
    return jnp.dot(a, b, preferred_element_type=f32)


def _dot_nt(a, b):
    return lax.dot_general(a, b, (((1,), (1,)), ((), ())), preferred_element_type=f32)


def _split3(x):
    hi = x.astype(bf16)
    r1 = x - hi.astype(f32)
    mid = r1.astype(bf16)
    lo = (r1 - mid.astype(f32)).astype(bf16)
    return hi, mid, lo


def _rows(mod, tm):
    mr = mod.shape[0]
    if mr == 1 or mr == tm:
        return mod
    return jnp.concatenate([mod] * (tm // mr), axis=0)


def _mod_spec(mod, k, tiles_per_block, ngrid):
    mr = mod.shape[1]
    if ngrid == 1:
        return pl.BlockSpec((None, mr, D_MODEL), lambda i: (i // tiles_per_block, 0, k))
    return pl.BlockSpec((None, mr, D_MODEL), lambda i, j: (i // tiles_per_block, 0, k))


def _ada_kernel(c_ref, w_ref, b_ref, o_ref):
    h = _silu(c_ref[...]).astype(bf16)
    o_ref[...] = _dot(h, w_ref[...].astype(bf16)) + b_ref[...]


def _ada(c_all, w, b, layer):
    m = c_all.shape[0]
    n = w.shape[2]
    tn = 1024
    return pl.pallas_call(
        _ada_kernel,
        grid=(n // tn,),
        in_specs=[pl.BlockSpec((m, D_MODEL), lambda j: (0, 0)),
                  pl.BlockSpec((None, D_MODEL, tn), lambda j: (layer, 0, j)),
                  pl.BlockSpec((None, 1, tn), lambda j: (layer, 0, j))],
        out_specs=pl.BlockSpec((m, tn), lambda j: (0, j)),
        out_shape=jax.ShapeDtypeStruct((m, n), f32),
        compiler_params=_cparams(("arbitrary",)),
        name="ada",
    )(c_all, w, b)


def _bias_lookup_kernel(tbl_ref, bk_ref, o_ref):
    bk = bk_ref[...]
    for h in range(N_HEADS):
        acc = jnp.zeros(bk.shape, f32)
        for k in range(N_BUCKETS):
            acc = jnp.where(bk == k, tbl_ref[k, h], acc)
        o_ref[h] = acc


def _bias_lookup(tbl, buckets):
    n = buckets.shape[0]
    tr = 256
    return pl.pallas_call(
        _bias_lookup_kernel,
        grid=(n // tr,),
        in_specs=[pl.BlockSpec(memory_space=pltpu.SMEM),
                  pl.BlockSpec((tr, 128), lambda i: (i, 0))],
        out_specs=pl.BlockSpec((N_HEADS, tr, 128), lambda i: (0, i, 0)),
        out_shape=jax.ShapeDtypeStruct((N_HEADS, n, 128), f32),
        compiler_params=_cparams(("arbitrary",)),
        name="bias_lookup",
    )(tbl, buckets)


def _t5_bucket_np(dist):
    n = np.maximum(dist, 0)
    exact = N_BUCKETS // 2
    ratio = np.maximum(n, 1).astype(np.float32) / np.float32(exact)
    log_ratio = np.log(ratio).astype(np.float32) / np.float32(math.log(MAX_DISTANCE / exact))
    large = np.minimum(exact + (log_ratio * np.float32(N_BUCKETS - exact)).astype(np.int32), N_BUCKETS - 1)
    return np.where(n < exact, n, large).astype(np.int32)


def _inproj_kernel(x_ref, sc_ref, sh_ref, w_ref, b_ref, o_ref, h_scr):
    @pl.when(pl.program_id(1) == 0)
    def _():
        tm = x_ref.shape[0]
        h_scr[...] = (x_ref[...] * (1.0 + _rows(sc_ref[...], tm)) + _rows(sh_ref[...], tm)).astype(bf16)

    o_ref[...] = _dot(h_scr[...], w_ref[...]) + b_ref[...]


def _inproj(x, mod, w, b, layer, tm):
    m = x.shape[0]
    n = w.shape[2]
    tn = 512
    tpb = (m // tm) // mod.shape[0]
    return pl.pallas_call(
        _inproj_kernel,
        grid=(m // tm, n // tn),
        in_specs=[pl.BlockSpec((tm, D_MODEL), lambda i, j: (i, 0)),
                  _mod_spec(mod, 1, tpb, 2), _mod_spec(mod, 0, tpb, 2),
                  pl.BlockSpec((None, D_MODEL, tn), lambda i, j: (layer, 0, j)),
                  pl.BlockSpec((None, 1, tn), lambda i, j: (layer, 0, j))],
        out_specs=pl.BlockSpec((tm, tn), lambda i, j: (i, j)),
        out_shape=jax.ShapeDtypeStruct((m, n), f32),
        scratch_shapes=[pltpu.VMEM((tm, D_MODEL), bf16)],
        compiler_params=_cparams(("parallel", "arbitrary")),
        name="inproj",
    )(x, mod, mod, w, b)


def _cmp_bias_kernel(pe_ref, wlt_ref, o_ref):
    for kv in range(2):
        acc = jnp.zeros((8, 512), f32)
        for l in range(CMP_STRIDE):
            w = wlt_ref[kv, l]
            lead = _dot(jnp.broadcast_to(pe_ref[kv, 0, l:l + 1, :], (8, 256)).astype(bf16), w[:, 0:256])
            tail = _dot(jnp.broadcast_to(pe_ref[kv, 1, l:l + 1, :], (8, 256)).astype(bf16), w[:, 256:512])
            acc = acc + jnp.concatenate([lead, tail], axis=1)
        o_ref[kv] = acc


def _cmp_bias(pe, wlt):
    return pl.pallas_call(
        _cmp_bias_kernel,
        out_shape=jax.ShapeDtypeStruct((2, 8, 512), f32),
        compiler_params=pltpu.CompilerParams(vmem_limit_bytes=VMEM_LIMIT),
        name="cmp_bias",
    )(pe, wlt)


def _compress_chunks(piece, n_rows, kv, wlt_ref, cb_ref, w2_ref):
    acc = jnp.zeros((n_rows, 512), f32)
    for l in range(CMP_STRIDE):
        acc = acc + _dot(piece(l), wlt_ref[kv, l])
    acc = acc + cb_ref[kv, 0:1, :]
    z = acc[:, 0:256] + pltpu.roll(acc[:, 256:512], n_rows - 1, 0)
    return _dot(_silu(z).astype(bf16), w2_ref[kv])


def _compress_prompt_kernel(x_ref, wlt_ref, cb_ref, w2_ref, kc_ref, vc_ref):
    n_chunk = x_ref.shape[0]
    for kv, dst in enumerate((kc_ref, vc_ref)):
        piece = lambda l: x_ref[:, l * KV_W + kv * 256:l * KV_W + (kv + 1) * 256].astype(bf16)
        out = _compress_chunks(piece, n_chunk, kv, wlt_ref, cb_ref, w2_ref)
        row = lax.broadcasted_iota(i32, out.shape, 0)
        dst[...] = jnp.where(row < n_chunk - 1, out, 0.0)


def _compress_prompt(rows, wlt, cb, w2, layer):
    nb, n_chunk, width = rows.shape
    lsel = lambda a: pl.BlockSpec((None,) + a.shape[1:], lambda b: (layer,) + (0,) * (a.ndim - 1))
    return pl.pallas_call(
        _compress_prompt_kernel,
        grid=(nb,),
        in_specs=[pl.BlockSpec((None, n_chunk, width), lambda b: (b, 0, 0)),
                  lsel(wlt), pl.BlockSpec(cb.shape, lambda b: (0, 0, 0)), lsel(w2)],
        out_specs=[pl.BlockSpec((None, n_chunk, 256), lambda b: (b, 0, 0))] * 2,
        out_shape=[jax.ShapeDtypeStruct((nb, n_chunk, 256), f32)] * 2,
        compiler_params=_cparams(("arbitrary",)),
        name="compress_prompt",
    )(rows, wlt, cb, w2)


def _topk_mask_rows(imp, sidx, n_keep):
    n_rows = imp.shape[0]
    rank = jnp.zeros(imp.shape, i32)
    for j in range(n_rows):
        row = imp[j:j + 1, :]
        beats = jnp.where(row > imp, 1, jnp.where(row == imp, jnp.where(sidx > j, 1, 0), 0))
        rank = rank + beats
    return jnp.where(rank < n_keep, jnp.where(imp > -1e29, 1.0, 0.0), 0.0)


def _nsa_prompt_kernel(q_ref, ng_ref, ks_ref, vs_ref, kw_ref, vw_ref, kc_ref, vc_ref,
                       bc_ref, bt_ref, ct_ref, e_ref, o_ref, selk_scr, s_scr, mx_scr):
    i = pl.program_id(1)
    q0 = i * TQ
    rows = GROUP * TQ
    n_sel_blk = ks_ref.shape[0] // SEL_LEN

    d0 = lax.broadcasted_iota(i32, (TQ, TK), 0) - lax.broadcasted_iota(i32, (TQ, TK), 1)
    dist_c = (q0 + (lax.broadcasted_iota(i32, (rows, 128), 0) & (TQ - 1))
              - CMP_STRIDE * lax.broadcasted_iota(i32, (rows, 128), 1) - (CMP_LEN - 1))
    mask_c = dist_c >= 0
    sig = _sigmoid(ng_ref[...])

    sidx = lax.broadcasted_iota(i32, (n_sel_blk, TQ), 0)
    cur = (q0 + lax.broadcasted_iota(i32, (n_sel_blk, TQ), 1)) // SEL_LEN
    forced = jnp.where(sidx == 0, 1, jnp.where(sidx == cur, 1, jnp.where(sidx == cur - 1, 1, 0)))

    def branch(qs, g, k_ref, v_ref, kt_hi, window):
        n_tiles = k_ref.shape[0] // TK
        ones = jnp.ones((TK, HEAD_DIM), f32)

        def score_tile(kt, slot):
            k0 = pl.multiple_of(jnp.clip(kt, 0, n_tiles - 1) * TK, TK)
            k = k_ref[pl.ds(k0, TK), g * 64:(g + 1) * 64].astype(bf16)
            delta = q0 - kt * TK
            dist = d0 + delta
            if window:
                live = jnp.where(kt >= 0, 0.0, NEG)
                addm = jnp.where(dist >= 0, jnp.where(dist < WINDOW, live, NEG), NEG)
            else:
                addm = jnp.where(dist >= 0, selk_scr[:, pl.ds(k0, TK)], NEG)
            va = jnp.clip(delta, 0, 256) // 128
            vb = jnp.clip(delta - 128, 0, 256) // 128
            bias = jnp.concatenate([bt_ref[g * 3 + va], bt_ref[g * 3 + vb]], axis=1)
            s = _dot_nt(qs, k) + bias + jnp.concatenate([addm] * GROUP, axis=0)
            s_scr[:, pl.ds(slot * TK, TK)] = s
            return jnp.maximum(s[:, 0:128], s[:, 128:256])

        def value_tile(kt, slot, mb):
            k0 = pl.multiple_of(jnp.clip(kt, 0, n_tiles - 1) * TK, TK)
            v = jnp.concatenate([v_ref[pl.ds(k0, TK), g * 64:(g + 1) * 64], ones], axis=1).astype(bf16)
            p = jnp.exp(s_scr[:, pl.ds(slot * TK, TK)] - jnp.concatenate([mb, mb], axis=1))
            return _dot(p.astype(bf16), v)

        if window:
            tiles = [(kt_hi - 2 + j, j) for j in range(3)]
            mx = jnp.full((rows, 128), NEG, f32)
            for kt, slot in tiles:
                mx = jnp.maximum(mx, score_tile(kt, slot))
            mb = jnp.broadcast_to(jnp.max(mx, axis=-1, keepdims=True), (rows, 128))
            acc = jnp.zeros((rows, 2 * HEAD_DIM), f32)
            for kt, slot in tiles:
                acc = acc + value_tile(kt, slot, mb)
        else:
            n_quads = kt_hi // 4 + 1

            def scores(qd, mx):
                for j in range(4):
                    mx = jnp.maximum(mx, score_tile(4 * qd + j, 4 * qd + j))
                return mx

            mx = lax.fori_loop(0, n_quads, scores, jnp.full((rows, 128), NEG, f32))
            mx_scr[...] = jnp.broadcast_to(jnp.max(mx, axis=-1, keepdims=True), (rows, 128))

            def values(qd, acc):
                mb = mx_scr[...]
                for j in range(4):
                    acc = acc + value_tile(4 * qd + j, 4 * qd + j, mb)
                return acc

            acc = lax.fori_loop(0, n_quads, values, jnp.zeros((rows, 2 * HEAD_DIM), f32))
        return acc[:, 0:HEAD_DIM] / jnp.maximum(acc[:, HEAD_DIM:HEAD_DIM + 1], 1e-30)

    for g in range(N_KV):
        qg = q_ref[:, g * 256:(g + 1) * 256]
        qs = (jnp.concatenate([qg[:, r * 64:(r + 1) * 64] for r in range(GROUP)], axis=0) * SCALE).astype(bf16)

        kcg = kc_ref[:, g * 64:(g + 1) * 64].astype(bf16)
        vcg = vc_ref[:, g * 64:(g + 1) * 64].astype(bf16)
        bias_c = jnp.concatenate([bc_ref[g * GROUP + r] for r in range(GROUP)], axis=0)
        s = jnp.where(mask_c, _dot_nt(qs, kcg) + bias_c, NEG)
        p = jnp.where(mask_c, jnp.exp(s - jnp.max(s, axis=-1, keepdims=True)), 0.0)
        p = p / jnp.maximum(jnp.sum(p, axis=-1, keepdims=True), 1e-30)
        o_c = _dot(p.astype(bf16), vcg)

        p_sum = p[0:TQ] + p[TQ:2 * TQ] + p[2 * TQ:3 * TQ] + p[3 * TQ:4 * TQ]
        ct = ct_ref[...]
        imp_t = sum(_dot_nt(ct, piece) for piece in _split3(p_sum))[0:n_sel_blk]
        imp_t = jnp.where(forced > 0, imp_t + FORCE_BONUS, imp_t)
        imp_t = jnp.where(sidx <= cur, imp_t, NEG)
        sel_t = _topk_mask_rows(imp_t, sidx, N_SEL)
        sel_pad = jnp.concatenate([sel_t, jnp.zeros((128 - n_sel_blk, TQ), f32)], axis=0)
        sel_q = sel_pad.T.astype(bf16)
        selk_scr[...] = jnp.where(_dot(sel_q, e_ref[...]) > 0.5, 0.0, NEG)

        kt_hi = (q0 + TQ - 1) // TK
        o_s = branch(qs, g, ks_ref, vs_ref, kt_hi, False)
        o_w = branch(qs, g, kw_ref, vw_ref, kt_hi, True)

        gate = lambda c: jnp.concatenate(
            [sig[:, g * 12 + r * 3 + c:g * 12 + r * 3 + c + 1] for r in range(GROUP)], axis=0)
        o = gate(0) * o_c + gate(1) * o_s + gate(2) * o_w
        o_ref[:, g * 256:(g + 1) * 256] = jnp.concatenate(
            [o[r * TQ:(r + 1) * TQ] for r in range(GROUP)], axis=1).astype(bf16)


def _nsa_prompt(proj, kc, vc, bias_c, bias_t, ct, e, nb, t):
    nq = t // TQ
    full = lambda a: pl.BlockSpec(a.shape, lambda b, i: (0,) * a.ndim)
    kv = lambda c: pl.BlockSpec((t, 256), lambda b, i: (b, c // 256))
    return pl.pallas_call(
        _nsa_prompt_kernel,
        grid=(nb, nq),
        in_specs=[pl.BlockSpec((TQ, 1024), lambda b, i: (b * nq + i, C_Q // 1024)),
                  pl.BlockSpec((TQ, 128), lambda b, i: (b * nq + i, C_NG // 128)),
                  kv(C_KVS), kv(C_KVS + 256), kv(C_KVW), kv(C_KVW + 256),
                  pl.BlockSpec((None, t // CMP_STRIDE, 256), lambda b, i: (b, 0, 0)),
                  pl.BlockSpec((None, t // CMP_STRIDE, 256), lambda b, i: (b, 0, 0)),
                  pl.BlockSpec((N_HEADS, TQ, 128), lambda b, i: (0, i, 0)),
                  full(bias_t), full(ct), full(e)],
        out_specs=pl.BlockSpec((TQ, 1024), lambda b, i: (b * nq + i, 0)),
        out_shape=jax.ShapeDtypeStruct((nb * t, 1024), bf16),
        scratch_shapes=[pltpu.VMEM((TQ, t), f32), pltpu.VMEM((GROUP * TQ, t), f32),
                        pltpu.VMEM((GROUP * TQ, 128), f32)],
        compiler_params=_cparams(("parallel", "arbitrary")),
        name="nsa_prompt",
    )(proj, proj, proj, proj, proj, proj, kc, vc, bias_c, bias_t, ct, e)


def _lru_gates(xc, ry, wa_ref, ba_ref, wi_ref, bi_ref, lam_ref):
    xb = xc.astype(bf16)
    ra = jnp.concatenate([_dot(xb[:, c * 256:(c + 1) * 256], wa_ref[c]) for c in range(4)], axis=1)
    ri = jnp.concatenate([_dot(xb[:, c * 256:(c + 1) * 256], wi_ref[c]) for c in range(4)], axis=1)
    r = _sigmoid(ra + ba_ref[...])
    ig = _sigmoid(ri + bi_ref[...])
    nl = -lam_ref[...]
    softplus = jnp.maximum(nl, 0.0) + jnp.log1p(jnp.exp(-jnp.abs(nl)))
    log_a = -LRU_C * r * softplus
    a = jnp.exp(log_a)
    u = jnp.sqrt(jnp.tanh(-log_a) * (a * a + 1.0)) * (ig * xc)
    return a, u, jax.nn.gelu(ry)


def _rglru_prompt_kernel(rx_ref, ry_ref, cw_ref, cb_ref, wa_ref, ba_ref, wi_ref, bi_ref, lam_ref,
                         o_ref, hl_ref, xp_scr, a_scr, u_scr, hs_scr, h_scr):
    tt = pl.program_id(1)
    tr = rx_ref.shape[0]

    @pl.when(tt == 0)
    def _():
        xp_scr[0:8, :] = jnp.zeros((8, D_RNN), f32)
        h_scr[...] = jnp.zeros((1, D_RNN), f32)

    x = rx_ref[...]
    xp_scr[8:8 + tr, :] = x
    y = x * cw_ref[CONV_W - 1:CONV_W, :] + cb_ref[...]
    for k in range(CONV_W - 1):
        y = y + xp_scr[5 + k:5 + k + tr, :] * cw_ref[k:k + 1, :]
    xp_scr[0:8, :] = xp_scr[tr:tr + 8, :]

    a, u, gate = _lru_gates(y, ry_ref[...], wa_ref, ba_ref, wi_ref, bi_ref, lam_ref)
    a_scr[...] = a
    u_scr[...] = u

    def step(t, h):
        h = a_scr[pl.ds(t, 1), :] * h + u_scr[pl.ds(t, 1), :]
        hs_scr[pl.ds(t, 1), :] = h
        return h

    h = lax.fori_loop(0, tr, step, h_scr[...], unroll=8)
    h_scr[...] = h
    hl_ref[...] = h
    o_ref[...] = (hs_scr[...] * gate).astype(bf16)


def _rnn_specs(rnn_w, layer, ngrid):
    zeros = lambda n: (0,) * n
    if ngrid == 2:
        return [pl.BlockSpec((None,) + a.shape[1:], lambda b, i, n=a.ndim - 1: (layer,) + zeros(n)) for a in rnn_w]
    return [pl.BlockSpec((None,) + a.shape[1:], lambda i, n=a.ndim - 1: (layer,) + zeros(n)) for a in rnn_w]


def _rglru_prompt(proj, nb, t, rnn_w, layer):
    tr = 256
    nt = t // tr
    return pl.pallas_call(
        _rglru_prompt_kernel,
        grid=(nb, nt),
        in_specs=[pl.BlockSpec((tr, D_RNN), lambda b, i: (b * nt + i, C_RX // D_RNN)),
                  pl.BlockSpec((tr, D_RNN), lambda b, i: (b * nt + i, C_RY // D_RNN))]
        + _rnn_specs(rnn_w, layer, 2),
        out_specs=[pl.BlockSpec((tr, D_RNN), lambda b, i: (b * nt + i, 0)),
                   pl.BlockSpec((None, 1, D_RNN), lambda b, i: (b, 0, 0))],
        out_shape=[jax.ShapeDtypeStruct((nb * t, D_RNN), bf16),
                   jax.ShapeDtypeStruct((nb, 1, D_RNN), f32)],
        scratch_shapes=[pltpu.VMEM((tr + 8, D_RNN), f32), pltpu.VMEM((tr, D_RNN), f32),
                        pltpu.VMEM((tr, D_RNN), f32), pltpu.VMEM((tr, D_RNN), f32),
                        pltpu.VMEM((1, D_RNN), f32)],
        compiler_params=_cparams(("parallel", "arbitrary")),
        name="rglru_prompt",
    )(proj, proj, *rnn_w)


def _rglru_sample_kernel(rx_ref, ry_ref, cbuf_ref, h0_ref, cw_ref, cb_ref, wa_ref, ba_ref, wi_ref, bi_ref,
                         lam_ref, o_ref, hl_ref):
    ns = h0_ref.shape[0]
    n_t = rx_ref.shape[0] // ns
    xp = [cbuf_ref[k] for k in range(CONV_W - 1)] + [rx_ref[s * ns:(s + 1) * ns, :] for s in range(n_t)]
    h = h0_ref[...]
    for s in range(n_t):
        y = xp[s + CONV_W - 1] * cw_ref[CONV_W - 1:CONV_W, :] + cb_ref[...]
        for k in range(CONV_W - 1):
            y = y + xp[s + k] * cw_ref[k:k + 1, :]
        a, u, gate = _lru_gates(y, ry_ref[s * ns:(s + 1) * ns, :], wa_ref, ba_ref, wi_ref, bi_ref, lam_ref)
        h = a * h + u
        o_ref[s * ns:(s + 1) * ns, :] = (h * gate).astype(bf16)
    hl_ref[...] = h


def _rglru_sample(proj, cbuf, h0, rnn_w, layer):
    ms = proj.shape[0]
    ns = h0.shape[1]
    return pl.pallas_call(
        _rglru_sample_kernel,
        grid=(1,),
        in_specs=[pl.BlockSpec((ms, D_RNN), lambda i: (0, C_RX // D_RNN)),
                  pl.BlockSpec((ms, D_RNN), lambda i: (0, C_RY // D_RNN)),
                  pl.BlockSpec((None, CONV_W - 1, ns, D_RNN), lambda i: (layer, 0, 0, 0)),
                  pl.BlockSpec((None, ns, D_RNN), lambda i: (layer, 0, 0))]
        + _rnn_specs(rnn_w, layer, 1),
        out_specs=[pl.BlockSpec((ms, D_RNN), lambda i: (0, 0)),
                   pl.BlockSpec((ns, D_RNN), lambda i: (0, 0))],
        out_shape=[jax.ShapeDtypeStruct((ms, D_RNN), bf16),
                   jax.ShapeDtypeStruct((ns, D_RNN), f32)],
        compiler_params=_cparams(("arbitrary",)),
        name="rglru_sample",
    )(proj, proj, cbuf, h0, *rnn_w)


def _merge_kernel(oa_ref, orn_ref, ga_ref, gr_ref, wa_ref, wr_ref, o_ref):
    a1 = _dot(oa_ref[...], wa_ref[...])
    a2 = _dot(orn_ref[...], wr_ref[...])
    o_ref[...] = (_sigmoid(ga_ref[...]) * a1 + _sigmoid(gr_ref[...]) * a2).astype(bf16)


def _merge(o_attn, o_rnn, proj, w_ao, w_ro, layer, tm):
    m = o_attn.shape[0]
    tn = 512
    nn = D_MODEL // tn
    return pl.pallas_call(
        _merge_kernel,
        grid=(m // tm, nn),
        in_specs=[pl.BlockSpec((tm, 1024), lambda i, j: (i, 0)),
                  pl.BlockSpec((tm, 1024), lambda i, j: (i, 0)),
                  pl.BlockSpec((tm, tn), lambda i, j: (i, j)),
                  pl.BlockSpec((tm, tn), lambda i, j: (i, nn + j)),
                  pl.BlockSpec((None, 1024, tn), lambda i, j: (layer, 0, j)),
                  pl.BlockSpec((None, 1024, tn), lambda i, j: (layer, 0, j))],
        out_specs=pl.BlockSpec((tm, tn), lambda i, j: (i, j)),
        out_shape=jax.ShapeDtypeStruct((m, D_MODEL), bf16),
        compiler_params=_cparams(("parallel", "arbitrary")),
        name="merge",
    )(o_attn, o_rnn, proj, proj, w_ao, w_ro)


def _outproj_kernel(a_ref, x_ref, gt_ref, w_ref, lg_ref, lb_ref, o_ref):
    mix = _dot(a_ref[...], w_ref[...])
    z = ALPHA * x_ref[...] + (1.0 + _rows(gt_ref[...], x_ref.shape[0])) * mix
    o_ref[...] = _layer_norm(z, lg_ref[...], lb_ref[...])


def _outproj(a, x, mod, w, ln_g, ln_b, layer, tm):
    m = a.shape[0]
    tpb = (m // tm) // mod.shape[0]
    ln_spec = pl.BlockSpec((None, None, 1, D_MODEL), lambda i: (layer, 0, 0, 0))
    return pl.pallas_call(
        _outproj_kernel,
        grid=(m // tm,),
        in_specs=[pl.BlockSpec((tm, D_MODEL), lambda i: (i, 0)),
                  pl.BlockSpec((tm, D_MODEL), lambda i: (i, 0)),
                  _mod_spec(mod, 2, tpb, 1),
                  pl.BlockSpec((None, D_MODEL, D_MODEL), lambda i: (layer, 0, 0)),
                  ln_spec, ln_spec],
        out_specs=pl.BlockSpec((tm, D_MODEL), lambda i: (i, 0)),
        out_shape=jax.ShapeDtypeStruct((m, D_MODEL), f32),
        compiler_params=_cparams(("parallel",)),
        name="outproj_ln",
    )(a, x, mod, w, ln_g, ln_b)


def _ffn_kernel(x_ref, sc_ref, sh_ref, gt_ref, wg_ref, wu_ref, wo_ref, lg_ref, lb_ref, o_ref, h_scr):
    j = pl.program_id(1)
    tm = x_ref.shape[0]

    @pl.when(j == 0)
    def _():
        h_scr[...] = (x_ref[...] * (1.0 + _rows(sc_ref[...], tm)) + _rows(sh_ref[...], tm)).astype(bf16)
        o_ref[...] = jnp.zeros(o_ref.shape, f32)

    h = h_scr[...]
    act = (_silu(_dot(h, wg_ref[...])) * _dot(h, wu_ref[...])).astype(bf16)
    o_ref[...] += _dot(act, wo_ref[...])

    @pl.when(j == pl.num_programs(1) - 1)
    def _():
        z = ALPHA * x_ref[...] + (1.0 + _rows(gt_ref[...], tm)) * o_ref[...]
        o_ref[...] = _layer_norm(z, lg_ref[...], lb_ref[...])


def _ffn(x, mod, w_in, w_out, ln_g, ln_b, layer, tm):
    m = x.shape[0]
    tf = 512
    nf = D_FF // tf
    tpb = (m // tm) // mod.shape[0]
    ln_spec = pl.BlockSpec((None, None, 1, D_MODEL), lambda i, j: (layer, 1, 0, 0))
    return pl.pallas_call(
        _ffn_kernel,
        grid=(m // tm, nf),
        in_specs=[pl.BlockSpec((tm, D_MODEL), lambda i, j: (i, 0), pipeline_mode=pl.Buffered(1)),
                  _mod_spec(mod, 4, tpb, 2), _mod_spec(mod, 3, tpb, 2), _mod_spec(mod, 5, tpb, 2),
                  pl.BlockSpec((None, D_MODEL, tf), lambda i, j: (layer, 0, j)),
                  pl.BlockSpec((None, D_MODEL, tf), lambda i, j: (layer, 0, nf + j)),
                  pl.BlockSpec((None, tf, D_MODEL), lambda i, j: (layer, j, 0)),
                  ln_spec, ln_spec],
        out_specs=pl.BlockSpec((tm, D_MODEL), lambda i, j: (i, 0)),
        out_shape=jax.ShapeDtypeStruct((m, D_MODEL), f32),
        scratch_shapes=[pltpu.VMEM((tm, D_MODEL), bf16)],
        compiler_params=_cparams(("parallel", "arbitrary")),
        name="ffn",
    )(x, mod, mod, mod, w_in, w_in, w_out, ln_g, ln_b)


def _nsa_sample_kernel(pt_ref, qrow_ref, ng_ref, newc_ref, news_ref, neww_ref, *rest):
    cc = rest[0:N_PAGES]
    cs = rest[N_PAGES:2 * N_PAGES]
    (win_ref, wlt_ref, cb_ref, w2_ref, selp_ref, bc_ref, mc_ref, bs_ref, ms_ref, bw_ref, mw_ref,
     smat_ref, smt_ref, c_ref, e_ref, dsel_ref, o_ref) = rest[2 * N_PAGES:]
    n_new = newc_ref.shape[0]
    n_chunk = N_PAGES * (PAGE // CMP_STRIDE)
    row8 = lax.broadcasted_iota(i32, (8, 256), 0)
    selp = selp_ref[...]

    comp = []
    for kv in range(2):
        perm = [_dot_nt(selp, cc[p][kv * 256:(kv + 1) * 256, :].astype(bf16)) for p in range(N_PAGES)]

        def piece(l):
            if l < n_new:
                extra = jnp.where(row8 == 0, jnp.broadcast_to(newc_ref[l:l + 1, kv * 256:(kv + 1) * 256], (8, 256)), 0.0)
            else:
                extra = jnp.zeros((8, 256), f32)
            return jnp.concatenate([y[l * 8:(l + 1) * 8, :] for y in perm] + [extra], axis=0).astype(bf16)

        out = _compress_chunks(piece, n_chunk + 8, kv, wlt_ref, cb_ref, w2_ref)
        comp.append(out[0:n_chunk].astype(bf16))
    kc, vc = comp

    qs = (qrow_ref[...] * SCALE).astype(bf16)
    dsel = dsel_ref[...]

    def diag(x):
        y = x * dsel
        return y[:, 0:64] + y[:, 64:128] + y[:, 128:192] + y[:, 192:256]

    mc = mc_ref[...]
    s = jnp.where(mc > 0.5, _dot_nt(qs, kc) + bc_ref[...], NEG)
    p = jnp.exp(s - jnp.max(s, axis=-1, keepdims=True)) * mc
    p = p / jnp.maximum(jnp.sum(p, axis=-1, keepdims=True), 1e-30)
    o_c = diag(_dot(p.astype(bf16), vc))

    smat = smat_ref[...]
    p_sum = sum(_dot(smat, piece_) for piece_ in _split3(p))
    cmat = c_ref[...]
    imp = sum(_dot(piece_, cmat) for piece_ in _split3(p_sum))
    n_blk = (N_PAGES * PAGE + n_new + SEL_LEN - 1) // SEL_LEN
    cur = (N_PAGES * PAGE) // SEL_LEN
    sidx = lax.broadcasted_iota(i32, imp.shape, 1)
    forced = jnp.where(sidx == 0, 1, jnp.where(sidx == cur, 1, jnp.where(sidx == cur - 1, 1, 0)))
    imp = jnp.where(forced > 0, imp + FORCE_BONUS, imp)
    imp = jnp.where(sidx <= cur, imp, NEG)
    rank = jnp.zeros(imp.shape, i32)
    for j in range(n_blk):
        col = imp[:, j:j + 1]
        rank = rank + jnp.where(col > imp, 1, jnp.where(col == imp, jnp.where(sidx > j, 1, 0), 0))
    sel = jnp.where(rank < N_SEL, jnp.where(imp > -1e29, jnp.where(sidx < n_blk, 1.0, 0.0), 0.0), 0.0)
    sel64 = _dot(smt_ref[...], sel.astype(bf16)).astype(bf16)
    selk = _dot(sel64, e_ref[...])

    def attend(kt_tiles, vt_tiles, bias, mask):
        s = jnp.concatenate([_dot(qs, kt) for kt in kt_tiles], axis=1) + bias
        s = jnp.where(mask > 0.5, s, NEG)
        p = jnp.exp(s - jnp.max(s, axis=-1, keepdims=True)) * mask
        p = (p / jnp.maximum(jnp.sum(p, axis=-1, keepdims=True), 1e-30)).astype(bf16)
        acc = jnp.zeros((64, 256), f32)
        off = 0
        for vt in vt_tiles:
            w = vt.shape[1]
            acc = acc + _dot_nt(p[:, off:off + w], vt)
            off += w
        return diag(acc)

    k_tiles = [cs[p][0:256, :].astype(bf16) for p in range(N_PAGES)] + [news_ref[0:256, :].astype(bf16)]
    v_tiles = [cs[p][256:512, :].astype(bf16) for p in range(N_PAGES)] + [news_ref[256:512, :].astype(bf16)]
    o_s = attend(k_tiles, v_tiles, bs_ref[...], ms_ref[...] * selk)

    k_tiles = [win_ref[0:256, :].astype(bf16), neww_ref[0:256, :].astype(bf16)]
    v_tiles = [win_ref[256:512, :].astype(bf16), neww_ref[256:512, :].astype(bf16)]
    o_w = attend(k_tiles, v_tiles, bw_ref[...], mw_ref[...])

    sig = _sigmoid(ng_ref[...])
    o_ref[...] = sig[:, 0:1] * o_c + sig[:, 1:2] * o_s + sig[:, 2:3] * o_w


def _nsa_sample(page_table, qrow, ng, newc, news_t, neww_t, cache_c, cache_s, win, layer, n_phys,
                wlt, cb, w2, consts):
    nb = qrow.shape[0]
    n_new = newc.shape[1]
    full = lambda a: pl.BlockSpec(a.shape, lambda b, pt: (0,) * a.ndim)
    lsel = lambda a: pl.BlockSpec((None,) + a.shape[1:], lambda b, pt: (layer,) + (0,) * (a.ndim - 1))
    per_b = lambda shape: pl.BlockSpec((None,) + shape, lambda b, pt: (b, 0, 0))

    def page_spec(p):
        return pl.BlockSpec((None, KV_W, PAGE), lambda b, pt: (layer * n_phys + pt[b, p], 0, 0))

    in_specs = ([per_b((64, 256)), per_b((64, 3)), per_b((n_new, KV_W)), per_b((KV_W, 128)), per_b((KV_W, 128))]
                + [page_spec(p) for p in range(N_PAGES)] * 2
                + [pl.BlockSpec((None, KV_W, WINDOW), lambda b, pt: (layer * nb + b, 0, 0)),
                   lsel(wlt), full(cb), lsel(w2)]
                + [full(c) for c in consts])
    grid_spec = pltpu.PrefetchScalarGridSpec(
        num_scalar_prefetch=1,
        grid=(nb,),
        in_specs=in_specs,
        out_specs=pl.BlockSpec((None, 64, 64), lambda b, pt: (b, 0, 0)),
    )
    return pl.pallas_call(
        _nsa_sample_kernel,
        grid_spec=grid_spec,
        out_shape=jax.ShapeDtypeStruct((nb, 64, 64), f32),
        compiler_params=_cparams(("arbitrary",)),
        name="nsa_sample",
    )(page_table, qrow, ng, newc, news_t, neww_t, *([cache_c] * N_PAGES), *([cache_s] * N_PAGES), win,
      wlt, cb, w2, *consts)


def _static_tables(t, past, n_new):
    ar = np.arange
    tiles = np.stack([d + ar(128)[:, None] - ar(128)[None, :] for d in (0, 128, 256)])
    cmp_p = ar(t)[:, None] - (CMP_STRIDE * ar(128)[None, :] + CMP_LEN - 1)
    qpos = past + ar(n_new)
    cmp_s = qpos[:, None] - (CMP_STRIDE * ar(128)[None, :] + CMP_LEN - 1)
    kpos_s = np.concatenate([ar(past), past + ar(128)])
    slc_s = qpos[:, None] - kpos_s[None, :]
    kpos_w = np.concatenate([past - WINDOW + ar(WINDOW), past + ar(128)])
    win_s = qpos[:, None] - kpos_w[None, :]
    parts = [tiles.reshape(-1, 128), cmp_p, cmp_s.reshape(-1, 128), slc_s.reshape(-1, 128), win_s.reshape(-1, 128)]
    sizes = [p.shape[0] for p in parts]
    flat = np.concatenate(parts, axis=0)
    pad = (-flat.shape[0]) % 256
    flat = np.concatenate([flat, np.zeros((pad, 128), flat.dtype)], axis=0)
    buckets = _t5_bucket_np(flat)
    real_s = np.concatenate([np.ones(past, bool), ar(128) < n_new])
    real_w = np.concatenate([np.ones(WINDOW, bool), ar(128) < n_new])
    mask_c = (cmp_s >= 0)
    mask_s = (slc_s >= 0) & real_s[None, :]
    mask_w = (win_s >= 0) & (win_s < WINDOW) & real_w[None, :]
    rep = lambda mk: np.tile(mk[None].astype(np.float32), (N_HEADS, 1, 1)).reshape(N_HEADS * n_new, -1)
    return buckets, sizes, rep(mask_c), rep(mask_s), rep(mask_w)


def _cmp_to_sel_np(n_cmp, n_sel):
    j = np.arange(n_cmp)[:, None]
    s = np.arange(n_sel)[None, :]
    lo = np.maximum(j * CMP_STRIDE, s * SEL_LEN)
    hi = np.minimum(j * CMP_STRIDE + CMP_LEN, (s + 1) * SEL_LEN)
    return (np.maximum(hi - lo, 0) / CMP_LEN).astype(np.float32)


def _kron4(w):
    eye = jnp.eye(N_KV, dtype=w.dtype)
    out = jnp.einsum('gh,...de->...gdhe', eye, w)
    return out.reshape(w.shape[:-2] + (256, 256))


def _block_diag_rnn(w):
    w4 = w.reshape(DEPTH, 4, 4, RNN_BLOCK, RNN_BLOCK)
    eye = jnp.eye(4, dtype=w.dtype)
    return jnp.einsum('jk,zcjde->zcjdke', eye, w4).reshape(DEPTH, 4, 256, 256)


def _pack_in(w, b):
    cuts = np.cumsum([1024, 512, 512, 512, 48, 1024, 1024, 4096])[:-1]
    q, kvc, kvs, kvw, ng, rx, ry, mg = jnp.split(w, cuts, axis=-1)
    bq, bkvc, bkvs, bkvw, bng, brx, bry, bmg = jnp.split(b, cuts, axis=-1)
    n_pad = N_PACK - (C_NG + 48)
    zw = jnp.zeros(w.shape[:-1] + (n_pad,), w.dtype)
    zb = jnp.zeros(b.shape[:-1] + (n_pad,), b.dtype)
    wp = jnp.concatenate([mg, q, rx, ry, kvc, kvs, kvw, ng, zw], axis=-1)
    bp = jnp.concatenate([bmg, bq, brx, bry, bkvc, bkvs, bkvw, bng, zb], axis=-1)
    return wp.astype(bf16), bp.reshape(DEPTH, 1, N_PACK)


def _feature_major(cache):
    d, n, rows = cache.shape[:3]
    return cache.transpose(0, 1, 3, 4, 5, 2).reshape(d * n, KV_W, rows)


def kernel(x_prompt, x_sample, c_prompt, c_sample, cache_cmp_kv, cache_slc_kv, cache_win_kv, state_conv, state_rnn_h, page_table, rel_bias, w_ada, b_ada, w_in, b_in, cmp_pos, cmp_w1, cmp_w2, w_attn_o, conv_w, conv_b, lru_wa, lru_ba, lru_wi, lru_bi, lru_lambda, w_rnn_o, w_out, w_ffn_in, w_ffn_out, ln_g, ln_b):
    nb, t, _ = x_prompt.shape
    ns, n_new, _ = x_sample.shape
    n_phys = cache_cmp_kv.shape[1]
    past = page_table.shape[1] * PAGE
    assert page_table.shape[1] == N_PAGES and cache_win_kv.shape[2] == WINDOW and t % (4 * TK) == 0
    assert n_new <= CMP_STRIDE and ns % 8 == 0
    mp, ms = nb * t, ns * n_new

    buckets, sizes, mask_c, mask_s, mask_w = _static_tables(t, past, n_new)
    bias_all = _bias_lookup(rel_bias, jnp.asarray(buckets))
    offs = np.cumsum([0] + sizes)
    seg = lambda k: bias_all[:, offs[k]:offs[k + 1]]
    bias_t = seg(0).reshape(N_KV, GROUP, 3, 128, 128).transpose(0, 2, 1, 3, 4).reshape(N_KV * 3, GROUP * 128, 128)
    bias_cp = seg(1)
    bias_cs = seg(2).reshape(N_HEADS * n_new, 128)
    bias_ss = seg(3).reshape(N_HEADS * n_new, past + 128)
    bias_ws = seg(4).reshape(N_HEADS * n_new, WINDOW + 128)

    n_sel_p = t // SEL_LEN
    ct_p = np.zeros((128, 128), np.float32)
    ct_p[:n_sel_p, :t // CMP_STRIDE - 1] = _cmp_to_sel_np(t // CMP_STRIDE - 1, n_sel_p).T
    e_p = (np.arange(128)[:, None] == (np.arange(t)[None, :] // SEL_LEN)).astype(np.float32)
    n_cmp_s = (past + n_new + CMP_STRIDE - 1) // CMP_STRIDE - 1
    n_sel_s = (past + n_new + SEL_LEN - 1) // SEL_LEN
    c_s = np.zeros((128, 128), np.float32)
    c_s[:n_cmp_s, :n_sel_s] = _cmp_to_sel_np(n_cmp_s, n_sel_s)
    e_s = (np.arange(128)[:, None] == (np.arange(past + 128)[None, :] // SEL_LEN)).astype(np.float32)
    hq = np.arange(N_HEADS * n_new)
    smat = ((hq[None, :] // (GROUP * n_new)) * n_new + hq[None, :] % n_new
            == np.arange(N_KV * n_new)[:, None]).astype(np.float32)
    dsel = (hq[:, None] // (GROUP * n_new) == np.arange(256)[None, :] // HEAD_DIM).astype(np.float32)
    pos = np.arange(PAGE)
    selp = ((pos % CMP_STRIDE) * (PAGE // CMP_STRIDE) + pos // CMP_STRIDE)[None, :] == np.arange(PAGE)[:, None]
    consts_s = (jnp.asarray(selp, bf16), bias_cs, jnp.asarray(mask_c), bias_ss, jnp.asarray(mask_s), bias_ws,
                jnp.asarray(mask_w), jnp.asarray(smat, bf16), jnp.asarray(smat.T, bf16), jnp.asarray(c_s, bf16),
                jnp.asarray(e_s, bf16), jnp.asarray(dsel))

    w_in_p, b_in_p = _pack_in(w_in, b_in)
    w1 = cmp_w1.reshape(DEPTH, 2, 2, CMP_STRIDE, HEAD_DIM, HEAD_DIM)
    wlt = jnp.concatenate([_kron4(w1[:, :, 0]), _kron4(w1[:, :, 1])], axis=-1).astype(bf16)
    w2k = _kron4(cmp_w2).astype(bf16)
    pe = jnp.tile(cmp_pos.reshape(DEPTH, 2, 2, CMP_STRIDE, HEAD_DIM), (1, 1, 1, 1, N_KV))
    rnn_w = (conv_w, conv_b.reshape(DEPTH, 1, D_RNN), _block_diag_rnn(lru_wa).astype(bf16),
             lru_ba.reshape(DEPTH, 1, D_RNN), _block_diag_rnn(lru_wi).astype(bf16),
             lru_bi.reshape(DEPTH, 1, D_RNN), lru_lambda.reshape(DEPTH, 1, D_RNN))
    w_ao, w_ro, w_o = w_attn_o.astype(bf16), w_rnn_o.astype(bf16), w_out.astype(bf16)
    w_f1, w_f2 = w_ffn_in.astype(bf16), w_ffn_out.astype(bf16)
    ln_g4, ln_b4 = ln_g.reshape(DEPTH, 2, 1, D_MODEL), ln_b.reshape(DEPTH, 2, 1, D_MODEL)
    b_ada3 = b_ada.reshape(DEPTH, 1, 6 * D_MODEL)

    n_cpad = -(nb + ns) % 8
    c_all = jnp.concatenate([c_prompt, c_sample, jnp.zeros((n_cpad, D_MODEL), f32)], axis=0)
    cache_ct, cache_st, win_t = _feature_major(cache_cmp_kv), _feature_major(cache_slc_kv), _feature_major(cache_win_kv)
    cbuf_t = state_conv.transpose(0, 2, 1, 3)

    y_p = x_prompt.reshape(mp, D_MODEL)
    y_s = x_sample.transpose(1, 0, 2).reshape(ms, D_MODEL)
    st_p, st_s = [], []
    for l in range(DEPTH):
        ada = _ada(c_all, w_ada, b_ada3, l)
        mod_p = ada[:nb].reshape(nb, 1, 6 * D_MODEL)
        mod_s = ada[nb:nb + ns].reshape(1, ns, 6 * D_MODEL)
        cb = _cmp_bias(pe[l], wlt[l])

        proj = _inproj(y_p, mod_p, w_in_p, b_in_p, l, tm=1024)
        rows_c = proj[:, C_KVC:C_KVC + KV_W].reshape(nb, t // CMP_STRIDE, CMP_STRIDE * KV_W)
        kc, vc = _compress_prompt(rows_c, wlt, cb, w2k, l)
        o_attn = _nsa_prompt(proj, kc, vc, bias_cp, bias_t, jnp.asarray(ct_p, bf16), jnp.asarray(e_p, bf16), nb, t)
        o_rnn, h_last = _rglru_prompt(proj, nb, t, rnn_w, l)
        merged = _merge(o_attn, o_rnn, proj, w_ao, w_ro, l, tm=1024)
        x1 = _outproj(merged, y_p, mod_p, w_o, ln_g4, ln_b4, l, tm=256)
        y_p = _ffn(x1, mod_p, w_f1, w_f2, ln_g4, ln_b4, l, tm=1024)
        kv = lambda c: proj[:, c:c + KV_W].reshape(nb, t, 2, N_KV, HEAD_DIM)
        st_p.append((kv(C_KVC), kv(C_KVS), kv(C_KVW)[:, t - min(WINDOW, t):],
                     proj[:, C_RX:C_RX + D_RNN].reshape(nb, t, D_RNN)[:, t - (CONV_W - 1):],
                     h_last.reshape(nb, D_RNN)))

        proj = _inproj(y_s, mod_s, w_in_p, b_in_p, l, tm=ms)
        proj3 = proj.reshape(n_new, ns, N_PACK)
        q5 = proj3[:, :, C_Q:C_Q + 1024].reshape(n_new, ns, N_KV, GROUP, HEAD_DIM).transpose(1, 2, 3, 0, 4)
        qrow = (q5[:, :, :, :, None, :] * jnp.eye(N_KV, dtype=f32)[None, :, None, None, :, None]
                ).reshape(ns, N_HEADS * n_new, 256)
        ng = proj3[:, :, C_NG:C_NG + 48].reshape(n_new, ns, N_HEADS, 3).transpose(1, 2, 0, 3).reshape(
            ns, N_HEADS * n_new, 3)
        new_rows = lambda c: proj3[:, :, c:c + KV_W].transpose(1, 0, 2)
        new_t = lambda c: jnp.pad(proj3[:, :, c:c + KV_W].transpose(1, 2, 0), ((0, 0), (0, 0), (0, 128 - n_new)))
        o64 = _nsa_sample(page_table, qrow, ng, new_rows(C_KVC), new_t(C_KVS), new_t(C_KVW),
                          cache_ct, cache_st, win_t, l, n_phys, wlt, cb, w2k, consts_s)
        o_attn = o64.reshape(ns, N_HEADS, n_new, HEAD_DIM).transpose(2, 0, 1, 3).reshape(ms, 1024).astype(bf16)
        o_rnn, h_last = _rglru_sample(proj, cbuf_t, state_rnn_h, rnn_w, l)
        merged = _merge(o_attn, o_rnn, proj, w_ao, w_ro, l, tm=ms)
        x1 = _outproj(merged, y_s, mod_s, w_o, ln_g4, ln_b4, l, tm=ms)
        y_s = _ffn(x1, mod_s, w_f1, w_f2, ln_g4, ln_b4, l, tm=ms)
        kv = lambda c: new_rows(c).reshape(ns, n_new, 2, N_KV, HEAD_DIM)
        xp = jnp.concatenate([state_conv[l], proj3[:, :, C_RX:C_RX + D_RNN].transpose(1, 0, 2)], axis=1)
        st_s.append((kv(C_KVC), kv(C_KVS), kv(C_KVW), xp[:, n_new:], h_last))

    stack = lambda sts, k: jnp.stack([s[k] for s in sts])
    return (y_p.reshape(nb, t, D_MODEL), y_s.reshape(n_new, ns, D_MODEL).transpose(1, 0, 2),
            stack(st_p, 0), stack(st_p, 1), stack(st_p, 2), stack(st_p, 3), stack(st_p, 4),
            stack(st_s, 0), stack(st_s, 1), stack(st_s, 2), stack(st_s, 3), stack(st_s, 4))
```

```python
import math

import numpy as np
import jax
import jax.numpy as jnp
from jax import lax
from jax.experimental import pallas as pl
from jax.experimental.pallas import tpu as pltpu

f32 = jnp.float32
bf16 = jnp.bfloat16
i32 = jnp.int32

D_MODEL = 2048
N_HEADS = 16
N_KV = 4
GROUP = 4
HEAD_DIM = 64
CMP_STRIDE = 16
CMP_LEN = 32
SEL_LEN = 64
N_SEL = 16
WINDOW = 512
PAGE = 128
FORCE_BONUS = 1e3
N_BUCKETS = 32
MAX_DISTANCE = 128
D_RNN = 1024
RNN_BLOCK = 64
CONV_W = 4
LRU_C = 8.0
D_FF = 5632
DEPTH = 2
ALPHA = (2.0 * DEPTH) ** 0.25
SCALE = HEAD_DIM ** -0.5
NEG = -1e30
KV_W = 2 * N_KV * HEAD_DIM

C_MG = 0
C_Q = 4096
C_RX = 5120
C_RY = 6144
C_KVC = 7168
C_KVS = 7680
C_KVW = 8192
C_NG = 8704
N_PACK = 9216

VMEM_LIMIT = 56 * 1024 * 1024

TQ = 128
TK = 256
N_PAGES = 16


def _cparams(sem):
    return pltpu.CompilerParams(dimension_semantics=sem, vmem_limit_bytes=VMEM_LIMIT)


def _sigmoid(x):
    return 1.0 / (1.0 + jnp.exp(-x))


def _silu(x):
    return x * _sigmoid(x)


def _layer_norm(z, g, b):
    mu = jnp.mean(z, axis=-1, keepdims=True)
    zc = z - mu
    var = jnp.mean(zc * zc, axis=-1, keepdims=True)
    return zc * lax.rsqrt(var + 1e-5) * g + b


def _dot(a, b):
    return jnp.dot(a, b, preferred_element_type=f32)


def _dot_nt(a, b):
    return lax.dot_general(a, b, (((1,), (1,)), ((), ())), preferred_element_type=f32)


def _split3(x):
    hi = x.astype(bf16)
    r1 = x - hi.astype(f32)
    mid = r1.astype(bf16)
    lo = (r1 - mid.astype(f32)).astype(bf16)
    return hi, mid, lo


def _rows(mod, tm):
    mr = mod.shape[0]
    if mr == 1 or mr == tm:
        return mod
    return jnp.concatenate([mod] * (tm // mr), axis=0)


def _mod_spec(mod, k, tiles_per_block, ngrid):
    mr = mod.shape[1]
    if ngrid == 1:
        return pl.BlockSpec((None, mr, D_MODEL), lambda i: (i // tiles_per_block, 0, k))
    return pl.BlockSpec((None, mr, D_MODEL), lambda i, j: (i // tiles_per_block, 0, k))


def _ada_kernel(c_ref, w_ref, b_ref, o_ref):
    h = _silu(c_ref[...]).astype(bf16)
    o_ref[...] = _dot(h, w_ref[...].astype(bf16)) + b_ref[...]


def _ada(c_all, w, b, layer):
    m = c_all.shape[0]
    n = w.shape[2]
    tn = 1024
    return pl.pallas_call(
        _ada_kernel,
        grid=(n // tn,),
        in_specs=[pl.BlockSpec((m, D_MODEL), lambda j: (0, 0)),
                  pl.BlockSpec((None, D_MODEL, tn), lambda j: (layer, 0, j)),
                  pl.BlockSpec((None, 1, tn), lambda j: (layer, 0, j))],
        out_specs=pl.BlockSpec((m, tn), lambda j: (0, j)),
        out_shape=jax.ShapeDtypeStruct((m, n), f32),
        compiler_params=_cparams(("arbitrary",)),
        name="ada",
    )(c_all, w, b)


def _bias_lookup_kernel(tbl_ref, bk_ref, o_ref):
    bk = bk_ref[...]
    for h in range(N_HEADS):
        acc = jnp.zeros(bk.shape, f32)
        for k in range(N_BUCKETS):
            acc = jnp.where(bk == k, tbl_ref[k, h], acc)
        o_ref[h] = acc


def _bias_lookup(tbl, buckets):
    n = buckets.shape[0]
    tr = 256
    return pl.pallas_call(
        _bias_lookup_kernel,
        grid=(n // tr,),
        in_specs=[pl.BlockSpec(memory_space=pltpu.SMEM),
                  pl.BlockSpec((tr, 128), lambda i: (i, 0))],
        out_specs=pl.BlockSpec((N_HEADS, tr, 128), lambda i: (0, i, 0)),
        out_shape=jax.ShapeDtypeStruct((N_HEADS, n, 128), f32),
        compiler_params=_cparams(("arbitrary",)),
        name="bias_lookup",
    )(tbl, buckets)


def _t5_bucket_np(dist):
    n = np.maximum(dist, 0)
    exact = N_BUCKETS // 2
    ratio = np.maximum(n, 1).astype(np.float32) / np.float32(exact)
    log_ratio = np.log(ratio).astype(np.float32) / np.float32(math.log(MAX_DISTANCE / exact))
    large = np.minimum(exact + (log_ratio * np.float32(N_BUCKETS - exact)).astype(np.int32), N_BUCKETS - 1)
    return np.where(n < exact, n, large).astype(np.int32)


IN_TN = 512
KV_J0 = C_KVC // IN_TN


def _inproj_kernel(x_ref, sc_ref, sh_ref, w_ref, b_ref, o_ref, okt_ref, h_scr):
    j = pl.program_id(1)

    @pl.when(j == 0)
    def _():
        tm = x_ref.shape[0]
        h_scr[...] = (x_ref[...] * (1.0 + _rows(sc_ref[...], tm)) + _rows(sh_ref[...], tm)).astype(bf16)

    res = _dot(h_scr[...], w_ref[...]) + b_ref[...]
    o_ref[...] = res

    @pl.when(jnp.logical_and(j >= KV_J0, j < KV_J0 + 3))
    def _():
        okt_ref[...] = res.T


def _inproj(x, mod, w, b, layer, tm, seq_rows):
    m = x.shape[0]
    n = w.shape[2]
    tn = IN_TN
    tpb = (m // tm) // mod.shape[0]
    tps = seq_rows // tm
    kt_spec = pl.BlockSpec((None, tn, tm), lambda i, j: (i // tps, jnp.clip(j - KV_J0, 0, 2), i % tps))
    return pl.pallas_call(
        _inproj_kernel,
        grid=(m // tm, n // tn),
        in_specs=[pl.BlockSpec((tm, D_MODEL), lambda i, j: (i, 0)),
                  _mod_spec(mod, 1, tpb, 2), _mod_spec(mod, 0, tpb, 2),
                  pl.BlockSpec((None, D_MODEL, tn), lambda i, j: (layer, 0, j)),
                  pl.BlockSpec((None, 1, tn), lambda i, j: (layer, 0, j))],
        out_specs=[pl.BlockSpec((tm, tn), lambda i, j: (i, j)), kt_spec],
        out_shape=[jax.ShapeDtypeStruct((m, n), f32),
                   jax.ShapeDtypeStruct((m // seq_rows, 3 * tn, seq_rows), f32)],
        scratch_shapes=[pltpu.VMEM((tm, D_MODEL), bf16)],
        compiler_params=_cparams(("parallel", "arbitrary")),
        name="inproj",
    )(x, mod, mod, w, b)


def _cmp_bias_kernel(pe_ref, wlt_ref, o_ref):
    for kv in range(2):
        acc = jnp.zeros((8, 512), f32)
        for l in range(CMP_STRIDE):
            w = wlt_ref[kv, l]
            lead = _dot(jnp.broadcast_to(pe_ref[kv, 0, l:l + 1, :], (8, 256)).astype(bf16), w[:, 0:256])
            tail = _dot(jnp.broadcast_to(pe_ref[kv, 1, l:l + 1, :], (8, 256)).astype(bf16), w[:, 256:512])
            acc = acc + jnp.concatenate([lead, tail], axis=1)
        o_ref[kv] = acc


def _cmp_bias(pe, wlt):
    return pl.pallas_call(
        _cmp_bias_kernel,
        out_shape=jax.ShapeDtypeStruct((2, 8, 512), f32),
        compiler_params=pltpu.CompilerParams(vmem_limit_bytes=VMEM_LIMIT),
        name="cmp_bias",
    )(pe, wlt)


def _compress_chunks(piece, n_rows, kv, wlt_ref, cb_ref, w2_ref):
    acc = jnp.zeros((n_rows, 512), f32)
    for l in range(CMP_STRIDE):
        acc = acc + _dot(piece(l), wlt_ref[kv, l])
    acc = acc + cb_ref[kv, 0:1, :]
    z = acc[:, 0:256] + pltpu.roll(acc[:, 256:512], n_rows - 1, 0)
    return _dot(_silu(z).astype(bf16), w2_ref[kv])


def _compress_prompt_kernel(x_ref, wlt_ref, cb_ref, w2_ref, kc_ref, vc_ref):
    n_chunk = x_ref.shape[0]
    for kv, dst in enumerate((kc_ref, vc_ref)):
        piece = lambda l: x_ref[:, l * KV_W + kv * 256:l * KV_W + (kv + 1) * 256].astype(bf16)
        out = _compress_chunks(piece, n_chunk, kv, wlt_ref, cb_ref, w2_ref)
        row = lax.broadcasted_iota(i32, out.shape, 0)
        dst[...] = jnp.where(row < n_chunk - 1, out, 0.0)


def _compress_prompt(rows, wlt, cb, w2, layer):
    nb, n_chunk, width = rows.shape
    lsel = lambda a: pl.BlockSpec((None,) + a.shape[1:], lambda b: (layer,) + (0,) * (a.ndim - 1))
    return pl.pallas_call(
        _compress_prompt_kernel,
        grid=(nb,),
        in_specs=[pl.BlockSpec((None, n_chunk, width), lambda b: (b, 0, 0)),
                  lsel(wlt), pl.BlockSpec(cb.shape, lambda b: (0, 0, 0)), lsel(w2)],
        out_specs=[pl.BlockSpec((None, n_chunk, 256), lambda b: (b, 0, 0))] * 2,
        out_shape=[jax.ShapeDtypeStruct((nb, n_chunk, 256), f32)] * 2,
        compiler_params=_cparams(("arbitrary",)),
        name="compress_prompt",
    )(rows, wlt, cb, w2)


def _topk_mask_rows(imp, sidx, n_keep):
    n_rows = imp.shape[0]
    rank = jnp.zeros(imp.shape, i32)
    for j in range(n_rows):
        row = imp[j:j + 1, :]
        beats = jnp.where(row > imp, 1, jnp.where(row == imp, jnp.where(sidx > j, 1, 0), 0))
        rank = rank + beats
    return jnp.where(rank < n_keep, jnp.where(imp > -1e29, 1.0, 0.0), 0.0)


def _nsa_prompt_kernel(q_ref, ng_ref, ks_ref, vs_ref, kw_ref, vw_ref, kc_ref, vc_ref,
                       bc_ref, bt_ref, ct_ref, e_ref, o_ref, selk_scr, s_scr, mx_scr):
    i = pl.program_id(1)
    q0 = i * TQ
    rows = GROUP * TQ
    n_sel_blk = ks_ref.shape[0] // SEL_LEN

    d0 = lax.broadcasted_iota(i32, (TQ, TK), 0) - lax.broadcasted_iota(i32, (TQ, TK), 1)
    dist_c = (q0 + (lax.broadcasted_iota(i32, (rows, 128), 0) & (TQ - 1))
              - CMP_STRIDE * lax.broadcasted_iota(i32, (rows, 128), 1) - (CMP_LEN - 1))
    mask_c = dist_c >= 0
    sig = _sigmoid(ng_ref[...])

    sidx = lax.broadcasted_iota(i32, (n_sel_blk, TQ), 0)
    cur = (q0 + lax.broadcasted_iota(i32, (n_sel_blk, TQ), 1)) // SEL_LEN
    forced = jnp.where(sidx == 0, 1, jnp.where(sidx == cur, 1, jnp.where(sidx == cur - 1, 1, 0)))

    def branch(qs, g, k_ref, v_ref, kt_hi, window):
        n_tiles = k_ref.shape[0] // TK
        ones = jnp.ones((TK, HEAD_DIM), f32)

        def score_tile(kt, slot):
            k0 = pl.multiple_of(jnp.clip(kt, 0, n_tiles - 1) * TK, TK)
            k = k_ref[pl.ds(k0, TK), g * 64:(g + 1) * 64].astype(bf16)
            delta = q0 - kt * TK
            dist = d0 + delta
            if window:
                live = jnp.where(kt >= 0, 0.0, NEG)
                addm = jnp.where(dist >= 0, jnp.where(dist < WINDOW, live, NEG), NEG)
            else:
                addm = jnp.where(dist >= 0, selk_scr[:, pl.ds(k0, TK)], NEG)
            va = jnp.clip(delta, 0, 256) // 128
            vb = jnp.clip(delta - 128, 0, 256) // 128
            bias = jnp.concatenate([bt_ref[g * 3 + va], bt_ref[g * 3 + vb]], axis=1)
            s = _dot_nt(qs, k) + bias + jnp.concatenate([addm] * GROUP, axis=0)
            s_scr[:, pl.ds(slot * TK, TK)] = s
            return jnp.maximum(s[:, 0:128], s[:, 128:256])

        def value_tile(kt, slot, mb):
            k0 = pl.multiple_of(jnp.clip(kt, 0, n_tiles - 1) * TK, TK)
            v = jnp.concatenate([v_ref[pl.ds(k0, TK), g * 64:(g + 1) * 64], ones], axis=1).astype(bf16)
            p = jnp.exp(s_scr[:, pl.ds(slot * TK, TK)] - jnp.concatenate([mb, mb], axis=1))
            return _dot(p.astype(bf16), v)

        if window:
            tiles = [(kt_hi - 2 + j, j) for j in range(3)]
            mx = jnp.full((rows, 128), NEG, f32)
            for kt, slot in tiles:
                mx = jnp.maximum(mx, score_tile(kt, slot))
            mb = jnp.broadcast_to(jnp.max(mx, axis=-1, keepdims=True), (rows, 128))
            acc = jnp.zeros((rows, 2 * HEAD_DIM), f32)
            for kt, slot in tiles:
                acc = acc + value_tile(kt, slot, mb)
        else:
            n_quads = kt_hi // 4 + 1

            def scores(qd, mx):
                for j in range(4):
                    mx = jnp.maximum(mx, score_tile(4 * qd + j, 4 * qd + j))
                return mx

            mx = lax.fori_loop(0, n_quads, scores, jnp.full((rows, 128), NEG, f32))
            mx_scr[...] = jnp.broadcast_to(jnp.max(mx, axis=-1, keepdims=True), (rows, 128))

            def values(qd, acc):
                mb = mx_scr[...]
                for j in range(4):
                    acc = acc + value_tile(4 * qd + j, 4 * qd + j, mb)
                return acc

            acc = lax.fori_loop(0, n_quads, values, jnp.zeros((rows, 2 * HEAD_DIM), f32))
        return acc[:, 0:HEAD_DIM] / jnp.maximum(acc[:, HEAD_DIM:HEAD_DIM + 1], 1e-30)

    for g in range(N_KV):
        qg = q_ref[:, g * 256:(g + 1) * 256]
        qs = (jnp.concatenate([qg[:, r * 64:(r + 1) * 64] for r in range(GROUP)], axis=0) * SCALE).astype(bf16)

        kcg = kc_ref[:, g * 64:(g + 1) * 64].astype(bf16)
        vcg = vc_ref[:, g * 64:(g + 1) * 64].astype(bf16)
        bias_c = jnp.concatenate([bc_ref[g * GROUP + r] for r in range(GROUP)], axis=0)
        s = jnp.where(mask_c, _dot_nt(qs, kcg) + bias_c, NEG)
        p = jnp.where(mask_c, jnp.exp(s - jnp.max(s, axis=-1, keepdims=True)), 0.0)
        p = p / jnp.maximum(jnp.sum(p, axis=-1, keepdims=True), 1e-30)
        o_c = _dot(p.astype(bf16), vcg)

        p_sum = p[0:TQ] + p[TQ:2 * TQ] + p[2 * TQ:3 * TQ] + p[3 * TQ:4 * TQ]
        ct = ct_ref[...]
        imp_t = sum(_dot_nt(ct, piece) for piece in _split3(p_sum))[0:n_sel_blk]
        imp_t = jnp.where(forced > 0, imp_t + FORCE_BONUS, imp_t)
        imp_t = jnp.where(sidx <= cur, imp_t, NEG)
        sel_t = _topk_mask_rows(imp_t, sidx, N_SEL)
        sel_pad = jnp.concatenate([sel_t, jnp.zeros((128 - n_sel_blk, TQ), f32)], axis=0)
        sel_q = sel_pad.T.astype(bf16)
        selk_scr[...] = jnp.where(_dot(sel_q, e_ref[...]) > 0.5, 0.0, NEG)

        kt_hi = (q0 + TQ - 1) // TK
        o_s = branch(qs, g, ks_ref, vs_ref, kt_hi, False)
        o_w = branch(qs, g, kw_ref, vw_ref, kt_hi, True)

        gate = lambda c: jnp.concatenate(
            [sig[:, g * 12 + r * 3 + c:g * 12 + r * 3 + c + 1] for r in range(GROUP)], axis=0)
        o = gate(0) * o_c + gate(1) * o_s + gate(2) * o_w
        o_ref[:, g * 256:(g + 1) * 256] = jnp.concatenate(
            [o[r * TQ:(r + 1) * TQ] for r in range(GROUP)], axis=1).astype(bf16)


def _nsa_prompt(proj, kc, vc, bias_c, bias_t, ct, e, nb, t):
    nq = t // TQ
    full = lambda a: pl.BlockSpec(a.shape, lambda b, i: (0,) * a.ndim)
    kv = lambda c: pl.BlockSpec((t, 256), lambda b, i: (b, c // 256))
    return pl.pallas_call(
        _nsa_prompt_kernel,
        grid=(nb, nq),
        in_specs=[pl.BlockSpec((TQ, 1024), lambda b, i: (b * nq + i, C_Q // 1024)),
                  pl.BlockSpec((TQ, 128), lambda b, i: (b * nq + i, C_NG // 128)),
                  kv(C_KVS), kv(C_KVS + 256), kv(C_KVW), kv(C_KVW + 256),
                  pl.BlockSpec((None, t // CMP_STRIDE, 256), lambda b, i: (b, 0, 0)),
                  pl.BlockSpec((None, t // CMP_STRIDE, 256), lambda b, i: (b, 0, 0)),
                  pl.BlockSpec((N_HEADS, TQ, 128), lambda b, i: (0, i, 0)),
                  full(bias_t), full(ct), full(e)],
        out_specs=pl.BlockSpec((TQ, 1024), lambda b, i: (b * nq + i, 0)),
        out_shape=jax.ShapeDtypeStruct((nb * t, 1024), bf16),
        scratch_shapes=[pltpu.VMEM((TQ, t), f32), pltpu.VMEM((GROUP * TQ, t), f32),
                        pltpu.VMEM((GROUP * TQ, 128), f32)],
        compiler_params=_cparams(("parallel", "arbitrary")),
        name="nsa_prompt",
    )(proj, proj, proj, proj, proj, proj, kc, vc, bias_c, bias_t, ct, e)


def _lru_gates(xc, ry, wa_ref, ba_ref, wi_ref, bi_ref, lam_ref):
    xb = xc.astype(bf16)
    ra = jnp.concatenate([_dot(xb[:, c * 256:(c + 1) * 256], wa_ref[c]) for c in range(4)], axis=1)
    ri = jnp.concatenate([_dot(xb[:, c * 256:(c + 1) * 256], wi_ref[c]) for c in range(4)], axis=1)
    r = _sigmoid(ra + ba_ref[...])
    ig = _sigmoid(ri + bi_ref[...])
    nl = -lam_ref[...]
    softplus = jnp.maximum(nl, 0.0) + jnp.log1p(jnp.exp(-jnp.abs(nl)))
    log_a = -LRU_C * r * softplus
    a = jnp.exp(log_a)
    u = jnp.sqrt(jnp.tanh(-log_a) * (a * a + 1.0)) * (ig * xc)
    return a, u, jax.nn.gelu(ry)


def _rglru_prompt_kernel(rx_ref, ry_ref, cw_ref, cb_ref, wa_ref, ba_ref, wi_ref, bi_ref, lam_ref,
                         o_ref, hl_ref, xp_scr, a_scr, u_scr, hs_scr, h_scr):
    tt = pl.program_id(1)
    tr = rx_ref.shape[0]

    @pl.when(tt == 0)
    def _():
        xp_scr[0:8, :] = jnp.zeros((8, D_RNN), f32)
        h_scr[...] = jnp.zeros((1, D_RNN), f32)

    x = rx_ref[...]
    xp_scr[8:8 + tr, :] = x
    y = x * cw_ref[CONV_W - 1:CONV_W, :] + cb_ref[...]
    for k in range(CONV_W - 1):
        y = y + xp_scr[5 + k:5 + k + tr, :] * cw_ref[k:k + 1, :]
    xp_scr[0:8, :] = xp_scr[tr:tr + 8, :]

    a, u, gate = _lru_gates(y, ry_ref[...], wa_ref, ba_ref, wi_ref, bi_ref, lam_ref)
    a_scr[...] = a
    u_scr[...] = u

    def step(t, h):
        h = a_scr[pl.ds(t, 1), :] * h + u_scr[pl.ds(t, 1), :]
        hs_scr[pl.ds(t, 1), :] = h
        return h

    h = lax.fori_loop(0, tr, step, h_scr[...], unroll=8)
    h_scr[...] = h
    hl_ref[...] = h
    o_ref[...] = (hs_scr[...] * gate).astype(bf16)


def _rnn_specs(rnn_w, layer, ngrid):
    zeros = lambda n: (0,) * n
    if ngrid == 2:
        return [pl.BlockSpec((None,) + a.shape[1:], lambda b, i, n=a.ndim - 1: (layer,) + zeros(n)) for a in rnn_w]
    return [pl.BlockSpec((None,) + a.shape[1:], lambda i, n=a.ndim - 1: (layer,) + zeros(n)) for a in rnn_w]


def _rglru_prompt(proj, nb, t, rnn_w, layer):
    tr = 256
    nt = t // tr
    return pl.pallas_call(
        _rglru_prompt_kernel,
        grid=(nb, nt),
        in_specs=[pl.BlockSpec((tr, D_RNN), lambda b, i: (b * nt + i, C_RX // D_RNN)),
                  pl.BlockSpec((tr, D_RNN), lambda b, i: (b * nt + i, C_RY // D_RNN))]
        + _rnn_specs(rnn_w, layer, 2),
        out_specs=[pl.BlockSpec((tr, D_RNN), lambda b, i: (b * nt + i, 0)),
                   pl.BlockSpec((None, 1, D_RNN), lambda b, i: (b, 0, 0))],
        out_shape=[jax.ShapeDtypeStruct((nb * t, D_RNN), bf16),
                   jax.ShapeDtypeStruct((nb, 1, D_RNN), f32)],
        scratch_shapes=[pltpu.VMEM((tr + 8, D_RNN), f32), pltpu.VMEM((tr, D_RNN), f32),
                        pltpu.VMEM((tr, D_RNN), f32), pltpu.VMEM((tr, D_RNN), f32),
                        pltpu.VMEM((1, D_RNN), f32)],
        compiler_params=_cparams(("parallel", "arbitrary")),
        name="rglru_prompt",
    )(proj, proj, *rnn_w)


def _rglru_sample_kernel(rx_ref, ry_ref, cbuf_ref, h0_ref, cw_ref, cb_ref, wa_ref, ba_ref, wi_ref, bi_ref,
                         lam_ref, o_ref, hl_ref):
    ns = h0_ref.shape[0]
    n_t = rx_ref.shape[0] // ns
    xp = [cbuf_ref[k] for k in range(CONV_W - 1)] + [rx_ref[s * ns:(s + 1) * ns, :] for s in range(n_t)]
    h = h0_ref[...]
    for s in range(n_t):
        y = xp[s + CONV_W - 1] * cw_ref[CONV_W - 1:CONV_W, :] + cb_ref[...]
        for k in range(CONV_W - 1):
            y = y + xp[s + k] * cw_ref[k:k + 1, :]
        a, u, gate = _lru_gates(y, ry_ref[s * ns:(s + 1) * ns, :], wa_ref, ba_ref, wi_ref, bi_ref, lam_ref)
        h = a * h + u
        o_ref[s * ns:(s + 1) * ns, :] = (h * gate).astype(bf16)
    hl_ref[...] = h


def _rglru_sample(proj, cbuf, h0, rnn_w, layer):
    ms = proj.shape[0]
    ns = h0.shape[1]
    return pl.pallas_call(
        _rglru_sample_kernel,
        grid=(1,),
        in_specs=[pl.BlockSpec((ms, D_RNN), lambda i: (0, C_RX // D_RNN)),
                  pl.BlockSpec((ms, D_RNN), lambda i: (0, C_RY // D_RNN)),
                  pl.BlockSpec((None, CONV_W - 1, ns, D_RNN), lambda i: (layer, 0, 0, 0)),
                  pl.BlockSpec((None, ns, D_RNN), lambda i: (layer, 0, 0))]
        + _rnn_specs(rnn_w, layer, 1),
        out_specs=[pl.BlockSpec((ms, D_RNN), lambda i: (0, 0)),
                   pl.BlockSpec((ns, D_RNN), lambda i: (0, 0))],
        out_shape=[jax.ShapeDtypeStruct((ms, D_RNN), bf16),
                   jax.ShapeDtypeStruct((ns, D_RNN), f32)],
        compiler_params=_cparams(("arbitrary",)),
        name="rglru_sample",
    )(proj, proj, cbuf, h0, *rnn_w)


def _merge_kernel(oa_ref, orn_ref, ga_ref, gr_ref, wa_ref, wr_ref, o_ref):
    a1 = _dot(oa_ref[...], wa_ref[...])
    a2 = _dot(orn_ref[...], wr_ref[...])
    o_ref[...] = (_sigmoid(ga_ref[...]) * a1 + _sigmoid(gr_ref[...]) * a2).astype(bf16)


def _merge(o_attn, o_rnn, proj, w_ao, w_ro, layer, tm):
    m = o_attn.shape[0]
    tn = 512
    nn = D_MODEL // tn
    return pl.pallas_call(
        _merge_kernel,
        grid=(m // tm, nn),
        in_specs=[pl.BlockSpec((tm, 1024), lambda i, j: (i, 0)),
                  pl.BlockSpec((tm, 1024), lambda i, j: (i, 0)),
                  pl.BlockSpec((tm, tn), lambda i, j: (i, j)),
                  pl.BlockSpec((tm, tn), lambda i, j: (i, nn + j)),
                  pl.BlockSpec((None, 1024, tn), lambda i, j: (layer, 0, j)),
                  pl.BlockSpec((None, 1024, tn), lambda i, j: (layer, 0, j))],
        out_specs=pl.BlockSpec((tm, tn), lambda i, j: (i, j)),
        out_shape=jax.ShapeDtypeStruct((m, D_MODEL), bf16),
        compiler_params=_cparams(("parallel", "arbitrary")),
        name="merge",
    )(o_attn, o_rnn, proj, proj, w_ao, w_ro)


def _outproj_kernel(a_ref, x_ref, gt_ref, w_ref, lg_ref, lb_ref, o_ref):
    mix = _dot(a_ref[...], w_ref[...])
    z = ALPHA * x_ref[...] + (1.0 + _rows(gt_ref[...], x_ref.shape[0])) * mix
    o_ref[...] = _layer_norm(z, lg_ref[...], lb_ref[...])


def _outproj(a, x, mod, w, ln_g, ln_b, layer, tm):
    m = a.shape[0]
    tpb = (m // tm) // mod.shape[0]
    ln_spec = pl.BlockSpec((None, None, 1, D_MODEL), lambda i: (layer, 0, 0, 0))
    return pl.pallas_call(
        _outproj_kernel,
        grid=(m // tm,),
        in_specs=[pl.BlockSpec((tm, D_MODEL), lambda i: (i, 0)),
                  pl.BlockSpec((tm, D_MODEL), lambda i: (i, 0)),
                  _mod_spec(mod, 2, tpb, 1),
                  pl.BlockSpec((None, D_MODEL, D_MODEL), lambda i: (layer, 0, 0)),
                  ln_spec, ln_spec],
        out_specs=pl.BlockSpec((tm, D_MODEL), lambda i: (i, 0)),
        out_shape=jax.ShapeDtypeStruct((m, D_MODEL), f32),
        compiler_params=_cparams(("parallel",)),
        name="outproj_ln",
    )(a, x, mod, w, ln_g, ln_b)


def _ffn_kernel(x_ref, sc_ref, sh_ref, gt_ref, wg_ref, wu_ref, wo_ref, lg_ref, lb_ref, o_ref, h_scr):
    j = pl.program_id(1)
    tm = x_ref.shape[0]

    @pl.when(j == 0)
    def _():
        h_scr[...] = (x_ref[...] * (1.0 + _rows(sc_ref[...], tm)) + _rows(sh_ref[...], tm)).astype(bf16)
        o_ref[...] = jnp.zeros(o_ref.shape, f32)

    h = h_scr[...]
    act = (_silu(_dot(h, wg_ref[...])) * _dot(h, wu_ref[...])).astype(bf16)
    o_ref[...] += _dot(act, wo_ref[...])

    @pl.when(j == pl.num_programs(1) - 1)
    def _():
        z = ALPHA * x_ref[...] + (1.0 + _rows(gt_ref[...], tm)) * o_ref[...]
        o_ref[...] = _layer_norm(z, lg_ref[...], lb_ref[...])


def _ffn(x, mod, w_in, w_out, ln_g, ln_b, layer, tm):
    m = x.shape[0]
    tf = 512
    nf = D_FF // tf
    tpb = (m // tm) // mod.shape[0]
    ln_spec = pl.BlockSpec((None, None, 1, D_MODEL), lambda i, j: (layer, 1, 0, 0))
    return pl.pallas_call(
        _ffn_kernel,
        grid=(m // tm, nf),
        in_specs=[pl.BlockSpec((tm, D_MODEL), lambda i, j: (i, 0), pipeline_mode=pl.Buffered(1)),
                  _mod_spec(mod, 4, tpb, 2), _mod_spec(mod, 3, tpb, 2), _mod_spec(mod, 5, tpb, 2),
                  pl.BlockSpec((None, D_MODEL, tf), lambda i, j: (layer, 0, j)),
                  pl.BlockSpec((None, D_MODEL, tf), lambda i, j: (layer, 0, nf + j)),
                  pl.BlockSpec((None, tf, D_MODEL), lambda i, j: (layer, j, 0)),
                  ln_spec, ln_spec],
        out_specs=pl.BlockSpec((tm, D_MODEL), lambda i, j: (i, 0)),
        out_shape=jax.ShapeDtypeStruct((m, D_MODEL), f32),
        scratch_shapes=[pltpu.VMEM((tm, D_MODEL), bf16)],
        compiler_params=_cparams(("parallel", "arbitrary")),
        name="ffn",
    )(x, mod, mod, mod, w_in, w_in, w_out, ln_g, ln_b)


CMP_BATCH = 4


def _compress_sample_kernel(pt_ref, newc_ref, *rest):
    n_pg = CMP_BATCH * N_PAGES
    pages = rest[0:n_pg]
    wlt_ref, cb_ref, w2_ref, selp_ref, o_ref, x_scr = rest[n_pg:]
    n_new = newc_ref.shape[1]
    per = N_PAGES * (PAGE // CMP_STRIDE) + 8
    row8 = lax.broadcasted_iota(i32, (8, 256), 0)
    selp = selp_ref[...]
    for s in range(CMP_BATCH):
        for p in range(N_PAGES):
            y = _dot_nt(selp, pages[s * N_PAGES + p][...].astype(bf16))
            for l in range(CMP_STRIDE):
                x_scr[l, s * per + p * 8:s * per + (p + 1) * 8, :] = y[l * 8:(l + 1) * 8, :]
        for l in range(CMP_STRIDE):
            if l < n_new:
                extra = jnp.where(row8 == 0, jnp.broadcast_to(newc_ref[s, l:l + 1, :], (8, 256)), 0.0)
            else:
                extra = jnp.zeros((8, 256), f32)
            x_scr[l, s * per + per - 8:s * per + per, :] = extra
    acc = jnp.zeros((CMP_BATCH * per, 512), f32)
    for l in range(CMP_STRIDE):
        acc = acc + _dot(x_scr[l].astype(bf16), wlt_ref[l])
    acc = acc + cb_ref[0:1, :]
    for s in range(CMP_BATCH):
        a = acc[s * per:(s + 1) * per]
        z = a[:, 0:256] + pltpu.roll(a[:, 256:512], per - 1, 0)
        o_ref[s] = _dot(_silu(z).astype(bf16), w2_ref[...])[0:per - 8].astype(bf16)


def _compress_sample(page_table, newc, cache_c, layer, n_phys, wlt, cb, w2, selp):
    ns, n_new, _ = newc.shape
    per = N_PAGES * (PAGE // CMP_STRIDE) + 8

    def page_spec(s, p):
        return pl.BlockSpec((None, 256, PAGE),
                            lambda i, kv, pt: (layer * n_phys + pt[i * CMP_BATCH + s, p], kv, 0))

    in_specs = ([pl.BlockSpec((CMP_BATCH, n_new, 256), lambda i, kv, pt: (i, 0, kv))]
                + [page_spec(s, p) for s in range(CMP_BATCH) for p in range(N_PAGES)]
                + [pl.BlockSpec((None, None, CMP_STRIDE, 256, 512), lambda i, kv, pt: (layer, kv, 0, 0, 0)),
                   pl.BlockSpec((None, 8, 512), lambda i, kv, pt: (kv, 0, 0)),
                   pl.BlockSpec((None, None, 256, 256), lambda i, kv, pt: (layer, kv, 0, 0)),
                   pl.BlockSpec(selp.shape, lambda i, kv, pt: (0, 0))])
    grid_spec = pltpu.PrefetchScalarGridSpec(
        num_scalar_prefetch=1,
        grid=(ns // CMP_BATCH, 2),
        in_specs=in_specs,
        out_specs=pl.BlockSpec((None, CMP_BATCH, per - 8, 256), lambda i, kv, pt: (kv, i, 0, 0)),
        scratch_shapes=[pltpu.VMEM((CMP_STRIDE, CMP_BATCH * per, 256), f32)],
    )
    return pl.pallas_call(
        _compress_sample_kernel,
        grid_spec=grid_spec,
        out_shape=jax.ShapeDtypeStruct((2, ns, per - 8, 256), bf16),
        compiler_params=_cparams(("arbitrary", "arbitrary")),
        name="compress_sample",
    )(page_table, newc, *([cache_c] * (CMP_BATCH * N_PAGES)), wlt, cb, w2, selp)


ATT_BATCH = 2


def _nsa_sample_kernel(pt_ref, qrow_ref, ng_ref, kc_ref, vc_ref, news_ref, neww_ref, *rest):
    n_pg = ATT_BATCH * N_PAGES
    cs = rest[0:n_pg]
    (win_ref, bc_ref, mc_ref, bs_ref, ms_ref, bw_ref, mw_ref,
     smat_ref, smt_ref, c_ref, e_ref, dsel_ref, o_ref, new_scr) = rest[n_pg:]
    n_new = news_ref.shape[1]
    dsel = dsel_ref[...]
    mc = mc_ref[...]
    smat = smat_ref[...]
    cmat = c_ref[...]

    @pl.when(pl.program_id(0) == 0)
    def _():
        new_scr[...] = jnp.zeros(new_scr.shape, f32)

    def diag(x):
        y = x * dsel
        return y[:, 0:64] + y[:, 64:128] + y[:, 128:192] + y[:, 192:256]

    def one_sample(b):
        qs = (qrow_ref[b] * SCALE).astype(bf16)
        kc = kc_ref[b]
        vc = vc_ref[b]

        s = jnp.where(mc > 0.5, _dot_nt(qs, kc) + bc_ref[...], NEG)
        p = jnp.exp(s - jnp.max(s, axis=-1, keepdims=True)) * mc
        p = p / jnp.maximum(jnp.sum(p, axis=-1, keepdims=True), 1e-30)
        o_c = diag(_dot(p.astype(bf16), vc))

        p_sum = sum(_dot(smat, piece_) for piece_ in _split3(p))
        imp = sum(_dot(piece_, cmat) for piece_ in _split3(p_sum))
        n_blk = (N_PAGES * PAGE + n_new + SEL_LEN - 1) // SEL_LEN
        cur = (N_PAGES * PAGE) // SEL_LEN
        sidx = lax.broadcasted_iota(i32, imp.shape, 1)
        forced = jnp.where(sidx == 0, 1, jnp.where(sidx == cur, 1, jnp.where(sidx == cur - 1, 1, 0)))
        imp = jnp.where(forced > 0, imp + FORCE_BONUS, imp)
        imp = jnp.where(sidx <= cur, imp, NEG)
        rank = jnp.zeros(imp.shape, i32)
        for j in range(n_blk):
            col = imp[:, j:j + 1]
            rank = rank + jnp.where(col > imp, 1, jnp.where(col == imp, jnp.where(sidx > j, 1, 0), 0))
        sel = jnp.where(rank < N_SEL, jnp.where(imp > -1e29, jnp.where(sidx < n_blk, 1.0, 0.0), 0.0), 0.0)
        sel64 = _dot(smt_ref[...], sel.astype(bf16)).astype(bf16)
        selk = _dot(sel64, e_ref[...])

        def attend(kt_tiles, vt_tiles, new_rows, bias, mask):
            k_new, v_new = new_rows[:, 0:256].astype(bf16), new_rows[:, 256:512].astype(bf16)
            s = jnp.concatenate([_dot(qs, kt) for kt in kt_tiles] + [_dot_nt(qs, k_new)], axis=1) + bias
            s = jnp.where(mask > 0.5, s, NEG)
            p = jnp.exp(s - jnp.max(s, axis=-1, keepdims=True)) * mask
            p = (p / jnp.maximum(jnp.sum(p, axis=-1, keepdims=True), 1e-30)).astype(bf16)
            off = s.shape[1] - 128
            acc = _dot(p[:, off:off + 128], v_new)
            off = 0
            for vt in vt_tiles:
                w = vt.shape[1]
                acc = acc + _dot_nt(p[:, off:off + w], vt)
                off += w
            return diag(acc)

        new_scr[b, 0, 0:n_new, :] = news_ref[b]
        pages = cs[b * N_PAGES:(b + 1) * N_PAGES]
        k_tiles = [pg[0:256, :].astype(bf16) for pg in pages]
        v_tiles = [pg[256:512, :].astype(bf16) for pg in pages]
        o_s = attend(k_tiles, v_tiles, new_scr[b, 0], bs_ref[...], ms_ref[...] * selk)

        new_scr[b, 1, 0:n_new, :] = neww_ref[b]
        o_w = attend([win_ref[b, 0:256, :].astype(bf16)], [win_ref[b, 256:512, :].astype(bf16)],
                     new_scr[b, 1], bw_ref[...], mw_ref[...])

        sig = _sigmoid(ng_ref[b])
        o_ref[b] = sig[:, 0:1] * o_c + sig[:, 1:2] * o_s + sig[:, 2:3] * o_w

    for b in range(ATT_BATCH):
        one_sample(b)


def _nsa_sample(page_table, qrow, ng, kvc, news, neww, cache_s, win, layer, n_phys, consts):
    nb = qrow.shape[0]
    n_new = news.shape[1]
    ab = ATT_BATCH
    full = lambda a: pl.BlockSpec(a.shape, lambda i, pt: (0,) * a.ndim)
    per_b = lambda shape: pl.BlockSpec((ab,) + shape, lambda i, pt: (i, 0, 0))
    cmp_spec = lambda kv: pl.BlockSpec((None, ab) + kvc.shape[2:], lambda i, pt: (kv, i, 0, 0))

    def page_spec(s, p):
        return pl.BlockSpec((None, KV_W, PAGE), lambda i, pt: (layer * n_phys + pt[i * ab + s, p], 0, 0))

    in_specs = ([per_b((64, 256)), per_b((64, 3)), cmp_spec(0), cmp_spec(1),
                 per_b((n_new, KV_W)), per_b((n_new, KV_W))]
                + [page_spec(s, p) for s in range(ab) for p in range(N_PAGES)]
                + [pl.BlockSpec((ab, KV_W, WINDOW), lambda i, pt: ((layer * nb) // ab + i, 0, 0))]
                + [full(c) for c in consts])
    grid_spec = pltpu.PrefetchScalarGridSpec(
        num_scalar_prefetch=1,
        grid=(nb // ab,),
        in_specs=in_specs,
        out_specs=pl.BlockSpec((ab, 64, 64), lambda i, pt: (i, 0, 0)),
        scratch_shapes=[pltpu.VMEM((ab, 2, 128, KV_W), f32)],
    )
    return pl.pallas_call(
        _nsa_sample_kernel,
        grid_spec=grid_spec,
        out_shape=jax.ShapeDtypeStruct((nb, 64, 64), f32),
        compiler_params=_cparams(("arbitrary",)),
        name="nsa_sample",
    )(page_table, qrow, ng, kvc, kvc, news, neww, *([cache_s] * (ab * N_PAGES)), win, *consts)


def _static_tables(t, past, n_new):
    ar = np.arange
    tiles = np.stack([d + ar(128)[:, None] - ar(128)[None, :] for d in (0, 128, 256)])
    cmp_p = ar(t)[:, None] - (CMP_STRIDE * ar(128)[None, :] + CMP_LEN - 1)
    qpos = past + ar(n_new)
    cmp_s = qpos[:, None] - (CMP_STRIDE * ar(128)[None, :] + CMP_LEN - 1)
    kpos_s = np.concatenate([ar(past), past + ar(128)])
    slc_s = qpos[:, None] - kpos_s[None, :]
    kpos_w = np.concatenate([past - WINDOW + ar(WINDOW), past + ar(128)])
    win_s = qpos[:, None] - kpos_w[None, :]
    parts = [tiles.reshape(-1, 128), cmp_p, cmp_s.reshape(-1, 128), slc_s.reshape(-1, 128), win_s.reshape(-1, 128)]
    sizes = [p.shape[0] for p in parts]
    flat = np.concatenate(parts, axis=0)
    pad = (-flat.shape[0]) % 256
    flat = np.concatenate([flat, np.zeros((pad, 128), flat.dtype)], axis=0)
    buckets = _t5_bucket_np(flat)
    real_s = np.concatenate([np.ones(past, bool), ar(128) < n_new])
    real_w = np.concatenate([np.ones(WINDOW, bool), ar(128) < n_new])
    mask_c = (cmp_s >= 0)
    mask_s = (slc_s >= 0) & real_s[None, :]
    mask_w = (win_s >= 0) & (win_s < WINDOW) & real_w[None, :]
    rep = lambda mk: np.tile(mk[None].astype(np.float32), (N_HEADS, 1, 1)).reshape(N_HEADS * n_new, -1)
    return buckets, sizes, rep(mask_c), rep(mask_s), rep(mask_w)


def _cmp_to_sel_np(n_cmp, n_sel):
    j = np.arange(n_cmp)[:, None]
    s = np.arange(n_sel)[None, :]
    lo = np.maximum(j * CMP_STRIDE, s * SEL_LEN)
    hi = np.minimum(j * CMP_STRIDE + CMP_LEN, (s + 1) * SEL_LEN)
    return (np.maximum(hi - lo, 0) / CMP_LEN).astype(np.float32)


def _kron4(w):
    eye = jnp.eye(N_KV, dtype=w.dtype)
    out = jnp.einsum('gh,...de->...gdhe', eye, w)
    return out.reshape(w.shape[:-2] + (256, 256))


def _block_diag_rnn(w):
    w4 = w.reshape(DEPTH, 4, 4, RNN_BLOCK, RNN_BLOCK)
    eye = jnp.eye(4, dtype=w.dtype)
    return jnp.einsum('jk,zcjde->zcjdke', eye, w4).reshape(DEPTH, 4, 256, 256)


def _pack_in(w, b):
    cuts = np.cumsum([1024, 512, 512, 512, 48, 1024, 1024, 4096])[:-1]
    q, kvc, kvs, kvw, ng, rx, ry, mg = jnp.split(w.astype(bf16), cuts, axis=-1)
    bq, bkvc, bkvs, bkvw, bng, brx, bry, bmg = jnp.split(b, cuts, axis=-1)
    n_pad = N_PACK - (C_NG + 48)
    zw = jnp.zeros(w.shape[:-1] + (n_pad,), bf16)
    zb = jnp.zeros(b.shape[:-1] + (n_pad,), b.dtype)
    wp = jnp.concatenate([mg, q, rx, ry, kvc, kvs, kvw, ng, zw], axis=-1)
    bp = jnp.concatenate([bmg, bq, brx, bry, bkvc, bkvs, bkvw, bng, zb], axis=-1)
    return wp.astype(bf16), bp.reshape(DEPTH, 1, N_PACK)


def _feature_major(cache):
    d, n, rows = cache.shape[:3]
    return cache.transpose(0, 1, 3, 4, 5, 2).reshape(d * n, KV_W, rows)


def kernel(x_prompt, x_sample, c_prompt, c_sample, cache_cmp_kv, cache_slc_kv, cache_win_kv, state_conv, state_rnn_h, page_table, rel_bias, w_ada, b_ada, w_in, b_in, cmp_pos, cmp_w1, cmp_w2, w_attn_o, conv_w, conv_b, lru_wa, lru_ba, lru_wi, lru_bi, lru_lambda, w_rnn_o, w_out, w_ffn_in, w_ffn_out, ln_g, ln_b):
    nb, t, _ = x_prompt.shape
    ns, n_new, _ = x_sample.shape
    n_phys = cache_cmp_kv.shape[1]
    past = page_table.shape[1] * PAGE
    assert page_table.shape[1] == N_PAGES and cache_win_kv.shape[2] == WINDOW and t % (4 * TK) == 0
    assert n_new <= CMP_STRIDE and ns % 8 == 0 and ns % CMP_BATCH == 0 and ns % ATT_BATCH == 0
    mp, ms = nb * t, ns * n_new

    buckets, sizes, mask_c, mask_s, mask_w = _static_tables(t, past, n_new)
    bias_all = _bias_lookup(rel_bias, jnp.asarray(buckets))
    offs = np.cumsum([0] + sizes)
    seg = lambda k: bias_all[:, offs[k]:offs[k + 1]]
    bias_t = seg(0).reshape(N_KV, GROUP, 3, 128, 128).transpose(0, 2, 1, 3, 4).reshape(N_KV * 3, GROUP * 128, 128)
    bias_cp = seg(1)
    bias_cs = seg(2).reshape(N_HEADS * n_new, 128)
    bias_ss = seg(3).reshape(N_HEADS * n_new, past + 128)
    bias_ws = seg(4).reshape(N_HEADS * n_new, WINDOW + 128)

    n_sel_p = t // SEL_LEN
    ct_p = np.zeros((128, 128), np.float32)
    ct_p[:n_sel_p, :t // CMP_STRIDE - 1] = _cmp_to_sel_np(t // CMP_STRIDE - 1, n_sel_p).T
    e_p = (np.arange(128)[:, None] == (np.arange(t)[None, :] // SEL_LEN)).astype(np.float32)
    n_cmp_s = (past + n_new + CMP_STRIDE - 1) // CMP_STRIDE - 1
    n_sel_s = (past + n_new + SEL_LEN - 1) // SEL_LEN
    c_s = np.zeros((128, 128), np.float32)
    c_s[:n_cmp_s, :n_sel_s] = _cmp_to_sel_np(n_cmp_s, n_sel_s)
    e_s = (np.arange(128)[:, None] == (np.arange(past + 128)[None, :] // SEL_LEN)).astype(np.float32)
    hq = np.arange(N_HEADS * n_new)
    smat = ((hq[None, :] // (GROUP * n_new)) * n_new + hq[None, :] % n_new
            == np.arange(N_KV * n_new)[:, None]).astype(np.float32)
    dsel = (hq[:, None] // (GROUP * n_new) == np.arange(256)[None, :] // HEAD_DIM).astype(np.float32)
    pos = np.arange(PAGE)
    selp = ((pos % CMP_STRIDE) * (PAGE // CMP_STRIDE) + pos // CMP_STRIDE)[None, :] == np.arange(PAGE)[:, None]
    selp = jnp.asarray(selp, bf16)
    consts_s = (bias_cs, jnp.asarray(mask_c), bias_ss, jnp.asarray(mask_s), bias_ws,
                jnp.asarray(mask_w), jnp.asarray(smat, bf16), jnp.asarray(smat.T, bf16), jnp.asarray(c_s, bf16),
                jnp.asarray(e_s, bf16), jnp.asarray(dsel))

    w_in_p, b_in_p = _pack_in(w_in, b_in)
    w1 = cmp_w1.reshape(DEPTH, 2, 2, CMP_STRIDE, HEAD_DIM, HEAD_DIM)
    wlt = jnp.concatenate([_kron4(w1[:, :, 0]), _kron4(w1[:, :, 1])], axis=-1).astype(bf16)
    w2k = _kron4(cmp_w2).astype(bf16)
    pe = jnp.tile(cmp_pos.reshape(DEPTH, 2, 2, CMP_STRIDE, HEAD_DIM), (1, 1, 1, 1, N_KV))
    rnn_w = (conv_w, conv_b.reshape(DEPTH, 1, D_RNN), _block_diag_rnn(lru_wa).astype(bf16),
             lru_ba.reshape(DEPTH, 1, D_RNN), _block_diag_rnn(lru_wi).astype(bf16),
             lru_bi.reshape(DEPTH, 1, D_RNN), lru_lambda.reshape(DEPTH, 1, D_RNN))
    w_ao, w_ro, w_o = w_attn_o.astype(bf16), w_rnn_o.astype(bf16), w_out.astype(bf16)
    w_f1, w_f2 = w_ffn_in.astype(bf16), w_ffn_out.astype(bf16)
    ln_g4, ln_b4 = ln_g.reshape(DEPTH, 2, 1, D_MODEL), ln_b.reshape(DEPTH, 2, 1, D_MODEL)
    b_ada3 = b_ada.reshape(DEPTH, 1, 6 * D_MODEL)

    n_cpad = -(nb + ns) % 8
    c_all = jnp.concatenate([c_prompt, c_sample, jnp.zeros((n_cpad, D_MODEL), f32)], axis=0)
    cache_ct, cache_st, win_t = _feature_major(cache_cmp_kv), _feature_major(cache_slc_kv), _feature_major(cache_win_kv)
    cbuf_t = state_conv.transpose(0, 2, 1, 3)

    y_p = x_prompt.reshape(mp, D_MODEL)
    y_s = x_sample.transpose(1, 0, 2).reshape(ms, D_MODEL)
    st_p, st_s = [], []
    for l in range(DEPTH):
        ada = _ada(c_all, w_ada, b_ada3, l)
        mod_p = ada[:nb].reshape(nb, 1, 6 * D_MODEL)
        mod_s = ada[nb:nb + ns].reshape(1, ns, 6 * D_MODEL)
        cb = _cmp_bias(pe[l], wlt[l])

        proj, kvt = _inproj(y_p, mod_p, w_in_p, b_in_p, l, tm=1024, seq_rows=t)
        rows_c = proj[:, C_KVC:C_KVC + KV_W].reshape(nb, t // CMP_STRIDE, CMP_STRIDE * KV_W)
        kc, vc = _compress_prompt(rows_c, wlt, cb, w2k, l)
        o_attn = _nsa_prompt(proj, kc, vc, bias_cp, bias_t, jnp.asarray(ct_p, bf16), jnp.asarray(e_p, bf16), nb, t)
        o_rnn, h_last = _rglru_prompt(proj, nb, t, rnn_w, l)
        merged = _merge(o_attn, o_rnn, proj, w_ao, w_ro, l, tm=1024)
        x1 = _outproj(merged, y_p, mod_p, w_o, ln_g4, ln_b4, l, tm=256)
        y_p = _ffn(x1, mod_p, w_f1, w_f2, ln_g4, ln_b4, l, tm=1024)
        kv = lambda k: kvt[:, k * KV_W:(k + 1) * KV_W].reshape(nb, 2, N_KV, HEAD_DIM, t).transpose(0, 4, 1, 2, 3)
        st_p.append((kv(0), kv(1), kv(2)[:, t - min(WINDOW, t):],
                     proj.reshape(nb, t, N_PACK)[:, t - (CONV_W - 1):, C_RX:C_RX + D_RNN],
                     h_last.reshape(nb, D_RNN)))

        proj, kvt = _inproj(y_s, mod_s, w_in_p, b_in_p, l, tm=ms, seq_rows=ms)
        proj3 = proj.reshape(n_new, ns, N_PACK)
        q5 = proj3[:, :, C_Q:C_Q + 1024].reshape(n_new, ns, N_KV, GROUP, HEAD_DIM).transpose(1, 2, 3, 0, 4)
        qrow = (q5[:, :, :, :, None, :] * jnp.eye(N_KV, dtype=f32)[None, :, None, None, :, None]
                ).reshape(ns, N_HEADS * n_new, 256)
        ng = proj3[:, :, C_NG:C_NG + 48].reshape(n_new, ns, N_HEADS, 3).transpose(1, 2, 0, 3).reshape(
            ns, N_HEADS * n_new, 3)
        new_rows = lambda c: proj3[:, :, c:c + KV_W].transpose(1, 0, 2)
        kvc_s = _compress_sample(page_table, new_rows(C_KVC), cache_ct, l, n_phys, wlt, cb, w2k, selp)
        o64 = _nsa_sample(page_table, qrow, ng, kvc_s, new_rows(C_KVS), new_rows(C_KVW),
                          cache_st, win_t, l, n_phys, consts_s)
        o_attn = o64.reshape(ns, N_HEADS, n_new, HEAD_DIM).transpose(2, 0, 1, 3).reshape(ms, 1024).astype(bf16)
        o_rnn, h_last = _rglru_sample(proj, cbuf_t, state_rnn_h, rnn_w, l)
        merged = _merge(o_attn, o_rnn, proj, w_ao, w_ro, l, tm=ms)
        x1 = _outproj(merged, y_s, mod_s, w_o, ln_g4, ln_b4, l, tm=ms)
        y_s = _ffn(x1, mod_s, w_f1, w_f2, ln_g4, ln_b4, l, tm=ms)
        kv = lambda k: kvt[0, k * KV_W:(k + 1) * KV_W].reshape(2, N_KV, HEAD_DIM, n_new, ns).transpose(4, 3, 0, 1, 2)
        xp = jnp.concatenate([state_conv[l], proj3[:, :, C_RX:C_RX + D_RNN].transpose(1, 0, 2)], axis=1)
        st_s.append((kv(0), kv(1), kv(2), xp[:, n_new:], h_last))

    stack = lambda sts, k: jnp.stack([s[k] for s in sts])
    return (y_p.reshape(nb, t, D_MODEL), y_s.reshape(n_new, ns, D_MODEL).transpose(1, 0, 2),
            stack(st_p, 0), stack(st_p, 1), stack(st_p, 2), stack(st_p, 3), stack(st_p, 4),
            stack(st_s, 0), stack(st_s, 1), stack(st_s, 2), stack(st_s, 3), stack(st_s, 4))
```

```python
import math

import numpy as np
import jax
import jax.numpy as jnp
from jax import lax
from jax.experimental import pallas as pl
from jax.experimental.pallas import tpu as pltpu

f32 = jnp.float32
bf16 = jnp.bfloat16
i32 = jnp.int32

D_MODEL = 2048
N_HEADS = 16
N_KV = 4
GROUP = 4
HEAD_DIM = 64
CMP_STRIDE = 16
CMP_LEN = 32
SEL_LEN = 64
N_SEL = 16
WINDOW = 512
PAGE = 128
FORCE_BONUS = 1e3
N_BUCKETS = 32
MAX_DISTANCE = 128
D_RNN = 1024
RNN_BLOCK = 64
CONV_W = 4
LRU_C = 8.0
D_FF = 5632
DEPTH = 2
ALPHA = (2.0 * DEPTH) ** 0.25
SCALE = HEAD_DIM ** -0.5
LOG2E = math.log2(math.e)
NEG = -1e30
KV_W = 2 * N_KV * HEAD_DIM

C_MG = 0
C_RX = 4096
C_RY = 5120
C_Q = 6144
C_KVC = 7168
C_KVS = 7680
C_KVW = 8192
C_NG = 8704
N_PACK = 9216

VMEM_LIMIT = 56 * 1024 * 1024

TQ = 128
TK = 256
N_PAGES = 16


def _cparams(sem):
    return pltpu.CompilerParams(dimension_semantics=sem, vmem_limit_bytes=VMEM_LIMIT)


def _sigmoid(x):
    return 1.0 / (1.0 + jnp.exp(-x))


def _silu(x):
    return x * _sigmoid(x)


def _layer_norm(z, g, b):
    mu = jnp.mean(z, axis=-1, keepdims=True)
    zc = z - mu
    var = jnp.mean(zc * zc, axis=-1, keepdims=True)
    return zc * lax.rsqrt(var + 1e-5) * g + b


def _dot(a, b):
    return jnp.dot(a, b, preferred_element_type=f32)


def _dot_nt(a, b):
    return lax.dot_general(a, b, (((1,), (1,)), ((), ())), preferred_element_type=f32)


def _split3(x):
    hi = x.astype(bf16)
    r1 = x - hi.astype(f32)
    mid = r1.astype(bf16)
    lo = (r1 - mid.astype(f32)).astype(bf16)
    return hi, mid, lo


def _rows(mod, tm):
    mr = mod.shape[0]
    if mr == 1 or mr == tm:
        return mod
    return jnp.concatenate([mod] * (tm // mr), axis=0)


def _mod_spec(mod, k, tiles_per_block, ngrid):
    mr = mod.shape[1]
    if ngrid == 1:
        return pl.BlockSpec((None, mr, D_MODEL), lambda i: (i // tiles_per_block, 0, k))
    return pl.BlockSpec((None, mr, D_MODEL), lambda i, j: (i // tiles_per_block, 0, k))


def _ada_kernel(c_ref, w_ref, b_ref, o_ref):
    h = _silu(c_ref[...]).astype(bf16)
    o_ref[...] = _dot(h, w_ref[...].astype(bf16)) + b_ref[...]


def _ada(c_all, w, b, layer):
    m = c_all.shape[0]
    n = w.shape[2]
    tn = 1024
    return pl.pallas_call(
        _ada_kernel,
        grid=(n // tn,),
        in_specs=[pl.BlockSpec((m, D_MODEL), lambda j: (0, 0)),
                  pl.BlockSpec((None, D_MODEL, tn), lambda j: (layer, 0, j)),
                  pl.BlockSpec((None, 1, tn), lambda j: (layer, 0, j))],
        out_specs=pl.BlockSpec((m, tn), lambda j: (0, j)),
        out_shape=jax.ShapeDtypeStruct((m, n), f32),
        compiler_params=_cparams(("arbitrary",)),
        name="ada",
    )(c_all, w, b)


def _bias_lookup_kernel(tbl_ref, bk_ref, o_ref):
    bk = bk_ref[...]
    for h in range(N_HEADS):
        acc = jnp.zeros(bk.shape, f32)
        for k in range(N_BUCKETS):
            acc = jnp.where(bk == k, tbl_ref[k, h], acc)
        o_ref[h] = acc


def _bias_lookup(tbl, buckets):
    n = buckets.shape[0]
    tr = 256
    return pl.pallas_call(
        _bias_lookup_kernel,
        grid=(n // tr,),
        in_specs=[pl.BlockSpec(memory_space=pltpu.SMEM),
                  pl.BlockSpec((tr, 128), lambda i: (i, 0))],
        out_specs=pl.BlockSpec((N_HEADS, tr, 128), lambda i: (0, i, 0)),
        out_shape=jax.ShapeDtypeStruct((N_HEADS, n, 128), f32),
        compiler_params=_cparams(("arbitrary",)),
        name="bias_lookup",
    )(tbl, buckets)


def _t5_bucket_np(dist):
    n = np.maximum(dist, 0)
    exact = N_BUCKETS // 2
    ratio = np.maximum(n, 1).astype(np.float32) / np.float32(exact)
    log_ratio = np.log(ratio).astype(np.float32) / np.float32(math.log(MAX_DISTANCE / exact))
    large = np.minimum(exact + (log_ratio * np.float32(N_BUCKETS - exact)).astype(np.int32), N_BUCKETS - 1)
    return np.where(n < exact, n, large).astype(np.int32)


IN_TN = 512
KV_J0 = C_KVC // IN_TN


IN_PIECES = (("mg", 0, 8), ("rxy", 8, 4), ("qkv", 12, 5), ("ng", 17, 1))


def _inproj_kernel(x_ref, sc_ref, sh_ref, wmg_ref, wrxy_ref, wqkv_ref, wng_ref, b_ref,
                   o_ref, okc_ref, oks_ref, okw_ref, h_scr):
    j = pl.program_id(1)

    @pl.when(j == 0)
    def _():
        tm = x_ref.shape[0]
        h_scr[...] = (x_ref[...] * (1.0 + _rows(sc_ref[...], tm)) + _rows(sh_ref[...], tm)).astype(bf16)

    def emit(w_ref):
        res = _dot(h_scr[...], w_ref[...]) + b_ref[...]
        o_ref[...] = res
        return res

    for w_ref, (_, j0, nj) in zip((wmg_ref, wrxy_ref, wqkv_ref, wng_ref), IN_PIECES):
        @pl.when(jnp.logical_and(j >= j0, j < j0 + nj))
        def _(w_ref=w_ref, j0=j0):
            res = emit(w_ref)
            if j0 == IN_PIECES[2][1]:
                for k, okt_ref in enumerate((okc_ref, oks_ref, okw_ref)):
                    @pl.when(j == KV_J0 + k)
                    def _(okt_ref=okt_ref):
                        okt_ref[...] = res.T


def _inproj(x, mod, w_pieces, b, layer, tm, seq_rows):
    m = x.shape[0]
    tn = IN_TN
    tpb = (m // tm) // mod.shape[0]
    tps = seq_rows // tm
    kt_spec = pl.BlockSpec((None, tn, tm), lambda i, j: (i // tps, 0, i % tps))
    kt_shape = jax.ShapeDtypeStruct((m // seq_rows, tn, seq_rows), f32)

    def w_spec(j0, nj):
        return pl.BlockSpec((None, D_MODEL, tn), lambda i, j: (layer, 0, jnp.clip(j - j0, 0, nj - 1)))

    return pl.pallas_call(
        _inproj_kernel,
        grid=(m // tm, N_PACK // tn),
        in_specs=[pl.BlockSpec((tm, D_MODEL), lambda i, j: (i, 0)),
                  _mod_spec(mod, 1, tpb, 2), _mod_spec(mod, 0, tpb, 2)]
        + [w_spec(j0, nj) for (_, j0, nj) in IN_PIECES]
        + [pl.BlockSpec((None, 1, tn), lambda i, j: (layer, 0, j))],
        out_specs=[pl.BlockSpec((tm, tn), lambda i, j: (i, j)), kt_spec, kt_spec, kt_spec],
        out_shape=[jax.ShapeDtypeStruct((m, N_PACK), f32), kt_shape, kt_shape, kt_shape],
        scratch_shapes=[pltpu.VMEM((tm, D_MODEL), bf16)],
        compiler_params=_cparams(("parallel", "arbitrary")),
        name="inproj",
    )(x, mod, mod, *w_pieces, b)


def _cmp_bias_kernel(pe_ref, wlt_ref, o_ref):
    for kv in range(2):
        acc = jnp.zeros((8, 512), f32)
        for l in range(CMP_STRIDE):
            w = wlt_ref[kv, l]
            lead = _dot(jnp.broadcast_to(pe_ref[kv, 0, l:l + 1, :], (8, 256)).astype(bf16), w[:, 0:256])
            tail = _dot(jnp.broadcast_to(pe_ref[kv, 1, l:l + 1, :], (8, 256)).astype(bf16), w[:, 256:512])
            acc = acc + jnp.concatenate([lead, tail], axis=1)
        o_ref[kv] = acc


def _cmp_bias(pe, wlt):
    return pl.pallas_call(
        _cmp_bias_kernel,
        out_shape=jax.ShapeDtypeStruct((2, 8, 512), f32),
        compiler_params=pltpu.CompilerParams(vmem_limit_bytes=VMEM_LIMIT),
        name="cmp_bias",
    )(pe, wlt)


def _compress_chunks(piece, n_rows, kv, wlt_ref, cb_ref, w2_ref):
    acc = jnp.zeros((n_rows, 512), f32)
    for l in range(CMP_STRIDE):
        acc = acc + _dot(piece(l), wlt_ref[kv, l])
    acc = acc + cb_ref[kv, 0:1, :]
    z = acc[:, 0:256] + pltpu.roll(acc[:, 256:512], n_rows - 1, 0)
    return _dot(_silu(z).astype(bf16), w2_ref[kv])


def _compress_prompt_kernel(x_ref, wlt_ref, cb_ref, w2_ref, kc_ref, vc_ref):
    n_chunk = x_ref.shape[0]
    for kv, dst in enumerate((kc_ref, vc_ref)):
        piece = lambda l: x_ref[:, l * KV_W + kv * 256:l * KV_W + (kv + 1) * 256].astype(bf16)
        out = _compress_chunks(piece, n_chunk, kv, wlt_ref, cb_ref, w2_ref)
        row = lax.broadcasted_iota(i32, out.shape, 0)
        dst[...] = jnp.where(row < n_chunk - 1, out, 0.0)


def _compress_prompt(rows, wlt, cb, w2, layer):
    nb, n_chunk, width = rows.shape
    lsel = lambda a: pl.BlockSpec((None,) + a.shape[1:], lambda b: (layer,) + (0,) * (a.ndim - 1))
    return pl.pallas_call(
        _compress_prompt_kernel,
        grid=(nb,),
        in_specs=[pl.BlockSpec((None, n_chunk, width), lambda b: (b, 0, 0)),
                  lsel(wlt), pl.BlockSpec(cb.shape, lambda b: (0, 0, 0)), lsel(w2)],
        out_specs=[pl.BlockSpec((None, n_chunk, 256), lambda b: (b, 0, 0))] * 2,
        out_shape=[jax.ShapeDtypeStruct((nb, n_chunk, 256), f32)] * 2,
        compiler_params=_cparams(("arbitrary",)),
        name="compress_prompt",
    )(rows, wlt, cb, w2)


def _topk_mask_rows(imp, sidx, n_keep):
    n_rows = imp.shape[0]
    rank = jnp.zeros(imp.shape, i32)
    for j in range(n_rows):
        row = imp[j:j + 1, :]
        beats = jnp.where(row > imp, 1, jnp.where(row == imp, jnp.where(sidx > j, 1, 0), 0))
        rank = rank + beats
    return jnp.where(rank < n_keep, jnp.where(imp > -1e29, 1.0, 0.0), 0.0)


def _nsa_prompt_kernel(q_ref, ng_ref, ks_ref, vs_ref, kw_ref, vw_ref, kc_ref, vc_ref,
                       bc_ref, bt_ref, ct_ref, e_ref, o_ref, selk_scr, s_scr, mx_scr):
    i = pl.program_id(1)
    q0 = i * TQ
    rows = GROUP * TQ
    n_sel_blk = ks_ref.shape[0] // SEL_LEN

    d0 = lax.broadcasted_iota(i32, (TQ, TK), 0) - lax.broadcasted_iota(i32, (TQ, TK), 1)
    dist_c = (q0 + (lax.broadcasted_iota(i32, (rows, 128), 0) & (TQ - 1))
              - CMP_STRIDE * lax.broadcasted_iota(i32, (rows, 128), 1) - (CMP_LEN - 1))
    mask_c = dist_c >= 0
    sig = _sigmoid(ng_ref[...])

    sidx = lax.broadcasted_iota(i32, (n_sel_blk, TQ), 0)
    cur = (q0 + lax.broadcasted_iota(i32, (n_sel_blk, TQ), 1)) // SEL_LEN
    forced = jnp.where(sidx == 0, 1, jnp.where(sidx == cur, 1, jnp.where(sidx == cur - 1, 1, 0)))

    def branch(qs, g, k_ref, v_ref, kt_hi, window):
        n_tiles = k_ref.shape[0] // TK
        ones = jnp.ones((TK, HEAD_DIM), f32)

        def score_tile(kt, slot):
            k0 = pl.multiple_of(jnp.clip(kt, 0, n_tiles - 1) * TK, TK)
            k = k_ref[pl.ds(k0, TK), g * 64:(g + 1) * 64].astype(bf16)
            delta = q0 - kt * TK
            dist = d0 + delta
            if window:
                live = jnp.where(kt >= 0, 0.0, NEG)
                addm = jnp.where(dist >= 0, jnp.where(dist < WINDOW, live, NEG), NEG)
            else:
                addm = jnp.where(dist >= 0, selk_scr[:, pl.ds(k0, TK)], NEG)
            va = jnp.clip(delta, 0, 256) // 128
            vb = jnp.clip(delta - 128, 0, 256) // 128
            bias = jnp.concatenate([bt_ref[g * 3 + va], bt_ref[g * 3 + vb]], axis=1)
            s = _dot_nt(qs, k) + bias + jnp.concatenate([addm] * GROUP, axis=0)
            s_scr[:, pl.ds(slot * TK, TK)] = s
            return jnp.maximum(s[:, 0:128], s[:, 128:256])

        def value_tile(kt, slot, mb):
            k0 = pl.multiple_of(jnp.clip(kt, 0, n_tiles - 1) * TK, TK)
            v = jnp.concatenate([v_ref[pl.ds(k0, TK), g * 64:(g + 1) * 64], ones], axis=1).astype(bf16)
            p = jnp.exp2(s_scr[:, pl.ds(slot * TK, TK)] - jnp.concatenate([mb, mb], axis=1))
            return _dot(p.astype(bf16), v)

        if window:
            tiles = [(kt_hi - 2 + j, j) for j in range(3)]
            mx = jnp.full((rows, 128), NEG, f32)
            for kt, slot in tiles:
                mx = jnp.maximum(mx, score_tile(kt, slot))
            mb = jnp.broadcast_to(jnp.max(mx, axis=-1, keepdims=True), (rows, 128))
            acc = jnp.zeros((rows, 2 * HEAD_DIM), f32)
            for kt, slot in tiles:
                acc = acc + value_tile(kt, slot, mb)
        else:
            n_quads = kt_hi // 4 + 1

            def scores(qd, mx):
                for j in range(4):
                    mx = jnp.maximum(mx, score_tile(4 * qd + j, 4 * qd + j))
                return mx

            mx = lax.fori_loop(0, n_quads, scores, jnp.full((rows, 128), NEG, f32))
            mx_scr[...] = jnp.broadcast_to(jnp.max(mx, axis=-1, keepdims=True), (rows, 128))

            def values(qd, acc):
                mb = mx_scr[...]
                for j in range(4):
                    acc = acc + value_tile(4 * qd + j, 4 * qd + j, mb)
                return acc

            acc = lax.fori_loop(0, n_quads, values, jnp.zeros((rows, 2 * HEAD_DIM), f32))
        return acc[:, 0:HEAD_DIM] / jnp.maximum(acc[:, HEAD_DIM:HEAD_DIM + 1], 1e-30)

    for g in range(N_KV):
        qg = q_ref[:, g * 256:(g + 1) * 256]
        qst = jnp.concatenate([qg[:, r * 64:(r + 1) * 64] for r in range(GROUP)], axis=0)
        qs = (qst * SCALE).astype(bf16)
        qs2 = (qst * (SCALE * LOG2E)).astype(bf16)

        kcg = kc_ref[:, g * 64:(g + 1) * 64].astype(bf16)
        vcg = vc_ref[:, g * 64:(g + 1) * 64].astype(bf16)
        bias_c = jnp.concatenate([bc_ref[g * GROUP + r] for r in range(GROUP)], axis=0)
        s = jnp.where(mask_c, _dot_nt(qs, kcg) + bias_c, NEG)
        p = jnp.where(mask_c, jnp.exp(s - jnp.max(s, axis=-1, keepdims=True)), 0.0)
        p = p / jnp.maximum(jnp.sum(p, axis=-1, keepdims=True), 1e-30)
        o_c = _dot(p.astype(bf16), vcg)

        p_sum = p[0:TQ] + p[TQ:2 * TQ] + p[2 * TQ:3 * TQ] + p[3 * TQ:4 * TQ]
        ct = ct_ref[...]
        imp_t = sum(_dot_nt(ct, piece) for piece in _split3(p_sum))[0:n_sel_blk]
        imp_t = jnp.where(forced > 0, imp_t + FORCE_BONUS, imp_t)
        imp_t = jnp.where(sidx <= cur, imp_t, NEG)
        sel_t = _topk_mask_rows(imp_t, sidx, N_SEL)
        sel_pad = jnp.concatenate([sel_t, jnp.zeros((128 - n_sel_blk, TQ), f32)], axis=0)
        sel_q = sel_pad.T.astype(bf16)
        selk_scr[...] = jnp.where(_dot(sel_q, e_ref[...]) > 0.5, 0.0, NEG)

        kt_hi = (q0 + TQ - 1) // TK
        o_s = branch(qs2, g, ks_ref, vs_ref, kt_hi, False)
        o_w = branch(qs2, g, kw_ref, vw_ref, kt_hi, True)

        gate = lambda c: jnp.concatenate(
            [sig[:, g * 12 + r * 3 + c:g * 12 + r * 3 + c + 1] for r in range(GROUP)], axis=0)
        o = gate(0) * o_c + gate(1) * o_s + gate(2) * o_w
        o_ref[:, g * 256:(g + 1) * 256] = jnp.concatenate(
            [o[r * TQ:(r + 1) * TQ] for r in range(GROUP)], axis=1).astype(bf16)


def _nsa_prompt(proj, kc, vc, bias_c, bias_t, ct, e, nb, t):
    nq = t // TQ
    full = lambda a: pl.BlockSpec(a.shape, lambda b, i: (0,) * a.ndim)
    kv = lambda c: pl.BlockSpec((t, 256), lambda b, i: (b, c // 256))
    return pl.pallas_call(
        _nsa_prompt_kernel,
        grid=(nb, nq),
        in_specs=[pl.BlockSpec((TQ, 1024), lambda b, i: (b * nq + i, C_Q // 1024)),
                  pl.BlockSpec((TQ, 128), lambda b, i: (b * nq + i, C_NG // 128)),
                  kv(C_KVS), kv(C_KVS + 256), kv(C_KVW), kv(C_KVW + 256),
                  pl.BlockSpec((None, t // CMP_STRIDE, 256), lambda b, i: (b, 0, 0)),
                  pl.BlockSpec((None, t // CMP_STRIDE, 256), lambda b, i: (b, 0, 0)),
                  pl.BlockSpec((N_HEADS, TQ, 128), lambda b, i: (0, i, 0)),
                  full(bias_t), full(ct), full(e)],
        out_specs=pl.BlockSpec((TQ, 1024), lambda b, i: (b * nq + i, 0)),
        out_shape=jax.ShapeDtypeStruct((nb * t, 1024), bf16),
        scratch_shapes=[pltpu.VMEM((TQ, t), f32), pltpu.VMEM((GROUP * TQ, t), f32),
                        pltpu.VMEM((GROUP * TQ, 128), f32)],
        compiler_params=_cparams(("parallel", "arbitrary")),
        name="nsa_prompt",
    )(proj, proj, proj, proj, proj, proj, kc, vc, bias_c, bias_t, ct, e)


def _lru_gates(xc, ry, wa_ref, ba_ref, wi_ref, bi_ref, lam_ref):
    xb = xc.astype(bf16)
    ra = jnp.concatenate([_dot(xb[:, c * 256:(c + 1) * 256], wa_ref[c]) for c in range(4)], axis=1)
    ri = jnp.concatenate([_dot(xb[:, c * 256:(c + 1) * 256], wi_ref[c]) for c in range(4)], axis=1)
    r = _sigmoid(ra + ba_ref[...])
    ig = _sigmoid(ri + bi_ref[...])
    nl = -lam_ref[...]
    softplus = jnp.maximum(nl, 0.0) + jnp.log1p(jnp.exp(-jnp.abs(nl)))
    log_a = -LRU_C * r * softplus
    a = jnp.exp(log_a)
    u = jnp.sqrt(jnp.tanh(-log_a) * (a * a + 1.0)) * (ig * xc)
    return a, u, jax.nn.gelu(ry)


def _rglru_prompt_kernel(rx_ref, ry_ref, cw_ref, cb_ref, wa_ref, ba_ref, wi_ref, bi_ref, lam_ref,
                         o_ref, hl_ref, xp_scr, a_scr, u_scr, hs_scr, h_scr):
    tt = pl.program_id(1)
    tr = rx_ref.shape[0]

    @pl.when(tt == 0)
    def _():
        xp_scr[0:8, :] = jnp.zeros((8, D_RNN), f32)
        h_scr[...] = jnp.zeros((1, D_RNN), f32)

    x = rx_ref[...]
    xp_scr[8:8 + tr, :] = x
    y = x * cw_ref[CONV_W - 1:CONV_W, :] + cb_ref[...]
    for k in range(CONV_W - 1):
        y = y + xp_scr[5 + k:5 + k + tr, :] * cw_ref[k:k + 1, :]
    xp_scr[0:8, :] = xp_scr[tr:tr + 8, :]

    a, u, gate = _lru_gates(y, ry_ref[...], wa_ref, ba_ref, wi_ref, bi_ref, lam_ref)
    a_scr[...] = a
    u_scr[...] = u

    def step(t, h):
        h = a_scr[pl.ds(t, 1), :] * h + u_scr[pl.ds(t, 1), :]
        hs_scr[pl.ds(t, 1), :] = h
        return h

    h = lax.fori_loop(0, tr, step, h_scr[...], unroll=8)
    h_scr[...] = h
    hl_ref[...] = h
    o_ref[...] = (hs_scr[...] * gate).astype(bf16)


def _rnn_specs(rnn_w, layer, ngrid):
    zeros = lambda n: (0,) * n
    if ngrid == 2:
        return [pl.BlockSpec((None,) + a.shape[1:], lambda b, i, n=a.ndim - 1: (layer,) + zeros(n)) for a in rnn_w]
    return [pl.BlockSpec((None,) + a.shape[1:], lambda i, n=a.ndim - 1: (layer,) + zeros(n)) for a in rnn_w]


def _rglru_prompt(proj, nb, t, rnn_w, layer):
    tr = 256
    nt = t // tr
    return pl.pallas_call(
        _rglru_prompt_kernel,
        grid=(nb, nt),
        in_specs=[pl.BlockSpec((tr, D_RNN), lambda b, i: (b * nt + i, C_RX // D_RNN)),
                  pl.BlockSpec((tr, D_RNN), lambda b, i: (b * nt + i, C_RY // D_RNN))]
        + _rnn_specs(rnn_w, layer, 2),
        out_specs=[pl.BlockSpec((tr, D_RNN), lambda b, i: (b * nt + i, 0)),
                   pl.BlockSpec((None, 1, D_RNN), lambda b, i: (b, 0, 0))],
        out_shape=[jax.ShapeDtypeStruct((nb * t, D_RNN), bf16),
                   jax.ShapeDtypeStruct((nb, 1, D_RNN), f32)],
        scratch_shapes=[pltpu.VMEM((tr + 8, D_RNN), f32), pltpu.VMEM((tr, D_RNN), f32),
                        pltpu.VMEM((tr, D_RNN), f32), pltpu.VMEM((tr, D_RNN), f32),
                        pltpu.VMEM((1, D_RNN), f32)],
        compiler_params=_cparams(("parallel", "arbitrary")),
        name="rglru_prompt",
    )(proj, proj, *rnn_w)


def _rglru_sample_kernel(rx_ref, ry_ref, cbuf_ref, h0_ref, cw_ref, cb_ref, wa_ref, ba_ref, wi_ref, bi_ref,
                         lam_ref, o_ref, hl_ref):
    ns = h0_ref.shape[0]
    n_t = rx_ref.shape[0] // ns
    xp = [cbuf_ref[k] for k in range(CONV_W - 1)] + [rx_ref[s * ns:(s + 1) * ns, :] for s in range(n_t)]
    h = h0_ref[...]
    for s in range(n_t):
        y = xp[s + CONV_W - 1] * cw_ref[CONV_W - 1:CONV_W, :] + cb_ref[...]
        for k in range(CONV_W - 1):
            y = y + xp[s + k] * cw_ref[k:k + 1, :]
        a, u, gate = _lru_gates(y, ry_ref[s * ns:(s + 1) * ns, :], wa_ref, ba_ref, wi_ref, bi_ref, lam_ref)
        h = a * h + u
        o_ref[s * ns:(s + 1) * ns, :] = (h * gate).astype(bf16)
    hl_ref[...] = h


def _rglru_sample(proj, cbuf, h0, rnn_w, layer):
    ms = proj.shape[0]
    ns = h0.shape[1]
    return pl.pallas_call(
        _rglru_sample_kernel,
        grid=(1,),
        in_specs=[pl.BlockSpec((ms, D_RNN), lambda i: (0, C_RX // D_RNN)),
                  pl.BlockSpec((ms, D_RNN), lambda i: (0, C_RY // D_RNN)),
                  pl.BlockSpec((None, CONV_W - 1, ns, D_RNN), lambda i: (layer, 0, 0, 0)),
                  pl.BlockSpec((None, ns, D_RNN), lambda i: (layer, 0, 0))]
        + _rnn_specs(rnn_w, layer, 1),
        out_specs=[pl.BlockSpec((ms, D_RNN), lambda i: (0, 0)),
                   pl.BlockSpec((ns, D_RNN), lambda i: (0, 0))],
        out_shape=[jax.ShapeDtypeStruct((ms, D_RNN), bf16),
                   jax.ShapeDtypeStruct((ns, D_RNN), f32)],
        compiler_params=_cparams(("arbitrary",)),
        name="rglru_sample",
    )(proj, proj, cbuf, h0, *rnn_w)


def _merge_kernel(oa_ref, orn_ref, ga_ref, gr_ref, wa_ref, wr_ref, o_ref):
    a1 = _dot(oa_ref[...], wa_ref[...])
    a2 = _dot(orn_ref[...], wr_ref[...])
    o_ref[...] = (_sigmoid(ga_ref[...]) * a1 + _sigmoid(gr_ref[...]) * a2).astype(bf16)


def _merge(o_attn, o_rnn, proj, w_ao, w_ro, layer, tm):
    m = o_attn.shape[0]
    tn = 512
    nn = D_MODEL // tn
    return pl.pallas_call(
        _merge_kernel,
        grid=(m // tm, nn),
        in_specs=[pl.BlockSpec((tm, 1024), lambda i, j: (i, 0)),
                  pl.BlockSpec((tm, 1024), lambda i, j: (i, 0)),
                  pl.BlockSpec((tm, tn), lambda i, j: (i, j)),
                  pl.BlockSpec((tm, tn), lambda i, j: (i, nn + j)),
                  pl.BlockSpec((None, 1024, tn), lambda i, j: (layer, 0, j)),
                  pl.BlockSpec((None, 1024, tn), lambda i, j: (layer, 0, j))],
        out_specs=pl.BlockSpec((tm, tn), lambda i, j: (i, j)),
        out_shape=jax.ShapeDtypeStruct((m, D_MODEL), bf16),
        compiler_params=_cparams(("parallel", "arbitrary")),
        name="merge",
    )(o_attn, o_rnn, proj, proj, w_ao, w_ro)


def _outproj_kernel(a_ref, x_ref, gt_ref, w_ref, lg_ref, lb_ref, o_ref):
    mix = _dot(a_ref[...], w_ref[...])
    z = ALPHA * x_ref[...] + (1.0 + _rows(gt_ref[...], x_ref.shape[0])) * mix
    o_ref[...] = _layer_norm(z, lg_ref[...], lb_ref[...])


def _outproj(a, x, mod, w, ln_g, ln_b, layer, tm):
    m = a.shape[0]
    tpb = (m // tm) // mod.shape[0]
    ln_spec = pl.BlockSpec((None, None, 1, D_MODEL), lambda i: (layer, 0, 0, 0))
    return pl.pallas_call(
        _outproj_kernel,
        grid=(m // tm,),
        in_specs=[pl.BlockSpec((tm, D_MODEL), lambda i: (i, 0)),
                  pl.BlockSpec((tm, D_MODEL), lambda i: (i, 0)),
                  _mod_spec(mod, 2, tpb, 1),
                  pl.BlockSpec((None, D_MODEL, D_MODEL), lambda i: (layer, 0, 0)),
                  ln_spec, ln_spec],
        out_specs=pl.BlockSpec((tm, D_MODEL), lambda i: (i, 0)),
        out_shape=jax.ShapeDtypeStruct((m, D_MODEL), f32),
        compiler_params=_cparams(("parallel",)),
        name="outproj_ln",
    )(a, x, mod, w, ln_g, ln_b)


def _ffn_kernel(x_ref, sc_ref, sh_ref, gt_ref, wg_ref, wu_ref, wo_ref, lg_ref, lb_ref, o_ref, h_scr):
    j = pl.program_id(1)
    tm = x_ref.shape[0]

    @pl.when(j == 0)
    def _():
        h_scr[...] = (x_ref[...] * (1.0 + _rows(sc_ref[...], tm)) + _rows(sh_ref[...], tm)).astype(bf16)
        o_ref[...] = jnp.zeros(o_ref.shape, f32)

    h = h_scr[...]
    act = (_silu(_dot(h, wg_ref[...])) * _dot(h, wu_ref[...])).astype(bf16)
    o_ref[...] += _dot(act, wo_ref[...])

    @pl.when(j == pl.num_programs(1) - 1)
    def _():
        z = ALPHA * x_ref[...] + (1.0 + _rows(gt_ref[...], tm)) * o_ref[...]
        o_ref[...] = _layer_norm(z, lg_ref[...], lb_ref[...])


def _ffn(x, mod, w_in, w_out, ln_g, ln_b, layer, tm):
    m = x.shape[0]
    tf = 512
    nf = D_FF // tf
    tpb = (m // tm) // mod.shape[0]
    ln_spec = pl.BlockSpec((None, None, 1, D_MODEL), lambda i, j: (layer, 1, 0, 0))
    return pl.pallas_call(
        _ffn_kernel,
        grid=(m // tm, nf),
        in_specs=[pl.BlockSpec((tm, D_MODEL), lambda i, j: (i, 0)),
                  _mod_spec(mod, 4, tpb, 2), _mod_spec(mod, 3, tpb, 2), _mod_spec(mod, 5, tpb, 2),
                  pl.BlockSpec((None, D_MODEL, tf), lambda i, j: (layer, 0, j)),
                  pl.BlockSpec((None, D_MODEL, tf), lambda i, j: (layer, 0, nf + j)),
                  pl.BlockSpec((None, tf, D_MODEL), lambda i, j: (layer, j, 0)),
                  ln_spec, ln_spec],
        out_specs=pl.BlockSpec((tm, D_MODEL), lambda i, j: (i, 0)),
        out_shape=jax.ShapeDtypeStruct((m, D_MODEL), f32),
        scratch_shapes=[pltpu.VMEM((tm, D_MODEL), bf16)],
        compiler_params=_cparams(("parallel", "arbitrary")),
        name="ffn",
    )(x, mod, mod, mod, w_in, w_in, w_out, ln_g, ln_b)


CMP_BATCH = 4


def _compress_sample_kernel(pt_ref, newc_ref, *rest):
    n_pg = CMP_BATCH * N_PAGES
    pages = rest[0:n_pg]
    wlt_ref, cb_ref, w2_ref, selp_ref, o_ref, x_scr = rest[n_pg:]
    n_new = newc_ref.shape[1]
    per = N_PAGES * (PAGE // CMP_STRIDE) + 8
    row8 = lax.broadcasted_iota(i32, (8, 256), 0)
    selp = selp_ref[...]
    for s in range(CMP_BATCH):
        for p in range(N_PAGES):
            y = _dot_nt(selp, pages[s * N_PAGES + p][...].astype(bf16))
            for l in range(CMP_STRIDE):
                x_scr[l, s * per + p * 8:s * per + (p + 1) * 8, :] = y[l * 8:(l + 1) * 8, :]
        for l in range(CMP_STRIDE):
            if l < n_new:
                extra = jnp.where(row8 == 0, jnp.broadcast_to(newc_ref[s, l:l + 1, :], (8, 256)), 0.0)
            else:
                extra = jnp.zeros((8, 256), f32)
            x_scr[l, s * per + per - 8:s * per + per, :] = extra
    acc = jnp.zeros((CMP_BATCH * per, 512), f32)
    for l in range(CMP_STRIDE):
        acc = acc + _dot(x_scr[l].astype(bf16), wlt_ref[l])
    acc = acc + cb_ref[0:1, :]
    for s in range(CMP_BATCH):
        a = acc[s * per:(s + 1) * per]
        z = a[:, 0:256] + pltpu.roll(a[:, 256:512], per - 1, 0)
        o_ref[s] = _dot(_silu(z).astype(bf16), w2_ref[...])[0:per - 8].astype(bf16)


def _compress_sample(page_table, newc, cache_c, layer, n_phys, wlt, cb, w2, selp):
    ns, n_new, _ = newc.shape
    per = N_PAGES * (PAGE // CMP_STRIDE) + 8

    def page_spec(s, p):
        return pl.BlockSpec((None, 256, PAGE),
                            lambda i, kv, pt: (layer * n_phys + pt[i * CMP_BATCH + s, p], kv, 0))

    in_specs = ([pl.BlockSpec((CMP_BATCH, n_new, 256), lambda i, kv, pt: (i, 0, kv))]
                + [page_spec(s, p) for s in range(CMP_BATCH) for p in range(N_PAGES)]
                + [pl.BlockSpec((None, None, CMP_STRIDE, 256, 512), lambda i, kv, pt: (layer, kv, 0, 0, 0)),
                   pl.BlockSpec((None, 8, 512), lambda i, kv, pt: (kv, 0, 0)),
                   pl.BlockSpec((None, None, 256, 256), lambda i, kv, pt: (layer, kv, 0, 0)),
                   pl.BlockSpec(selp.shape, lambda i, kv, pt: (0, 0))])
    grid_spec = pltpu.PrefetchScalarGridSpec(
        num_scalar_prefetch=1,
        grid=(ns // CMP_BATCH, 2),
        in_specs=in_specs,
        out_specs=pl.BlockSpec((None, CMP_BATCH, per - 8, 256), lambda i, kv, pt: (kv, i, 0, 0)),
        scratch_shapes=[pltpu.VMEM((CMP_STRIDE, CMP_BATCH * per, 256), f32)],
    )
    return pl.pallas_call(
        _compress_sample_kernel,
        grid_spec=grid_spec,
        out_shape=jax.ShapeDtypeStruct((2, ns, per - 8, 256), bf16),
        compiler_params=_cparams(("arbitrary", "arbitrary")),
        name="compress_sample",
    )(page_table, newc, *([cache_c] * (CMP_BATCH * N_PAGES)), wlt, cb, w2, selp)


ATT_BATCH = 2


def _nsa_sample_kernel(pt_ref, qrow_ref, ng_ref, kc_ref, vc_ref, news_ref, neww_ref, *rest):
    n_pg = ATT_BATCH * N_PAGES
    cs = rest[0:n_pg]
    (win_ref, bc_ref, mc_ref, bs_ref, ms_ref, bw_ref, mw_ref,
     smat_ref, smt_ref, c_ref, e_ref, dsel_ref, o_ref, new_scr) = rest[n_pg:]
    n_new = news_ref.shape[1]
    dsel = dsel_ref[...]
    mc = mc_ref[...]
    smat = smat_ref[...]
    cmat = c_ref[...]

    @pl.when(pl.program_id(0) == 0)
    def _():
        new_scr[...] = jnp.zeros(new_scr.shape, f32)

    def diag(x):
        y = x * dsel
        return y[:, 0:64] + y[:, 64:128] + y[:, 128:192] + y[:, 192:256]

    def one_sample(b):
        qs = (qrow_ref[b] * SCALE).astype(bf16)
        kc = kc_ref[b]
        vc = vc_ref[b]

        s = jnp.where(mc > 0.5, _dot_nt(qs, kc) + bc_ref[...], NEG)
        p = jnp.exp(s - jnp.max(s, axis=-1, keepdims=True)) * mc
        p = p / jnp.maximum(jnp.sum(p, axis=-1, keepdims=True), 1e-30)
        o_c = diag(_dot(p.astype(bf16), vc))

        p_sum = sum(_dot(smat, piece_) for piece_ in _split3(p))
        imp = sum(_dot(piece_, cmat) for piece_ in _split3(p_sum))
        n_blk = (N_PAGES * PAGE + n_new + SEL_LEN - 1) // SEL_LEN
        cur = (N_PAGES * PAGE) // SEL_LEN
        sidx = lax.broadcasted_iota(i32, imp.shape, 1)
        forced = jnp.where(sidx == 0, 1, jnp.where(sidx == cur, 1, jnp.where(sidx == cur - 1, 1, 0)))
        imp = jnp.where(forced > 0, imp + FORCE_BONUS, imp)
        imp = jnp.where(sidx <= cur, imp, NEG)
        rank = jnp.zeros(imp.shape, i32)
        for j in range(n_blk):
            col = imp[:, j:j + 1]
            rank = rank + jnp.where(col > imp, 1, jnp.where(col == imp, jnp.where(sidx > j, 1, 0), 0))
        sel = jnp.where(rank < N_SEL, jnp.where(imp > -1e29, jnp.where(sidx < n_blk, 1.0, 0.0), 0.0), 0.0)
        sel64 = _dot(smt_ref[...], sel.astype(bf16)).astype(bf16)
        selk = _dot(sel64, e_ref[...])

        def attend(kt_tiles, vt_tiles, new_rows, bias, mask):
            k_new, v_new = new_rows[:, 0:256].astype(bf16), new_rows[:, 256:512].astype(bf16)
            s = jnp.concatenate([_dot(qs, kt) for kt in kt_tiles] + [_dot_nt(qs, k_new)], axis=1) + bias
            s = jnp.where(mask > 0.5, s, NEG)
            p = jnp.exp(s - jnp.max(s, axis=-1, keepdims=True)) * mask
            p = (p / jnp.maximum(jnp.sum(p, axis=-1, keepdims=True), 1e-30)).astype(bf16)
            off = s.shape[1] - 128
            acc = _dot(p[:, off:off + 128], v_new)
            off = 0
            for vt in vt_tiles:
                w = vt.shape[1]
                acc = acc + _dot_nt(p[:, off:off + w], vt)
                off += w
            return diag(acc)

        new_scr[b, 0, 0:n_new, :] = news_ref[b]
        pages = cs[b * N_PAGES:(b + 1) * N_PAGES]
        k_tiles = [pg[0:256, :].astype(bf16) for pg in pages]
        v_tiles = [pg[256:512, :].astype(bf16) for pg in pages]
        o_s = attend(k_tiles, v_tiles, new_scr[b, 0], bs_ref[...], ms_ref[...] * selk)

        new_scr[b, 1, 0:n_new, :] = neww_ref[b]
        o_w = attend([win_ref[b, 0:256, :].astype(bf16)], [win_ref[b, 256:512, :].astype(bf16)],
                     new_scr[b, 1], bw_ref[...], mw_ref[...])

        sig = _sigmoid(ng_ref[b])
        o_ref[b] = sig[:, 0:1] * o_c + sig[:, 1:2] * o_s + sig[:, 2:3] * o_w

    for b in range(ATT_BATCH):
        one_sample(b)


def _nsa_sample(page_table, qrow, ng, kvc, news, neww, cache_s, win, layer, n_phys, consts):
    nb = qrow.shape[0]
    n_new = news.shape[1]
    ab = ATT_BATCH
    full = lambda a: pl.BlockSpec(a.shape, lambda i, pt: (0,) * a.ndim)
    per_b = lambda shape: pl.BlockSpec((ab,) + shape, lambda i, pt: (i, 0, 0))
    cmp_spec = lambda kv: pl.BlockSpec((None, ab) + kvc.shape[2:], lambda i, pt: (kv, i, 0, 0))

    def page_spec(s, p):
        return pl.BlockSpec((None, KV_W, PAGE), lambda i, pt: (layer * n_phys + pt[i * ab + s, p], 0, 0))

    in_specs = ([per_b((64, 256)), per_b((64, 3)), cmp_spec(0), cmp_spec(1),
                 per_b((n_new, KV_W)), per_b((n_new, KV_W))]
                + [page_spec(s, p) for s in range(ab) for p in range(N_PAGES)]
                + [pl.BlockSpec((ab, KV_W, WINDOW), lambda i, pt: ((layer * nb) // ab + i, 0, 0))]
                + [full(c) for c in consts])
    grid_spec = pltpu.PrefetchScalarGridSpec(
        num_scalar_prefetch=1,
        grid=(nb // ab,),
        in_specs=in_specs,
        out_specs=pl.BlockSpec((ab, 64, 64), lambda i, pt: (i, 0, 0)),
        scratch_shapes=[pltpu.VMEM((ab, 2, 128, KV_W), f32)],
    )
    return pl.pallas_call(
        _nsa_sample_kernel,
        grid_spec=grid_spec,
        out_shape=jax.ShapeDtypeStruct((nb, 64, 64), f32),
        compiler_params=_cparams(("arbitrary",)),
        name="nsa_sample",
    )(page_table, qrow, ng, kvc, kvc, news, neww, *([cache_s] * (ab * N_PAGES)), win, *consts)


def _static_tables(t, past, n_new):
    ar = np.arange
    tiles = np.stack([d + ar(128)[:, None] - ar(128)[None, :] for d in (0, 128, 256)])
    cmp_p = ar(t)[:, None] - (CMP_STRIDE * ar(128)[None, :] + CMP_LEN - 1)
    qpos = past + ar(n_new)
    cmp_s = qpos[:, None] - (CMP_STRIDE * ar(128)[None, :] + CMP_LEN - 1)
    kpos_s = np.concatenate([ar(past), past + ar(128)])
    slc_s = qpos[:, None] - kpos_s[None, :]
    kpos_w = np.concatenate([past - WINDOW + ar(WINDOW), past + ar(128)])
    win_s = qpos[:, None] - kpos_w[None, :]
    parts = [tiles.reshape(-1, 128), cmp_p, cmp_s.reshape(-1, 128), slc_s.reshape(-1, 128), win_s.reshape(-1, 128)]
    sizes = [p.shape[0] for p in parts]
    flat = np.concatenate(parts, axis=0)
    pad = (-flat.shape[0]) % 256
    flat = np.concatenate([flat, np.zeros((pad, 128), flat.dtype)], axis=0)
    buckets = _t5_bucket_np(flat)
    real_s = np.concatenate([np.ones(past, bool), ar(128) < n_new])
    real_w = np.concatenate([np.ones(WINDOW, bool), ar(128) < n_new])
    mask_c = (cmp_s >= 0)
    mask_s = (slc_s >= 0) & real_s[None, :]
    mask_w = (win_s >= 0) & (win_s < WINDOW) & real_w[None, :]
    rep = lambda mk: np.tile(mk[None].astype(np.float32), (N_HEADS, 1, 1)).reshape(N_HEADS * n_new, -1)
    return buckets, sizes, rep(mask_c), rep(mask_s), rep(mask_w)


def _cmp_to_sel_np(n_cmp, n_sel):
    j = np.arange(n_cmp)[:, None]
    s = np.arange(n_sel)[None, :]
    lo = np.maximum(j * CMP_STRIDE, s * SEL_LEN)
    hi = np.minimum(j * CMP_STRIDE + CMP_LEN, (s + 1) * SEL_LEN)
    return (np.maximum(hi - lo, 0) / CMP_LEN).astype(np.float32)


def _kron4(w):
    eye = jnp.eye(N_KV, dtype=w.dtype)
    out = jnp.einsum('gh,...de->...gdhe', eye, w)
    return out.reshape(w.shape[:-2] + (256, 256))


def _block_diag_rnn(w):
    w4 = w.reshape(DEPTH, 4, 4, RNN_BLOCK, RNN_BLOCK)
    eye = jnp.eye(4, dtype=w.dtype)
    return jnp.einsum('jk,zcjde->zcjdke', eye, w4).reshape(DEPTH, 4, 256, 256)


def _split_in(w, b):
    n_qkv = 1024 + 3 * KV_W
    n_ng = 3 * N_HEADS
    o_rxy = n_qkv + n_ng
    o_mg = o_rxy + 2 * D_RNN
    n_pad = N_PACK - (C_NG + n_ng)
    w_mg = w[..., o_mg:].astype(bf16)
    w_rxy = w[..., o_rxy:o_mg].astype(bf16)
    w_qkv = w[..., :n_qkv].astype(bf16)
    w_ng = jnp.pad(w[..., n_qkv:o_rxy].astype(bf16), ((0, 0), (0, 0), (0, IN_TN - n_ng)))
    bp = jnp.concatenate([b[..., o_mg:], b[..., o_rxy:o_mg], b[..., :n_qkv], b[..., n_qkv:o_rxy],
                          jnp.zeros(b.shape[:-1] + (n_pad,), b.dtype)], axis=-1)
    return (w_mg, w_rxy, w_qkv, w_ng), bp.reshape(DEPTH, 1, N_PACK)


def _feature_major(cache):
    d, n, rows = cache.shape[:3]
    return cache.transpose(0, 1, 3, 4, 5, 2).reshape(d * n, KV_W, rows)


def kernel(x_prompt, x_sample, c_prompt, c_sample, cache_cmp_kv, cache_slc_kv, cache_win_kv, state_conv, state_rnn_h, page_table, rel_bias, w_ada, b_ada, w_in, b_in, cmp_pos, cmp_w1, cmp_w2, w_attn_o, conv_w, conv_b, lru_wa, lru_ba, lru_wi, lru_bi, lru_lambda, w_rnn_o, w_out, w_ffn_in, w_ffn_out, ln_g, ln_b):
    nb, t, _ = x_prompt.shape
    ns, n_new, _ = x_sample.shape
    n_phys = cache_cmp_kv.shape[1]
    past = page_table.shape[1] * PAGE
    assert page_table.shape[1] == N_PAGES and cache_win_kv.shape[2] == WINDOW and t % (4 * TK) == 0
    assert n_new <= CMP_STRIDE and ns % 8 == 0 and ns % CMP_BATCH == 0 and ns % ATT_BATCH == 0
    mp, ms = nb * t, ns * n_new

    buckets, sizes, mask_c, mask_s, mask_w = _static_tables(t, past, n_new)
    bias_all = _bias_lookup(rel_bias, jnp.asarray(buckets))
    offs = np.cumsum([0] + sizes)
    seg = lambda k: bias_all[:, offs[k]:offs[k + 1]]
    bias_t = (seg(0) * LOG2E).reshape(N_KV, GROUP, 3, 128, 128).transpose(0, 2, 1, 3, 4).reshape(
        N_KV * 3, GROUP * 128, 128)
    bias_cp = seg(1)
    bias_cs = seg(2).reshape(N_HEADS * n_new, 128)
    bias_ss = seg(3).reshape(N_HEADS * n_new, past + 128)
    bias_ws = seg(4).reshape(N_HEADS * n_new, WINDOW + 128)

    n_sel_p = t // SEL_LEN
    ct_p = np.zeros((128, 128), np.float32)
    ct_p[:n_sel_p, :t // CMP_STRIDE - 1] = _cmp_to_sel_np(t // CMP_STRIDE - 1, n_sel_p).T
    e_p = (np.arange(128)[:, None] == (np.arange(t)[None, :] // SEL_LEN)).astype(np.float32)
    n_cmp_s = (past + n_new + CMP_STRIDE - 1) // CMP_STRIDE - 1
    n_sel_s = (past + n_new + SEL_LEN - 1) // SEL_LEN
    c_s = np.zeros((128, 128), np.float32)
    c_s[:n_cmp_s, :n_sel_s] = _cmp_to_sel_np(n_cmp_s, n_sel_s)
    e_s = (np.arange(128)[:, None] == (np.arange(past + 128)[None, :] // SEL_LEN)).astype(np.float32)
    hq = np.arange(N_HEADS * n_new)
    smat = ((hq[None, :] // (GROUP * n_new)) * n_new + hq[None, :] % n_new
            == np.arange(N_KV * n_new)[:, None]).astype(np.float32)
    dsel = (hq[:, None] // (GROUP * n_new) == np.arange(256)[None, :] // HEAD_DIM).astype(np.float32)
    pos = np.arange(PAGE)
    selp = ((pos % CMP_STRIDE) * (PAGE // CMP_STRIDE) + pos // CMP_STRIDE)[None, :] == np.arange(PAGE)[:, None]
    selp = jnp.asarray(selp, bf16)
    consts_s = (bias_cs, jnp.asarray(mask_c), bias_ss, jnp.asarray(mask_s), bias_ws,
                jnp.asarray(mask_w), jnp.asarray(smat, bf16), jnp.asarray(smat.T, bf16), jnp.asarray(c_s, bf16),
                jnp.asarray(e_s, bf16), jnp.asarray(dsel))

    w_in_p, b_in_p = _split_in(w_in, b_in)
    w1 = cmp_w1.reshape(DEPTH, 2, 2, CMP_STRIDE, HEAD_DIM, HEAD_DIM)
    wlt = jnp.concatenate([_kron4(w1[:, :, 0]), _kron4(w1[:, :, 1])], axis=-1).astype(bf16)
    w2k = _kron4(cmp_w2).astype(bf16)
    pe = jnp.tile(cmp_pos.reshape(DEPTH, 2, 2, CMP_STRIDE, HEAD_DIM), (1, 1, 1, 1, N_KV))
    rnn_w = (conv_w, conv_b.reshape(DEPTH, 1, D_RNN), _block_diag_rnn(lru_wa).astype(bf16),
             lru_ba.reshape(DEPTH, 1, D_RNN), _block_diag_rnn(lru_wi).astype(bf16),
             lru_bi.reshape(DEPTH, 1, D_RNN), lru_lambda.reshape(DEPTH, 1, D_RNN))
    w_ao, w_ro, w_o = w_attn_o.astype(bf16), w_rnn_o.astype(bf16), w_out.astype(bf16)
    w_f1, w_f2 = w_ffn_in.astype(bf16), w_ffn_out.astype(bf16)
    ln_g4, ln_b4 = ln_g.reshape(DEPTH, 2, 1, D_MODEL), ln_b.reshape(DEPTH, 2, 1, D_MODEL)
    b_ada3 = b_ada.reshape(DEPTH, 1, 6 * D_MODEL)

    n_cpad = -(nb + ns) % 8
    c_all = jnp.concatenate([c_prompt, c_sample, jnp.zeros((n_cpad, D_MODEL), f32)], axis=0)
    cache_ct, cache_st, win_t = _feature_major(cache_cmp_kv), _feature_major(cache_slc_kv), _feature_major(cache_win_kv)
    cbuf_t = state_conv.transpose(0, 2, 1, 3)

    y_p = x_prompt.reshape(mp, D_MODEL)
    y_s = x_sample.transpose(1, 0, 2).reshape(ms, D_MODEL)
    st_p, st_s = [], []
    for l in range(DEPTH):
        ada = _ada(c_all, w_ada, b_ada3, l)
        mod_p = ada[:nb].reshape(nb, 1, 6 * D_MODEL)
        mod_s = ada[nb:nb + ns].reshape(1, ns, 6 * D_MODEL)
        cb = _cmp_bias(pe[l], wlt[l])

        proj, *kvt = _inproj(y_p, mod_p, w_in_p, b_in_p, l, tm=1024, seq_rows=t)
        rows_c = proj[:, C_KVC:C_KVC + KV_W].reshape(nb, t // CMP_STRIDE, CMP_STRIDE * KV_W)
        kc, vc = _compress_prompt(rows_c, wlt, cb, w2k, l)
        o_attn = _nsa_prompt(proj, kc, vc, bias_cp, bias_t, jnp.asarray(ct_p, bf16), jnp.asarray(e_p, bf16), nb, t)
        o_rnn, h_last = _rglru_prompt(proj, nb, t, rnn_w, l)
        merged = _merge(o_attn, o_rnn, proj, w_ao, w_ro, l, tm=1024)
        x1 = _outproj(merged, y_p, mod_p, w_o, ln_g4, ln_b4, l, tm=256)
        y_p = _ffn(x1, mod_p, w_f1, w_f2, ln_g4, ln_b4, l, tm=512)
        kv = lambda k, kvt=kvt: kvt[k].reshape(nb, 2, N_KV, HEAD_DIM, t).transpose(0, 4, 1, 2, 3)
        st_p.append((kv(0), kv(1), kv(2)[:, t - min(WINDOW, t):],
                     proj.reshape(nb, t, N_PACK)[:, t - (CONV_W - 1):, C_RX:C_RX + D_RNN],
                     h_last.reshape(nb, D_RNN)))

        proj, *kvt = _inproj(y_s, mod_s, w_in_p, b_in_p, l, tm=ms, seq_rows=ms)
        proj3 = proj.reshape(n_new, ns, N_PACK)
        q5 = proj3[:, :, C_Q:C_Q + 1024].reshape(n_new, ns, N_KV, GROUP, HEAD_DIM).transpose(1, 2, 3, 0, 4)
        qrow = (q5[:, :, :, :, None, :] * jnp.eye(N_KV, dtype=f32)[None, :, None, None, :, None]
                ).reshape(ns, N_HEADS * n_new, 256)
        ng = proj3[:, :, C_NG:C_NG + 48].reshape(n_new, ns, N_HEADS, 3).transpose(1, 2, 0, 3).reshape(
            ns, N_HEADS * n_new, 3)
        new_rows = lambda c: proj3[:, :, c:c + KV_W].transpose(1, 0, 2)
        kvc_s = _compress_sample(page_table, new_rows(C_KVC), cache_ct, l, n_phys, wlt, cb, w2k, selp)
        o64 = _nsa_sample(page_table, qrow, ng, kvc_s, new_rows(C_KVS), new_rows(C_KVW),
                          cache_st, win_t, l, n_phys, consts_s)
        o_attn = o64.reshape(ns, N_HEADS, n_new, HEAD_DIM).transpose(2, 0, 1, 3).reshape(ms, 1024).astype(bf16)
        o_rnn, h_last = _rglru_sample(proj, cbuf_t, state_rnn_h, rnn_w, l)
        merged = _merge(o_attn, o_rnn, proj, w_ao, w_ro, l, tm=ms)
        x1 = _outproj(merged, y_s, mod_s, w_o, ln_g4, ln_b4, l, tm=ms)
        y_s = _ffn(x1, mod_s, w_f1, w_f2, ln_g4, ln_b4, l, tm=ms)
        kv = lambda k, kvt=kvt: kvt[k].reshape(2, N_KV, HEAD_DIM, n_new, ns).transpose(4, 3, 0, 1, 2)
        xp = jnp.concatenate([state_conv[l], proj3[:, :, C_RX:C_RX + D_RNN].transpose(1, 0, 2)], axis=1)
        st_s.append((kv(0), kv(1), kv(2), xp[:, n_new:], h_last))

    stack = lambda sts, k: jnp.stack([s[k] for s in sts])
    return (y_p.reshape(nb, t, D_MODEL), y_s.reshape(n_new, ns, D_MODEL).transpose(1, 0, 2),
            stack(st_p, 0), stack(st_p, 1), stack(st_p, 2), stack(st_p, 3), stack(st_p, 4),
            stack(st_s, 0), stack(st_s, 1), stack(st_s, 2), stack(st_s, 3), stack(st_s, 4))
```

```python
import math

import numpy as np
import jax
import jax.numpy as jnp
from jax import lax
from jax.experimental import pallas as pl
from jax.experimental.pallas import tpu as pltpu

f32 = jnp.float32
bf16 = jnp.bfloat16
i32 = jnp.int32

D_MODEL = 2048
N_HEADS = 16
N_KV = 4
GROUP = 4
HEAD_DIM = 64
CMP_STRIDE = 16
CMP_LEN = 32
SEL_LEN = 64
N_SEL = 16
WINDOW = 512
PAGE = 128
FORCE_BONUS = 1e3
N_BUCKETS = 32
MAX_DISTANCE = 128
D_RNN = 1024
RNN_BLOCK = 64
CONV_W = 4
LRU_C = 8.0
D_FF = 5632
DEPTH = 2
ALPHA = (2.0 * DEPTH) ** 0.25
SCALE = HEAD_DIM ** -0.5
LOG2E = math.log2(math.e)
NEG = -1e30
KV_W = 2 * N_KV * HEAD_DIM

C_MG = 0
C_RX = 4096
C_RY = 5120
C_Q = 6144
C_KVC = 7168
C_KVS = 7680
C_KVW = 8192
C_NG = 8704
N_PACK = 9216

VMEM_LIMIT = 56 * 1024 * 1024

TQ = 128
TK = 256
N_PAGES = 16


def _cparams(sem):
    return pltpu.CompilerParams(dimension_semantics=sem, vmem_limit_bytes=VMEM_LIMIT)


def _sigmoid(x):
    return 1.0 / (1.0 + jnp.exp(-x))


def _silu(x):
    return x * _sigmoid(x)


def _layer_norm(z, g, b):
    mu = jnp.mean(z, axis=-1, keepdims=True)
    zc = z - mu
    var = jnp.mean(zc * zc, axis=-1, keepdims=True)
    return zc * lax.rsqrt(var + 1e-5) * g + b


def _dot(a, b):
    return jnp.dot(a, b, preferred_element_type=f32)


def _dot_nt(a, b):
    return lax.dot_general(a, b, (((1,), (1,)), ((), ())), preferred_element_type=f32)


def _split3(x):
    hi = x.astype(bf16)
    r1 = x - hi.astype(f32)
    mid = r1.astype(bf16)
    lo = (r1 - mid.astype(f32)).astype(bf16)
    return hi, mid, lo


def _rows(mod, tm):
    mr = mod.shape[0]
    if mr == 1 or mr == tm:
        return mod
    return jnp.concatenate([mod] * (tm // mr), axis=0)


def _mod_spec(mod, k, tiles_per_block, ngrid):
    mr = mod.shape[1]
    if ngrid == 1:
        return pl.BlockSpec((None, mr, D_MODEL), lambda i: (i // tiles_per_block, 0, k))
    return pl.BlockSpec((None, mr, D_MODEL), lambda i, j: (i // tiles_per_block, 0, k))


def _ada_kernel(c_ref, w_ref, b_ref, o_ref):
    h = _silu(c_ref[...]).astype(bf16)
    o_ref[...] = _dot(h, w_ref[...].astype(bf16)) + b_ref[...]


def _ada(c_all, w, b, layer):
    m = c_all.shape[0]
    n = w.shape[2]
    tn = 1024
    return pl.pallas_call(
        _ada_kernel,
        grid=(n // tn,),
        in_specs=[pl.BlockSpec((m, D_MODEL), lambda j: (0, 0)),
                  pl.BlockSpec((None, D_MODEL, tn), lambda j: (layer, 0, j)),
                  pl.BlockSpec((None, 1, tn), lambda j: (layer, 0, j))],
        out_specs=pl.BlockSpec((m, tn), lambda j: (0, j)),
        out_shape=jax.ShapeDtypeStruct((m, n), f32),
        compiler_params=_cparams(("arbitrary",)),
        name="ada",
    )(c_all, w, b)


def _bias_lookup_kernel(tbl_ref, bk_ref, o_ref):
    bk = bk_ref[...]
    for h in range(N_HEADS):
        acc = jnp.zeros(bk.shape, f32)
        for k in range(N_BUCKETS):
            acc = jnp.where(bk == k, tbl_ref[k, h], acc)
        o_ref[h] = acc


def _bias_lookup(tbl, buckets):
    n = buckets.shape[0]
    tr = 256
    return pl.pallas_call(
        _bias_lookup_kernel,
        grid=(n // tr,),
        in_specs=[pl.BlockSpec(memory_space=pltpu.SMEM),
                  pl.BlockSpec((tr, 128), lambda i: (i, 0))],
        out_specs=pl.BlockSpec((N_HEADS, tr, 128), lambda i: (0, i, 0)),
        out_shape=jax.ShapeDtypeStruct((N_HEADS, n, 128), f32),
        compiler_params=_cparams(("arbitrary",)),
        name="bias_lookup",
    )(tbl, buckets)


def _t5_bucket_np(dist):
    n = np.maximum(dist, 0)
    exact = N_BUCKETS // 2
    ratio = np.maximum(n, 1).astype(np.float32) / np.float32(exact)
    log_ratio = np.log(ratio).astype(np.float32) / np.float32(math.log(MAX_DISTANCE / exact))
    large = np.minimum(exact + (log_ratio * np.float32(N_BUCKETS - exact)).astype(np.int32), N_BUCKETS - 1)
    return np.where(n < exact, n, large).astype(np.int32)


IN_TN = 512
KV_J0 = C_KVC // IN_TN


IN_PIECES = (("mg", 0, 8), ("rxy", 8, 4), ("qkv", 12, 5), ("ng", 17, 1))


def _inproj_kernel(x_ref, sc_ref, sh_ref, wmg_ref, wrxy_ref, wqkv_ref, wng_ref, b_ref,
                   o_ref, okc_ref, oks_ref, okw_ref, h_scr):
    j = pl.program_id(1)

    @pl.when(j == 0)
    def _():
        tm = x_ref.shape[0]
        h_scr[...] = (x_ref[...] * (1.0 + _rows(sc_ref[...], tm)) + _rows(sh_ref[...], tm)).astype(bf16)

    def emit(w_ref):
        res = _dot(h_scr[...], w_ref[...]) + b_ref[...]
        o_ref[...] = res
        return res

    for w_ref, (_, j0, nj) in zip((wmg_ref, wrxy_ref, wqkv_ref, wng_ref), IN_PIECES):
        @pl.when(jnp.logical_and(j >= j0, j < j0 + nj))
        def _(w_ref=w_ref, j0=j0):
            res = emit(w_ref)
            if j0 == IN_PIECES[2][1]:
                for k, okt_ref in enumerate((okc_ref, oks_ref, okw_ref)):
                    @pl.when(j == KV_J0 + k)
                    def _(okt_ref=okt_ref):
                        okt_ref[...] = res.T


def _inproj(x, mod, w_pieces, b, layer, tm, seq_rows):
    m = x.shape[0]
    tn = IN_TN
    tpb = (m // tm) // mod.shape[0]
    tps = seq_rows // tm
    kt_spec = pl.BlockSpec((None, tn, tm), lambda i, j: (i // tps, 0, i % tps))
    kt_shape = jax.ShapeDtypeStruct((m // seq_rows, tn, seq_rows), f32)

    def w_spec(j0, nj):
        return pl.BlockSpec((None, D_MODEL, tn), lambda i, j: (layer, 0, jnp.clip(j - j0, 0, nj - 1)))

    return pl.pallas_call(
        _inproj_kernel,
        grid=(m // tm, N_PACK // tn),
        in_specs=[pl.BlockSpec((tm, D_MODEL), lambda i, j: (i, 0)),
                  _mod_spec(mod, 1, tpb, 2), _mod_spec(mod, 0, tpb, 2)]
        + [w_spec(j0, nj) for (_, j0, nj) in IN_PIECES]
        + [pl.BlockSpec((None, 1, tn), lambda i, j: (layer, 0, j))],
        out_specs=[pl.BlockSpec((tm, tn), lambda i, j: (i, j)), kt_spec, kt_spec, kt_spec],
        out_shape=[jax.ShapeDtypeStruct((m, N_PACK), f32), kt_shape, kt_shape, kt_shape],
        scratch_shapes=[pltpu.VMEM((tm, D_MODEL), bf16)],
        compiler_params=_cparams(("parallel", "arbitrary")),
        name="inproj",
    )(x, mod, mod, *w_pieces, b)


def _cmp_bias_kernel(pe_ref, wlt_ref, o_ref):
    for kv in range(2):
        acc = jnp.zeros((8, 512), f32)
        for l in range(CMP_STRIDE):
            w = wlt_ref[kv, l]
            lead = _dot(jnp.broadcast_to(pe_ref[kv, 0, l:l + 1, :], (8, 256)).astype(bf16), w[:, 0:256])
            tail = _dot(jnp.broadcast_to(pe_ref[kv, 1, l:l + 1, :], (8, 256)).astype(bf16), w[:, 256:512])
            acc = acc + jnp.concatenate([lead, tail], axis=1)
        o_ref[kv] = acc


def _cmp_bias(pe, wlt):
    return pl.pallas_call(
        _cmp_bias_kernel,
        out_shape=jax.ShapeDtypeStruct((2, 8, 512), f32),
        compiler_params=pltpu.CompilerParams(vmem_limit_bytes=VMEM_LIMIT),
        name="cmp_bias",
    )(pe, wlt)


def _compress_chunks(piece, n_rows, kv, wlt_ref, cb_ref, w2_ref):
    acc = jnp.zeros((n_rows, 512), f32)
    for l in range(CMP_STRIDE):
        acc = acc + _dot(piece(l), wlt_ref[kv, l])
    acc = acc + cb_ref[kv, 0:1, :]
    z = acc[:, 0:256] + pltpu.roll(acc[:, 256:512], n_rows - 1, 0)
    return _dot(_silu(z).astype(bf16), w2_ref[kv])


def _compress_prompt_kernel(x_ref, wlt_ref, cb_ref, w2_ref, kc_ref, vc_ref):
    n_chunk = x_ref.shape[0]
    for kv, dst in enumerate((kc_ref, vc_ref)):
        piece = lambda l: x_ref[:, l * KV_W + kv * 256:l * KV_W + (kv + 1) * 256].astype(bf16)
        out = _compress_chunks(piece, n_chunk, kv, wlt_ref, cb_ref, w2_ref)
        row = lax.broadcasted_iota(i32, out.shape, 0)
        dst[...] = jnp.where(row < n_chunk - 1, out, 0.0)


def _compress_prompt(rows, wlt, cb, w2, layer):
    nb, n_chunk, width = rows.shape
    lsel = lambda a: pl.BlockSpec((None,) + a.shape[1:], lambda b: (layer,) + (0,) * (a.ndim - 1))
    return pl.pallas_call(
        _compress_prompt_kernel,
        grid=(nb,),
        in_specs=[pl.BlockSpec((None, n_chunk, width), lambda b: (b, 0, 0)),
                  lsel(wlt), pl.BlockSpec(cb.shape, lambda b: (0, 0, 0)), lsel(w2)],
        out_specs=[pl.BlockSpec((None, n_chunk, 256), lambda b: (b, 0, 0))] * 2,
        out_shape=[jax.ShapeDtypeStruct((nb, n_chunk, 256), f32)] * 2,
        compiler_params=_cparams(("arbitrary",)),
        name="compress_prompt",
    )(rows, wlt, cb, w2)


def _topk_mask_rows(imp, sidx, n_keep):
    n_rows = imp.shape[0]
    rank = jnp.zeros(imp.shape, i32)
    for j in range(n_rows):
        row = imp[j:j + 1, :]
        beats = jnp.where(row > imp, 1, jnp.where(row == imp, jnp.where(sidx > j, 1, 0), 0))
        rank = rank + beats
    return jnp.where(rank < n_keep, jnp.where(imp > -1e29, 1.0, 0.0), 0.0)


def _nsa_prompt_kernel(q_ref, ng_ref, ks_ref, vs_ref, kw_ref, vw_ref, kc_ref, vc_ref,
                       bc_ref, bt_ref, ct_ref, e_ref, o_ref, selk_scr, s_scr, mx_scr):
    i = pl.program_id(1)
    q0 = i * TQ
    rows = GROUP * TQ
    n_sel_blk = ks_ref.shape[0] // SEL_LEN

    d0 = lax.broadcasted_iota(i32, (TQ, TK), 0) - lax.broadcasted_iota(i32, (TQ, TK), 1)
    dist_c = (q0 + (lax.broadcasted_iota(i32, (rows, 128), 0) & (TQ - 1))
              - CMP_STRIDE * lax.broadcasted_iota(i32, (rows, 128), 1) - (CMP_LEN - 1))
    mask_c = dist_c >= 0
    sig = _sigmoid(ng_ref[...])

    sidx = lax.broadcasted_iota(i32, (n_sel_blk, TQ), 0)
    cur = (q0 + lax.broadcasted_iota(i32, (n_sel_blk, TQ), 1)) // SEL_LEN
    forced = jnp.where(sidx == 0, 1, jnp.where(sidx == cur, 1, jnp.where(sidx == cur - 1, 1, 0)))

    def branch(qs, g, k_ref, v_ref, kt_hi, window):
        n_tiles = k_ref.shape[0] // TK
        ones = jnp.ones((TK, HEAD_DIM), f32)

        def score_tile(kt, slot):
            k0 = pl.multiple_of(jnp.clip(kt, 0, n_tiles - 1) * TK, TK)
            k = k_ref[pl.ds(k0, TK), g * 64:(g + 1) * 64].astype(bf16)
            delta = q0 - kt * TK
            dist = d0 + delta
            if window:
                live = jnp.where(kt >= 0, 0.0, NEG)
                addm = jnp.where(dist >= 0, jnp.where(dist < WINDOW, live, NEG), NEG)
            else:
                addm = jnp.where(dist >= 0, selk_scr[:, pl.ds(k0, TK)], NEG)
            va = jnp.clip(delta, 0, 256) // 128
            vb = jnp.clip(delta - 128, 0, 256) // 128
            bias = jnp.concatenate([bt_ref[g * 3 + va], bt_ref[g * 3 + vb]], axis=1)
            s = _dot_nt(qs, k) + bias + jnp.concatenate([addm] * GROUP, axis=0)
            s_scr[:, pl.ds(slot * TK, TK)] = s
            return jnp.maximum(s[:, 0:128], s[:, 128:256])

        def value_tile(kt, slot, mb):
            k0 = pl.multiple_of(jnp.clip(kt, 0, n_tiles - 1) * TK, TK)
            v = jnp.concatenate([v_ref[pl.ds(k0, TK), g * 64:(g + 1) * 64], ones], axis=1).astype(bf16)
            p = jnp.exp2(s_scr[:, pl.ds(slot * TK, TK)] - jnp.concatenate([mb, mb], axis=1))
            return _dot(p.astype(bf16), v)

        if window:
            tiles = [(kt_hi - 2 + j, j) for j in range(3)]
            mx = jnp.full((rows, 128), NEG, f32)
            for kt, slot in tiles:
                mx = jnp.maximum(mx, score_tile(kt, slot))
            mb = jnp.broadcast_to(jnp.max(mx, axis=-1, keepdims=True), (rows, 128))
            acc = jnp.zeros((rows, 2 * HEAD_DIM), f32)
            for kt, slot in tiles:
                acc = acc + value_tile(kt, slot, mb)
        else:
            n_quads = kt_hi // 4 + 1

            def scores(qd, mx):
                for j in range(4):
                    mx = jnp.maximum(mx, score_tile(4 * qd + j, 4 * qd + j))
                return mx

            mx = lax.fori_loop(0, n_quads, scores, jnp.full((rows, 128), NEG, f32))
            mx_scr[...] = jnp.broadcast_to(jnp.max(mx, axis=-1, keepdims=True), (rows, 128))

            def values(qd, acc):
                mb = mx_scr[...]
                for j in range(4):
                    acc = acc + value_tile(4 * qd + j, 4 * qd + j, mb)
                return acc

            acc = lax.fori_loop(0, n_quads, values, jnp.zeros((rows, 2 * HEAD_DIM), f32))
        return acc[:, 0:HEAD_DIM] / jnp.maximum(acc[:, HEAD_DIM:HEAD_DIM + 1], 1e-30)

    for g in range(N_KV):
        qg = q_ref[:, g * 256:(g + 1) * 256]
        qst = jnp.concatenate([qg[:, r * 64:(r + 1) * 64] for r in range(GROUP)], axis=0)
        qs = (qst * SCALE).astype(bf16)
        qs2 = (qst * (SCALE * LOG2E)).astype(bf16)

        kcg = kc_ref[:, g * 64:(g + 1) * 64].astype(bf16)
        vcg = vc_ref[:, g * 64:(g + 1) * 64].astype(bf16)
        bias_c = jnp.concatenate([bc_ref[g * GROUP + r] for r in range(GROUP)], axis=0)
        s = jnp.where(mask_c, _dot_nt(qs, kcg) + bias_c, NEG)
        p = jnp.where(mask_c, jnp.exp(s - jnp.max(s, axis=-1, keepdims=True)), 0.0)
        p = p / jnp.maximum(jnp.sum(p, axis=-1, keepdims=True), 1e-30)
        o_c = _dot(p.astype(bf16), vcg)

        p_sum = p[0:TQ] + p[TQ:2 * TQ] + p[2 * TQ:3 * TQ] + p[3 * TQ:4 * TQ]
        ct = ct_ref[...]
        imp_t = sum(_dot_nt(ct, piece) for piece in _split3(p_sum))[0:n_sel_blk]
        imp_t = jnp.where(forced > 0, imp_t + FORCE_BONUS, imp_t)
        imp_t = jnp.where(sidx <= cur, imp_t, NEG)
        sel_t = _topk_mask_rows(imp_t, sidx, N_SEL)
        sel_pad = jnp.concatenate([sel_t, jnp.zeros((128 - n_sel_blk, TQ), f32)], axis=0)
        sel_q = sel_pad.T.astype(bf16)
        selk_scr[...] = jnp.where(_dot(sel_q, e_ref[...]) > 0.5, 0.0, NEG)

        kt_hi = (q0 + TQ - 1) // TK
        o_s = branch(qs2, g, ks_ref, vs_ref, kt_hi, False)
        o_w = branch(qs2, g, kw_ref, vw_ref, kt_hi, True)

        gate = lambda c: jnp.concatenate(
            [sig[:, g * 12 + r * 3 + c:g * 12 + r * 3 + c + 1] for r in range(GROUP)], axis=0)
        o = gate(0) * o_c + gate(1) * o_s + gate(2) * o_w
        o_ref[:, g * 256:(g + 1) * 256] = jnp.concatenate(
            [o[r * TQ:(r + 1) * TQ] for r in range(GROUP)], axis=1).astype(bf16)


def _nsa_prompt(proj, kc, vc, bias_c, bias_t, ct, e, nb, t):
    nq = t // TQ
    full = lambda a: pl.BlockSpec(a.shape, lambda b, i: (0,) * a.ndim)
    kv = lambda c: pl.BlockSpec((t, 256), lambda b, i: (b, c // 256))
    return pl.pallas_call(
        _nsa_prompt_kernel,
        grid=(nb, nq),
        in_specs=[pl.BlockSpec((TQ, 1024), lambda b, i: (b * nq + i, C_Q // 1024)),
                  pl.BlockSpec((TQ, 128), lambda b, i: (b * nq + i, C_NG // 128)),
                  kv(C_KVS), kv(C_KVS + 256), kv(C_KVW), kv(C_KVW + 256),
                  pl.BlockSpec((None, t // CMP_STRIDE, 256), lambda b, i: (b, 0, 0)),
                  pl.BlockSpec((None, t // CMP_STRIDE, 256), lambda b, i: (b, 0, 0)),
                  pl.BlockSpec((N_HEADS, TQ, 128), lambda b, i: (0, i, 0)),
                  full(bias_t), full(ct), full(e)],
        out_specs=pl.BlockSpec((TQ, 1024), lambda b, i: (b * nq + i, 0)),
        out_shape=jax.ShapeDtypeStruct((nb * t, 1024), bf16),
        scratch_shapes=[pltpu.VMEM((TQ, t), f32), pltpu.VMEM((GROUP * TQ, t), f32),
                        pltpu.VMEM((GROUP * TQ, 128), f32)],
        compiler_params=_cparams(("parallel", "arbitrary")),
        name="nsa_prompt",
    )(proj, proj, proj, proj, proj, proj, kc, vc, bias_c, bias_t, ct, e)


def _lru_gates(xc, ry, wa_ref, ba_ref, wi_ref, bi_ref, lam_ref):
    xb = xc.astype(bf16)
    ra = jnp.concatenate([_dot(xb[:, c * 256:(c + 1) * 256], wa_ref[c]) for c in range(4)], axis=1)
    ri = jnp.concatenate([_dot(xb[:, c * 256:(c + 1) * 256], wi_ref[c]) for c in range(4)], axis=1)
    r = _sigmoid(ra + ba_ref[...])
    ig = _sigmoid(ri + bi_ref[...])
    nl = -lam_ref[...]
    softplus = jnp.maximum(nl, 0.0) + jnp.log1p(jnp.exp(-jnp.abs(nl)))
    log_a = -LRU_C * r * softplus
    a = jnp.exp(log_a)
    u = jnp.sqrt(jnp.tanh(-log_a) * (a * a + 1.0)) * (ig * xc)
    return a, u, jax.nn.gelu(ry)


def _rglru_prompt_kernel(rx_ref, ry_ref, cw_ref, cb_ref, wa_ref, ba_ref, wi_ref, bi_ref, lam_ref,
                         o_ref, hl_ref, xp_scr, a_scr, u_scr, hs_scr, h_scr):
    tt = pl.program_id(1)
    tr = rx_ref.shape[0]

    @pl.when(tt == 0)
    def _():
        xp_scr[0:8, :] = jnp.zeros((8, D_RNN), f32)
        h_scr[...] = jnp.zeros((1, D_RNN), f32)

    x = rx_ref[...]
    xp_scr[8:8 + tr, :] = x
    y = x * cw_ref[CONV_W - 1:CONV_W, :] + cb_ref[...]
    for k in range(CONV_W - 1):
        y = y + xp_scr[5 + k:5 + k + tr, :] * cw_ref[k:k + 1, :]
    xp_scr[0:8, :] = xp_scr[tr:tr + 8, :]

    a, u, gate = _lru_gates(y, ry_ref[...], wa_ref, ba_ref, wi_ref, bi_ref, lam_ref)
    a_scr[...] = a
    u_scr[...] = u

    def step(t, h):
        h = a_scr[pl.ds(t, 1), :] * h + u_scr[pl.ds(t, 1), :]
        hs_scr[pl.ds(t, 1), :] = h
        return h

    h = lax.fori_loop(0, tr, step, h_scr[...], unroll=8)
    h_scr[...] = h
    hl_ref[...] = h
    o_ref[...] = (hs_scr[...] * gate).astype(bf16)


def _rnn_specs(rnn_w, layer, ngrid):
    zeros = lambda n: (0,) * n
    if ngrid == 2:
        return [pl.BlockSpec((None,) + a.shape[1:], lambda b, i, n=a.ndim - 1: (layer,) + zeros(n)) for a in rnn_w]
    return [pl.BlockSpec((None,) + a.shape[1:], lambda i, n=a.ndim - 1: (layer,) + zeros(n)) for a in rnn_w]


def _rglru_prompt(proj, nb, t, rnn_w, layer):
    tr = 256
    nt = t // tr
    return pl.pallas_call(
        _rglru_prompt_kernel,
        grid=(nb, nt),
        in_specs=[pl.BlockSpec((tr, D_RNN), lambda b, i: (b * nt + i, C_RX // D_RNN)),
                  pl.BlockSpec((tr, D_RNN), lambda b, i: (b * nt + i, C_RY // D_RNN))]
        + _rnn_specs(rnn_w, layer, 2),
        out_specs=[pl.BlockSpec((tr, D_RNN), lambda b, i: (b * nt + i, 0)),
                   pl.BlockSpec((None, 1, D_RNN), lambda b, i: (b, 0, 0))],
        out_shape=[jax.ShapeDtypeStruct((nb * t, D_RNN), bf16),
                   jax.ShapeDtypeStruct((nb, 1, D_RNN), f32)],
        scratch_shapes=[pltpu.VMEM((tr + 8, D_RNN), f32), pltpu.VMEM((tr, D_RNN), f32),
                        pltpu.VMEM((tr, D_RNN), f32), pltpu.VMEM((tr, D_RNN), f32),
                        pltpu.VMEM((1, D_RNN), f32)],
        compiler_params=_cparams(("parallel", "arbitrary")),
        name="rglru_prompt",
    )(proj, proj, *rnn_w)


def _rglru_sample_kernel(rx_ref, ry_ref, cbuf_ref, h0_ref, cw_ref, cb_ref, wa_ref, ba_ref, wi_ref, bi_ref,
                         lam_ref, o_ref, hl_ref):
    ns = h0_ref.shape[0]
    n_t = rx_ref.shape[0] // ns
    xp = [cbuf_ref[k] for k in range(CONV_W - 1)] + [rx_ref[s * ns:(s + 1) * ns, :] for s in range(n_t)]
    h = h0_ref[...]
    for s in range(n_t):
        y = xp[s + CONV_W - 1] * cw_ref[CONV_W - 1:CONV_W, :] + cb_ref[...]
        for k in range(CONV_W - 1):
            y = y + xp[s + k] * cw_ref[k:k + 1, :]
        a, u, gate = _lru_gates(y, ry_ref[s * ns:(s + 1) * ns, :], wa_ref, ba_ref, wi_ref, bi_ref, lam_ref)
        h = a * h + u
        o_ref[s * ns:(s + 1) * ns, :] = (h * gate).astype(bf16)
    hl_ref[...] = h


def _rglru_sample(proj, cbuf, h0, rnn_w, layer):
    ms = proj.shape[0]
    ns = h0.shape[1]
    return pl.pallas_call(
        _rglru_sample_kernel,
        grid=(1,),
        in_specs=[pl.BlockSpec((ms, D_RNN), lambda i: (0, C_RX // D_RNN)),
                  pl.BlockSpec((ms, D_RNN), lambda i: (0, C_RY // D_RNN)),
                  pl.BlockSpec((None, CONV_W - 1, ns, D_RNN), lambda i: (layer, 0, 0, 0)),
                  pl.BlockSpec((None, ns, D_RNN), lambda i: (layer, 0, 0))]
        + _rnn_specs(rnn_w, layer, 1),
        out_specs=[pl.BlockSpec((ms, D_RNN), lambda i: (0, 0)),
                   pl.BlockSpec((ns, D_RNN), lambda i: (0, 0))],
        out_shape=[jax.ShapeDtypeStruct((ms, D_RNN), bf16),
                   jax.ShapeDtypeStruct((ns, D_RNN), f32)],
        compiler_params=_cparams(("arbitrary",)),
        name="rglru_sample",
    )(proj, proj, cbuf, h0, *rnn_w)


def _mixout_kernel(oa_ref, orn_ref, ga_ref, gr_ref, x_ref, gt_ref, wa_ref, wr_ref, wo_ref, lg_ref, lb_ref, o_ref):
    a1 = _dot(oa_ref[...], wa_ref[...])
    a2 = _dot(orn_ref[...], wr_ref[...])
    merged = (_sigmoid(ga_ref[...]) * a1 + _sigmoid(gr_ref[...]) * a2).astype(bf16)
    mix = _dot(merged, wo_ref[...])
    z = ALPHA * x_ref[...] + (1.0 + _rows(gt_ref[...], x_ref.shape[0])) * mix
    o_ref[...] = _layer_norm(z, lg_ref[...], lb_ref[...])


def _mixout(o_attn, o_rnn, proj, x, mod, w_ao, w_ro, w_o, ln_g, ln_b, layer, tm):
    m = o_attn.shape[0]
    tpb = (m // tm) // mod.shape[0]
    ln_spec = pl.BlockSpec((None, None, 1, D_MODEL), lambda i: (layer, 0, 0, 0))
    w_spec = lambda k: pl.BlockSpec((None, k, D_MODEL), lambda i: (layer, 0, 0), pipeline_mode=pl.Buffered(1))
    return pl.pallas_call(
        _mixout_kernel,
        grid=(m // tm,),
        in_specs=[pl.BlockSpec((tm, 1024), lambda i: (i, 0)),
                  pl.BlockSpec((tm, 1024), lambda i: (i, 0)),
                  pl.BlockSpec((tm, D_MODEL), lambda i: (i, C_MG // D_MODEL)),
                  pl.BlockSpec((tm, D_MODEL), lambda i: (i, C_MG // D_MODEL + 1)),
                  pl.BlockSpec((tm, D_MODEL), lambda i: (i, 0)),
                  _mod_spec(mod, 2, tpb, 1),
                  w_spec(1024), w_spec(1024), w_spec(D_MODEL),
                  ln_spec, ln_spec],
        out_specs=pl.BlockSpec((tm, D_MODEL), lambda i: (i, 0)),
        out_shape=jax.ShapeDtypeStruct((m, D_MODEL), f32),
        compiler_params=_cparams(("parallel",)),
        name="mixout",
    )(o_attn, o_rnn, proj, proj, x, mod, w_ao, w_ro, w_o, ln_g, ln_b)


def _ffn_kernel(x_ref, sc_ref, sh_ref, gt_ref, wg_ref, wu_ref, wo_ref, lg_ref, lb_ref, o_ref, h_scr):
    j = pl.program_id(1)
    tm = x_ref.shape[0]

    @pl.when(j == 0)
    def _():
        h_scr[...] = (x_ref[...] * (1.0 + _rows(sc_ref[...], tm)) + _rows(sh_ref[...], tm)).astype(bf16)
        o_ref[...] = jnp.zeros(o_ref.shape, f32)

    h = h_scr[...]
    act = (_silu(_dot(h, wg_ref[...])) * _dot(h, wu_ref[...])).astype(bf16)
    o_ref[...] += _dot(act, wo_ref[...])

    @pl.when(j == pl.num_programs(1) - 1)
    def _():
        z = ALPHA * x_ref[...] + (1.0 + _rows(gt_ref[...], tm)) * o_ref[...]
        o_ref[...] = _layer_norm(z, lg_ref[...], lb_ref[...])


def _ffn(x, mod, w_in, w_out, ln_g, ln_b, layer, tm):
    m = x.shape[0]
    tf = 512
    nf = D_FF // tf
    tpb = (m // tm) // mod.shape[0]
    ln_spec = pl.BlockSpec((None, None, 1, D_MODEL), lambda i, j: (layer, 1, 0, 0))
    return pl.pallas_call(
        _ffn_kernel,
        grid=(m // tm, nf),
        in_specs=[pl.BlockSpec((tm, D_MODEL), lambda i, j: (i, 0)),
                  _mod_spec(mod, 4, tpb, 2), _mod_spec(mod, 3, tpb, 2), _mod_spec(mod, 5, tpb, 2),
                  pl.BlockSpec((None, D_MODEL, tf), lambda i, j: (layer, 0, j)),
                  pl.BlockSpec((None, D_MODEL, tf), lambda i, j: (layer, 0, nf + j)),
                  pl.BlockSpec((None, tf, D_MODEL), lambda i, j: (layer, j, 0)),
                  ln_spec, ln_spec],
        out_specs=pl.BlockSpec((tm, D_MODEL), lambda i, j: (i, 0)),
        out_shape=jax.ShapeDtypeStruct((m, D_MODEL), f32),
        scratch_shapes=[pltpu.VMEM((tm, D_MODEL), bf16)],
        compiler_params=_cparams(("parallel", "arbitrary")),
        name="ffn",
    )(x, mod, mod, mod, w_in, w_in, w_out, ln_g, ln_b)


CMP_BATCH = 4


def _compress_sample_kernel(pt_ref, newc_ref, *rest):
    n_pg = CMP_BATCH * N_PAGES
    pages = rest[0:n_pg]
    wlt_ref, cb_ref, w2_ref, selp_ref, o_ref, x_scr = rest[n_pg:]
    n_new = newc_ref.shape[1]
    per = N_PAGES * (PAGE // CMP_STRIDE) + 8
    row8 = lax.broadcasted_iota(i32, (8, 256), 0)
    selp = selp_ref[...]
    for s in range(CMP_BATCH):
        for p in range(N_PAGES):
            y = _dot_nt(selp, pages[s * N_PAGES + p][...].astype(bf16))
            for l in range(CMP_STRIDE):
                x_scr[l, s * per + p * 8:s * per + (p + 1) * 8, :] = y[l * 8:(l + 1) * 8, :]
        for l in range(CMP_STRIDE):
            if l < n_new:
                extra = jnp.where(row8 == 0, jnp.broadcast_to(newc_ref[s, l:l + 1, :], (8, 256)), 0.0)
            else:
                extra = jnp.zeros((8, 256), f32)
            x_scr[l, s * per + per - 8:s * per + per, :] = extra
    acc = jnp.zeros((CMP_BATCH * per, 512), f32)
    for l in range(CMP_STRIDE):
        acc = acc + _dot(x_scr[l].astype(bf16), wlt_ref[l])
    acc = acc + cb_ref[0:1, :]
    for s in range(CMP_BATCH):
        a = acc[s * per:(s + 1) * per]
        z = a[:, 0:256] + pltpu.roll(a[:, 256:512], per - 1, 0)
        o_ref[s] = _dot(_silu(z).astype(bf16), w2_ref[...])[0:per - 8].astype(bf16)


def _compress_sample(page_table, newc, cache_c, layer, n_phys, wlt, cb, w2, selp):
    ns, n_new, _ = newc.shape
    per = N_PAGES * (PAGE // CMP_STRIDE) + 8

    def page_spec(s, p):
        return pl.BlockSpec((None, 256, PAGE),
                            lambda i, kv, pt: (layer * n_phys + pt[i * CMP_BATCH + s, p], kv, 0))

    in_specs = ([pl.BlockSpec((CMP_BATCH, n_new, 256), lambda i, kv, pt: (i, 0, kv))]
                + [page_spec(s, p) for s in range(CMP_BATCH) for p in range(N_PAGES)]
                + [pl.BlockSpec((None, None, CMP_STRIDE, 256, 512), lambda i, kv, pt: (layer, kv, 0, 0, 0)),
                   pl.BlockSpec((None, 8, 512), lambda i, kv, pt: (kv, 0, 0)),
                   pl.BlockSpec((None, None, 256, 256), lambda i, kv, pt: (layer, kv, 0, 0)),
                   pl.BlockSpec(selp.shape, lambda i, kv, pt: (0, 0))])
    grid_spec = pltpu.PrefetchScalarGridSpec(
        num_scalar_prefetch=1,
        grid=(ns // CMP_BATCH, 2),
        in_specs=in_specs,
        out_specs=pl.BlockSpec((None, CMP_BATCH, per - 8, 256), lambda i, kv, pt: (kv, i, 0, 0)),
        scratch_shapes=[pltpu.VMEM((CMP_STRIDE, CMP_BATCH * per, 256), f32)],
    )
    return pl.pallas_call(
        _compress_sample_kernel,
        grid_spec=grid_spec,
        out_shape=jax.ShapeDtypeStruct((2, ns, per - 8, 256), bf16),
        compiler_params=_cparams(("arbitrary", "arbitrary")),
        name="compress_sample",
    )(page_table, newc, *([cache_c] * (CMP_BATCH * N_PAGES)), wlt, cb, w2, selp)


ATT_BATCH = 2


def _nsa_sample_kernel(pt_ref, qrow_ref, ng_ref, kc_ref, vc_ref, news_ref, neww_ref, *rest):
    n_pg = ATT_BATCH * N_PAGES
    cs = rest[0:n_pg]
    (win_ref, bc_ref, mc_ref, bs_ref, ms_ref, bw_ref, mw_ref,
     smat_ref, smt_ref, c_ref, e_ref, dsel_ref, o_ref, new_scr) = rest[n_pg:]
    n_new = news_ref.shape[1]
    dsel = dsel_ref[...]
    mc = mc_ref[...]
    smat = smat_ref[...]
    cmat = c_ref[...]

    @pl.when(pl.program_id(0) == 0)
    def _():
        new_scr[...] = jnp.zeros(new_scr.shape, f32)

    def diag(x):
        y = x * dsel
        return y[:, 0:64] + y[:, 64:128] + y[:, 128:192] + y[:, 192:256]

    def one_sample(b):
        qs = (qrow_ref[b] * SCALE).astype(bf16)
        kc = kc_ref[b]
        vc = vc_ref[b]

        s = jnp.where(mc > 0.5, _dot_nt(qs, kc) + bc_ref[...], NEG)
        p = jnp.exp(s - jnp.max(s, axis=-1, keepdims=True)) * mc
        p = p / jnp.maximum(jnp.sum(p, axis=-1, keepdims=True), 1e-30)
        o_c = diag(_dot(p.astype(bf16), vc))

        p_sum = sum(_dot(smat, piece_) for piece_ in _split3(p))
        imp = sum(_dot(piece_, cmat) for piece_ in _split3(p_sum))
        n_blk = (N_PAGES * PAGE + n_new + SEL_LEN - 1) // SEL_LEN
        cur = (N_PAGES * PAGE) // SEL_LEN
        sidx = lax.broadcasted_iota(i32, imp.shape, 1)
        forced = jnp.where(sidx == 0, 1, jnp.where(sidx == cur, 1, jnp.where(sidx == cur - 1, 1, 0)))
        imp = jnp.where(forced > 0, imp + FORCE_BONUS, imp)
        imp = jnp.where(sidx <= cur, imp, NEG)
        rank = jnp.zeros(imp.shape, i32)
        for j in range(n_blk):
            col = imp[:, j:j + 1]
            rank = rank + jnp.where(col > imp, 1, jnp.where(col == imp, jnp.where(sidx > j, 1, 0), 0))
        sel = jnp.where(rank < N_SEL, jnp.where(imp > -1e29, jnp.where(sidx < n_blk, 1.0, 0.0), 0.0), 0.0)
        sel64 = _dot(smt_ref[...], sel.astype(bf16)).astype(bf16)
        selk = _dot(sel64, e_ref[...])

        def attend(kt_tiles, vt_tiles, new_rows, bias, mask):
            k_new, v_new = new_rows[:, 0:256].astype(bf16), new_rows[:, 256:512].astype(bf16)
            k_past = jnp.concatenate(kt_tiles, axis=1)
            v_past = jnp.concatenate(vt_tiles, axis=1)
            n_past = k_past.shape[1]
            s = jnp.concatenate([_dot(qs, k_past), _dot_nt(qs, k_new)], axis=1) + bias
            s = jnp.where(mask > 0.5, s, NEG)
            p = jnp.exp(s - jnp.max(s, axis=-1, keepdims=True)) * mask
            p = (p / jnp.maximum(jnp.sum(p, axis=-1, keepdims=True), 1e-30)).astype(bf16)
            acc = _dot_nt(p[:, 0:n_past], v_past) + _dot(p[:, n_past:n_past + 128], v_new)
            return diag(acc)

        new_scr[b, 0, 0:n_new, :] = news_ref[b]
        pages = cs[b * N_PAGES:(b + 1) * N_PAGES]
        k_tiles = [pg[0:256, :].astype(bf16) for pg in pages]
        v_tiles = [pg[256:512, :].astype(bf16) for pg in pages]
        o_s = attend(k_tiles, v_tiles, new_scr[b, 0], bs_ref[...], ms_ref[...] * selk)

        new_scr[b, 1, 0:n_new, :] = neww_ref[b]
        o_w = attend([win_ref[b, 0:256, :].astype(bf16)], [win_ref[b, 256:512, :].astype(bf16)],
                     new_scr[b, 1], bw_ref[...], mw_ref[...])

        sig = _sigmoid(ng_ref[b])
        o_ref[b] = sig[:, 0:1] * o_c + sig[:, 1:2] * o_s + sig[:, 2:3] * o_w

    for b in range(ATT_BATCH):
        one_sample(b)


def _nsa_sample(page_table, qrow, ng, kvc, news, neww, cache_s, win, layer, n_phys, consts):
    nb = qrow.shape[0]
    n_new = news.shape[1]
    ab = ATT_BATCH
    full = lambda a: pl.BlockSpec(a.shape, lambda i, pt: (0,) * a.ndim)
    per_b = lambda shape: pl.BlockSpec((ab,) + shape, lambda i, pt: (i, 0, 0))
    cmp_spec = lambda kv: pl.BlockSpec((None, ab) + kvc.shape[2:], lambda i, pt: (kv, i, 0, 0))

    def page_spec(s, p):
        return pl.BlockSpec((None, KV_W, PAGE), lambda i, pt: (layer * n_phys + pt[i * ab + s, p], 0, 0))

    in_specs = ([per_b((64, 256)), per_b((64, 3)), cmp_spec(0), cmp_spec(1),
                 per_b((n_new, KV_W)), per_b((n_new, KV_W))]
                + [page_spec(s, p) for s in range(ab) for p in range(N_PAGES)]
                + [pl.BlockSpec((ab, KV_W, WINDOW), lambda i, pt: ((layer * nb) // ab + i, 0, 0))]
                + [full(c) for c in consts])
    grid_spec = pltpu.PrefetchScalarGridSpec(
        num_scalar_prefetch=1,
        grid=(nb // ab,),
        in_specs=in_specs,
        out_specs=pl.BlockSpec((ab, 64, 64), lambda i, pt: (i, 0, 0)),
        scratch_shapes=[pltpu.VMEM((ab, 2, 128, KV_W), f32)],
    )
    return pl.pallas_call(
        _nsa_sample_kernel,
        grid_spec=grid_spec,
        out_shape=jax.ShapeDtypeStruct((nb, 64, 64), f32),
        compiler_params=_cparams(("arbitrary",)),
        name="nsa_sample",
    )(page_table, qrow, ng, kvc, kvc, news, neww, *([cache_s] * (ab * N_PAGES)), win, *consts)


def _static_tables(t, past, n_new):
    ar = np.arange
    tiles = np.stack([d + ar(128)[:, None] - ar(128)[None, :] for d in (0, 128, 256)])
    cmp_p = ar(t)[:, None] - (CMP_STRIDE * ar(128)[None, :] + CMP_LEN - 1)
    qpos = past + ar(n_new)
    cmp_s = qpos[:, None] - (CMP_STRIDE * ar(128)[None, :] + CMP_LEN - 1)
    kpos_s = np.concatenate([ar(past), past + ar(128)])
    slc_s = qpos[:, None] - kpos_s[None, :]
    kpos_w = np.concatenate([past - WINDOW + ar(WINDOW), past + ar(128)])
    win_s = qpos[:, None] - kpos_w[None, :]
    parts = [tiles.reshape(-1, 128), cmp_p, cmp_s.reshape(-1, 128), slc_s.reshape(-1, 128), win_s.reshape(-1, 128)]
    sizes = [p.shape[0] for p in parts]
    flat = np.concatenate(parts, axis=0)
    pad = (-flat.shape[0]) % 256
    flat = np.concatenate([flat, np.zeros((pad, 128), flat.dtype)], axis=0)
    buckets = _t5_bucket_np(flat)
    real_s = np.concatenate([np.ones(past, bool), ar(128) < n_new])
    real_w = np.concatenate([np.ones(WINDOW, bool), ar(128) < n_new])
    mask_c = (cmp_s >= 0)
    mask_s = (slc_s >= 0) & real_s[None, :]
    mask_w = (win_s >= 0) & (win_s < WINDOW) & real_w[None, :]
    rep = lambda mk: np.tile(mk[None].astype(np.float32), (N_HEADS, 1, 1)).reshape(N_HEADS * n_new, -1)
    return buckets, sizes, rep(mask_c), rep(mask_s), rep(mask_w)


def _cmp_to_sel_np(n_cmp, n_sel):
    j = np.arange(n_cmp)[:, None]
    s = np.arange(n_sel)[None, :]
    lo = np.maximum(j * CMP_STRIDE, s * SEL_LEN)
    hi = np.minimum(j * CMP_STRIDE + CMP_LEN, (s + 1) * SEL_LEN)
    return (np.maximum(hi - lo, 0) / CMP_LEN).astype(np.float32)


def _kron4(w):
    eye = jnp.eye(N_KV, dtype=w.dtype)
    out = jnp.einsum('gh,...de->...gdhe', eye, w)
    return out.reshape(w.shape[:-2] + (256, 256))


def _block_diag_rnn(w):
    w4 = w.reshape(DEPTH, 4, 4, RNN_BLOCK, RNN_BLOCK)
    eye = jnp.eye(4, dtype=w.dtype)
    return jnp.einsum('jk,zcjde->zcjdke', eye, w4).reshape(DEPTH, 4, 256, 256)


def _split_in(w, b):
    n_qkv = 1024 + 3 * KV_W
    n_ng = 3 * N_HEADS
    o_rxy = n_qkv + n_ng
    o_mg = o_rxy + 2 * D_RNN
    n_pad = N_PACK - (C_NG + n_ng)
    w_mg = w[..., o_mg:].astype(bf16)
    w_rxy = w[..., o_rxy:o_mg].astype(bf16)
    w_qkv = w[..., :n_qkv].astype(bf16)
    w_ng = jnp.pad(w[..., n_qkv:o_rxy].astype(bf16), ((0, 0), (0, 0), (0, IN_TN - n_ng)))
    bp = jnp.concatenate([b[..., o_mg:], b[..., o_rxy:o_mg], b[..., :n_qkv], b[..., n_qkv:o_rxy],
                          jnp.zeros(b.shape[:-1] + (n_pad,), b.dtype)], axis=-1)
    return (w_mg, w_rxy, w_qkv, w_ng), bp.reshape(DEPTH, 1, N_PACK)


def _feature_major(cache):
    d, n, rows = cache.shape[:3]
    return cache.transpose(0, 1, 3, 4, 5, 2).reshape(d * n, KV_W, rows)


def kernel(x_prompt, x_sample, c_prompt, c_sample, cache_cmp_kv, cache_slc_kv, cache_win_kv, state_conv, state_rnn_h, page_table, rel_bias, w_ada, b_ada, w_in, b_in, cmp_pos, cmp_w1, cmp_w2, w_attn_o, conv_w, conv_b, lru_wa, lru_ba, lru_wi, lru_bi, lru_lambda, w_rnn_o, w_out, w_ffn_in, w_ffn_out, ln_g, ln_b):
    nb, t, _ = x_prompt.shape
    ns, n_new, _ = x_sample.shape
    n_phys = cache_cmp_kv.shape[1]
    past = page_table.shape[1] * PAGE
    assert page_table.shape[1] == N_PAGES and cache_win_kv.shape[2] == WINDOW and t % (4 * TK) == 0
    assert n_new <= CMP_STRIDE and ns % 8 == 0 and ns % CMP_BATCH == 0 and ns % ATT_BATCH == 0
    mp, ms = nb * t, ns * n_new

    buckets, sizes, mask_c, mask_s, mask_w = _static_tables(t, past, n_new)
    bias_all = _bias_lookup(rel_bias, jnp.asarray(buckets))
    offs = np.cumsum([0] + sizes)
    seg = lambda k: bias_all[:, offs[k]:offs[k + 1]]
    bias_t = (seg(0) * LOG2E).reshape(N_KV, GROUP, 3, 128, 128).transpose(0, 2, 1, 3, 4).reshape(
        N_KV * 3, GROUP * 128, 128)
    bias_cp = seg(1)
    bias_cs = seg(2).reshape(N_HEADS * n_new, 128)
    bias_ss = seg(3).reshape(N_HEADS * n_new, past + 128)
    bias_ws = seg(4).reshape(N_HEADS * n_new, WINDOW + 128)

    n_sel_p = t // SEL_LEN
    ct_p = np.zeros((128, 128), np.float32)
    ct_p[:n_sel_p, :t // CMP_STRIDE - 1] = _cmp_to_sel_np(t // CMP_STRIDE - 1, n_sel_p).T
    e_p = (np.arange(128)[:, None] == (np.arange(t)[None, :] // SEL_LEN)).astype(np.float32)
    n_cmp_s = (past + n_new + CMP_STRIDE - 1) // CMP_STRIDE - 1
    n_sel_s = (past + n_new + SEL_LEN - 1) // SEL_LEN
    c_s = np.zeros((128, 128), np.float32)
    c_s[:n_cmp_s, :n_sel_s] = _cmp_to_sel_np(n_cmp_s, n_sel_s)
    e_s = (np.arange(128)[:, None] == (np.arange(past + 128)[None, :] // SEL_LEN)).astype(np.float32)
    hq = np.arange(N_HEADS * n_new)
    smat = ((hq[None, :] // (GROUP * n_new)) * n_new + hq[None, :] % n_new
            == np.arange(N_KV * n_new)[:, None]).astype(np.float32)
    dsel = (hq[:, None] // (GROUP * n_new) == np.arange(256)[None, :] // HEAD_DIM).astype(np.float32)
    pos = np.arange(PAGE)
    selp = ((pos % CMP_STRIDE) * (PAGE // CMP_STRIDE) + pos // CMP_STRIDE)[None, :] == np.arange(PAGE)[:, None]
    selp = jnp.asarray(selp, bf16)
    consts_s = (bias_cs, jnp.asarray(mask_c), bias_ss, jnp.asarray(mask_s), bias_ws,
                jnp.asarray(mask_w), jnp.asarray(smat, bf16), jnp.asarray(smat.T, bf16), jnp.asarray(c_s, bf16),
                jnp.asarray(e_s, bf16), jnp.asarray(dsel))

    w_in_p, b_in_p = _split_in(w_in, b_in)
    w1 = cmp_w1.reshape(DEPTH, 2, 2, CMP_STRIDE, HEAD_DIM, HEAD_DIM)
    wlt = jnp.concatenate([_kron4(w1[:, :, 0]), _kron4(w1[:, :, 1])], axis=-1).astype(bf16)
    w2k = _kron4(cmp_w2).astype(bf16)
    pe = jnp.tile(cmp_pos.reshape(DEPTH, 2, 2, CMP_STRIDE, HEAD_DIM), (1, 1, 1, 1, N_KV))
    rnn_w = (conv_w, conv_b.reshape(DEPTH, 1, D_RNN), _block_diag_rnn(lru_wa).astype(bf16),
             lru_ba.reshape(DEPTH, 1, D_RNN), _block_diag_rnn(lru_wi).astype(bf16),
             lru_bi.reshape(DEPTH, 1, D_RNN), lru_lambda.reshape(DEPTH, 1, D_RNN))
    w_ao, w_ro, w_o = w_attn_o.astype(bf16), w_rnn_o.astype(bf16), w_out.astype(bf16)
    w_f1, w_f2 = w_ffn_in.astype(bf16), w_ffn_out.astype(bf16)
    ln_g4, ln_b4 = ln_g.reshape(DEPTH, 2, 1, D_MODEL), ln_b.reshape(DEPTH, 2, 1, D_MODEL)
    b_ada3 = b_ada.reshape(DEPTH, 1, 6 * D_MODEL)

    n_cpad = -(nb + ns) % 8
    c_all = jnp.concatenate([c_prompt, c_sample, jnp.zeros((n_cpad, D_MODEL), f32)], axis=0)
    cache_ct, cache_st, win_t = _feature_major(cache_cmp_kv), _feature_major(cache_slc_kv), _feature_major(cache_win_kv)
    cbuf_t = state_conv.transpose(0, 2, 1, 3)

    y_p = x_prompt.reshape(mp, D_MODEL)
    y_s = x_sample.transpose(1, 0, 2).reshape(ms, D_MODEL)
    st_p, st_s = [], []
    for l in range(DEPTH):
        ada = _ada(c_all, w_ada, b_ada3, l)
        mod_p = ada[:nb].reshape(nb, 1, 6 * D_MODEL)
        mod_s = ada[nb:nb + ns].reshape(1, ns, 6 * D_MODEL)
        cb = _cmp_bias(pe[l], wlt[l])

        proj, *kvt = _inproj(y_p, mod_p, w_in_p, b_in_p, l, tm=1024, seq_rows=t)
        rows_c = proj[:, C_KVC:C_KVC + KV_W].reshape(nb, t // CMP_STRIDE, CMP_STRIDE * KV_W)
        kc, vc = _compress_prompt(rows_c, wlt, cb, w2k, l)
        o_attn = _nsa_prompt(proj, kc, vc, bias_cp, bias_t, jnp.asarray(ct_p, bf16), jnp.asarray(e_p, bf16), nb, t)
        o_rnn, h_last = _rglru_prompt(proj, nb, t, rnn_w, l)
        x1 = _mixout(o_attn, o_rnn, proj, y_p, mod_p, w_ao, w_ro, w_o, ln_g4, ln_b4, l, tm=256)
        y_p = _ffn(x1, mod_p, w_f1, w_f2, ln_g4, ln_b4, l, tm=512)
        kv = lambda k, kvt=kvt: kvt[k].reshape(nb, 2, N_KV, HEAD_DIM, t).transpose(0, 4, 1, 2, 3)
        st_p.append((kv(0), kv(1), kv(2)[:, t - min(WINDOW, t):],
                     proj.reshape(nb, t, N_PACK)[:, t - (CONV_W - 1):, C_RX:C_RX + D_RNN],
                     h_last.reshape(nb, D_RNN)))

        proj, *kvt = _inproj(y_s, mod_s, w_in_p, b_in_p, l, tm=ms, seq_rows=ms)
        proj3 = proj.reshape(n_new, ns, N_PACK)
        q5 = proj3[:, :, C_Q:C_Q + 1024].reshape(n_new, ns, N_KV, GROUP, HEAD_DIM).transpose(1, 2, 3, 0, 4)
        qrow = (q5[:, :, :, :, None, :] * jnp.eye(N_KV, dtype=f32)[None, :, None, None, :, None]
                ).reshape(ns, N_HEADS * n_new, 256)
        ng = proj3[:, :, C_NG:C_NG + 48].reshape(n_new, ns, N_HEADS, 3).transpose(1, 2, 0, 3).reshape(
            ns, N_HEADS * n_new, 3)
        new_rows = lambda c: proj3[:, :, c:c + KV_W].transpose(1, 0, 2)
        kvc_s = _compress_sample(page_table, new_rows(C_KVC), cache_ct, l, n_phys, wlt, cb, w2k, selp)
        o64 = _nsa_sample(page_table, qrow, ng, kvc_s, new_rows(C_KVS), new_rows(C_KVW),
                          cache_st, win_t, l, n_phys, consts_s)
        o_attn = o64.reshape(ns, N_HEADS, n_new, HEAD_DIM).transpose(2, 0, 1, 3).reshape(ms, 1024).astype(bf16)
        o_rnn, h_last = _rglru_sample(proj, cbuf_t, state_rnn_h, rnn_w, l)
        x1 = _mixout(o_attn, o_rnn, proj, y_s, mod_s, w_ao, w_ro, w_o, ln_g4, ln_b4, l, tm=min(256, ms))
        y_s = _ffn(x1, mod_s, w_f1, w_f2, ln_g4, ln_b4, l, tm=ms)
        kv = lambda k, kvt=kvt: kvt[k].reshape(2, N_KV, HEAD_DIM, n_new, ns).transpose(4, 3, 0, 1, 2)
        xp = jnp.concatenate([state_conv[l], proj3[:, :, C_RX:C_RX + D_RNN].transpose(1, 0, 2)], axis=1)
        st_s.append((kv(0), kv(1), kv(2), xp[:, n_new:], h_last))

    stack = lambda sts, k: jnp.stack([s[k] for s in sts])
    return (y_p.reshape(nb, t, D_MODEL), y_s.reshape(n_new, ns, D_MODEL).transpose(1, 0, 2),
            stack(st_p, 0), stack(st_p, 1), stack(st_p, 2), stack(st_p, 3), stack(st_p, 4),
            stack(st_s, 0), stack(st_s, 1), stack(st_s, 2), stack(st_s, 3), stack(st_s, 4))
```

```python
import math

import numpy as np
import jax
import jax.numpy as jnp
from jax import lax
from jax.experimental import pallas as pl
from jax.experimental.pallas import tpu as pltpu

f32 = jnp.float32
bf16 = jnp.bfloat16
i32 = jnp.int32

D_MODEL = 2048
N_HEADS = 16
N_KV = 4
GROUP = 4
HEAD_DIM = 64
CMP_STRIDE = 16
CMP_LEN = 32
SEL_LEN = 64
N_SEL = 16
WINDOW = 512
PAGE = 128
FORCE_BONUS = 1e3
N_BUCKETS = 32
MAX_DISTANCE = 128
D_RNN = 1024
RNN_BLOCK = 64
CONV_W = 4
LRU_C = 8.0
D_FF = 5632
DEPTH = 2
ALPHA = (2.0 * DEPTH) ** 0.25
SCALE = HEAD_DIM ** -0.5
LOG2E = math.log2(math.e)
NEG = -1e30
KV_W = 2 * N_KV * HEAD_DIM

C_MG = 0
C_RX = 4096
C_RY = 5120
C_Q = 6144
C_KVC = 7168
C_KVS = 7680
C_KVW = 8192
C_NG = 8704
N_PACK = 9216

VMEM_LIMIT = 56 * 1024 * 1024

TQ = 128
TK = 256
SLC_UNROLL = 4
N_PAGES = 16
TM_INPROJ = 1024
TM_MIXOUT = 256
TM_FFN = 512


def _cparams(sem):
    return pltpu.CompilerParams(dimension_semantics=sem, vmem_limit_bytes=VMEM_LIMIT)


def _sigmoid(x):
    return 1.0 / (1.0 + jnp.exp(-x))


def _silu(x):
    return x * _sigmoid(x)


def _layer_norm(z, g, b):
    mu = jnp.mean(z, axis=-1, keepdims=True)
    zc = z - mu
    var = jnp.mean(zc * zc, axis=-1, keepdims=True)
    return zc * lax.rsqrt(var + 1e-5) * g + b


def _dot(a, b):
    return jnp.dot(a, b, preferred_element_type=f32)


def _dot_nt(a, b):
    return lax.dot_general(a, b, (((1,), (1,)), ((), ())), preferred_element_type=f32)


def _split3(x):
    hi = x.astype(bf16)
    r1 = x - hi.astype(f32)
    mid = r1.astype(bf16)
    lo = (r1 - mid.astype(f32)).astype(bf16)
    return hi, mid, lo


def _rows(mod, tm):
    mr = mod.shape[0]
    if mr == 1 or mr == tm:
        return mod
    return jnp.concatenate([mod] * (tm // mr), axis=0)


def _mod_spec(mod, k, tiles_per_block, ngrid):
    mr = mod.shape[1]
    if ngrid == 1:
        return pl.BlockSpec((None, mr, D_MODEL), lambda i: (i // tiles_per_block, 0, k))
    return pl.BlockSpec((None, mr, D_MODEL), lambda i, j: (i // tiles_per_block, 0, k))


def _ada_kernel(c_ref, w_ref, b_ref, o_ref):
    h = _silu(c_ref[...]).astype(bf16)
    o_ref[...] = _dot(h, w_ref[...].astype(bf16)) + b_ref[...]


def _ada(c_all, w, b, layer):
    m = c_all.shape[0]
    n = w.shape[2]
    tn = 1024
    return pl.pallas_call(
        _ada_kernel,
        grid=(n // tn,),
        in_specs=[pl.BlockSpec((m, D_MODEL), lambda j: (0, 0)),
                  pl.BlockSpec((None, D_MODEL, tn), lambda j: (layer, 0, j)),
                  pl.BlockSpec((None, 1, tn), lambda j: (layer, 0, j))],
        out_specs=pl.BlockSpec((m, tn), lambda j: (0, j)),
        out_shape=jax.ShapeDtypeStruct((m, n), f32),
        compiler_params=_cparams(("arbitrary",)),
        name="ada",
    )(c_all, w, b)


def _bias_lookup_kernel(tbl_ref, bk_ref, o_ref):
    bk = bk_ref[...]
    for h in range(N_HEADS):
        acc = jnp.zeros(bk.shape, f32)
        for k in range(N_BUCKETS):
            acc = jnp.where(bk == k, tbl_ref[k, h], acc)
        o_ref[h] = acc


def _bias_lookup(tbl, buckets):
    n = buckets.shape[0]
    tr = 256
    return pl.pallas_call(
        _bias_lookup_kernel,
        grid=(n // tr,),
        in_specs=[pl.BlockSpec(memory_space=pltpu.SMEM),
                  pl.BlockSpec((tr, 128), lambda i: (i, 0))],
        out_specs=pl.BlockSpec((N_HEADS, tr, 128), lambda i: (0, i, 0)),
        out_shape=jax.ShapeDtypeStruct((N_HEADS, n, 128), f32),
        compiler_params=_cparams(("arbitrary",)),
        name="bias_lookup",
    )(tbl, buckets)


def _t5_bucket_np(dist):
    n = np.maximum(dist, 0)
    exact = N_BUCKETS // 2
    ratio = np.maximum(n, 1).astype(np.float32) / np.float32(exact)
    log_ratio = np.log(ratio).astype(np.float32) / np.float32(math.log(MAX_DISTANCE / exact))
    large = np.minimum(exact + (log_ratio * np.float32(N_BUCKETS - exact)).astype(np.int32), N_BUCKETS - 1)
    return np.where(n < exact, n, large).astype(np.int32)


IN_TN = 512
KV_J0 = C_KVC // IN_TN


IN_PIECES = (("mg", 0, 8), ("rxy", 8, 4), ("qkv", 12, 5), ("ng", 17, 1))


def _inproj_kernel(x_ref, sc_ref, sh_ref, wmg_ref, wrxy_ref, wqkv_ref, wng_ref, b_ref,
                   o_ref, okc_ref, oks_ref, okw_ref, h_scr):
    j = pl.program_id(1)

    @pl.when(j == 0)
    def _():
        tm = x_ref.shape[0]
        h_scr[...] = (x_ref[...] * (1.0 + _rows(sc_ref[...], tm)) + _rows(sh_ref[...], tm)).astype(bf16)

    def emit(w_ref):
        res = _dot(h_scr[...], w_ref[...]) + b_ref[...]
        o_ref[...] = res
        return res

    for w_ref, (_, j0, nj) in zip((wmg_ref, wrxy_ref, wqkv_ref, wng_ref), IN_PIECES):
        @pl.when(jnp.logical_and(j >= j0, j < j0 + nj))
        def _(w_ref=w_ref, j0=j0):
            res = emit(w_ref)
            if j0 == IN_PIECES[2][1]:
                for k, okt_ref in enumerate((okc_ref, oks_ref, okw_ref)):
                    @pl.when(j == KV_J0 + k)
                    def _(okt_ref=okt_ref):
                        okt_ref[...] = res.T


def _inproj(x, mod, w_pieces, b, layer, tm, seq_rows):
    m = x.shape[0]
    tn = IN_TN
    tpb = (m // tm) // mod.shape[0]
    tps = seq_rows // tm
    kt_spec = pl.BlockSpec((None, tn, tm), lambda i, j: (i // tps, 0, i % tps))
    kt_shape = jax.ShapeDtypeStruct((m // seq_rows, tn, seq_rows), f32)

    def w_spec(j0, nj):
        return pl.BlockSpec((None, D_MODEL, tn), lambda i, j: (layer, 0, jnp.clip(j - j0, 0, nj - 1)))

    return pl.pallas_call(
        _inproj_kernel,
        grid=(m // tm, N_PACK // tn),
        in_specs=[pl.BlockSpec((tm, D_MODEL), lambda i, j: (i, 0)),
                  _mod_spec(mod, 1, tpb, 2), _mod_spec(mod, 0, tpb, 2)]
        + [w_spec(j0, nj) for (_, j0, nj) in IN_PIECES]
        + [pl.BlockSpec((None, 1, tn), lambda i, j: (layer, 0, j))],
        out_specs=[pl.BlockSpec((tm, tn), lambda i, j: (i, j)), kt_spec, kt_spec, kt_spec],
        out_shape=[jax.ShapeDtypeStruct((m, N_PACK), f32), kt_shape, kt_shape, kt_shape],
        scratch_shapes=[pltpu.VMEM((tm, D_MODEL), bf16)],
        compiler_params=_cparams(("parallel", "arbitrary")),
        name="inproj",
    )(x, mod, mod, *w_pieces, b)


def _cmp_bias_kernel(pe_ref, wlt_ref, o_ref):
    for kv in range(2):
        acc = jnp.zeros((8, 512), f32)
        for l in range(CMP_STRIDE):
            w = wlt_ref[kv, l]
            lead = _dot(jnp.broadcast_to(pe_ref[kv, 0, l:l + 1, :], (8, 256)).astype(bf16), w[:, 0:256])
            tail = _dot(jnp.broadcast_to(pe_ref[kv, 1, l:l + 1, :], (8, 256)).astype(bf16), w[:, 256:512])
            acc = acc + jnp.concatenate([lead, tail], axis=1)
        o_ref[kv] = acc


def _cmp_bias(pe, wlt):
    return pl.pallas_call(
        _cmp_bias_kernel,
        out_shape=jax.ShapeDtypeStruct((2, 8, 512), f32),
        compiler_params=pltpu.CompilerParams(vmem_limit_bytes=VMEM_LIMIT),
        name="cmp_bias",
    )(pe, wlt)


def _compress_chunks(piece, n_rows, kv, wlt_ref, cb_ref, w2_ref):
    acc = jnp.zeros((n_rows, 512), f32)
    for l in range(CMP_STRIDE):
        acc = acc + _dot(piece(l), wlt_ref[kv, l])
    acc = acc + cb_ref[kv, 0:1, :]
    z = acc[:, 0:256] + pltpu.roll(acc[:, 256:512], n_rows - 1, 0)
    return _dot(_silu(z).astype(bf16), w2_ref[kv])


def _compress_prompt_kernel(x_ref, wlt_ref, cb_ref, w2_ref, kc_ref, vc_ref):
    n_chunk = x_ref.shape[0]
    for kv, dst in enumerate((kc_ref, vc_ref)):
        piece = lambda l: x_ref[:, l * KV_W + kv * 256:l * KV_W + (kv + 1) * 256].astype(bf16)
        out = _compress_chunks(piece, n_chunk, kv, wlt_ref, cb_ref, w2_ref)
        row = lax.broadcasted_iota(i32, out.shape, 0)
        dst[...] = jnp.where(row < n_chunk - 1, out, 0.0)


def _compress_prompt(rows, wlt, cb, w2, layer):
    nb, n_chunk, width = rows.shape
    lsel = lambda a: pl.BlockSpec((None,) + a.shape[1:], lambda b: (layer,) + (0,) * (a.ndim - 1))
    return pl.pallas_call(
        _compress_prompt_kernel,
        grid=(nb,),
        in_specs=[pl.BlockSpec((None, n_chunk, width), lambda b: (b, 0, 0)),
                  lsel(wlt), pl.BlockSpec(cb.shape, lambda b: (0, 0, 0)), lsel(w2)],
        out_specs=[pl.BlockSpec((None, n_chunk, 256), lambda b: (b, 0, 0))] * 2,
        out_shape=[jax.ShapeDtypeStruct((nb, n_chunk, 256), f32)] * 2,
        compiler_params=_cparams(("arbitrary",)),
        name="compress_prompt",
    )(rows, wlt, cb, w2)


def _topk_mask_rows(imp, sidx, n_keep):
    n_rows = imp.shape[0]
    rank = jnp.zeros(imp.shape, i32)
    for j in range(n_rows):
        row = imp[j:j + 1, :]
        beats = jnp.where(row > imp, 1, jnp.where(row == imp, jnp.where(sidx > j, 1, 0), 0))
        rank = rank + beats
    return jnp.where(rank < n_keep, jnp.where(imp > -1e29, 1.0, 0.0), 0.0)


def _nsa_prompt_kernel(q_ref, ng_ref, ks_ref, vs_ref, kw_ref, vw_ref, kc_ref, vc_ref,
                       bc_ref, bt_ref, ct_ref, e_ref, o_ref, selk_scr, s_scr, mx_scr):
    i = pl.program_id(1)
    q0 = i * TQ
    rows = GROUP * TQ
    n_sel_blk = ks_ref.shape[0] // SEL_LEN

    d0 = lax.broadcasted_iota(i32, (TQ, TK), 0) - lax.broadcasted_iota(i32, (TQ, TK), 1)
    dist_c = (q0 + (lax.broadcasted_iota(i32, (rows, 128), 0) & (TQ - 1))
              - CMP_STRIDE * lax.broadcasted_iota(i32, (rows, 128), 1) - (CMP_LEN - 1))
    mask_c = dist_c >= 0
    sig = _sigmoid(ng_ref[...])

    sidx = lax.broadcasted_iota(i32, (n_sel_blk, TQ), 0)
    cur = (q0 + lax.broadcasted_iota(i32, (n_sel_blk, TQ), 1)) // SEL_LEN
    forced = jnp.where(sidx == 0, 1, jnp.where(sidx == cur, 1, jnp.where(sidx == cur - 1, 1, 0)))

    def branch(qs, g, k_ref, v_ref, kt_hi, window):
        n_tiles = k_ref.shape[0] // TK
        ones = jnp.ones((TK, HEAD_DIM), f32)

        def score_tile(kt, slot):
            k0 = pl.multiple_of(jnp.clip(kt, 0, n_tiles - 1) * TK, TK)
            k = k_ref[pl.ds(k0, TK), g * 64:(g + 1) * 64].astype(bf16)
            delta = q0 - kt * TK
            dist = d0 + delta
            if window:
                live = jnp.where(kt >= 0, 0.0, NEG)
                addm = jnp.where(dist >= 0, jnp.where(dist < WINDOW, live, NEG), NEG)
            else:
                addm = jnp.where(dist >= 0, selk_scr[:, pl.ds(k0, TK)], NEG)
            va = jnp.clip(delta, 0, 256) // 128
            vb = jnp.clip(delta - 128, 0, 256) // 128
            bias = jnp.concatenate([bt_ref[g * 3 + va], bt_ref[g * 3 + vb]], axis=1)
            s = _dot_nt(qs, k) + bias + jnp.concatenate([addm] * GROUP, axis=0)
            s_scr[:, pl.ds(slot * TK, TK)] = s
            return jnp.maximum(s[:, 0:128], s[:, 128:256])

        def value_tile(kt, slot, mb):
            k0 = pl.multiple_of(jnp.clip(kt, 0, n_tiles - 1) * TK, TK)
            v = jnp.concatenate([v_ref[pl.ds(k0, TK), g * 64:(g + 1) * 64], ones], axis=1).astype(bf16)
            p = jnp.exp2(s_scr[:, pl.ds(slot * TK, TK)] - jnp.concatenate([mb, mb], axis=1))
            return _dot(p.astype(bf16), v)

        if window:
            tiles = [(kt_hi - 2 + j, j) for j in range(3)]
            mx = jnp.full((rows, 128), NEG, f32)
            for kt, slot in tiles:
                mx = jnp.maximum(mx, score_tile(kt, slot))
            mb = jnp.broadcast_to(jnp.max(mx, axis=-1, keepdims=True), (rows, 128))
            acc = jnp.zeros((rows, 2 * HEAD_DIM), f32)
            for kt, slot in tiles:
                acc = acc + value_tile(kt, slot, mb)
        else:
            n_groups = kt_hi // SLC_UNROLL + 1

            def scores(qd, mx):
                for j in range(SLC_UNROLL):
                    mx = jnp.maximum(mx, score_tile(SLC_UNROLL * qd + j, SLC_UNROLL * qd + j))
                return mx

            mx = lax.fori_loop(0, n_groups, scores, jnp.full((rows, 128), NEG, f32))
            mx_scr[...] = jnp.broadcast_to(jnp.max(mx, axis=-1, keepdims=True), (rows, 128))

            def values(qd, acc):
                mb = mx_scr[...]
                for j in range(SLC_UNROLL):
                    acc = acc + value_tile(SLC_UNROLL * qd + j, SLC_UNROLL * qd + j, mb)
                return acc

            acc = lax.fori_loop(0, n_groups, values, jnp.zeros((rows, 2 * HEAD_DIM), f32))
        return acc[:, 0:HEAD_DIM] / jnp.maximum(acc[:, HEAD_DIM:HEAD_DIM + 1], 1e-30)

    for g in range(N_KV):
        qg = q_ref[:, g * 256:(g + 1) * 256]
        qst = jnp.concatenate([qg[:, r * 64:(r + 1) * 64] for r in range(GROUP)], axis=0)
        qs = (qst * SCALE).astype(bf16)
        qs2 = (qst * (SCALE * LOG2E)).astype(bf16)

        kcg = kc_ref[:, g * 64:(g + 1) * 64].astype(bf16)
        vcg = vc_ref[:, g * 64:(g + 1) * 64].astype(bf16)
        bias_c = jnp.concatenate([bc_ref[g * GROUP + r] for r in range(GROUP)], axis=0)
        s = jnp.where(mask_c, _dot_nt(qs, kcg) + bias_c, NEG)
        p = jnp.where(mask_c, jnp.exp(s - jnp.max(s, axis=-1, keepdims=True)), 0.0)
        p = p / jnp.maximum(jnp.sum(p, axis=-1, keepdims=True), 1e-30)
        o_c = _dot(p.astype(bf16), vcg)

        p_sum = p[0:TQ] + p[TQ:2 * TQ] + p[2 * TQ:3 * TQ] + p[3 * TQ:4 * TQ]
        ct = ct_ref[...]
        imp_t = sum(_dot_nt(ct, piece) for piece in _split3(p_sum))[0:n_sel_blk]
        imp_t = jnp.where(forced > 0, imp_t + FORCE_BONUS, imp_t)
        imp_t = jnp.where(sidx <= cur, imp_t, NEG)
        sel_t = _topk_mask_rows(imp_t, sidx, N_SEL)
        sel_pad = jnp.concatenate([sel_t, jnp.zeros((128 - n_sel_blk, TQ), f32)], axis=0)
        sel_q = sel_pad.T.astype(bf16)
        selk_scr[...] = jnp.where(_dot(sel_q, e_ref[...]) > 0.5, 0.0, NEG)

        kt_hi = (q0 + TQ - 1) // TK
        o_s = branch(qs2, g, ks_ref, vs_ref, kt_hi, False)
        o_w = branch(qs2, g, kw_ref, vw_ref, kt_hi, True)

        gate = lambda c: jnp.concatenate(
            [sig[:, g * 12 + r * 3 + c:g * 12 + r * 3 + c + 1] for r in range(GROUP)], axis=0)
        o = gate(0) * o_c + gate(1) * o_s + gate(2) * o_w
        o_ref[:, g * 256:(g + 1) * 256] = jnp.concatenate(
            [o[r * TQ:(r + 1) * TQ] for r in range(GROUP)], axis=1).astype(bf16)


def _nsa_prompt(proj, kc, vc, bias_c, bias_t, ct, e, nb, t):
    nq = t // TQ
    full = lambda a: pl.BlockSpec(a.shape, lambda b, i: (0,) * a.ndim)
    kv = lambda c: pl.BlockSpec((t, 256), lambda b, i: (b, c // 256))
    return pl.pallas_call(
        _nsa_prompt_kernel,
        grid=(nb, nq),
        in_specs=[pl.BlockSpec((TQ, 1024), lambda b, i: (b * nq + i, C_Q // 1024)),
                  pl.BlockSpec((TQ, 128), lambda b, i: (b * nq + i, C_NG // 128)),
                  kv(C_KVS), kv(C_KVS + 256), kv(C_KVW), kv(C_KVW + 256),
                  pl.BlockSpec((None, t // CMP_STRIDE, 256), lambda b, i: (b, 0, 0)),
                  pl.BlockSpec((None, t // CMP_STRIDE, 256), lambda b, i: (b, 0, 0)),
                  pl.BlockSpec((N_HEADS, TQ, 128), lambda b, i: (0, i, 0)),
                  full(bias_t), full(ct), full(e)],
        out_specs=pl.BlockSpec((TQ, 1024), lambda b, i: (b * nq + i, 0)),
        out_shape=jax.ShapeDtypeStruct((nb * t, 1024), bf16),
        scratch_shapes=[pltpu.VMEM((TQ, t), f32), pltpu.VMEM((GROUP * TQ, t), f32),
                        pltpu.VMEM((GROUP * TQ, 128), f32)],
        compiler_params=_cparams(("parallel", "arbitrary")),
        name="nsa_prompt",
    )(proj, proj, proj, proj, proj, proj, kc, vc, bias_c, bias_t, ct, e)


def _lru_gates(xc, ry, wa_ref, ba_ref, wi_ref, bi_ref, lam_ref):
    xb = xc.astype(bf16)
    ra = jnp.concatenate([_dot(xb[:, c * 256:(c + 1) * 256], wa_ref[c]) for c in range(4)], axis=1)
    ri = jnp.concatenate([_dot(xb[:, c * 256:(c + 1) * 256], wi_ref[c]) for c in range(4)], axis=1)
    r = _sigmoid(ra + ba_ref[...])
    ig = _sigmoid(ri + bi_ref[...])
    nl = -lam_ref[...]
    softplus = jnp.maximum(nl, 0.0) + jnp.log1p(jnp.exp(-jnp.abs(nl)))
    log_a = -LRU_C * r * softplus
    a = jnp.exp(log_a)
    u = jnp.sqrt(jnp.tanh(-log_a) * (a * a + 1.0)) * (ig * xc)
    return a, u, jax.nn.gelu(ry)


def _rglru_prompt_kernel(rx_ref, ry_ref, cw_ref, cb_ref, wa_ref, ba_ref, wi_ref, bi_ref, lam_ref,
                         o_ref, hl_ref, xp_scr, a_scr, u_scr, hs_scr, h_scr):
    tt = pl.program_id(1)
    tr = rx_ref.shape[0]

    @pl.when(tt == 0)
    def _():
        xp_scr[0:8, :] = jnp.zeros((8, D_RNN), f32)
        h_scr[...] = jnp.zeros((1, D_RNN), f32)

    x = rx_ref[...]
    xp_scr[8:8 + tr, :] = x
    y = x * cw_ref[CONV_W - 1:CONV_W, :] + cb_ref[...]
    for k in range(CONV_W - 1):
        y = y + xp_scr[5 + k:5 + k + tr, :] * cw_ref[k:k + 1, :]
    xp_scr[0:8, :] = xp_scr[tr:tr + 8, :]

    a, u, gate = _lru_gates(y, ry_ref[...], wa_ref, ba_ref, wi_ref, bi_ref, lam_ref)
    a_scr[...] = a
    u_scr[...] = u

    def step(t, h):
        h = a_scr[pl.ds(t, 1), :] * h + u_scr[pl.ds(t, 1), :]
        hs_scr[pl.ds(t, 1), :] = h
        return h

    h = lax.fori_loop(0, tr, step, h_scr[...], unroll=8)
    h_scr[...] = h
    hl_ref[...] = h
    o_ref[...] = (hs_scr[...] * gate).astype(bf16)


def _rnn_specs(rnn_w, layer, ngrid):
    zeros = lambda n: (0,) * n
    if ngrid == 2:
        return [pl.BlockSpec((None,) + a.shape[1:], lambda b, i, n=a.ndim - 1: (layer,) + zeros(n)) for a in rnn_w]
    return [pl.BlockSpec((None,) + a.shape[1:], lambda i, n=a.ndim - 1: (layer,) + zeros(n)) for a in rnn_w]


def _rglru_prompt(proj, nb, t, rnn_w, layer):
    tr = 256
    nt = t // tr
    return pl.pallas_call(
        _rglru_prompt_kernel,
        grid=(nb, nt),
        in_specs=[pl.BlockSpec((tr, D_RNN), lambda b, i: (b * nt + i, C_RX // D_RNN)),
                  pl.BlockSpec((tr, D_RNN), lambda b, i: (b * nt + i, C_RY // D_RNN))]
        + _rnn_specs(rnn_w, layer, 2),
        out_specs=[pl.BlockSpec((tr, D_RNN), lambda b, i: (b * nt + i, 0)),
                   pl.BlockSpec((None, 1, D_RNN), lambda b, i: (b, 0, 0))],
        out_shape=[jax.ShapeDtypeStruct((nb * t, D_RNN), bf16),
                   jax.ShapeDtypeStruct((nb, 1, D_RNN), f32)],
        scratch_shapes=[pltpu.VMEM((tr + 8, D_RNN), f32), pltpu.VMEM((tr, D_RNN), f32),
                        pltpu.VMEM((tr, D_RNN), f32), pltpu.VMEM((tr, D_RNN), f32),
                        pltpu.VMEM((1, D_RNN), f32)],
        compiler_params=_cparams(("parallel", "arbitrary")),
        name="rglru_prompt",
    )(proj, proj, *rnn_w)


def _rglru_sample_kernel(rx_ref, ry_ref, cbuf_ref, h0_ref, cw_ref, cb_ref, wa_ref, ba_ref, wi_ref, bi_ref,
                         lam_ref, o_ref, hl_ref):
    ns = h0_ref.shape[0]
    n_t = rx_ref.shape[0] // ns
    xp = [cbuf_ref[k] for k in range(CONV_W - 1)] + [rx_ref[s * ns:(s + 1) * ns, :] for s in range(n_t)]
    h = h0_ref[...]
    for s in range(n_t):
        y = xp[s + CONV_W - 1] * cw_ref[CONV_W - 1:CONV_W, :] + cb_ref[...]
        for k in range(CONV_W - 1):
            y = y + xp[s + k] * cw_ref[k:k + 1, :]
        a, u, gate = _lru_gates(y, ry_ref[s * ns:(s + 1) * ns, :], wa_ref, ba_ref, wi_ref, bi_ref, lam_ref)
        h = a * h + u
        o_ref[s * ns:(s + 1) * ns, :] = (h * gate).astype(bf16)
    hl_ref[...] = h


def _rglru_sample(proj, cbuf, h0, rnn_w, layer):
    ms = proj.shape[0]
    ns = h0.shape[1]
    return pl.pallas_call(
        _rglru_sample_kernel,
        grid=(1,),
        in_specs=[pl.BlockSpec((ms, D_RNN), lambda i: (0, C_RX // D_RNN)),
                  pl.BlockSpec((ms, D_RNN), lambda i: (0, C_RY // D_RNN)),
                  pl.BlockSpec((None, CONV_W - 1, ns, D_RNN), lambda i: (layer, 0, 0, 0)),
                  pl.BlockSpec((None, ns, D_RNN), lambda i: (layer, 0, 0))]
        + _rnn_specs(rnn_w, layer, 1),
        out_specs=[pl.BlockSpec((ms, D_RNN), lambda i: (0, 0)),
                   pl.BlockSpec((ns, D_RNN), lambda i: (0, 0))],
        out_shape=[jax.ShapeDtypeStruct((ms, D_RNN), bf16),
                   jax.ShapeDtypeStruct((ns, D_RNN), f32)],
        compiler_params=_cparams(("arbitrary",)),
        name="rglru_sample",
    )(proj, proj, cbuf, h0, *rnn_w)


def _mixout_kernel(oa_ref, orn_ref, ga_ref, gr_ref, x_ref, gt_ref, wa_ref, wr_ref, wo_ref, lg_ref, lb_ref, o_ref):
    a1 = _dot(oa_ref[...], wa_ref[...])
    a2 = _dot(orn_ref[...], wr_ref[...])
    merged = (_sigmoid(ga_ref[...]) * a1 + _sigmoid(gr_ref[...]) * a2).astype(bf16)
    mix = _dot(merged, wo_ref[...])
    z = ALPHA * x_ref[...] + (1.0 + _rows(gt_ref[...], x_ref.shape[0])) * mix
    o_ref[...] = _layer_norm(z, lg_ref[...], lb_ref[...])


def _mixout(o_attn, o_rnn, proj, x, mod, w_ao, w_ro, w_o, ln_g, ln_b, layer, tm):
    m = o_attn.shape[0]
    tpb = (m // tm) // mod.shape[0]
    ln_spec = pl.BlockSpec((None, None, 1, D_MODEL), lambda i: (layer, 0, 0, 0))
    w_spec = lambda k: pl.BlockSpec((None, k, D_MODEL), lambda i: (layer, 0, 0), pipeline_mode=pl.Buffered(1))
    return pl.pallas_call(
        _mixout_kernel,
        grid=(m // tm,),
        in_specs=[pl.BlockSpec((tm, 1024), lambda i: (i, 0)),
                  pl.BlockSpec((tm, 1024), lambda i: (i, 0)),
                  pl.BlockSpec((tm, D_MODEL), lambda i: (i, C_MG // D_MODEL)),
                  pl.BlockSpec((tm, D_MODEL), lambda i: (i, C_MG // D_MODEL + 1)),
                  pl.BlockSpec((tm, D_MODEL), lambda i: (i, 0)),
                  _mod_spec(mod, 2, tpb, 1),
                  w_spec(1024), w_spec(1024), w_spec(D_MODEL),
                  ln_spec, ln_spec],
        out_specs=pl.BlockSpec((tm, D_MODEL), lambda i: (i, 0)),
        out_shape=jax.ShapeDtypeStruct((m, D_MODEL), f32),
        compiler_params=_cparams(("parallel",)),
        name="mixout",
    )(o_attn, o_rnn, proj, proj, x, mod, w_ao, w_ro, w_o, ln_g, ln_b)


def _ffn_kernel(x_ref, sc_ref, sh_ref, gt_ref, wg_ref, wu_ref, wo_ref, lg_ref, lb_ref, o_ref, h_scr):
    j = pl.program_id(1)
    tm = x_ref.shape[0]

    @pl.when(j == 0)
    def _():
        h_scr[...] = (x_ref[...] * (1.0 + _rows(sc_ref[...], tm)) + _rows(sh_ref[...], tm)).astype(bf16)
        o_ref[...] = jnp.zeros(o_ref.shape, f32)

    h = h_scr[...]
    act = (_silu(_dot(h, wg_ref[...])) * _dot(h, wu_ref[...])).astype(bf16)
    o_ref[...] += _dot(act, wo_ref[...])

    @pl.when(j == pl.num_programs(1) - 1)
    def _():
        z = ALPHA * x_ref[...] + (1.0 + _rows(gt_ref[...], tm)) * o_ref[...]
        o_ref[...] = _layer_norm(z, lg_ref[...], lb_ref[...])


def _ffn(x, mod, w_in, w_out, ln_g, ln_b, layer, tm):
    m = x.shape[0]
    tf = 512
    nf = D_FF // tf
    tpb = (m // tm) // mod.shape[0]
    ln_spec = pl.BlockSpec((None, None, 1, D_MODEL), lambda i, j: (layer, 1, 0, 0))
    return pl.pallas_call(
        _ffn_kernel,
        grid=(m // tm, nf),
        in_specs=[pl.BlockSpec((tm, D_MODEL), lambda i, j: (i, 0)),
                  _mod_spec(mod, 4, tpb, 2), _mod_spec(mod, 3, tpb, 2), _mod_spec(mod, 5, tpb, 2),
                  pl.BlockSpec((None, D_MODEL, tf), lambda i, j: (layer, 0, j)),
                  pl.BlockSpec((None, D_MODEL, tf), lambda i, j: (layer, 0, nf + j)),
                  pl.BlockSpec((None, tf, D_MODEL), lambda i, j: (layer, j, 0)),
                  ln_spec, ln_spec],
        out_specs=pl.BlockSpec((tm, D_MODEL), lambda i, j: (i, 0)),
        out_shape=jax.ShapeDtypeStruct((m, D_MODEL), f32),
        scratch_shapes=[pltpu.VMEM((tm, D_MODEL), bf16)],
        compiler_params=_cparams(("parallel", "arbitrary")),
        name="ffn",
    )(x, mod, mod, mod, w_in, w_in, w_out, ln_g, ln_b)


CMP_BATCH = 4


def _compress_sample_kernel(pt_ref, newc_ref, *rest):
    n_pg = CMP_BATCH * N_PAGES
    pages = rest[0:n_pg]
    wlt_ref, cb_ref, w2_ref, selp_ref, o_ref, x_scr = rest[n_pg:]
    n_new = newc_ref.shape[1]
    per = N_PAGES * (PAGE // CMP_STRIDE) + 8
    row8 = lax.broadcasted_iota(i32, (8, 256), 0)
    selp = selp_ref[...]
    for s in range(CMP_BATCH):
        for p in range(N_PAGES):
            y = _dot_nt(selp, pages[s * N_PAGES + p][...].astype(bf16))
            for l in range(CMP_STRIDE):
                x_scr[l, s * per + p * 8:s * per + (p + 1) * 8, :] = y[l * 8:(l + 1) * 8, :]
        for l in range(CMP_STRIDE):
            if l < n_new:
                extra = jnp.where(row8 == 0, jnp.broadcast_to(newc_ref[s, l:l + 1, :], (8, 256)), 0.0)
            else:
                extra = jnp.zeros((8, 256), f32)
            x_scr[l, s * per + per - 8:s * per + per, :] = extra
    acc = jnp.zeros((CMP_BATCH * per, 512), f32)
    for l in range(CMP_STRIDE):
        acc = acc + _dot(x_scr[l].astype(bf16), wlt_ref[l])
    acc = acc + cb_ref[0:1, :]
    for s in range(CMP_BATCH):
        a = acc[s * per:(s + 1) * per]
        z = a[:, 0:256] + pltpu.roll(a[:, 256:512], per - 1, 0)
        o_ref[s] = _dot(_silu(z).astype(bf16), w2_ref[...])[0:per - 8].astype(bf16)


def _compress_sample(page_table, newc, cache_c, layer, n_phys, wlt, cb, w2, selp):
    ns, n_new, _ = newc.shape
    per = N_PAGES * (PAGE // CMP_STRIDE) + 8

    def page_spec(s, p):
        return pl.BlockSpec((None, 256, PAGE),
                            lambda i, kv, pt: (layer * n_phys + pt[i * CMP_BATCH + s, p], kv, 0))

    in_specs = ([pl.BlockSpec((CMP_BATCH, n_new, 256), lambda i, kv, pt: (i, 0, kv))]
                + [page_spec(s, p) for s in range(CMP_BATCH) for p in range(N_PAGES)]
                + [pl.BlockSpec((None, None, CMP_STRIDE, 256, 512), lambda i, kv, pt: (layer, kv, 0, 0, 0)),
                   pl.BlockSpec((None, 8, 512), lambda i, kv, pt: (kv, 0, 0)),
                   pl.BlockSpec((None, None, 256, 256), lambda i, kv, pt: (layer, kv, 0, 0)),
                   pl.BlockSpec(selp.shape, lambda i, kv, pt: (0, 0))])
    grid_spec = pltpu.PrefetchScalarGridSpec(
        num_scalar_prefetch=1,
        grid=(ns // CMP_BATCH, 2),
        in_specs=in_specs,
        out_specs=pl.BlockSpec((None, CMP_BATCH, per - 8, 256), lambda i, kv, pt: (kv, i, 0, 0)),
        scratch_shapes=[pltpu.VMEM((CMP_STRIDE, CMP_BATCH * per, 256), f32)],
    )
    return pl.pallas_call(
        _compress_sample_kernel,
        grid_spec=grid_spec,
        out_shape=jax.ShapeDtypeStruct((2, ns, per - 8, 256), bf16),
        compiler_params=_cparams(("arbitrary", "arbitrary")),
        name="compress_sample",
    )(page_table, newc, *([cache_c] * (CMP_BATCH * N_PAGES)), wlt, cb, w2, selp)


ATT_BATCH = 4


def _nsa_sample_kernel(pt_ref, qrow_ref, ng_ref, kc_ref, vc_ref, news_ref, neww_ref, *rest):
    n_pg = ATT_BATCH * N_PAGES
    cs = rest[0:n_pg]
    (win_ref, bc_ref, mc_ref, bs_ref, ms_ref, bw_ref, mw_ref,
     smat_ref, smt_ref, c_ref, e_ref, dsel_ref, o_ref, new_scr) = rest[n_pg:]
    n_new = news_ref.shape[1]
    dsel = dsel_ref[...]
    mc = mc_ref[...]
    smat = smat_ref[...]
    cmat = c_ref[...]

    @pl.when(pl.program_id(0) == 0)
    def _():
        new_scr[...] = jnp.zeros(new_scr.shape, f32)

    def diag(x):
        y = x * dsel
        return y[:, 0:64] + y[:, 64:128] + y[:, 128:192] + y[:, 192:256]

    def one_sample(b):
        qs = (qrow_ref[b] * SCALE).astype(bf16)
        kc = kc_ref[b]
        vc = vc_ref[b]

        s = jnp.where(mc > 0.5, _dot_nt(qs, kc) + bc_ref[...], NEG)
        p = jnp.exp(s - jnp.max(s, axis=-1, keepdims=True)) * mc
        p = p / jnp.maximum(jnp.sum(p, axis=-1, keepdims=True), 1e-30)
        o_c = diag(_dot(p.astype(bf16), vc))

        p_sum = sum(_dot(smat, piece_) for piece_ in _split3(p))
        imp = sum(_dot(piece_, cmat) for piece_ in _split3(p_sum))
        n_blk = (N_PAGES * PAGE + n_new + SEL_LEN - 1) // SEL_LEN
        cur = (N_PAGES * PAGE) // SEL_LEN
        sidx = lax.broadcasted_iota(i32, imp.shape, 1)
        forced = jnp.where(sidx == 0, 1, jnp.where(sidx == cur, 1, jnp.where(sidx == cur - 1, 1, 0)))
        imp = jnp.where(forced > 0, imp + FORCE_BONUS, imp)
        imp = jnp.where(sidx <= cur, imp, NEG)
        rank = jnp.zeros(imp.shape, i32)
        for j in range(n_blk):
            col = imp[:, j:j + 1]
            rank = rank + jnp.where(col > imp, 1, jnp.where(col == imp, jnp.where(sidx > j, 1, 0), 0))
        sel = jnp.where(rank < N_SEL, jnp.where(imp > -1e29, jnp.where(sidx < n_blk, 1.0, 0.0), 0.0), 0.0)
        sel64 = _dot(smt_ref[...], sel.astype(bf16)).astype(bf16)
        selk = _dot(sel64, e_ref[...])

        def attend(kt_tiles, vt_tiles, new_rows, bias, mask):
            k_new, v_new = new_rows[:, 0:256].astype(bf16), new_rows[:, 256:512].astype(bf16)
            k_past = jnp.concatenate(kt_tiles, axis=1)
            v_past = jnp.concatenate(vt_tiles, axis=1)
            n_past = k_past.shape[1]
            s = jnp.concatenate([_dot(qs, k_past), _dot_nt(qs, k_new)], axis=1) + bias
            s = jnp.where(mask > 0.5, s, NEG)
            p = jnp.exp(s - jnp.max(s, axis=-1, keepdims=True)) * mask
            p = (p / jnp.maximum(jnp.sum(p, axis=-1, keepdims=True), 1e-30)).astype(bf16)
            acc = _dot_nt(p[:, 0:n_past], v_past) + _dot(p[:, n_past:n_past + 128], v_new)
            return diag(acc)

        new_scr[b, 0, 0:n_new, :] = news_ref[b]
        pages = cs[b * N_PAGES:(b + 1) * N_PAGES]
        k_tiles = [pg[0:256, :].astype(bf16) for pg in pages]
        v_tiles = [pg[256:512, :].astype(bf16) for pg in pages]
        o_s = attend(k_tiles, v_tiles, new_scr[b, 0], bs_ref[...], ms_ref[...] * selk)

        new_scr[b, 1, 0:n_new, :] = neww_ref[b]
        o_w = attend([win_ref[b, 0:256, :].astype(bf16)], [win_ref[b, 256:512, :].astype(bf16)],
                     new_scr[b, 1], bw_ref[...], mw_ref[...])

        sig = _sigmoid(ng_ref[b])
        o_ref[b] = sig[:, 0:1] * o_c + sig[:, 1:2] * o_s + sig[:, 2:3] * o_w

    for b in range(ATT_BATCH):
        one_sample(b)


def _nsa_sample(page_table, qrow, ng, kvc, news, neww, cache_s, win, layer, n_phys, consts):
    nb = qrow.shape[0]
    n_new = news.shape[1]
    ab = ATT_BATCH
    full = lambda a: pl.BlockSpec(a.shape, lambda i, pt: (0,) * a.ndim)
    per_b = lambda shape: pl.BlockSpec((ab,) + shape, lambda i, pt: (i, 0, 0))
    cmp_spec = lambda kv: pl.BlockSpec((None, ab) + kvc.shape[2:], lambda i, pt: (kv, i, 0, 0))

    def page_spec(s, p):
        return pl.BlockSpec((None, KV_W, PAGE), lambda i, pt: (layer * n_phys + pt[i * ab + s, p], 0, 0))

    in_specs = ([per_b((64, 256)), per_b((64, 3)), cmp_spec(0), cmp_spec(1),
                 per_b((n_new, KV_W)), per_b((n_new, KV_W))]
                + [page_spec(s, p) for s in range(ab) for p in range(N_PAGES)]
                + [pl.BlockSpec((ab, KV_W, WINDOW), lambda i, pt: ((layer * nb) // ab + i, 0, 0))]
                + [full(c) for c in consts])
    grid_spec = pltpu.PrefetchScalarGridSpec(
        num_scalar_prefetch=1,
        grid=(nb // ab,),
        in_specs=in_specs,
        out_specs=pl.BlockSpec((ab, 64, 64), lambda i, pt: (i, 0, 0)),
        scratch_shapes=[pltpu.VMEM((ab, 2, 128, KV_W), f32)],
    )
    return pl.pallas_call(
        _nsa_sample_kernel,
        grid_spec=grid_spec,
        out_shape=jax.ShapeDtypeStruct((nb, 64, 64), f32),
        compiler_params=_cparams(("arbitrary",)),
        name="nsa_sample",
    )(page_table, qrow, ng, kvc, kvc, news, neww, *([cache_s] * (ab * N_PAGES)), win, *consts)


def _static_tables(t, past, n_new):
    ar = np.arange
    tiles = np.stack([d + ar(128)[:, None] - ar(128)[None, :] for d in (0, 128, 256)])
    cmp_p = ar(t)[:, None] - (CMP_STRIDE * ar(128)[None, :] + CMP_LEN - 1)
    qpos = past + ar(n_new)
    cmp_s = qpos[:, None] - (CMP_STRIDE * ar(128)[None, :] + CMP_LEN - 1)
    kpos_s = np.concatenate([ar(past), past + ar(128)])
    slc_s = qpos[:, None] - kpos_s[None, :]
    kpos_w = np.concatenate([past - WINDOW + ar(WINDOW), past + ar(128)])
    win_s = qpos[:, None] - kpos_w[None, :]
    parts = [tiles.reshape(-1, 128), cmp_p, cmp_s.reshape(-1, 128), slc_s.reshape(-1, 128), win_s.reshape(-1, 128)]
    sizes = [p.shape[0] for p in parts]
    flat = np.concatenate(parts, axis=0)
    pad = (-flat.shape[0]) % 256
    flat = np.concatenate([flat, np.zeros((pad, 128), flat.dtype)], axis=0)
    buckets = _t5_bucket_np(flat)
    real_s = np.concatenate([np.ones(past, bool), ar(128) < n_new])
    real_w = np.concatenate([np.ones(WINDOW, bool), ar(128) < n_new])
    mask_c = (cmp_s >= 0)
    mask_s = (slc_s >= 0) & real_s[None, :]
    mask_w = (win_s >= 0) & (win_s < WINDOW) & real_w[None, :]
    rep = lambda mk: np.tile(mk[None].astype(np.float32), (N_HEADS, 1, 1)).reshape(N_HEADS * n_new, -1)
    return buckets, sizes, rep(mask_c), rep(mask_s), rep(mask_w)


def _cmp_to_sel_np(n_cmp, n_sel):
    j = np.arange(n_cmp)[:, None]
    s = np.arange(n_sel)[None, :]
    lo = np.maximum(j * CMP_STRIDE, s * SEL_LEN)
    hi = np.minimum(j * CMP_STRIDE + CMP_LEN, (s + 1) * SEL_LEN)
    return (np.maximum(hi - lo, 0) / CMP_LEN).astype(np.float32)


def _kron4(w):
    eye = jnp.eye(N_KV, dtype=w.dtype)
    out = jnp.einsum('gh,...de->...gdhe', eye, w)
    return out.reshape(w.shape[:-2] + (256, 256))


def _block_diag_rnn(w):
    w4 = w.reshape(DEPTH, 4, 4, RNN_BLOCK, RNN_BLOCK)
    eye = jnp.eye(4, dtype=w.dtype)
    return jnp.einsum('jk,zcjde->zcjdke', eye, w4).reshape(DEPTH, 4, 256, 256)


def _split_in(w, b):
    n_qkv = 1024 + 3 * KV_W
    n_ng = 3 * N_HEADS
    o_rxy = n_qkv + n_ng
    o_mg = o_rxy + 2 * D_RNN
    n_pad = N_PACK - (C_NG + n_ng)
    w_mg = w[..., o_mg:].astype(bf16)
    w_rxy = w[..., o_rxy:o_mg].astype(bf16)
    w_qkv = w[..., :n_qkv].astype(bf16)
    w_ng = jnp.pad(w[..., n_qkv:o_rxy].astype(bf16), ((0, 0), (0, 0), (0, IN_TN - n_ng)))
    bp = jnp.concatenate([b[..., o_mg:], b[..., o_rxy:o_mg], b[..., :n_qkv], b[..., n_qkv:o_rxy],
                          jnp.zeros(b.shape[:-1] + (n_pad,), b.dtype)], axis=-1)
    return (w_mg, w_rxy, w_qkv, w_ng), bp.reshape(DEPTH, 1, N_PACK)


def _feature_major(cache):
    d, n, rows = cache.shape[:3]
    return cache.transpose(0, 1, 3, 4, 5, 2).reshape(d * n, KV_W, rows)


def kernel(x_prompt, x_sample, c_prompt, c_sample, cache_cmp_kv, cache_slc_kv, cache_win_kv, state_conv, state_rnn_h, page_table, rel_bias, w_ada, b_ada, w_in, b_in, cmp_pos, cmp_w1, cmp_w2, w_attn_o, conv_w, conv_b, lru_wa, lru_ba, lru_wi, lru_bi, lru_lambda, w_rnn_o, w_out, w_ffn_in, w_ffn_out, ln_g, ln_b):
    nb, t, _ = x_prompt.shape
    ns, n_new, _ = x_sample.shape
    n_phys = cache_cmp_kv.shape[1]
    past = page_table.shape[1] * PAGE
    assert page_table.shape[1] == N_PAGES and cache_win_kv.shape[2] == WINDOW and t % (SLC_UNROLL * TK) == 0
    assert n_new <= CMP_STRIDE and ns % 8 == 0 and ns % CMP_BATCH == 0 and ns % ATT_BATCH == 0
    mp, ms = nb * t, ns * n_new

    buckets, sizes, mask_c, mask_s, mask_w = _static_tables(t, past, n_new)
    bias_all = _bias_lookup(rel_bias, jnp.asarray(buckets))
    offs = np.cumsum([0] + sizes)
    seg = lambda k: bias_all[:, offs[k]:offs[k + 1]]
    bias_t = (seg(0) * LOG2E).reshape(N_KV, GROUP, 3, 128, 128).transpose(0, 2, 1, 3, 4).reshape(
        N_KV * 3, GROUP * 128, 128)
    bias_cp = seg(1)
    bias_cs = seg(2).reshape(N_HEADS * n_new, 128)
    bias_ss = seg(3).reshape(N_HEADS * n_new, past + 128)
    bias_ws = seg(4).reshape(N_HEADS * n_new, WINDOW + 128)

    n_sel_p = t // SEL_LEN
    ct_p = np.zeros((128, 128), np.float32)
    ct_p[:n_sel_p, :t // CMP_STRIDE - 1] = _cmp_to_sel_np(t // CMP_STRIDE - 1, n_sel_p).T
    e_p = (np.arange(128)[:, None] == (np.arange(t)[None, :] // SEL_LEN)).astype(np.float32)
    n_cmp_s = (past + n_new + CMP_STRIDE - 1) // CMP_STRIDE - 1
    n_sel_s = (past + n_new + SEL_LEN - 1) // SEL_LEN
    c_s = np.zeros((128, 128), np.float32)
    c_s[:n_cmp_s, :n_sel_s] = _cmp_to_sel_np(n_cmp_s, n_sel_s)
    e_s = (np.arange(128)[:, None] == (np.arange(past + 128)[None, :] // SEL_LEN)).astype(np.float32)
    hq = np.arange(N_HEADS * n_new)
    smat = ((hq[None, :] // (GROUP * n_new)) * n_new + hq[None, :] % n_new
            == np.arange(N_KV * n_new)[:, None]).astype(np.float32)
    dsel = (hq[:, None] // (GROUP * n_new) == np.arange(256)[None, :] // HEAD_DIM).astype(np.float32)
    pos = np.arange(PAGE)
    selp = ((pos % CMP_STRIDE) * (PAGE // CMP_STRIDE) + pos // CMP_STRIDE)[None, :] == np.arange(PAGE)[:, None]
    selp = jnp.asarray(selp, bf16)
    consts_s = (bias_cs, jnp.asarray(mask_c), bias_ss, jnp.asarray(mask_s), bias_ws,
                jnp.asarray(mask_w), jnp.asarray(smat, bf16), jnp.asarray(smat.T, bf16), jnp.asarray(c_s, bf16),
                jnp.asarray(e_s, bf16), jnp.asarray(dsel))

    w_in_p, b_in_p = _split_in(w_in, b_in)
    w1 = cmp_w1.reshape(DEPTH, 2, 2, CMP_STRIDE, HEAD_DIM, HEAD_DIM)
    wlt = jnp.concatenate([_kron4(w1[:, :, 0]), _kron4(w1[:, :, 1])], axis=-1).astype(bf16)
    w2k = _kron4(cmp_w2).astype(bf16)
    pe = jnp.tile(cmp_pos.reshape(DEPTH, 2, 2, CMP_STRIDE, HEAD_DIM), (1, 1, 1, 1, N_KV))
    rnn_w = (conv_w, conv_b.reshape(DEPTH, 1, D_RNN), _block_diag_rnn(lru_wa).astype(bf16),
             lru_ba.reshape(DEPTH, 1, D_RNN), _block_diag_rnn(lru_wi).astype(bf16),
             lru_bi.reshape(DEPTH, 1, D_RNN), lru_lambda.reshape(DEPTH, 1, D_RNN))
    w_ao, w_ro, w_o = w_attn_o.astype(bf16), w_rnn_o.astype(bf16), w_out.astype(bf16)
    w_f1, w_f2 = w_ffn_in.astype(bf16), w_ffn_out.astype(bf16)
    ln_g4, ln_b4 = ln_g.reshape(DEPTH, 2, 1, D_MODEL), ln_b.reshape(DEPTH, 2, 1, D_MODEL)
    b_ada3 = b_ada.reshape(DEPTH, 1, 6 * D_MODEL)

    n_cpad = -(nb + ns) % 8
    c_all = jnp.concatenate([c_prompt, c_sample, jnp.zeros((n_cpad, D_MODEL), f32)], axis=0)
    cache_ct, cache_st, win_t = _feature_major(cache_cmp_kv), _feature_major(cache_slc_kv), _feature_major(cache_win_kv)
    cbuf_t = state_conv.transpose(0, 2, 1, 3)

    y_p = x_prompt.reshape(mp, D_MODEL)
    y_s = x_sample.transpose(1, 0, 2).reshape(ms, D_MODEL)
    st_p, st_s = [], []
    for l in range(DEPTH):
        ada = _ada(c_all, w_ada, b_ada3, l)
        mod_p = ada[:nb].reshape(nb, 1, 6 * D_MODEL)
        mod_s = ada[nb:nb + ns].reshape(1, ns, 6 * D_MODEL)
        cb = _cmp_bias(pe[l], wlt[l])

        proj, *kvt = _inproj(y_p, mod_p, w_in_p, b_in_p, l, tm=TM_INPROJ, seq_rows=t)
        rows_c = proj[:, C_KVC:C_KVC + KV_W].reshape(nb, t // CMP_STRIDE, CMP_STRIDE * KV_W)
        kc, vc = _compress_prompt(rows_c, wlt, cb, w2k, l)
        o_attn = _nsa_prompt(proj, kc, vc, bias_cp, bias_t, jnp.asarray(ct_p, bf16), jnp.asarray(e_p, bf16), nb, t)
        o_rnn, h_last = _rglru_prompt(proj, nb, t, rnn_w, l)
        x1 = _mixout(o_attn, o_rnn, proj, y_p, mod_p, w_ao, w_ro, w_o, ln_g4, ln_b4, l, tm=TM_MIXOUT)
        y_p = _ffn(x1, mod_p, w_f1, w_f2, ln_g4, ln_b4, l, tm=TM_FFN)
        kv = lambda k, kvt=kvt: kvt[k].reshape(nb, 2, N_KV, HEAD_DIM, t).transpose(0, 4, 1, 2, 3)
        st_p.append((kv(0), kv(1), kv(2)[:, t - min(WINDOW, t):],
                     proj.reshape(nb, t, N_PACK)[:, t - (CONV_W - 1):, C_RX:C_RX + D_RNN],
                     h_last.reshape(nb, D_RNN)))

        proj, *kvt = _inproj(y_s, mod_s, w_in_p, b_in_p, l, tm=ms, seq_rows=ms)
        proj3 = proj.reshape(n_new, ns, N_PACK)
        q5 = proj3[:, :, C_Q:C_Q + 1024].reshape(n_new, ns, N_KV, GROUP, HEAD_DIM).transpose(1, 2, 3, 0, 4)
        qrow = (q5[:, :, :, :, None, :] * jnp.eye(N_KV, dtype=f32)[None, :, None, None, :, None]
                ).reshape(ns, N_HEADS * n_new, 256)
        ng = proj3[:, :, C_NG:C_NG + 48].reshape(n_new, ns, N_HEADS, 3).transpose(1, 2, 0, 3).reshape(
            ns, N_HEADS * n_new, 3)
        new_rows = lambda c: proj3[:, :, c:c + KV_W].transpose(1, 0, 2)
        kvc_s = _compress_sample(page_table, new_rows(C_KVC), cache_ct, l, n_phys, wlt, cb, w2k, selp)
        o64 = _nsa_sample(page_table, qrow, ng, kvc_s, new_rows(C_KVS), new_rows(C_KVW),
                          cache_st, win_t, l, n_phys, consts_s)
        o_attn = o64.reshape(ns, N_HEADS, n_new, HEAD_DIM).transpose(2, 0, 1, 3).reshape(ms, 1024).astype(bf16)
        o_rnn, h_last = _rglru_sample(proj, cbuf_t, state_rnn_h, rnn_w, l)
        x1 = _mixout(o_attn, o_rnn, proj, y_s, mod_s, w_ao, w_ro, w_o, ln_g4, ln_b4, l, tm=min(TM_MIXOUT, ms))
        y_s = _ffn(x1, mod_s, w_f1, w_f2, ln_g4, ln_b4, l, tm=ms)
        kv = lambda k, kvt=kvt: kvt[k].reshape(2, N_KV, HEAD_DIM, n_new, ns).transpose(4, 3, 0, 1, 2)
        xp = jnp.concatenate([state_conv[l], proj3[:, :, C_RX:C_RX + D_RNN].transpose(1, 0, 2)], axis=1)
        st_s.append((kv(0), kv(1), kv(2), xp[:, n_new:], h_last))

    stack = lambda sts, k: jnp.stack([s[k] for s in sts])
    return (y_p.reshape(nb, t, D_MODEL), y_s.reshape(n_new, ns, D_MODEL).transpose(1, 0, 2),
            stack(st_p, 0), stack(st_p, 1), stack(st_p, 2), stack(st_p, 3), stack(st_p, 4),
            stack(st_s, 0), stack(st_s, 1), stack(st_s, 2), stack(st_s, 3), stack(st_s, 4))
```

```python
import math

import numpy as np
import jax
import jax.numpy as jnp
from jax import lax
from jax.experimental import pallas as pl
from jax.experimental.pallas import tpu as pltpu

f32 = jnp.float32
bf16 = jnp.bfloat16
i32 = jnp.int32

D_MODEL = 2048
N_HEADS = 16
N_KV = 4
GROUP = 4
HEAD_DIM = 64
CMP_STRIDE = 16
CMP_LEN = 32
SEL_LEN = 64
N_SEL = 16
WINDOW = 512
PAGE = 128
FORCE_BONUS = 1e3
N_BUCKETS = 32
MAX_DISTANCE = 128
D_RNN = 1024
RNN_BLOCK = 64
CONV_W = 4
LRU_C = 8.0
D_FF = 5632
DEPTH = 2
ALPHA = (2.0 * DEPTH) ** 0.25
SCALE = HEAD_DIM ** -0.5
LOG2E = math.log2(math.e)
NEG = -1e30
KV_W = 2 * N_KV * HEAD_DIM

C_MG = 0
C_RX = 4096
C_RY = 5120
C_Q = 6144
C_KVC = 7168
C_KVS = 7680
C_KVW = 8192
C_NG = 8704
N_PACK = 9216

VMEM_LIMIT = 56 * 1024 * 1024

TQ = 128
TK = 256
SLC_UNROLL = 4
N_PAGES = 16
TM_INPROJ = 1024
TM_MIXOUT = 256
TM_FFN = 512


def _cparams(sem):
    return pltpu.CompilerParams(dimension_semantics=sem, vmem_limit_bytes=VMEM_LIMIT)


def _sigmoid(x):
    return 1.0 / (1.0 + jnp.exp(-x))


def _silu(x):
    return x * _sigmoid(x)


def _layer_norm(z, g, b):
    mu = jnp.mean(z, axis=-1, keepdims=True)
    zc = z - mu
    var = jnp.mean(zc * zc, axis=-1, keepdims=True)
    return zc * lax.rsqrt(var + 1e-5) * g + b


def _dot(a, b):
    return jnp.dot(a, b, preferred_element_type=f32)


def _dot_nt(a, b):
    return lax.dot_general(a, b, (((1,), (1,)), ((), ())), preferred_element_type=f32)


def _split3(x):
    hi = x.astype(bf16)
    r1 = x - hi.astype(f32)
    mid = r1.astype(bf16)
    lo = (r1 - mid.astype(f32)).astype(bf16)
    return hi, mid, lo


def _rows(mod, tm):
    mr = mod.shape[0]
    if mr == 1 or mr == tm:
        return mod
    return jnp.concatenate([mod] * (tm // mr), axis=0)


def _mod_spec(mod, k, tiles_per_block, ngrid):
    mr = mod.shape[1]
    if ngrid == 1:
        return pl.BlockSpec((None, mr, D_MODEL), lambda i: (i // tiles_per_block, 0, k))
    return pl.BlockSpec((None, mr, D_MODEL), lambda i, j: (i // tiles_per_block, 0, k))


def _ada_kernel(c_ref, w_ref, b_ref, o_ref):
    h = _silu(c_ref[...]).astype(bf16)
    o_ref[...] = _dot(h, w_ref[...].astype(bf16)) + b_ref[...]


def _ada(c_all, w, b, layer):
    m = c_all.shape[0]
    n = w.shape[2]
    tn = 1024
    return pl.pallas_call(
        _ada_kernel,
        grid=(n // tn,),
        in_specs=[pl.BlockSpec((m, D_MODEL), lambda j: (0, 0)),
                  pl.BlockSpec((None, D_MODEL, tn), lambda j: (layer, 0, j)),
                  pl.BlockSpec((None, 1, tn), lambda j: (layer, 0, j))],
        out_specs=pl.BlockSpec((m, tn), lambda j: (0, j)),
        out_shape=jax.ShapeDtypeStruct((m, n), f32),
        compiler_params=_cparams(("arbitrary",)),
        name="ada",
    )(c_all, w, b)


def _bias_lookup_kernel(tbl_ref, bk_ref, o_ref):
    bk = bk_ref[...]
    for h in range(N_HEADS):
        acc = jnp.zeros(bk.shape, f32)
        for k in range(N_BUCKETS):
            acc = jnp.where(bk == k, tbl_ref[k, h], acc)
        o_ref[h] = acc


def _bias_lookup(tbl, buckets):
    n = buckets.shape[0]
    tr = 256
    return pl.pallas_call(
        _bias_lookup_kernel,
        grid=(n // tr,),
        in_specs=[pl.BlockSpec(memory_space=pltpu.SMEM),
                  pl.BlockSpec((tr, 128), lambda i: (i, 0))],
        out_specs=pl.BlockSpec((N_HEADS, tr, 128), lambda i: (0, i, 0)),
        out_shape=jax.ShapeDtypeStruct((N_HEADS, n, 128), f32),
        compiler_params=_cparams(("arbitrary",)),
        name="bias_lookup",
    )(tbl, buckets)


def _t5_bucket_np(dist):
    n = np.maximum(dist, 0)
    exact = N_BUCKETS // 2
    ratio = np.maximum(n, 1).astype(np.float32) / np.float32(exact)
    log_ratio = np.log(ratio).astype(np.float32) / np.float32(math.log(MAX_DISTANCE / exact))
    large = np.minimum(exact + (log_ratio * np.float32(N_BUCKETS - exact)).astype(np.int32), N_BUCKETS - 1)
    return np.where(n < exact, n, large).astype(np.int32)


IN_TN = 512
KV_J0 = C_KVC // IN_TN


IN_PIECES = (("mg", 0, 8), ("rxy", 8, 4), ("qkv", 12, 5), ("ng", 17, 1))


def _inproj_kernel(x_ref, sc_ref, sh_ref, wmg_ref, wrxy_ref, wqkv_ref, wng_ref, b_ref,
                   o_ref, okc_ref, oks_ref, okw_ref, h_scr):
    j = pl.program_id(1)

    @pl.when(j == 0)
    def _():
        tm = x_ref.shape[0]
        h_scr[...] = (x_ref[...] * (1.0 + _rows(sc_ref[...], tm)) + _rows(sh_ref[...], tm)).astype(bf16)

    def emit(w_ref):
        res = _dot(h_scr[...], w_ref[...]) + b_ref[...]
        o_ref[...] = res
        return res

    for w_ref, (_, j0, nj) in zip((wmg_ref, wrxy_ref, wqkv_ref, wng_ref), IN_PIECES):
        @pl.when(jnp.logical_and(j >= j0, j < j0 + nj))
        def _(w_ref=w_ref, j0=j0):
            res = emit(w_ref)
            if j0 == IN_PIECES[2][1]:
                for k, okt_ref in enumerate((okc_ref, oks_ref, okw_ref)):
                    @pl.when(j == KV_J0 + k)
                    def _(okt_ref=okt_ref):
                        okt_ref[...] = res.T


def _inproj(x, mod, w_pieces, b, layer, tm, seq_rows):
    m = x.shape[0]
    tn = IN_TN
    tpb = (m // tm) // mod.shape[0]
    tps = seq_rows // tm
    kt_spec = pl.BlockSpec((None, tn, tm), lambda i, j: (i // tps, 0, i % tps))
    kt_shape = jax.ShapeDtypeStruct((m // seq_rows, tn, seq_rows), f32)

    def w_spec(j0, nj):
        return pl.BlockSpec((None, D_MODEL, tn), lambda i, j: (layer, 0, jnp.clip(j - j0, 0, nj - 1)))

    return pl.pallas_call(
        _inproj_kernel,
        grid=(m // tm, N_PACK // tn),
        in_specs=[pl.BlockSpec((tm, D_MODEL), lambda i, j: (i, 0)),
                  _mod_spec(mod, 1, tpb, 2), _mod_spec(mod, 0, tpb, 2)]
        + [w_spec(j0, nj) for (_, j0, nj) in IN_PIECES]
        + [pl.BlockSpec((None, 1, tn), lambda i, j: (layer, 0, j))],
        out_specs=[pl.BlockSpec((tm, tn), lambda i, j: (i, j)), kt_spec, kt_spec, kt_spec],
        out_shape=[jax.ShapeDtypeStruct((m, N_PACK), f32), kt_shape, kt_shape, kt_shape],
        scratch_shapes=[pltpu.VMEM((tm, D_MODEL), bf16)],
        compiler_params=_cparams(("parallel", "arbitrary")),
        name="inproj",
    )(x, mod, mod, *w_pieces, b)


def _cmp_bias_kernel(pe_ref, wlt_ref, o_ref):
    for kv in range(2):
        acc = jnp.zeros((8, 512), f32)
        for l in range(CMP_STRIDE):
            w = wlt_ref[kv, l]
            lead = _dot(jnp.broadcast_to(pe_ref[kv, 0, l:l + 1, :], (8, 256)).astype(bf16), w[:, 0:256])
            tail = _dot(jnp.broadcast_to(pe_ref[kv, 1, l:l + 1, :], (8, 256)).astype(bf16), w[:, 256:512])
            acc = acc + jnp.concatenate([lead, tail], axis=1)
        o_ref[kv] = acc


def _cmp_bias(pe, wlt):
    return pl.pallas_call(
        _cmp_bias_kernel,
        out_shape=jax.ShapeDtypeStruct((2, 8, 512), f32),
        compiler_params=pltpu.CompilerParams(vmem_limit_bytes=VMEM_LIMIT),
        name="cmp_bias",
    )(pe, wlt)


def _topk_mask_rows(imp, sidx, n_keep):
    n_rows = imp.shape[0]
    rank = jnp.zeros(imp.shape, i32)
    for j in range(n_rows):
        row = imp[j:j + 1, :]
        beats = jnp.where(row > imp, 1, jnp.where(row == imp, jnp.where(sidx > j, 1, 0), 0))
        rank = rank + beats
    return jnp.where(rank < n_keep, jnp.where(imp > -1e29, 1.0, 0.0), 0.0)


def _nsa_prompt_kernel(q_ref, ng_ref, ks_ref, vs_ref, kw_ref, vw_ref, kc_ref, vc_ref,
                       bc_ref, bt_ref, ct_ref, e_ref, o_ref, selk_scr, s_scr, mx_scr):
    i = pl.program_id(1)
    q0 = i * TQ
    rows = GROUP * TQ
    n_sel_blk = ks_ref.shape[0] // SEL_LEN

    d0 = lax.broadcasted_iota(i32, (TQ, TK), 0) - lax.broadcasted_iota(i32, (TQ, TK), 1)
    dist_c = (q0 + (lax.broadcasted_iota(i32, (rows, 128), 0) & (TQ - 1))
              - CMP_STRIDE * lax.broadcasted_iota(i32, (rows, 128), 1) - (CMP_LEN - 1))
    mask_c = dist_c >= 0
    sig = _sigmoid(ng_ref[...])

    sidx = lax.broadcasted_iota(i32, (n_sel_blk, TQ), 0)
    cur = (q0 + lax.broadcasted_iota(i32, (n_sel_blk, TQ), 1)) // SEL_LEN
    forced = jnp.where(sidx == 0, 1, jnp.where(sidx == cur, 1, jnp.where(sidx == cur - 1, 1, 0)))

    def branch(qs, g, k_ref, v_ref, kt_hi, window):
        n_tiles = k_ref.shape[0] // TK
        ones = jnp.ones((TK, HEAD_DIM), f32)

        def score_tile(kt, slot):
            k0 = pl.multiple_of(jnp.clip(kt, 0, n_tiles - 1) * TK, TK)
            k = k_ref[pl.ds(k0, TK), g * 64:(g + 1) * 64].astype(bf16)
            delta = q0 - kt * TK
            dist = d0 + delta
            if window:
                live = jnp.where(kt >= 0, 0.0, NEG)
                addm = jnp.where(dist >= 0, jnp.where(dist < WINDOW, live, NEG), NEG)
            else:
                addm = jnp.where(dist >= 0, selk_scr[:, pl.ds(k0, TK)], NEG)
            va = jnp.clip(delta, 0, 256) // 128
            vb = jnp.clip(delta - 128, 0, 256) // 128
            bias = jnp.concatenate([bt_ref[g * 3 + va], bt_ref[g * 3 + vb]], axis=1)
            s = _dot_nt(qs, k) + bias + jnp.concatenate([addm] * GROUP, axis=0)
            s_scr[:, pl.ds(slot * TK, TK)] = s
            return jnp.maximum(s[:, 0:128], s[:, 128:256])

        def value_tile(kt, slot, mb):
            k0 = pl.multiple_of(jnp.clip(kt, 0, n_tiles - 1) * TK, TK)
            v = jnp.concatenate([v_ref[pl.ds(k0, TK), g * 64:(g + 1) * 64], ones], axis=1).astype(bf16)
            p = jnp.exp2(s_scr[:, pl.ds(slot * TK, TK)] - jnp.concatenate([mb, mb], axis=1))
            return _dot(p.astype(bf16), v)

        if window:
            tiles = [(kt_hi - 2 + j, j) for j in range(3)]
            mx = jnp.full((rows, 128), NEG, f32)
            for kt, slot in tiles:
                mx = jnp.maximum(mx, score_tile(kt, slot))
            mb = jnp.broadcast_to(jnp.max(mx, axis=-1, keepdims=True), (rows, 128))
            acc = jnp.zeros((rows, 2 * HEAD_DIM), f32)
            for kt, slot in tiles:
                acc = acc + value_tile(kt, slot, mb)
        else:
            n_groups = kt_hi // SLC_UNROLL + 1

            def scores(qd, mx):
                for j in range(SLC_UNROLL):
                    mx = jnp.maximum(mx, score_tile(SLC_UNROLL * qd + j, SLC_UNROLL * qd + j))
                return mx

            mx = lax.fori_loop(0, n_groups, scores, jnp.full((rows, 128), NEG, f32))
            mx_scr[...] = jnp.broadcast_to(jnp.max(mx, axis=-1, keepdims=True), (rows, 128))

            def values(qd, acc):
                mb = mx_scr[...]
                for j in range(SLC_UNROLL):
                    acc = acc + value_tile(SLC_UNROLL * qd + j, SLC_UNROLL * qd + j, mb)
                return acc

            acc = lax.fori_loop(0, n_groups, values, jnp.zeros((rows, 2 * HEAD_DIM), f32))
        return acc[:, 0:HEAD_DIM] / jnp.maximum(acc[:, HEAD_DIM:HEAD_DIM + 1], 1e-30)

    for g in range(N_KV):
        qg = q_ref[:, g * 256:(g + 1) * 256]
        qst = jnp.concatenate([qg[:, r * 64:(r + 1) * 64] for r in range(GROUP)], axis=0)
        qs = (qst * SCALE).astype(bf16)
        qs2 = (qst * (SCALE * LOG2E)).astype(bf16)

        kcg = kc_ref[:, g * 64:(g + 1) * 64].astype(bf16)
        vcg = vc_ref[:, g * 64:(g + 1) * 64].astype(bf16)
        bias_c = jnp.concatenate([bc_ref[g * GROUP + r] for r in range(GROUP)], axis=0)
        s = jnp.where(mask_c, _dot_nt(qs, kcg) + bias_c, NEG)
        p = jnp.where(mask_c, jnp.exp(s - jnp.max(s, axis=-1, keepdims=True)), 0.0)
        p = p / jnp.maximum(jnp.sum(p, axis=-1, keepdims=True), 1e-30)
        o_c = _dot(p.astype(bf16), vcg)

        p_sum = p[0:TQ] + p[TQ:2 * TQ] + p[2 * TQ:3 * TQ] + p[3 * TQ:4 * TQ]
        ct = ct_ref[...]
        imp_t = sum(_dot_nt(ct, piece) for piece in _split3(p_sum))[0:n_sel_blk]
        imp_t = jnp.where(forced > 0, imp_t + FORCE_BONUS, imp_t)
        imp_t = jnp.where(sidx <= cur, imp_t, NEG)
        sel_t = _topk_mask_rows(imp_t, sidx, N_SEL)
        sel_pad = jnp.concatenate([sel_t, jnp.zeros((128 - n_sel_blk, TQ), f32)], axis=0)
        sel_q = sel_pad.T.astype(bf16)
        selk_scr[...] = jnp.where(_dot(sel_q, e_ref[...]) > 0.5, 0.0, NEG)

        kt_hi = (q0 + TQ - 1) // TK
        o_s = branch(qs2, g, ks_ref, vs_ref, kt_hi, False)
        o_w = branch(qs2, g, kw_ref, vw_ref, kt_hi, True)

        gate = lambda c: jnp.concatenate(
            [sig[:, g * 12 + r * 3 + c:g * 12 + r * 3 + c + 1] for r in range(GROUP)], axis=0)
        o = gate(0) * o_c + gate(1) * o_s + gate(2) * o_w
        o_ref[:, g * 256:(g + 1) * 256] = jnp.concatenate(
            [o[r * TQ:(r + 1) * TQ] for r in range(GROUP)], axis=1).astype(bf16)


def _nsa_prompt(proj, kvc, bias_c, bias_t, ct, e, nb, t):
    nq = t // TQ
    full = lambda a: pl.BlockSpec(a.shape, lambda b, i: (0,) * a.ndim)
    kv = lambda c: pl.BlockSpec((t, 256), lambda b, i: (b, c // 256))
    cmp_spec = lambda k: pl.BlockSpec((None, None, t // CMP_STRIDE, 256), lambda b, i: (k, b, 0, 0))
    return pl.pallas_call(
        _nsa_prompt_kernel,
        grid=(nb, nq),
        in_specs=[pl.BlockSpec((TQ, 1024), lambda b, i: (b * nq + i, C_Q // 1024)),
                  pl.BlockSpec((TQ, 128), lambda b, i: (b * nq + i, C_NG // 128)),
                  kv(C_KVS), kv(C_KVS + 256), kv(C_KVW), kv(C_KVW + 256),
                  cmp_spec(0), cmp_spec(1),
                  pl.BlockSpec((N_HEADS, TQ, 128), lambda b, i: (0, i, 0)),
                  full(bias_t), full(ct), full(e)],
        out_specs=pl.BlockSpec((TQ, 1024), lambda b, i: (b * nq + i, 0)),
        out_shape=jax.ShapeDtypeStruct((nb * t, 1024), bf16),
        scratch_shapes=[pltpu.VMEM((TQ, t), f32), pltpu.VMEM((GROUP * TQ, t), f32),
                        pltpu.VMEM((GROUP * TQ, 128), f32)],
        compiler_params=_cparams(("parallel", "arbitrary")),
        name="nsa_prompt",
    )(proj, proj, proj, proj, proj, proj, kvc, kvc, bias_c, bias_t, ct, e)


def _lru_gates(xc, ry, wa_ref, ba_ref, wi_ref, bi_ref, lam_ref):
    xb = xc.astype(bf16)
    ra = jnp.concatenate([_dot(xb[:, c * 256:(c + 1) * 256], wa_ref[c]) for c in range(4)], axis=1)
    ri = jnp.concatenate([_dot(xb[:, c * 256:(c + 1) * 256], wi_ref[c]) for c in range(4)], axis=1)
    r = _sigmoid(ra + ba_ref[...])
    ig = _sigmoid(ri + bi_ref[...])
    nl = -lam_ref[...]
    softplus = jnp.maximum(nl, 0.0) + jnp.log1p(jnp.exp(-jnp.abs(nl)))
    log_a = -LRU_C * r * softplus
    a = jnp.exp(log_a)
    u = jnp.sqrt(jnp.tanh(-log_a) * (a * a + 1.0)) * (ig * xc)
    return a, u, jax.nn.gelu(ry)


def _rglru_prompt_kernel(rx_ref, ry_ref, cw_ref, cb_ref, wa_ref, ba_ref, wi_ref, bi_ref, lam_ref,
                         o_ref, hl_ref, xp_scr, a_scr, u_scr, hs_scr, h_scr):
    tt = pl.program_id(1)
    tr = rx_ref.shape[0]

    @pl.when(tt == 0)
    def _():
        xp_scr[0:8, :] = jnp.zeros((8, D_RNN), f32)
        h_scr[...] = jnp.zeros((1, D_RNN), f32)

    x = rx_ref[...]
    xp_scr[8:8 + tr, :] = x
    y = x * cw_ref[CONV_W - 1:CONV_W, :] + cb_ref[...]
    for k in range(CONV_W - 1):
        y = y + xp_scr[5 + k:5 + k + tr, :] * cw_ref[k:k + 1, :]
    xp_scr[0:8, :] = xp_scr[tr:tr + 8, :]

    a, u, gate = _lru_gates(y, ry_ref[...], wa_ref, ba_ref, wi_ref, bi_ref, lam_ref)
    a_scr[...] = a
    u_scr[...] = u

    def step(t, h):
        h = a_scr[pl.ds(t, 1), :] * h + u_scr[pl.ds(t, 1), :]
        hs_scr[pl.ds(t, 1), :] = h
        return h

    h = lax.fori_loop(0, tr, step, h_scr[...], unroll=8)
    h_scr[...] = h
    hl_ref[...] = h
    o_ref[...] = (hs_scr[...] * gate).astype(bf16)


def _rnn_specs(rnn_w, layer, ngrid):
    zeros = lambda n: (0,) * n
    if ngrid == 2:
        return [pl.BlockSpec((None,) + a.shape[1:], lambda b, i, n=a.ndim - 1: (layer,) + zeros(n)) for a in rnn_w]
    return [pl.BlockSpec((None,) + a.shape[1:], lambda i, n=a.ndim - 1: (layer,) + zeros(n)) for a in rnn_w]


def _rglru_prompt(proj, nb, t, rnn_w, layer):
    tr = 256
    nt = t // tr
    return pl.pallas_call(
        _rglru_prompt_kernel,
        grid=(nb, nt),
        in_specs=[pl.BlockSpec((tr, D_RNN), lambda b, i: (b * nt + i, C_RX // D_RNN)),
                  pl.BlockSpec((tr, D_RNN), lambda b, i: (b * nt + i, C_RY // D_RNN))]
        + _rnn_specs(rnn_w, layer, 2),
        out_specs=[pl.BlockSpec((tr, D_RNN), lambda b, i: (b * nt + i, 0)),
                   pl.BlockSpec((None, 1, D_RNN), lambda b, i: (b, 0, 0))],
        out_shape=[jax.ShapeDtypeStruct((nb * t, D_RNN), bf16),
                   jax.ShapeDtypeStruct((nb, 1, D_RNN), f32)],
        scratch_shapes=[pltpu.VMEM((tr + 8, D_RNN), f32), pltpu.VMEM((tr, D_RNN), f32),
                        pltpu.VMEM((tr, D_RNN), f32), pltpu.VMEM((tr, D_RNN), f32),
                        pltpu.VMEM((1, D_RNN), f32)],
        compiler_params=_cparams(("parallel", "arbitrary")),
        name="rglru_prompt",
    )(proj, proj, *rnn_w)


def _rglru_sample_kernel(rx_ref, ry_ref, cbuf_ref, h0_ref, cw_ref, cb_ref, wa_ref, ba_ref, wi_ref, bi_ref,
                         lam_ref, o_ref, hl_ref):
    ns = h0_ref.shape[0]
    n_t = rx_ref.shape[0] // ns
    xp = [cbuf_ref[k] for k in range(CONV_W - 1)] + [rx_ref[s * ns:(s + 1) * ns, :] for s in range(n_t)]
    h = h0_ref[...]
    for s in range(n_t):
        y = xp[s + CONV_W - 1] * cw_ref[CONV_W - 1:CONV_W, :] + cb_ref[...]
        for k in range(CONV_W - 1):
            y = y + xp[s + k] * cw_ref[k:k + 1, :]
        a, u, gate = _lru_gates(y, ry_ref[s * ns:(s + 1) * ns, :], wa_ref, ba_ref, wi_ref, bi_ref, lam_ref)
        h = a * h + u
        o_ref[s * ns:(s + 1) * ns, :] = (h * gate).astype(bf16)
    hl_ref[...] = h


def _rglru_sample(proj, cbuf, h0, rnn_w, layer):
    ms = proj.shape[0]
    ns = h0.shape[1]
    return pl.pallas_call(
        _rglru_sample_kernel,
        grid=(1,),
        in_specs=[pl.BlockSpec((ms, D_RNN), lambda i: (0, C_RX // D_RNN)),
                  pl.BlockSpec((ms, D_RNN), lambda i: (0, C_RY // D_RNN)),
                  pl.BlockSpec((None, CONV_W - 1, ns, D_RNN), lambda i: (layer, 0, 0, 0)),
                  pl.BlockSpec((None, ns, D_RNN), lambda i: (layer, 0, 0))]
        + _rnn_specs(rnn_w, layer, 1),
        out_specs=[pl.BlockSpec((ms, D_RNN), lambda i: (0, 0)),
                   pl.BlockSpec((ns, D_RNN), lambda i: (0, 0))],
        out_shape=[jax.ShapeDtypeStruct((ms, D_RNN), bf16),
                   jax.ShapeDtypeStruct((ns, D_RNN), f32)],
        compiler_params=_cparams(("arbitrary",)),
        name="rglru_sample",
    )(proj, proj, cbuf, h0, *rnn_w)


def _mixout_kernel(oa_ref, orn_ref, ga_ref, gr_ref, x_ref, gt_ref, wa_ref, wr_ref, wo_ref, lg_ref, lb_ref, o_ref):
    a1 = _dot(oa_ref[...], wa_ref[...])
    a2 = _dot(orn_ref[...], wr_ref[...])
    merged = (_sigmoid(ga_ref[...]) * a1 + _sigmoid(gr_ref[...]) * a2).astype(bf16)
    mix = _dot(merged, wo_ref[...])
    z = ALPHA * x_ref[...] + (1.0 + _rows(gt_ref[...], x_ref.shape[0])) * mix
    o_ref[...] = _layer_norm(z, lg_ref[...], lb_ref[...])


def _mixout(o_attn, o_rnn, proj, x, mod, w_ao, w_ro, w_o, ln_g, ln_b, layer, tm):
    m = o_attn.shape[0]
    tpb = (m // tm) // mod.shape[0]
    ln_spec = pl.BlockSpec((None, None, 1, D_MODEL), lambda i: (layer, 0, 0, 0))
    w_spec = lambda k: pl.BlockSpec((None, k, D_MODEL), lambda i: (layer, 0, 0), pipeline_mode=pl.Buffered(1))
    return pl.pallas_call(
        _mixout_kernel,
        grid=(m // tm,),
        in_specs=[pl.BlockSpec((tm, 1024), lambda i: (i, 0)),
                  pl.BlockSpec((tm, 1024), lambda i: (i, 0)),
                  pl.BlockSpec((tm, D_MODEL), lambda i: (i, C_MG // D_MODEL)),
                  pl.BlockSpec((tm, D_MODEL), lambda i: (i, C_MG // D_MODEL + 1)),
                  pl.BlockSpec((tm, D_MODEL), lambda i: (i, 0)),
                  _mod_spec(mod, 2, tpb, 1),
                  w_spec(1024), w_spec(1024), w_spec(D_MODEL),
                  ln_spec, ln_spec],
        out_specs=pl.BlockSpec((tm, D_MODEL), lambda i: (i, 0)),
        out_shape=jax.ShapeDtypeStruct((m, D_MODEL), f32),
        compiler_params=_cparams(("parallel",)),
        name="mixout",
    )(o_attn, o_rnn, proj, proj, x, mod, w_ao, w_ro, w_o, ln_g, ln_b)


def _ffn_kernel(x_ref, sc_ref, sh_ref, gt_ref, wg_ref, wu_ref, wo_ref, lg_ref, lb_ref, o_ref, h_scr):
    j = pl.program_id(1)
    tm = x_ref.shape[0]

    @pl.when(j == 0)
    def _():
        h_scr[...] = (x_ref[...] * (1.0 + _rows(sc_ref[...], tm)) + _rows(sh_ref[...], tm)).astype(bf16)
        o_ref[...] = jnp.zeros(o_ref.shape, f32)

    h = h_scr[...]
    act = (_silu(_dot(h, wg_ref[...])) * _dot(h, wu_ref[...])).astype(bf16)
    o_ref[...] += _dot(act, wo_ref[...])

    @pl.when(j == pl.num_programs(1) - 1)
    def _():
        z = ALPHA * x_ref[...] + (1.0 + _rows(gt_ref[...], tm)) * o_ref[...]
        o_ref[...] = _layer_norm(z, lg_ref[...], lb_ref[...])


def _ffn(x, mod, w_in, w_out, ln_g, ln_b, layer, tm):
    m = x.shape[0]
    tf = 512
    nf = D_FF // tf
    tpb = (m // tm) // mod.shape[0]
    ln_spec = pl.BlockSpec((None, None, 1, D_MODEL), lambda i, j: (layer, 1, 0, 0))
    return pl.pallas_call(
        _ffn_kernel,
        grid=(m // tm, nf),
        in_specs=[pl.BlockSpec((tm, D_MODEL), lambda i, j: (i, 0)),
                  _mod_spec(mod, 4, tpb, 2), _mod_spec(mod, 3, tpb, 2), _mod_spec(mod, 5, tpb, 2),
                  pl.BlockSpec((None, D_MODEL, tf), lambda i, j: (layer, 0, j)),
                  pl.BlockSpec((None, D_MODEL, tf), lambda i, j: (layer, 0, nf + j)),
                  pl.BlockSpec((None, tf, D_MODEL), lambda i, j: (layer, j, 0)),
                  ln_spec, ln_spec],
        out_specs=pl.BlockSpec((tm, D_MODEL), lambda i, j: (i, 0)),
        out_shape=jax.ShapeDtypeStruct((m, D_MODEL), f32),
        scratch_shapes=[pltpu.VMEM((tm, D_MODEL), bf16)],
        compiler_params=_cparams(("parallel", "arbitrary")),
        name="ffn",
    )(x, mod, mod, mod, w_in, w_in, w_out, ln_g, ln_b)


CMP_BATCH = 4


def _compress_sample_kernel(pt_ref, newc_ref, *rest):
    batch = newc_ref.shape[0]
    n_pg = batch * N_PAGES
    pages = rest[0:n_pg]
    wlt_ref, cb_ref, w2_ref, selp_ref, o_ref, x_scr = rest[n_pg:]
    n_new = newc_ref.shape[1]
    per = N_PAGES * (PAGE // CMP_STRIDE) + 8
    row8 = lax.broadcasted_iota(i32, (8, 256), 0)
    selp = selp_ref[...]
    for s in range(batch):
        for p in range(N_PAGES):
            y = _dot_nt(selp, pages[s * N_PAGES + p][...].astype(bf16))
            for l in range(CMP_STRIDE):
                x_scr[l, s * per + p * 8:s * per + (p + 1) * 8, :] = y[l * 8:(l + 1) * 8, :]
        for l in range(CMP_STRIDE):
            if l < n_new:
                extra = jnp.where(row8 == 0, jnp.broadcast_to(newc_ref[s, l:l + 1, :], (8, 256)), 0.0)
            else:
                extra = jnp.zeros((8, 256), f32)
            x_scr[l, s * per + per - 8:s * per + per, :] = extra
    acc = jnp.zeros((batch * per, 512), f32)
    for l in range(CMP_STRIDE):
        acc = acc + _dot(x_scr[l].astype(bf16), wlt_ref[l])
    acc = acc + cb_ref[0:1, :]
    for s in range(batch):
        a = acc[s * per:(s + 1) * per]
        z = a[:, 0:256] + pltpu.roll(a[:, 256:512], per - 1, 0)
        o_ref[s] = _dot(_silu(z).astype(bf16), w2_ref[...])[0:per - 8].astype(bf16)


def _compress_sample(page_table, newc, rows_t, page_loc, layer, wlt, cb, w2, selp):
    ns, n_new, _ = newc.shape
    per = N_PAGES * (PAGE // CMP_STRIDE) + 8
    cbt = CMP_BATCH if ns % CMP_BATCH == 0 else 1

    def page_spec(s, p):
        def index(i, kv, pt):
            major, blk = page_loc(i * cbt + s, p, pt)
            return (major, kv, blk)
        return pl.BlockSpec((None, 256, PAGE), index)

    in_specs = ([pl.BlockSpec((cbt, n_new, 256), lambda i, kv, pt: (i, 0, kv))]
                + [page_spec(s, p) for s in range(cbt) for p in range(N_PAGES)]
                + [pl.BlockSpec((None, None, CMP_STRIDE, 256, 512), lambda i, kv, pt: (layer, kv, 0, 0, 0)),
                   pl.BlockSpec((None, 8, 512), lambda i, kv, pt: (kv, 0, 0)),
                   pl.BlockSpec((None, None, 256, 256), lambda i, kv, pt: (layer, kv, 0, 0)),
                   pl.BlockSpec(selp.shape, lambda i, kv, pt: (0, 0))])
    grid_spec = pltpu.PrefetchScalarGridSpec(
        num_scalar_prefetch=1,
        grid=(ns // cbt, 2),
        in_specs=in_specs,
        out_specs=pl.BlockSpec((None, cbt, per - 8, 256), lambda i, kv, pt: (kv, i, 0, 0)),
        scratch_shapes=[pltpu.VMEM((CMP_STRIDE, cbt * per, 256), f32)],
    )
    return pl.pallas_call(
        _compress_sample_kernel,
        grid_spec=grid_spec,
        out_shape=jax.ShapeDtypeStruct((2, ns, per - 8, 256), bf16),
        compiler_params=_cparams(("arbitrary", "arbitrary")),
        name="compress_sample",
    )(page_table, newc, *([rows_t] * (cbt * N_PAGES)), wlt, cb, w2, selp)


ATT_BATCH = 4


def _nsa_sample_kernel(pt_ref, qrow_ref, ng_ref, kc_ref, vc_ref, news_ref, neww_ref, *rest):
    n_pg = ATT_BATCH * N_PAGES
    cs = rest[0:n_pg]
    (win_ref, bc_ref, mc_ref, bs_ref, ms_ref, bw_ref, mw_ref,
     smat_ref, smt_ref, c_ref, e_ref, dsel_ref, o_ref, new_scr) = rest[n_pg:]
    n_new = news_ref.shape[1]
    dsel = dsel_ref[...]
    mc = mc_ref[...]
    smat = smat_ref[...]
    cmat = c_ref[...]

    @pl.when(pl.program_id(0) == 0)
    def _():
        new_scr[...] = jnp.zeros(new_scr.shape, f32)

    def diag(x):
        y = x * dsel
        return y[:, 0:64] + y[:, 64:128] + y[:, 128:192] + y[:, 192:256]

    def one_sample(b):
        qs = (qrow_ref[b] * SCALE).astype(bf16)
        kc = kc_ref[b]
        vc = vc_ref[b]

        s = jnp.where(mc > 0.5, _dot_nt(qs, kc) + bc_ref[...], NEG)
        p = jnp.exp(s - jnp.max(s, axis=-1, keepdims=True)) * mc
        p = p / jnp.maximum(jnp.sum(p, axis=-1, keepdims=True), 1e-30)
        o_c = diag(_dot(p.astype(bf16), vc))

        p_sum = sum(_dot(smat, piece_) for piece_ in _split3(p))
        imp = sum(_dot(piece_, cmat) for piece_ in _split3(p_sum))
        n_blk = (N_PAGES * PAGE + n_new + SEL_LEN - 1) // SEL_LEN
        cur = (N_PAGES * PAGE) // SEL_LEN
        sidx = lax.broadcasted_iota(i32, imp.shape, 1)
        forced = jnp.where(sidx == 0, 1, jnp.where(sidx == cur, 1, jnp.where(sidx == cur - 1, 1, 0)))
        imp = jnp.where(forced > 0, imp + FORCE_BONUS, imp)
        imp = jnp.where(sidx <= cur, imp, NEG)
        rank = jnp.zeros(imp.shape, i32)
        for j in range(n_blk):
            col = imp[:, j:j + 1]
            rank = rank + jnp.where(col > imp, 1, jnp.where(col == imp, jnp.where(sidx > j, 1, 0), 0))
        sel = jnp.where(rank < N_SEL, jnp.where(imp > -1e29, jnp.where(sidx < n_blk, 1.0, 0.0), 0.0), 0.0)
        sel64 = _dot(smt_ref[...], sel.astype(bf16)).astype(bf16)
        selk = _dot(sel64, e_ref[...])

        def attend(kt_tiles, vt_tiles, new_rows, bias, mask):
            k_new, v_new = new_rows[:, 0:256].astype(bf16), new_rows[:, 256:512].astype(bf16)
            k_past = jnp.concatenate(kt_tiles, axis=1)
            v_past = jnp.concatenate(vt_tiles, axis=1)
            n_past = k_past.shape[1]
            s = jnp.concatenate([_dot(qs, k_past), _dot_nt(qs, k_new)], axis=1) + bias
            s = jnp.where(mask > 0.5, s, NEG)
            p = jnp.exp(s - jnp.max(s, axis=-1, keepdims=True)) * mask
            p = (p / jnp.maximum(jnp.sum(p, axis=-1, keepdims=True), 1e-30)).astype(bf16)
            acc = _dot_nt(p[:, 0:n_past], v_past) + _dot(p[:, n_past:n_past + 128], v_new)
            return diag(acc)

        new_scr[b, 0, 0:n_new, :] = news_ref[b]
        pages = cs[b * N_PAGES:(b + 1) * N_PAGES]
        k_tiles = [pg[0:256, :].astype(bf16) for pg in pages]
        v_tiles = [pg[256:512, :].astype(bf16) for pg in pages]
        o_s = attend(k_tiles, v_tiles, new_scr[b, 0], bs_ref[...], ms_ref[...] * selk)

        new_scr[b, 1, 0:n_new, :] = neww_ref[b]
        o_w = attend([win_ref[b, 0:256, :].astype(bf16)], [win_ref[b, 256:512, :].astype(bf16)],
                     new_scr[b, 1], bw_ref[...], mw_ref[...])

        sig = _sigmoid(ng_ref[b])
        o_ref[b] = sig[:, 0:1] * o_c + sig[:, 1:2] * o_s + sig[:, 2:3] * o_w

    for b in range(ATT_BATCH):
        one_sample(b)


def _nsa_sample(page_table, qrow, ng, kvc, news, neww, cache_s, win, layer, n_phys, consts):
    nb = qrow.shape[0]
    n_new = news.shape[1]
    ab = ATT_BATCH
    full = lambda a: pl.BlockSpec(a.shape, lambda i, pt: (0,) * a.ndim)
    per_b = lambda shape: pl.BlockSpec((ab,) + shape, lambda i, pt: (i, 0, 0))
    cmp_spec = lambda kv: pl.BlockSpec((None, ab) + kvc.shape[2:], lambda i, pt: (kv, i, 0, 0))

    def page_spec(s, p):
        return pl.BlockSpec((None, KV_W, PAGE), lambda i, pt: (layer * n_phys + pt[i * ab + s, p], 0, 0))

    in_specs = ([per_b((64, 256)), per_b((64, 3)), cmp_spec(0), cmp_spec(1),
                 per_b((n_new, KV_W)), per_b((n_new, KV_W))]
                + [page_spec(s, p) for s in range(ab) for p in range(N_PAGES)]
                + [pl.BlockSpec((ab, KV_W, WINDOW), lambda i, pt: ((layer * nb) // ab + i, 0, 0))]
                + [full(c) for c in consts])
    grid_spec = pltpu.PrefetchScalarGridSpec(
        num_scalar_prefetch=1,
        grid=(nb // ab,),
        in_specs=in_specs,
        out_specs=pl.BlockSpec((ab, 64, 64), lambda i, pt: (i, 0, 0)),
        scratch_shapes=[pltpu.VMEM((ab, 2, 128, KV_W), f32)],
    )
    return pl.pallas_call(
        _nsa_sample_kernel,
        grid_spec=grid_spec,
        out_shape=jax.ShapeDtypeStruct((nb, 64, 64), f32),
        compiler_params=_cparams(("arbitrary",)),
        name="nsa_sample",
    )(page_table, qrow, ng, kvc, kvc, news, neww, *([cache_s] * (ab * N_PAGES)), win, *consts)


def _static_tables(t, past, n_new):
    ar = np.arange
    tiles = np.stack([d + ar(128)[:, None] - ar(128)[None, :] for d in (0, 128, 256)])
    cmp_p = ar(t)[:, None] - (CMP_STRIDE * ar(128)[None, :] + CMP_LEN - 1)
    qpos = past + ar(n_new)
    cmp_s = qpos[:, None] - (CMP_STRIDE * ar(128)[None, :] + CMP_LEN - 1)
    kpos_s = np.concatenate([ar(past), past + ar(128)])
    slc_s = qpos[:, None] - kpos_s[None, :]
    kpos_w = np.concatenate([past - WINDOW + ar(WINDOW), past + ar(128)])
    win_s = qpos[:, None] - kpos_w[None, :]
    parts = [tiles.reshape(-1, 128), cmp_p, cmp_s.reshape(-1, 128), slc_s.reshape(-1, 128), win_s.reshape(-1, 128)]
    sizes = [p.shape[0] for p in parts]
    flat = np.concatenate(parts, axis=0)
    pad = (-flat.shape[0]) % 256
    flat = np.concatenate([flat, np.zeros((pad, 128), flat.dtype)], axis=0)
    buckets = _t5_bucket_np(flat)
    real_s = np.concatenate([np.ones(past, bool), ar(128) < n_new])
    real_w = np.concatenate([np.ones(WINDOW, bool), ar(128) < n_new])
    mask_c = (cmp_s >= 0)
    mask_s = (slc_s >= 0) & real_s[None, :]
    mask_w = (win_s >= 0) & (win_s < WINDOW) & real_w[None, :]
    rep = lambda mk: np.tile(mk[None].astype(np.float32), (N_HEADS, 1, 1)).reshape(N_HEADS * n_new, -1)
    return buckets, sizes, rep(mask_c), rep(mask_s), rep(mask_w)


def _cmp_to_sel_np(n_cmp, n_sel):
    j = np.arange(n_cmp)[:, None]
    s = np.arange(n_sel)[None, :]
    lo = np.maximum(j * CMP_STRIDE, s * SEL_LEN)
    hi = np.minimum(j * CMP_STRIDE + CMP_LEN, (s + 1) * SEL_LEN)
    return (np.maximum(hi - lo, 0) / CMP_LEN).astype(np.float32)


def _kron4(w):
    eye = jnp.eye(N_KV, dtype=w.dtype)
    out = jnp.einsum('gh,...de->...gdhe', eye, w)
    return out.reshape(w.shape[:-2] + (256, 256))


def _block_diag_rnn(w):
    w4 = w.reshape(DEPTH, 4, 4, RNN_BLOCK, RNN_BLOCK)
    eye = jnp.eye(4, dtype=w.dtype)
    return jnp.einsum('jk,zcjde->zcjdke', eye, w4).reshape(DEPTH, 4, 256, 256)


def _split_in(w, b):
    n_qkv = 1024 + 3 * KV_W
    n_ng = 3 * N_HEADS
    o_rxy = n_qkv + n_ng
    o_mg = o_rxy + 2 * D_RNN
    n_pad = N_PACK - (C_NG + n_ng)
    w_mg = w[..., o_mg:].astype(bf16)
    w_rxy = w[..., o_rxy:o_mg].astype(bf16)
    w_qkv = w[..., :n_qkv].astype(bf16)
    w_ng = jnp.pad(w[..., n_qkv:o_rxy].astype(bf16), ((0, 0), (0, 0), (0, IN_TN - n_ng)))
    bp = jnp.concatenate([b[..., o_mg:], b[..., o_rxy:o_mg], b[..., :n_qkv], b[..., n_qkv:o_rxy],
                          jnp.zeros(b.shape[:-1] + (n_pad,), b.dtype)], axis=-1)
    return (w_mg, w_rxy, w_qkv, w_ng), bp.reshape(DEPTH, 1, N_PACK)


def _feature_major(cache):
    d, n, rows = cache.shape[:3]
    return cache.transpose(0, 1, 3, 4, 5, 2).reshape(d * n, KV_W, rows)


def kernel(x_prompt, x_sample, c_prompt, c_sample, cache_cmp_kv, cache_slc_kv, cache_win_kv, state_conv, state_rnn_h, page_table, rel_bias, w_ada, b_ada, w_in, b_in, cmp_pos, cmp_w1, cmp_w2, w_attn_o, conv_w, conv_b, lru_wa, lru_ba, lru_wi, lru_bi, lru_lambda, w_rnn_o, w_out, w_ffn_in, w_ffn_out, ln_g, ln_b):
    nb, t, _ = x_prompt.shape
    ns, n_new, _ = x_sample.shape
    n_phys = cache_cmp_kv.shape[1]
    past = page_table.shape[1] * PAGE
    assert page_table.shape[1] == N_PAGES and cache_win_kv.shape[2] == WINDOW and t % (SLC_UNROLL * TK) == 0
    assert t == N_PAGES * PAGE
    assert n_new <= CMP_STRIDE and ns % 8 == 0 and ns % CMP_BATCH == 0 and ns % ATT_BATCH == 0
    mp, ms = nb * t, ns * n_new

    buckets, sizes, mask_c, mask_s, mask_w = _static_tables(t, past, n_new)
    bias_all = _bias_lookup(rel_bias, jnp.asarray(buckets))
    offs = np.cumsum([0] + sizes)
    seg = lambda k: bias_all[:, offs[k]:offs[k + 1]]
    bias_t = (seg(0) * LOG2E).reshape(N_KV, GROUP, 3, 128, 128).transpose(0, 2, 1, 3, 4).reshape(
        N_KV * 3, GROUP * 128, 128)
    bias_cp = seg(1)
    bias_cs = seg(2).reshape(N_HEADS * n_new, 128)
    bias_ss = seg(3).reshape(N_HEADS * n_new, past + 128)
    bias_ws = seg(4).reshape(N_HEADS * n_new, WINDOW + 128)

    n_sel_p = t // SEL_LEN
    ct_p = np.zeros((128, 128), np.float32)
    ct_p[:n_sel_p, :t // CMP_STRIDE - 1] = _cmp_to_sel_np(t // CMP_STRIDE - 1, n_sel_p).T
    e_p = (np.arange(128)[:, None] == (np.arange(t)[None, :] // SEL_LEN)).astype(np.float32)
    n_cmp_s = (past + n_new + CMP_STRIDE - 1) // CMP_STRIDE - 1
    n_sel_s = (past + n_new + SEL_LEN - 1) // SEL_LEN
    c_s = np.zeros((128, 128), np.float32)
    c_s[:n_cmp_s, :n_sel_s] = _cmp_to_sel_np(n_cmp_s, n_sel_s)
    e_s = (np.arange(128)[:, None] == (np.arange(past + 128)[None, :] // SEL_LEN)).astype(np.float32)
    hq = np.arange(N_HEADS * n_new)
    smat = ((hq[None, :] // (GROUP * n_new)) * n_new + hq[None, :] % n_new
            == np.arange(N_KV * n_new)[:, None]).astype(np.float32)
    dsel = (hq[:, None] // (GROUP * n_new) == np.arange(256)[None, :] // HEAD_DIM).astype(np.float32)
    pos = np.arange(PAGE)
    selp = ((pos % CMP_STRIDE) * (PAGE // CMP_STRIDE) + pos // CMP_STRIDE)[None, :] == np.arange(PAGE)[:, None]
    selp = jnp.asarray(selp, bf16)
    consts_s = (bias_cs, jnp.asarray(mask_c), bias_ss, jnp.asarray(mask_s), bias_ws,
                jnp.asarray(mask_w), jnp.asarray(smat, bf16), jnp.asarray(smat.T, bf16), jnp.asarray(c_s, bf16),
                jnp.asarray(e_s, bf16), jnp.asarray(dsel))

    w_in_p, b_in_p = _split_in(w_in, b_in)
    w1 = cmp_w1.reshape(DEPTH, 2, 2, CMP_STRIDE, HEAD_DIM, HEAD_DIM)
    wlt = jnp.concatenate([_kron4(w1[:, :, 0]), _kron4(w1[:, :, 1])], axis=-1).astype(bf16)
    w2k = _kron4(cmp_w2).astype(bf16)
    pe = jnp.tile(cmp_pos.reshape(DEPTH, 2, 2, CMP_STRIDE, HEAD_DIM), (1, 1, 1, 1, N_KV))
    rnn_w = (conv_w, conv_b.reshape(DEPTH, 1, D_RNN), _block_diag_rnn(lru_wa).astype(bf16),
             lru_ba.reshape(DEPTH, 1, D_RNN), _block_diag_rnn(lru_wi).astype(bf16),
             lru_bi.reshape(DEPTH, 1, D_RNN), lru_lambda.reshape(DEPTH, 1, D_RNN))
    w_ao, w_ro, w_o = w_attn_o.astype(bf16), w_rnn_o.astype(bf16), w_out.astype(bf16)
    w_f1, w_f2 = w_ffn_in.astype(bf16), w_ffn_out.astype(bf16)
    ln_g4, ln_b4 = ln_g.reshape(DEPTH, 2, 1, D_MODEL), ln_b.reshape(DEPTH, 2, 1, D_MODEL)
    b_ada3 = b_ada.reshape(DEPTH, 1, 6 * D_MODEL)

    n_cpad = -(nb + ns) % 8
    c_all = jnp.concatenate([c_prompt, c_sample, jnp.zeros((n_cpad, D_MODEL), f32)], axis=0)
    cache_ct, cache_st, win_t = _feature_major(cache_cmp_kv), _feature_major(cache_slc_kv), _feature_major(cache_win_kv)
    cbuf_t = state_conv.transpose(0, 2, 1, 3)

    y_p = x_prompt.reshape(mp, D_MODEL)
    y_s = x_sample.transpose(1, 0, 2).reshape(ms, D_MODEL)
    st_p, st_s = [], []
    for l in range(DEPTH):
        ada = _ada(c_all, w_ada, b_ada3, l)
        mod_p = ada[:nb].reshape(nb, 1, 6 * D_MODEL)
        mod_s = ada[nb:nb + ns].reshape(1, ns, 6 * D_MODEL)
        cb = _cmp_bias(pe[l], wlt[l])

        proj, *kvt = _inproj(y_p, mod_p, w_in_p, b_in_p, l, tm=TM_INPROJ, seq_rows=t)
        kvc_p = _compress_sample(jnp.zeros((nb, 1), i32), jnp.zeros((nb, 1, KV_W), f32), kvt[0],
                                 lambda b, p, pt: (b, p), l, wlt, cb, w2k, selp)
        o_attn = _nsa_prompt(proj, kvc_p, bias_cp, bias_t, jnp.asarray(ct_p, bf16), jnp.asarray(e_p, bf16), nb, t)
        o_rnn, h_last = _rglru_prompt(proj, nb, t, rnn_w, l)
        x1 = _mixout(o_attn, o_rnn, proj, y_p, mod_p, w_ao, w_ro, w_o, ln_g4, ln_b4, l, tm=TM_MIXOUT)
        y_p = _ffn(x1, mod_p, w_f1, w_f2, ln_g4, ln_b4, l, tm=TM_FFN)
        kv = lambda k, kvt=kvt: kvt[k].reshape(nb, 2, N_KV, HEAD_DIM, t).transpose(0, 4, 1, 2, 3)
        st_p.append((kv(0), kv(1), kv(2)[:, t - min(WINDOW, t):],
                     proj.reshape(nb, t, N_PACK)[:, t - (CONV_W - 1):, C_RX:C_RX + D_RNN],
                     h_last.reshape(nb, D_RNN)))

        proj, *kvt = _inproj(y_s, mod_s, w_in_p, b_in_p, l, tm=ms, seq_rows=ms)
        proj3 = proj.reshape(n_new, ns, N_PACK)
        q5 = proj3[:, :, C_Q:C_Q + 1024].reshape(n_new, ns, N_KV, GROUP, HEAD_DIM).transpose(1, 2, 3, 0, 4)
        qrow = (q5[:, :, :, :, None, :] * jnp.eye(N_KV, dtype=f32)[None, :, None, None, :, None]
                ).reshape(ns, N_HEADS * n_new, 256)
        ng = proj3[:, :, C_NG:C_NG + 48].reshape(n_new, ns, N_HEADS, 3).transpose(1, 2, 0, 3).reshape(
            ns, N_HEADS * n_new, 3)
        new_rows = lambda c: proj3[:, :, c:c + KV_W].transpose(1, 0, 2)
        kvc_s = _compress_sample(page_table, new_rows(C_KVC), cache_ct,
                                 lambda b, p, pt: (l * n_phys + pt[b, p], 0), l, wlt, cb, w2k, selp)
        o64 = _nsa_sample(page_table, qrow, ng, kvc_s, new_rows(C_KVS), new_rows(C_KVW),
                          cache_st, win_t, l, n_phys, consts_s)
        o_attn = o64.reshape(ns, N_HEADS, n_new, HEAD_DIM).transpose(2, 0, 1, 3).reshape(ms, 1024).astype(bf16)
        o_rnn, h_last = _rglru_sample(proj, cbuf_t, state_rnn_h, rnn_w, l)
        x1 = _mixout(o_attn, o_rnn, proj, y_s, mod_s, w_ao, w_ro, w_o, ln_g4, ln_b4, l, tm=min(TM_MIXOUT, ms))
        y_s = _ffn(x1, mod_s, w_f1, w_f2, ln_g4, ln_b4, l, tm=ms)
        kv = lambda k, kvt=kvt: kvt[k].reshape(2, N_KV, HEAD_DIM, n_new, ns).transpose(4, 3, 0, 1, 2)
        xp = jnp.concatenate([state_conv[l], proj3[:, :, C_RX:C_RX + D_RNN].transpose(1, 0, 2)], axis=1)
        st_s.append((kv(0), kv(1), kv(2), xp[:, n_new:], h_last))

    stack = lambda sts, k: jnp.stack([s[k] for s in sts])
    return (y_p.reshape(nb, t, D_MODEL), y_s.reshape(n_new, ns, D_MODEL).transpose(1, 0, 2),
            stack(st_p, 0), stack(st_p, 1), stack(st_p, 2), stack(st_p, 3), stack(st_p, 4),
            stack(st_s, 0), stack(st_s, 1), stack(st_s, 2), stack(st_s, 3), stack(st_s, 4))
```

```python
import math

import numpy as np
import jax
import jax.numpy as jnp
from jax import lax
from jax.experimental import pallas as pl
from jax.experimental.pallas import tpu as pltpu

f32 = jnp.float32
bf16 = jnp.bfloat16
i32 = jnp.int32

D_MODEL = 2048
N_HEADS = 16
N_KV = 4
GROUP = 4
HEAD_DIM = 64
CMP_STRIDE = 16
CMP_LEN = 32
SEL_LEN = 64
N_SEL = 16
WINDOW = 512
PAGE = 128
FORCE_BONUS = 1e3
N_BUCKETS = 32
MAX_DISTANCE = 128
D_RNN = 1024
RNN_BLOCK = 64
CONV_W = 4
LRU_C = 8.0
D_FF = 5632
DEPTH = 2
ALPHA = (2.0 * DEPTH) ** 0.25
SCALE = HEAD_DIM ** -0.5
LOG2E = math.log2(math.e)
NEG = -1e30
KV_W = 2 * N_KV * HEAD_DIM

C_MG = 0
C_RX = 4096
C_RY = 5120
C_Q = 6144
C_KVC = 7168
C_KVS = 7680
C_KVW = 8192
C_NG = 8704
N_PACK = 9216

VMEM_LIMIT = 56 * 1024 * 1024

TQ = 128
TK = 256
SLC_UNROLL = 4
N_PAGES = 16
TM_INPROJ = 1024
TM_MIXOUT = 256
TM_FFN = 512


def _cparams(sem):
    return pltpu.CompilerParams(dimension_semantics=sem, vmem_limit_bytes=VMEM_LIMIT)


def _sigmoid(x):
    return 1.0 / (1.0 + jnp.exp(-x))


def _silu(x):
    return x * _sigmoid(x)


def _layer_norm(z, g, b):
    mu = jnp.mean(z, axis=-1, keepdims=True)
    zc = z - mu
    var = jnp.mean(zc * zc, axis=-1, keepdims=True)
    return zc * lax.rsqrt(var + 1e-5) * g + b


def _dot(a, b):
    return jnp.dot(a, b, preferred_element_type=f32)


def _dot_nt(a, b):
    return lax.dot_general(a, b, (((1,), (1,)), ((), ())), preferred_element_type=f32)


def _split3(x):
    hi = x.astype(bf16)
    r1 = x - hi.astype(f32)
    mid = r1.astype(bf16)
    lo = (r1 - mid.astype(f32)).astype(bf16)
    return hi, mid, lo


def _rows(mod, tm):
    mr = mod.shape[0]
    if mr == 1 or mr == tm:
        return mod
    return jnp.concatenate([mod] * (tm // mr), axis=0)


def _mod_spec(mod, k, tiles_per_block, ngrid):
    mr = mod.shape[1]
    if ngrid == 1:
        return pl.BlockSpec((None, mr, D_MODEL), lambda i: (i // tiles_per_block, 0, k))
    return pl.BlockSpec((None, mr, D_MODEL), lambda i, j: (i // tiles_per_block, 0, k))


def _ada_kernel(c_ref, w_ref, b_ref, o_ref):
    h = _silu(c_ref[...]).astype(bf16)
    o_ref[...] = _dot(h, w_ref[...].astype(bf16)) + b_ref[...]


def _ada(c_all, w, b, layer):
    m = c_all.shape[0]
    n = w.shape[2]
    tn = 1024
    return pl.pallas_call(
        _ada_kernel,
        grid=(n // tn,),
        in_specs=[pl.BlockSpec((m, D_MODEL), lambda j: (0, 0)),
                  pl.BlockSpec((None, D_MODEL, tn), lambda j: (layer, 0, j)),
                  pl.BlockSpec((None, 1, tn), lambda j: (layer, 0, j))],
        out_specs=pl.BlockSpec((m, tn), lambda j: (0, j)),
        out_shape=jax.ShapeDtypeStruct((m, n), f32),
        compiler_params=_cparams(("arbitrary",)),
        name="ada",
    )(c_all, w, b)


def _bias_lookup_kernel(tbl_ref, bk_ref, o_ref):
    bk = bk_ref[...]
    for h in range(N_HEADS):
        acc = jnp.zeros(bk.shape, f32)
        for k in range(N_BUCKETS):
            acc = jnp.where(bk == k, tbl_ref[k, h], acc)
        o_ref[h] = acc


def _bias_lookup(tbl, buckets):
    n = buckets.shape[0]
    tr = 256
    return pl.pallas_call(
        _bias_lookup_kernel,
        grid=(n // tr,),
        in_specs=[pl.BlockSpec(memory_space=pltpu.SMEM),
                  pl.BlockSpec((tr, 128), lambda i: (i, 0))],
        out_specs=pl.BlockSpec((N_HEADS, tr, 128), lambda i: (0, i, 0)),
        out_shape=jax.ShapeDtypeStruct((N_HEADS, n, 128), f32),
        compiler_params=_cparams(("arbitrary",)),
        name="bias_lookup",
    )(tbl, buckets)


def _t5_bucket_np(dist):
    n = np.maximum(dist, 0)
    exact = N_BUCKETS // 2
    ratio = np.maximum(n, 1).astype(np.float32) / np.float32(exact)
    log_ratio = np.log(ratio).astype(np.float32) / np.float32(math.log(MAX_DISTANCE / exact))
    large = np.minimum(exact + (log_ratio * np.float32(N_BUCKETS - exact)).astype(np.int32), N_BUCKETS - 1)
    return np.where(n < exact, n, large).astype(np.int32)


IN_TN = 512
KV_J0 = C_KVC // IN_TN


IN_PIECES = (("mg", 0, 8), ("rxy", 8, 4), ("qkv", 12, 5), ("ng", 17, 1))


def _inproj_kernel(x_ref, sc_ref, sh_ref, wmg_ref, wrxy_ref, wqkv_ref, wng_ref, b_ref,
                   o_ref, okc_ref, oks_ref, okw_ref, h_scr):
    j = pl.program_id(1)

    @pl.when(j == 0)
    def _():
        tm = x_ref.shape[0]
        h_scr[...] = (x_ref[...] * (1.0 + _rows(sc_ref[...], tm)) + _rows(sh_ref[...], tm)).astype(bf16)

    def emit(w_ref):
        res = _dot(h_scr[...], w_ref[...]) + b_ref[...]
        o_ref[...] = res
        return res

    for w_ref, (_, j0, nj) in zip((wmg_ref, wrxy_ref, wqkv_ref, wng_ref), IN_PIECES):
        @pl.when(jnp.logical_and(j >= j0, j < j0 + nj))
        def _(w_ref=w_ref, j0=j0):
            res = emit(w_ref)
            if j0 == IN_PIECES[2][1]:
                for k, okt_ref in enumerate((okc_ref, oks_ref, okw_ref)):
                    @pl.when(j == KV_J0 + k)
                    def _(okt_ref=okt_ref):
                        okt_ref[...] = res.T


def _inproj(x, mod, w_pieces, b, layer, tm, seq_rows):
    m = x.shape[0]
    tn = IN_TN
    tpb = (m // tm) // mod.shape[0]
    tps = seq_rows // tm
    kt_spec = pl.BlockSpec((None, tn, tm), lambda i, j: (i // tps, 0, i % tps))
    kt_shape = jax.ShapeDtypeStruct((m // seq_rows, tn, seq_rows), f32)

    def w_spec(j0, nj):
        return pl.BlockSpec((None, D_MODEL, tn), lambda i, j: (layer, 0, jnp.clip(j - j0, 0, nj - 1)))

    return pl.pallas_call(
        _inproj_kernel,
        grid=(m // tm, N_PACK // tn),
        in_specs=[pl.BlockSpec((tm, D_MODEL), lambda i, j: (i, 0)),
                  _mod_spec(mod, 1, tpb, 2), _mod_spec(mod, 0, tpb, 2)]
        + [w_spec(j0, nj) for (_, j0, nj) in IN_PIECES]
        + [pl.BlockSpec((None, 1, tn), lambda i, j: (layer, 0, j))],
        out_specs=[pl.BlockSpec((tm, tn), lambda i, j: (i, j)), kt_spec, kt_spec, kt_spec],
        out_shape=[jax.ShapeDtypeStruct((m, N_PACK), f32), kt_shape, kt_shape, kt_shape],
        scratch_shapes=[pltpu.VMEM((tm, D_MODEL), bf16)],
        compiler_params=_cparams(("parallel", "arbitrary")),
        name="inproj",
    )(x, mod, mod, *w_pieces, b)


def _cmp_bias_kernel(pe_ref, wlt_ref, o_ref):
    for kv in range(2):
        acc = jnp.zeros((8, 512), f32)
        for l in range(CMP_STRIDE):
            w = wlt_ref[kv, l]
            lead = _dot(jnp.broadcast_to(pe_ref[kv, 0, l:l + 1, :], (8, 256)).astype(bf16), w[:, 0:256])
            tail = _dot(jnp.broadcast_to(pe_ref[kv, 1, l:l + 1, :], (8, 256)).astype(bf16), w[:, 256:512])
            acc = acc + jnp.concatenate([lead, tail], axis=1)
        o_ref[kv] = acc


def _cmp_bias(pe, wlt):
    return pl.pallas_call(
        _cmp_bias_kernel,
        out_shape=jax.ShapeDtypeStruct((2, 8, 512), f32),
        compiler_params=pltpu.CompilerParams(vmem_limit_bytes=VMEM_LIMIT),
        name="cmp_bias",
    )(pe, wlt)


def _topk_mask_rows(imp, sidx, n_keep):
    n_rows = imp.shape[0]
    rank = jnp.zeros(imp.shape, i32)
    for j in range(n_rows):
        row = imp[j:j + 1, :]
        beats = jnp.where(row > imp, 1, jnp.where(row == imp, jnp.where(sidx > j, 1, 0), 0))
        rank = rank + beats
    return jnp.where(rank < n_keep, jnp.where(imp > -1e29, 1.0, 0.0), 0.0)


def _nsa_prompt_kernel(q_ref, ng_ref, ks_ref, vs_ref, kw_ref, vw_ref, kc_ref, vc_ref,
                       bc_ref, bt_ref, ct_ref, e_ref, o_ref, selk_scr, s_scr, mx_scr):
    i = pl.program_id(1)
    q0 = i * TQ
    rows = GROUP * TQ
    n_sel_blk = ks_ref.shape[0] // SEL_LEN

    d0 = lax.broadcasted_iota(i32, (TQ, TK), 0) - lax.broadcasted_iota(i32, (TQ, TK), 1)
    dist_c = (q0 + (lax.broadcasted_iota(i32, (rows, 128), 0) & (TQ - 1))
              - CMP_STRIDE * lax.broadcasted_iota(i32, (rows, 128), 1) - (CMP_LEN - 1))
    mask_c = dist_c >= 0
    sig = _sigmoid(ng_ref[...])

    sidx = lax.broadcasted_iota(i32, (n_sel_blk, TQ), 0)
    cur = (q0 + lax.broadcasted_iota(i32, (n_sel_blk, TQ), 1)) // SEL_LEN
    forced = jnp.where(sidx == 0, 1, jnp.where(sidx == cur, 1, jnp.where(sidx == cur - 1, 1, 0)))

    def branch(qs, g, k_ref, v_ref, kt_hi, window):
        n_tiles = k_ref.shape[0] // TK
        ones = jnp.ones((TK, HEAD_DIM), f32)

        def score_tile(kt, slot):
            k0 = pl.multiple_of(jnp.clip(kt, 0, n_tiles - 1) * TK, TK)
            k = k_ref[pl.ds(k0, TK), g * 64:(g + 1) * 64].astype(bf16)
            delta = q0 - kt * TK
            dist = d0 + delta
            if window:
                live = jnp.where(kt >= 0, 0.0, NEG)
                addm = jnp.where(dist >= 0, jnp.where(dist < WINDOW, live, NEG), NEG)
            else:
                addm = jnp.where(dist >= 0, selk_scr[:, pl.ds(k0, TK)], NEG)
            va = jnp.clip(delta, 0, 256) // 128
            vb = jnp.clip(delta - 128, 0, 256) // 128
            bias = jnp.concatenate([bt_ref[g * 3 + va], bt_ref[g * 3 + vb]], axis=1)
            s = _dot_nt(qs, k) + bias + jnp.concatenate([addm] * GROUP, axis=0)
            s_scr[:, pl.ds(slot * TK, TK)] = s
            return jnp.maximum(s[:, 0:128], s[:, 128:256])

        def value_tile(kt, slot, mb):
            k0 = pl.multiple_of(jnp.clip(kt, 0, n_tiles - 1) * TK, TK)
            v = jnp.concatenate([v_ref[pl.ds(k0, TK), g * 64:(g + 1) * 64], ones], axis=1).astype(bf16)
            p = jnp.exp2(s_scr[:, pl.ds(slot * TK, TK)] - jnp.concatenate([mb, mb], axis=1))
            return _dot(p.astype(bf16), v)

        if window:
            tiles = [(kt_hi - 2 + j, j) for j in range(3)]
            mx = jnp.full((rows, 128), NEG, f32)
            for kt, slot in tiles:
                mx = jnp.maximum(mx, score_tile(kt, slot))
            mb = jnp.broadcast_to(jnp.max(mx, axis=-1, keepdims=True), (rows, 128))
            acc = jnp.zeros((rows, 2 * HEAD_DIM), f32)
            for kt, slot in tiles:
                acc = acc + value_tile(kt, slot, mb)
        else:
            n_groups = kt_hi // SLC_UNROLL + 1

            def scores(qd, mx):
                for j in range(SLC_UNROLL):
                    mx = jnp.maximum(mx, score_tile(SLC_UNROLL * qd + j, SLC_UNROLL * qd + j))
                return mx

            mx = lax.fori_loop(0, n_groups, scores, jnp.full((rows, 128), NEG, f32))
            mx_scr[...] = jnp.broadcast_to(jnp.max(mx, axis=-1, keepdims=True), (rows, 128))

            def values(qd, acc):
                mb = mx_scr[...]
                for j in range(SLC_UNROLL):
                    acc = acc + value_tile(SLC_UNROLL * qd + j, SLC_UNROLL * qd + j, mb)
                return acc

            acc = lax.fori_loop(0, n_groups, values, jnp.zeros((rows, 2 * HEAD_DIM), f32))
        return acc[:, 0:HEAD_DIM] / jnp.maximum(acc[:, HEAD_DIM:HEAD_DIM + 1], 1e-30)

    for g in range(N_KV):
        qg = q_ref[:, g * 256:(g + 1) * 256]
        qst = jnp.concatenate([qg[:, r * 64:(r + 1) * 64] for r in range(GROUP)], axis=0)
        qs = (qst * SCALE).astype(bf16)
        qs2 = (qst * (SCALE * LOG2E)).astype(bf16)

        kcg = kc_ref[:, g * 64:(g + 1) * 64].astype(bf16)
        vcg = vc_ref[:, g * 64:(g + 1) * 64].astype(bf16)
        bias_c = jnp.concatenate([bc_ref[g * GROUP + r] for r in range(GROUP)], axis=0)
        s = jnp.where(mask_c, _dot_nt(qs, kcg) + bias_c, NEG)
        p = jnp.where(mask_c, jnp.exp(s - jnp.max(s, axis=-1, keepdims=True)), 0.0)
        p = p / jnp.maximum(jnp.sum(p, axis=-1, keepdims=True), 1e-30)
        o_c = _dot(p.astype(bf16), vcg)

        p_sum = p[0:TQ] + p[TQ:2 * TQ] + p[2 * TQ:3 * TQ] + p[3 * TQ:4 * TQ]
        ct = ct_ref[...]
        imp_t = sum(_dot_nt(ct, piece) for piece in _split3(p_sum))[0:n_sel_blk]
        imp_t = jnp.where(forced > 0, imp_t + FORCE_BONUS, imp_t)
        imp_t = jnp.where(sidx <= cur, imp_t, NEG)
        sel_t = _topk_mask_rows(imp_t, sidx, N_SEL)
        sel_pad = jnp.concatenate([sel_t, jnp.zeros((128 - n_sel_blk, TQ), f32)], axis=0)
        sel_q = sel_pad.T.astype(bf16)
        selk_scr[...] = jnp.where(_dot(sel_q, e_ref[...]) > 0.5, 0.0, NEG)

        kt_hi = (q0 + TQ - 1) // TK
        o_s = branch(qs2, g, ks_ref, vs_ref, kt_hi, False)
        o_w = branch(qs2, g, kw_ref, vw_ref, kt_hi, True)

        gate = lambda c: jnp.concatenate(
            [sig[:, g * 12 + r * 3 + c:g * 12 + r * 3 + c + 1] for r in range(GROUP)], axis=0)
        o = gate(0) * o_c + gate(1) * o_s + gate(2) * o_w
        o_ref[:, g * 256:(g + 1) * 256] = jnp.concatenate(
            [o[r * TQ:(r + 1) * TQ] for r in range(GROUP)], axis=1).astype(bf16)


def _nsa_prompt(proj, kvc, bias_c, bias_t, ct, e, nb, t):
    nq = t // TQ
    full = lambda a: pl.BlockSpec(a.shape, lambda b, i: (0,) * a.ndim)
    kv = lambda c: pl.BlockSpec((t, 256), lambda b, i: (b, c // 256))
    cmp_spec = lambda k: pl.BlockSpec((None, None, t // CMP_STRIDE, 256), lambda b, i: (k, b, 0, 0))
    return pl.pallas_call(
        _nsa_prompt_kernel,
        grid=(nb, nq),
        in_specs=[pl.BlockSpec((TQ, 1024), lambda b, i: (b * nq + i, C_Q // 1024)),
                  pl.BlockSpec((TQ, 128), lambda b, i: (b * nq + i, C_NG // 128)),
                  kv(C_KVS), kv(C_KVS + 256), kv(C_KVW), kv(C_KVW + 256),
                  cmp_spec(0), cmp_spec(1),
                  pl.BlockSpec((N_HEADS, TQ, 128), lambda b, i: (0, i, 0)),
                  full(bias_t), full(ct), full(e)],
        out_specs=pl.BlockSpec((TQ, 1024), lambda b, i: (b * nq + i, 0)),
        out_shape=jax.ShapeDtypeStruct((nb * t, 1024), bf16),
        scratch_shapes=[pltpu.VMEM((TQ, t), f32), pltpu.VMEM((GROUP * TQ, t), f32),
                        pltpu.VMEM((GROUP * TQ, 128), f32)],
        compiler_params=_cparams(("parallel", "arbitrary")),
        name="nsa_prompt",
    )(proj, proj, proj, proj, proj, proj, kvc, kvc, bias_c, bias_t, ct, e)


def _lru_gates(xc, ry, wa_ref, ba_ref, wi_ref, bi_ref, lam_ref):
    xb = xc.astype(bf16)
    ra = jnp.concatenate([_dot(xb[:, c * 256:(c + 1) * 256], wa_ref[c]) for c in range(4)], axis=1)
    ri = jnp.concatenate([_dot(xb[:, c * 256:(c + 1) * 256], wi_ref[c]) for c in range(4)], axis=1)
    r = _sigmoid(ra + ba_ref[...])
    ig = _sigmoid(ri + bi_ref[...])
    nl = -lam_ref[...]
    softplus = jnp.maximum(nl, 0.0) + jnp.log1p(jnp.exp(-jnp.abs(nl)))
    log_a = -LRU_C * r * softplus
    a = jnp.exp(log_a)
    u = jnp.sqrt(jnp.tanh(-log_a) * (a * a + 1.0)) * (ig * xc)
    return a, u, jax.nn.gelu(ry)


def _rglru_prompt_kernel(rx_ref, ry_ref, cw_ref, cb_ref, wa_ref, ba_ref, wi_ref, bi_ref, lam_ref,
                         o_ref, hl_ref, xp_scr, a_scr, u_scr, hs_scr, h_scr):
    tt = pl.program_id(1)
    tr = rx_ref.shape[0]

    @pl.when(tt == 0)
    def _():
        xp_scr[0:8, :] = jnp.zeros((8, D_RNN), f32)
        h_scr[...] = jnp.zeros((1, D_RNN), f32)

    x = rx_ref[...]
    xp_scr[8:8 + tr, :] = x
    y = x * cw_ref[CONV_W - 1:CONV_W, :] + cb_ref[...]
    for k in range(CONV_W - 1):
        y = y + xp_scr[5 + k:5 + k + tr, :] * cw_ref[k:k + 1, :]
    xp_scr[0:8, :] = xp_scr[tr:tr + 8, :]

    a, u, gate = _lru_gates(y, ry_ref[...], wa_ref, ba_ref, wi_ref, bi_ref, lam_ref)
    a_scr[...] = a
    u_scr[...] = u

    def step(t, h):
        h = a_scr[pl.ds(t, 1), :] * h + u_scr[pl.ds(t, 1), :]
        hs_scr[pl.ds(t, 1), :] = h
        return h

    h = lax.fori_loop(0, tr, step, h_scr[...], unroll=8)
    h_scr[...] = h
    hl_ref[...] = h
    o_ref[...] = (hs_scr[...] * gate).astype(bf16)


def _rnn_specs(rnn_w, layer, ngrid):
    zeros = lambda n: (0,) * n
    if ngrid == 2:
        return [pl.BlockSpec((None,) + a.shape[1:], lambda b, i, n=a.ndim - 1: (layer,) + zeros(n)) for a in rnn_w]
    return [pl.BlockSpec((None,) + a.shape[1:], lambda i, n=a.ndim - 1: (layer,) + zeros(n)) for a in rnn_w]


def _rglru_prompt(proj, nb, t, rnn_w, layer):
    tr = 256
    nt = t // tr
    return pl.pallas_call(
        _rglru_prompt_kernel,
        grid=(nb, nt),
        in_specs=[pl.BlockSpec((tr, D_RNN), lambda b, i: (b * nt + i, C_RX // D_RNN)),
                  pl.BlockSpec((tr, D_RNN), lambda b, i: (b * nt + i, C_RY // D_RNN))]
        + _rnn_specs(rnn_w, layer, 2),
        out_specs=[pl.BlockSpec((tr, D_RNN), lambda b, i: (b * nt + i, 0)),
                   pl.BlockSpec((None, 1, D_RNN), lambda b, i: (b, 0, 0))],
        out_shape=[jax.ShapeDtypeStruct((nb * t, D_RNN), bf16),
                   jax.ShapeDtypeStruct((nb, 1, D_RNN), f32)],
        scratch_shapes=[pltpu.VMEM((tr + 8, D_RNN), f32), pltpu.VMEM((tr, D_RNN), f32),
                        pltpu.VMEM((tr, D_RNN), f32), pltpu.VMEM((tr, D_RNN), f32),
                        pltpu.VMEM((1, D_RNN), f32)],
        compiler_params=_cparams(("parallel", "arbitrary")),
        name="rglru_prompt",
    )(proj, proj, *rnn_w)


def _rglru_sample_kernel(rx_ref, ry_ref, cbuf_ref, h0_ref, cw_ref, cb_ref, wa_ref, ba_ref, wi_ref, bi_ref,
                         lam_ref, o_ref, hl_ref):
    ns = h0_ref.shape[0]
    n_t = rx_ref.shape[0] // ns
    xp = [cbuf_ref[k] for k in range(CONV_W - 1)] + [rx_ref[s * ns:(s + 1) * ns, :] for s in range(n_t)]
    h = h0_ref[...]
    for s in range(n_t):
        y = xp[s + CONV_W - 1] * cw_ref[CONV_W - 1:CONV_W, :] + cb_ref[...]
        for k in range(CONV_W - 1):
            y = y + xp[s + k] * cw_ref[k:k + 1, :]
        a, u, gate = _lru_gates(y, ry_ref[s * ns:(s + 1) * ns, :], wa_ref, ba_ref, wi_ref, bi_ref, lam_ref)
        h = a * h + u
        o_ref[s * ns:(s + 1) * ns, :] = (h * gate).astype(bf16)
    hl_ref[...] = h


def _rglru_sample(proj, cbuf, h0, rnn_w, layer):
    ms = proj.shape[0]
    ns = h0.shape[1]
    return pl.pallas_call(
        _rglru_sample_kernel,
        grid=(1,),
        in_specs=[pl.BlockSpec((ms, D_RNN), lambda i: (0, C_RX // D_RNN)),
                  pl.BlockSpec((ms, D_RNN), lambda i: (0, C_RY // D_RNN)),
                  pl.BlockSpec((None, CONV_W - 1, ns, D_RNN), lambda i: (layer, 0, 0, 0)),
                  pl.BlockSpec((None, ns, D_RNN), lambda i: (layer, 0, 0))]
        + _rnn_specs(rnn_w, layer, 1),
        out_specs=[pl.BlockSpec((ms, D_RNN), lambda i: (0, 0)),
                   pl.BlockSpec((ns, D_RNN), lambda i: (0, 0))],
        out_shape=[jax.ShapeDtypeStruct((ms, D_RNN), bf16),
                   jax.ShapeDtypeStruct((ns, D_RNN), f32)],
        compiler_params=_cparams(("arbitrary",)),
        name="rglru_sample",
    )(proj, proj, cbuf, h0, *rnn_w)


def _mixout_kernel(oa_ref, orn_ref, ga_ref, gr_ref, x_ref, gt_ref, wa_ref, wr_ref, wo_ref, lg_ref, lb_ref, o_ref):
    a1 = _dot(oa_ref[...], wa_ref[...])
    a2 = _dot(orn_ref[...], wr_ref[...])
    merged = (_sigmoid(ga_ref[...]) * a1 + _sigmoid(gr_ref[...]) * a2).astype(bf16)
    mix = _dot(merged, wo_ref[...])
    z = ALPHA * x_ref[...] + (1.0 + _rows(gt_ref[...], x_ref.shape[0])) * mix
    o_ref[...] = _layer_norm(z, lg_ref[...], lb_ref[...])


def _mixout(o_attn, o_rnn, proj, x, mod, w_ao, w_ro, w_o, ln_g, ln_b, layer, tm):
    m = o_attn.shape[0]
    tpb = (m // tm) // mod.shape[0]
    ln_spec = pl.BlockSpec((None, None, 1, D_MODEL), lambda i: (layer, 0, 0, 0))
    w_spec = lambda k: pl.BlockSpec((None, k, D_MODEL), lambda i: (layer, 0, 0), pipeline_mode=pl.Buffered(1))
    return pl.pallas_call(
        _mixout_kernel,
        grid=(m // tm,),
        in_specs=[pl.BlockSpec((tm, 1024), lambda i: (i, 0)),
                  pl.BlockSpec((tm, 1024), lambda i: (i, 0)),
                  pl.BlockSpec((tm, D_MODEL), lambda i: (i, C_MG // D_MODEL)),
                  pl.BlockSpec((tm, D_MODEL), lambda i: (i, C_MG // D_MODEL + 1)),
                  pl.BlockSpec((tm, D_MODEL), lambda i: (i, 0)),
                  _mod_spec(mod, 2, tpb, 1),
                  w_spec(1024), w_spec(1024), w_spec(D_MODEL),
                  ln_spec, ln_spec],
        out_specs=pl.BlockSpec((tm, D_MODEL), lambda i: (i, 0)),
        out_shape=jax.ShapeDtypeStruct((m, D_MODEL), f32),
        compiler_params=_cparams(("parallel",)),
        name="mixout",
    )(o_attn, o_rnn, proj, proj, x, mod, w_ao, w_ro, w_o, ln_g, ln_b)


def _ffn_kernel(x_ref, sc_ref, sh_ref, gt_ref, wg_ref, wu_ref, wo_ref, lg_ref, lb_ref, o_ref, h_scr):
    j = pl.program_id(1)
    tm = x_ref.shape[0]

    @pl.when(j == 0)
    def _():
        h_scr[...] = (x_ref[...] * (1.0 + _rows(sc_ref[...], tm)) + _rows(sh_ref[...], tm)).astype(bf16)
        o_ref[...] = jnp.zeros(o_ref.shape, f32)

    h = h_scr[...]
    act = (_silu(_dot(h, wg_ref[...])) * _dot(h, wu_ref[...])).astype(bf16)
    o_ref[...] += _dot(act, wo_ref[...])

    @pl.when(j == pl.num_programs(1) - 1)
    def _():
        z = ALPHA * x_ref[...] + (1.0 + _rows(gt_ref[...], tm)) * o_ref[...]
        o_ref[...] = _layer_norm(z, lg_ref[...], lb_ref[...])


def _ffn(x, mod, w_in, w_out, ln_g, ln_b, layer, tm):
    m = x.shape[0]
    tf = 512
    nf = D_FF // tf
    tpb = (m // tm) // mod.shape[0]
    ln_spec = pl.BlockSpec((None, None, 1, D_MODEL), lambda i, j: (layer, 1, 0, 0))
    return pl.pallas_call(
        _ffn_kernel,
        grid=(m // tm, nf),
        in_specs=[pl.BlockSpec((tm, D_MODEL), lambda i, j: (i, 0)),
                  _mod_spec(mod, 4, tpb, 2), _mod_spec(mod, 3, tpb, 2), _mod_spec(mod, 5, tpb, 2),
                  pl.BlockSpec((None, D_MODEL, tf), lambda i, j: (layer, 0, j)),
                  pl.BlockSpec((None, D_MODEL, tf), lambda i, j: (layer, 0, nf + j)),
                  pl.BlockSpec((None, tf, D_MODEL), lambda i, j: (layer, j, 0)),
                  ln_spec, ln_spec],
        out_specs=pl.BlockSpec((tm, D_MODEL), lambda i, j: (i, 0)),
        out_shape=jax.ShapeDtypeStruct((m, D_MODEL), f32),
        scratch_shapes=[pltpu.VMEM((tm, D_MODEL), bf16)],
        compiler_params=_cparams(("parallel", "arbitrary")),
        name="ffn",
    )(x, mod, mod, mod, w_in, w_in, w_out, ln_g, ln_b)


CMP_BATCH = 4


def _compress_sample_kernel(pt_ref, newc_ref, *rest):
    batch = newc_ref.shape[0]
    n_pg = batch * N_PAGES
    pages = rest[0:n_pg]
    wlt_ref, cb_ref, w2_ref, selp_ref, o_ref, x_scr = rest[n_pg:]
    n_new = newc_ref.shape[1]
    per = N_PAGES * (PAGE // CMP_STRIDE) + 8
    row8 = lax.broadcasted_iota(i32, (8, 256), 0)
    selp = selp_ref[...]
    for s in range(batch):
        for p in range(N_PAGES):
            y = _dot_nt(selp, pages[s * N_PAGES + p][...].astype(bf16))
            for l in range(CMP_STRIDE):
                x_scr[l, s * per + p * 8:s * per + (p + 1) * 8, :] = y[l * 8:(l + 1) * 8, :]
        for l in range(CMP_STRIDE):
            if l < n_new:
                extra = jnp.where(row8 == 0, jnp.broadcast_to(newc_ref[s, l:l + 1, :], (8, 256)), 0.0)
            else:
                extra = jnp.zeros((8, 256), f32)
            x_scr[l, s * per + per - 8:s * per + per, :] = extra
    acc = jnp.zeros((batch * per, 512), f32)
    for l in range(CMP_STRIDE):
        acc = acc + _dot(x_scr[l].astype(bf16), wlt_ref[l])
    acc = acc + cb_ref[0:1, :]
    for s in range(batch):
        a = acc[s * per:(s + 1) * per]
        z = a[:, 0:256] + pltpu.roll(a[:, 256:512], per - 1, 0)
        o_ref[s] = _dot(_silu(z).astype(bf16), w2_ref[...])[0:per - 8].astype(bf16)


def _compress_sample(page_table, newc, rows_t, page_loc, layer, wlt, cb, w2, selp):
    ns, n_new, _ = newc.shape
    per = N_PAGES * (PAGE // CMP_STRIDE) + 8
    cbt = CMP_BATCH if ns % CMP_BATCH == 0 else 1

    def page_spec(s, p):
        def index(i, kv, pt):
            major, blk = page_loc(i * cbt + s, p, pt)
            return (major, kv, blk)
        return pl.BlockSpec((None, 256, PAGE), index)

    in_specs = ([pl.BlockSpec((cbt, n_new, 256), lambda i, kv, pt: (i, 0, kv))]
                + [page_spec(s, p) for s in range(cbt) for p in range(N_PAGES)]
                + [pl.BlockSpec((None, None, CMP_STRIDE, 256, 512), lambda i, kv, pt: (layer, kv, 0, 0, 0)),
                   pl.BlockSpec((None, 8, 512), lambda i, kv, pt: (kv, 0, 0)),
                   pl.BlockSpec((None, None, 256, 256), lambda i, kv, pt: (layer, kv, 0, 0)),
                   pl.BlockSpec(selp.shape, lambda i, kv, pt: (0, 0))])
    grid_spec = pltpu.PrefetchScalarGridSpec(
        num_scalar_prefetch=1,
        grid=(ns // cbt, 2),
        in_specs=in_specs,
        out_specs=pl.BlockSpec((None, cbt, per - 8, 256), lambda i, kv, pt: (kv, i, 0, 0)),
        scratch_shapes=[pltpu.VMEM((CMP_STRIDE, cbt * per, 256), f32)],
    )
    return pl.pallas_call(
        _compress_sample_kernel,
        grid_spec=grid_spec,
        out_shape=jax.ShapeDtypeStruct((2, ns, per - 8, 256), bf16),
        compiler_params=_cparams(("arbitrary", "arbitrary")),
        name="compress_sample",
    )(page_table, newc, *([rows_t] * (cbt * N_PAGES)), wlt, cb, w2, selp)


ATT_BATCH = 4


def _nsa_sample_kernel(pt_ref, qrow_ref, ng_ref, kc_ref, vc_ref, news_ref, neww_ref, *rest):
    n_pg = ATT_BATCH * N_PAGES
    cs = rest[0:n_pg]
    (win_ref, bc_ref, mc_ref, bs_ref, ms_ref, bw_ref, mw_ref,
     smat_ref, smt_ref, c_ref, e_ref, dsel_ref, o_ref, new_scr) = rest[n_pg:]
    n_new = news_ref.shape[1]
    dsel = dsel_ref[...]
    mc = mc_ref[...]
    smat = smat_ref[...]
    cmat = c_ref[...]

    @pl.when(pl.program_id(0) == 0)
    def _():
        new_scr[...] = jnp.zeros(new_scr.shape, f32)

    def diag(x):
        y = x * dsel
        return y[:, 0:64] + y[:, 64:128] + y[:, 128:192] + y[:, 192:256]

    def one_sample(b):
        qs = (qrow_ref[b] * SCALE).astype(bf16)
        kc = kc_ref[b]
        vc = vc_ref[b]

        s = jnp.where(mc > 0.5, _dot_nt(qs, kc) + bc_ref[...], NEG)
        p = jnp.exp(s - jnp.max(s, axis=-1, keepdims=True)) * mc
        p = p / jnp.maximum(jnp.sum(p, axis=-1, keepdims=True), 1e-30)
        o_c = diag(_dot(p.astype(bf16), vc))

        p_sum = sum(_dot(smat, piece_) for piece_ in _split3(p))
        imp = sum(_dot(piece_, cmat) for piece_ in _split3(p_sum))
        n_blk = (N_PAGES * PAGE + n_new + SEL_LEN - 1) // SEL_LEN
        cur = (N_PAGES * PAGE) // SEL_LEN
        sidx = lax.broadcasted_iota(i32, imp.shape, 1)
        forced = jnp.where(sidx == 0, 1, jnp.where(sidx == cur, 1, jnp.where(sidx == cur - 1, 1, 0)))
        imp = jnp.where(forced > 0, imp + FORCE_BONUS, imp)
        imp = jnp.where(sidx <= cur, imp, NEG)
        rank = jnp.zeros(imp.shape, i32)
        for j in range(n_blk):
            col = imp[:, j:j + 1]
            rank = rank + jnp.where(col > imp, 1, jnp.where(col == imp, jnp.where(sidx > j, 1, 0), 0))
        sel = jnp.where(rank < N_SEL, jnp.where(imp > -1e29, jnp.where(sidx < n_blk, 1.0, 0.0), 0.0), 0.0)
        sel64 = _dot(smt_ref[...], sel.astype(bf16)).astype(bf16)
        selk = _dot(sel64, e_ref[...])

        def attend(kt_tiles, vt_tiles, new_rows, bias, mask):
            k_new, v_new = new_rows[:, 0:256].astype(bf16), new_rows[:, 256:512].astype(bf16)
            k_past = jnp.concatenate(kt_tiles, axis=1)
            v_past = jnp.concatenate(vt_tiles, axis=1)
            n_past = k_past.shape[1]
            s = jnp.concatenate([_dot(qs, k_past), _dot_nt(qs, k_new)], axis=1) + bias
            s = jnp.where(mask > 0.5, s, NEG)
            p = jnp.exp(s - jnp.max(s, axis=-1, keepdims=True)) * mask
            p = (p / jnp.maximum(jnp.sum(p, axis=-1, keepdims=True), 1e-30)).astype(bf16)
            acc = _dot_nt(p[:, 0:n_past], v_past) + _dot(p[:, n_past:n_past + 128], v_new)
            return diag(acc)

        new_scr[b, 0, 0:n_new, :] = news_ref[b]
        pages = cs[b * N_PAGES:(b + 1) * N_PAGES]
        k_tiles = [pg[0:256, :].astype(bf16) for pg in pages]
        v_tiles = [pg[256:512, :].astype(bf16) for pg in pages]
        o_s = attend(k_tiles, v_tiles, new_scr[b, 0], bs_ref[...], ms_ref[...] * selk)

        new_scr[b, 1, 0:n_new, :] = neww_ref[b]
        o_w = attend([win_ref[b, 0:256, :].astype(bf16)], [win_ref[b, 256:512, :].astype(bf16)],
                     new_scr[b, 1], bw_ref[...], mw_ref[...])

        sig = _sigmoid(ng_ref[b])
        o_ref[b] = sig[:, 0:1] * o_c + sig[:, 1:2] * o_s + sig[:, 2:3] * o_w

    for b in range(ATT_BATCH):
        one_sample(b)


def _nsa_sample(page_table, qrow, ng, kvc, news, neww, cache_s, win, layer, n_phys, consts):
    nb = qrow.shape[0]
    n_new = news.shape[1]
    ab = ATT_BATCH
    full = lambda a: pl.BlockSpec(a.shape, lambda i, pt: (0,) * a.ndim)
    per_b = lambda shape: pl.BlockSpec((ab,) + shape, lambda i, pt: (i, 0, 0))
    cmp_spec = lambda kv: pl.BlockSpec((None, ab) + kvc.shape[2:], lambda i, pt: (kv, i, 0, 0))

    def page_spec(s, p):
        return pl.BlockSpec((None, KV_W, PAGE), lambda i, pt: (layer * n_phys + pt[i * ab + s, p], 0, 0))

    in_specs = ([per_b((64, 256)), per_b((64, 3)), cmp_spec(0), cmp_spec(1),
                 per_b((n_new, KV_W)), per_b((n_new, KV_W))]
                + [page_spec(s, p) for s in range(ab) for p in range(N_PAGES)]
                + [pl.BlockSpec((ab, KV_W, WINDOW), lambda i, pt: ((layer * nb) // ab + i, 0, 0))]
                + [full(c) for c in consts])
    grid_spec = pltpu.PrefetchScalarGridSpec(
        num_scalar_prefetch=1,
        grid=(nb // ab,),
        in_specs=in_specs,
        out_specs=pl.BlockSpec((ab, 64, 64), lambda i, pt: (i, 0, 0)),
        scratch_shapes=[pltpu.VMEM((ab, 2, 128, KV_W), f32)],
    )
    return pl.pallas_call(
        _nsa_sample_kernel,
        grid_spec=grid_spec,
        out_shape=jax.ShapeDtypeStruct((nb, 64, 64), f32),
        compiler_params=_cparams(("arbitrary",)),
        name="nsa_sample",
    )(page_table, qrow, ng, kvc, kvc, news, neww, *([cache_s] * (ab * N_PAGES)), win, *consts)


def _static_tables(t, past, n_new):
    ar = np.arange
    tiles = np.stack([d + ar(128)[:, None] - ar(128)[None, :] for d in (0, 128, 256)])
    cmp_p = ar(t)[:, None] - (CMP_STRIDE * ar(128)[None, :] + CMP_LEN - 1)
    qpos = past + ar(n_new)
    cmp_s = qpos[:, None] - (CMP_STRIDE * ar(128)[None, :] + CMP_LEN - 1)
    kpos_s = np.concatenate([ar(past), past + ar(128)])
    slc_s = qpos[:, None] - kpos_s[None, :]
    kpos_w = np.concatenate([past - WINDOW + ar(WINDOW), past + ar(128)])
    win_s = qpos[:, None] - kpos_w[None, :]
    parts = [tiles.reshape(-1, 128), cmp_p, cmp_s.reshape(-1, 128), slc_s.reshape(-1, 128), win_s.reshape(-1, 128)]
    sizes = [p.shape[0] for p in parts]
    flat = np.concatenate(parts, axis=0)
    pad = (-flat.shape[0]) % 256
    flat = np.concatenate([flat, np.zeros((pad, 128), flat.dtype)], axis=0)
    buckets = _t5_bucket_np(flat)
    real_s = np.concatenate([np.ones(past, bool), ar(128) < n_new])
    real_w = np.concatenate([np.ones(WINDOW, bool), ar(128) < n_new])
    mask_c = (cmp_s >= 0)
    mask_s = (slc_s >= 0) & real_s[None, :]
    mask_w = (win_s >= 0) & (win_s < WINDOW) & real_w[None, :]
    rep = lambda mk: np.tile(mk[None].astype(np.float32), (N_HEADS, 1, 1)).reshape(N_HEADS * n_new, -1)
    return buckets, sizes, rep(mask_c), rep(mask_s), rep(mask_w)


def _cmp_to_sel_np(n_cmp, n_sel):
    j = np.arange(n_cmp)[:, None]
    s = np.arange(n_sel)[None, :]
    lo = np.maximum(j * CMP_STRIDE, s * SEL_LEN)
    hi = np.minimum(j * CMP_STRIDE + CMP_LEN, (s + 1) * SEL_LEN)
    return (np.maximum(hi - lo, 0) / CMP_LEN).astype(np.float32)


def _kron4(w):
    lead = [(0, 0)] * (w.ndim - 2)
    span = (N_KV - 1) * HEAD_DIM
    return sum(jnp.pad(w, lead + [(g * HEAD_DIM, span - g * HEAD_DIM)] * 2) for g in range(N_KV))


def _block_diag_rnn(w):
    w4 = w.reshape(DEPTH, 4, 4, RNN_BLOCK, RNN_BLOCK)
    eye = jnp.eye(4, dtype=w.dtype)
    return jnp.einsum('jk,zcjde->zcjdke', eye, w4).reshape(DEPTH, 4, 256, 256)


def _split_in(w, b):
    n_qkv = 1024 + 3 * KV_W
    n_ng = 3 * N_HEADS
    o_rxy = n_qkv + n_ng
    o_mg = o_rxy + 2 * D_RNN
    n_pad = N_PACK - (C_NG + n_ng)
    w_mg = w[..., o_mg:].astype(bf16)
    w_rxy = w[..., o_rxy:o_mg].astype(bf16)
    w_qkv = w[..., :n_qkv].astype(bf16)
    w_ng = jnp.pad(w[..., n_qkv:o_rxy].astype(bf16), ((0, 0), (0, 0), (0, IN_TN - n_ng)))
    bp = jnp.concatenate([b[..., o_mg:], b[..., o_rxy:o_mg], b[..., :n_qkv], b[..., n_qkv:o_rxy],
                          jnp.zeros(b.shape[:-1] + (n_pad,), b.dtype)], axis=-1)
    return (w_mg, w_rxy, w_qkv, w_ng), bp.reshape(DEPTH, 1, N_PACK)


def _feature_major(cache):
    d, n, rows = cache.shape[:3]
    return cache.transpose(0, 1, 3, 4, 5, 2).reshape(d * n, KV_W, rows)


def kernel(x_prompt, x_sample, c_prompt, c_sample, cache_cmp_kv, cache_slc_kv, cache_win_kv, state_conv, state_rnn_h, page_table, rel_bias, w_ada, b_ada, w_in, b_in, cmp_pos, cmp_w1, cmp_w2, w_attn_o, conv_w, conv_b, lru_wa, lru_ba, lru_wi, lru_bi, lru_lambda, w_rnn_o, w_out, w_ffn_in, w_ffn_out, ln_g, ln_b):
    nb, t, _ = x_prompt.shape
    ns, n_new, _ = x_sample.shape
    n_phys = cache_cmp_kv.shape[1]
    past = page_table.shape[1] * PAGE
    assert page_table.shape[1] == N_PAGES and cache_win_kv.shape[2] == WINDOW and t % (SLC_UNROLL * TK) == 0
    assert t == N_PAGES * PAGE
    assert n_new <= CMP_STRIDE and ns % 8 == 0 and ns % CMP_BATCH == 0 and ns % ATT_BATCH == 0
    mp, ms = nb * t, ns * n_new

    buckets, sizes, mask_c, mask_s, mask_w = _static_tables(t, past, n_new)
    bias_all = _bias_lookup(rel_bias, jnp.asarray(buckets))
    offs = np.cumsum([0] + sizes)
    seg = lambda k: bias_all[:, offs[k]:offs[k + 1]]
    bias_t = (seg(0) * LOG2E).reshape(N_KV, GROUP, 3, 128, 128).transpose(0, 2, 1, 3, 4).reshape(
        N_KV * 3, GROUP * 128, 128)
    bias_cp = seg(1)
    bias_cs = seg(2).reshape(N_HEADS * n_new, 128)
    bias_ss = seg(3).reshape(N_HEADS * n_new, past + 128)
    bias_ws = seg(4).reshape(N_HEADS * n_new, WINDOW + 128)

    n_sel_p = t // SEL_LEN
    ct_p = np.zeros((128, 128), np.float32)
    ct_p[:n_sel_p, :t // CMP_STRIDE - 1] = _cmp_to_sel_np(t // CMP_STRIDE - 1, n_sel_p).T
    e_p = (np.arange(128)[:, None] == (np.arange(t)[None, :] // SEL_LEN)).astype(np.float32)
    n_cmp_s = (past + n_new + CMP_STRIDE - 1) // CMP_STRIDE - 1
    n_sel_s = (past + n_new + SEL_LEN - 1) // SEL_LEN
    c_s = np.zeros((128, 128), np.float32)
    c_s[:n_cmp_s, :n_sel_s] = _cmp_to_sel_np(n_cmp_s, n_sel_s)
    e_s = (np.arange(128)[:, None] == (np.arange(past + 128)[None, :] // SEL_LEN)).astype(np.float32)
    hq = np.arange(N_HEADS * n_new)
    smat = ((hq[None, :] // (GROUP * n_new)) * n_new + hq[None, :] % n_new
            == np.arange(N_KV * n_new)[:, None]).astype(np.float32)
    dsel = (hq[:, None] // (GROUP * n_new) == np.arange(256)[None, :] // HEAD_DIM).astype(np.float32)
    pos = np.arange(PAGE)
    selp = ((pos % CMP_STRIDE) * (PAGE // CMP_STRIDE) + pos // CMP_STRIDE)[None, :] == np.arange(PAGE)[:, None]
    selp = jnp.asarray(selp, bf16)
    consts_s = (bias_cs, jnp.asarray(mask_c), bias_ss, jnp.asarray(mask_s), bias_ws,
                jnp.asarray(mask_w), jnp.asarray(smat, bf16), jnp.asarray(smat.T, bf16), jnp.asarray(c_s, bf16),
                jnp.asarray(e_s, bf16), jnp.asarray(dsel))

    w_in_p, b_in_p = _split_in(w_in, b_in)
    w1 = cmp_w1.reshape(DEPTH, 2, 2, CMP_STRIDE, HEAD_DIM, HEAD_DIM)
    wlt = jnp.concatenate([_kron4(w1[:, :, 0]), _kron4(w1[:, :, 1])], axis=-1).astype(bf16)
    w2k = _kron4(cmp_w2).astype(bf16)
    pe = jnp.tile(cmp_pos.reshape(DEPTH, 2, 2, CMP_STRIDE, HEAD_DIM), (1, 1, 1, 1, N_KV))
    rnn_w = (conv_w, conv_b.reshape(DEPTH, 1, D_RNN), _block_diag_rnn(lru_wa).astype(bf16),
             lru_ba.reshape(DEPTH, 1, D_RNN), _block_diag_rnn(lru_wi).astype(bf16),
             lru_bi.reshape(DEPTH, 1, D_RNN), lru_lambda.reshape(DEPTH, 1, D_RNN))
    w_ao, w_ro, w_o = w_attn_o.astype(bf16), w_rnn_o.astype(bf16), w_out.astype(bf16)
    w_f1, w_f2 = w_ffn_in.astype(bf16), w_ffn_out.astype(bf16)
    ln_g4, ln_b4 = ln_g.reshape(DEPTH, 2, 1, D_MODEL), ln_b.reshape(DEPTH, 2, 1, D_MODEL)
    b_ada3 = b_ada.reshape(DEPTH, 1, 6 * D_MODEL)

    n_cpad = -(nb + ns) % 8
    c_all = jnp.concatenate([c_prompt, c_sample, jnp.zeros((n_cpad, D_MODEL), f32)], axis=0)
    cache_ct, cache_st, win_t = _feature_major(cache_cmp_kv), _feature_major(cache_slc_kv), _feature_major(cache_win_kv)
    cbuf_t = state_conv.transpose(0, 2, 1, 3)

    y_p = x_prompt.reshape(mp, D_MODEL)
    y_s = x_sample.transpose(1, 0, 2).reshape(ms, D_MODEL)
    st_p, st_s = [], []
    for l in range(DEPTH):
        ada = _ada(c_all, w_ada, b_ada3, l)
        mod_p = ada[:nb].reshape(nb, 1, 6 * D_MODEL)
        mod_s = ada[nb:nb + ns].reshape(1, ns, 6 * D_MODEL)
        cb = _cmp_bias(pe[l], wlt[l])

        proj, *kvt = _inproj(y_p, mod_p, w_in_p, b_in_p, l, tm=TM_INPROJ, seq_rows=t)
        kvc_p = _compress_sample(jnp.zeros((nb, 1), i32), jnp.zeros((nb, 1, KV_W), f32), kvt[0],
                                 lambda b, p, pt: (b, p), l, wlt, cb, w2k, selp)
        o_attn = _nsa_prompt(proj, kvc_p, bias_cp, bias_t, jnp.asarray(ct_p, bf16), jnp.asarray(e_p, bf16), nb, t)
        o_rnn, h_last = _rglru_prompt(proj, nb, t, rnn_w, l)
        x1 = _mixout(o_attn, o_rnn, proj, y_p, mod_p, w_ao, w_ro, w_o, ln_g4, ln_b4, l, tm=TM_MIXOUT)
        y_p = _ffn(x1, mod_p, w_f1, w_f2, ln_g4, ln_b4, l, tm=TM_FFN)
        kv = lambda k, kvt=kvt: kvt[k].reshape(nb, 2, N_KV, HEAD_DIM, t).transpose(0, 4, 1, 2, 3)
        st_p.append((kv(0), kv(1), kv(2)[:, t - min(WINDOW, t):],
                     proj.reshape(nb, t, N_PACK)[:, t - (CONV_W - 1):, C_RX:C_RX + D_RNN],
                     h_last.reshape(nb, D_RNN)))

        proj, *kvt = _inproj(y_s, mod_s, w_in_p, b_in_p, l, tm=ms, seq_rows=ms)
        proj3 = proj.reshape(n_new, ns, N_PACK)
        q5 = proj3[:, :, C_Q:C_Q + 1024].reshape(n_new, ns, N_KV, GROUP, HEAD_DIM).transpose(1, 2, 3, 0, 4)
        qrow = (q5[:, :, :, :, None, :] * jnp.eye(N_KV, dtype=f32)[None, :, None, None, :, None]
                ).reshape(ns, N_HEADS * n_new, 256)
        ng = proj3[:, :, C_NG:C_NG + 48].reshape(n_new, ns, N_HEADS, 3).transpose(1, 2, 0, 3).reshape(
            ns, N_HEADS * n_new, 3)
        new_rows = lambda c: proj3[:, :, c:c + KV_W].transpose(1, 0, 2)
        kvc_s = _compress_sample(page_table, new_rows(C_KVC), cache_ct,
                                 lambda b, p, pt: (l * n_phys + pt[b, p], 0), l, wlt, cb, w2k, selp)
        o64 = _nsa_sample(page_table, qrow, ng, kvc_s, new_rows(C_KVS), new_rows(C_KVW),
                          cache_st, win_t, l, n_phys, consts_s)
        o_attn = o64.reshape(ns, N_HEADS, n_new, HEAD_DIM).transpose(2, 0, 1, 3).reshape(ms, 1024).astype(bf16)
        o_rnn, h_last = _rglru_sample(proj, cbuf_t, state_rnn_h, rnn_w, l)
        x1 = _mixout(o_attn, o_rnn, proj, y_s, mod_s, w_ao, w_ro, w_o, ln_g4, ln_b4, l, tm=min(TM_MIXOUT, ms))
        y_s = _ffn(x1, mod_s, w_f1, w_f2, ln_g4, ln_b4, l, tm=ms)
        kv = lambda k, kvt=kvt: kvt[k].reshape(2, N_KV, HEAD_DIM, n_new, ns).transpose(4, 3, 0, 1, 2)
        xp = jnp.concatenate([state_conv[l], proj3[:, :, C_RX:C_RX + D_RNN].transpose(1, 0, 2)], axis=1)
        st_s.append((kv(0), kv(1), kv(2), xp[:, n_new:], h_last))

    stack = lambda sts, k: jnp.stack([s[k] for s in sts])
    return (y_p.reshape(nb, t, D_MODEL), y_s.reshape(n_new, ns, D_MODEL).transpose(1, 0, 2),
            stack(st_p, 0), stack(st_p, 1), stack(st_p, 2), stack(st_p, 3), stack(st_p, 4),
            stack(st_s, 0), stack(st_s, 1), stack(st_s, 2), stack(st_s, 3), stack(st_s, 4))
```

```python
import math

import numpy as np
import jax
import jax.numpy as jnp
from jax import lax
from jax.experimental import pallas as pl
from jax.experimental.pallas import tpu as pltpu

f32 = jnp.float32
bf16 = jnp.bfloat16
i32 = jnp.int32

D_MODEL = 2048
N_HEADS = 16
N_KV = 4
GROUP = 4
HEAD_DIM = 64
CMP_STRIDE = 16
CMP_LEN = 32
SEL_LEN = 64
N_SEL = 16
WINDOW = 512
PAGE = 128
FORCE_BONUS = 1e3
N_BUCKETS = 32
MAX_DISTANCE = 128
D_RNN = 1024
RNN_BLOCK = 64
CONV_W = 4
LRU_C = 8.0
D_FF = 5632
DEPTH = 2
ALPHA = (2.0 * DEPTH) ** 0.25
SCALE = HEAD_DIM ** -0.5
LOG2E = math.log2(math.e)
NEG = -1e30
KV_W = 2 * N_KV * HEAD_DIM

C_MG = 0
C_RX = 4096
C_RY = 5120
C_Q = 6144
C_KVC = 7168
C_KVS = 7680
C_KVW = 8192
C_NG = 8704
N_PACK = 9216

VMEM_LIMIT = 56 * 1024 * 1024

TQ = 128
TK = 256
SLC_UNROLL = 4
N_PAGES = 16
TM_INPROJ = 1024
TM_MIXOUT = 256
TM_FFN = 512


def _cparams(sem):
    return pltpu.CompilerParams(dimension_semantics=sem, vmem_limit_bytes=VMEM_LIMIT)


def _sigmoid(x):
    return 1.0 / (1.0 + jnp.exp(-x))


def _silu(x):
    return x * _sigmoid(x)


def _layer_norm(z, g, b):
    mu = jnp.mean(z, axis=-1, keepdims=True)
    zc = z - mu
    var = jnp.mean(zc * zc, axis=-1, keepdims=True)
    return zc * lax.rsqrt(var + 1e-5) * g + b


def _dot(a, b):
    return jnp.dot(a, b, preferred_element_type=f32)


def _dot_nt(a, b):
    return lax.dot_general(a, b, (((1,), (1,)), ((), ())), preferred_element_type=f32)


def _split3(x):
    hi = x.astype(bf16)
    r1 = x - hi.astype(f32)
    mid = r1.astype(bf16)
    lo = (r1 - mid.astype(f32)).astype(bf16)
    return hi, mid, lo


def _rows(mod, tm):
    mr = mod.shape[0]
    if mr == 1 or mr == tm:
        return mod
    return jnp.concatenate([mod] * (tm // mr), axis=0)


def _mod_spec(mod, k, tiles_per_block, ngrid):
    mr = mod.shape[1]
    if ngrid == 1:
        return pl.BlockSpec((None, mr, D_MODEL), lambda i: (i // tiles_per_block, 0, k))
    return pl.BlockSpec((None, mr, D_MODEL), lambda i, j: (i // tiles_per_block, 0, k))


def _ada_kernel(c_ref, w_ref, b_ref, o_ref):
    h = _silu(c_ref[...]).astype(bf16)
    o_ref[...] = _dot(h, w_ref[...].astype(bf16)) + b_ref[...]


def _ada(c_all, w, b, layer):
    m = c_all.shape[0]
    n = w.shape[2]
    tn = 1024
    return pl.pallas_call(
        _ada_kernel,
        grid=(n // tn,),
        in_specs=[pl.BlockSpec((m, D_MODEL), lambda j: (0, 0)),
                  pl.BlockSpec((None, D_MODEL, tn), lambda j: (layer, 0, j)),
                  pl.BlockSpec((None, 1, tn), lambda j: (layer, 0, j))],
        out_specs=pl.BlockSpec((m, tn), lambda j: (0, j)),
        out_shape=jax.ShapeDtypeStruct((m, n), f32),
        compiler_params=_cparams(("arbitrary",)),
        name="ada",
    )(c_all, w, b)


def _bias_lookup_kernel(tbl_ref, bk_ref, o_ref):
    bk = bk_ref[...]
    for h in range(N_HEADS):
        acc = jnp.zeros(bk.shape, f32)
        for k in range(N_BUCKETS):
            acc = jnp.where(bk == k, tbl_ref[k, h], acc)
        o_ref[h] = acc


def _bias_lookup(tbl, buckets):
    n = buckets.shape[0]
    tr = 256
    return pl.pallas_call(
        _bias_lookup_kernel,
        grid=(n // tr,),
        in_specs=[pl.BlockSpec(memory_space=pltpu.SMEM),
                  pl.BlockSpec((tr, 128), lambda i: (i, 0))],
        out_specs=pl.BlockSpec((N_HEADS, tr, 128), lambda i: (0, i, 0)),
        out_shape=jax.ShapeDtypeStruct((N_HEADS, n, 128), f32),
        compiler_params=_cparams(("arbitrary",)),
        name="bias_lookup",
    )(tbl, buckets)


def _t5_bucket_np(dist):
    n = np.maximum(dist, 0)
    exact = N_BUCKETS // 2
    ratio = np.maximum(n, 1).astype(np.float32) / np.float32(exact)
    log_ratio = np.log(ratio).astype(np.float32) / np.float32(math.log(MAX_DISTANCE / exact))
    large = np.minimum(exact + (log_ratio * np.float32(N_BUCKETS - exact)).astype(np.int32), N_BUCKETS - 1)
    return np.where(n < exact, n, large).astype(np.int32)


IN_TN = 512
KV_J0 = C_KVC // IN_TN


IN_PIECES = (("rxy_mg", 0, 12, 4), ("qkv", 12, 5, 0), ("ng", 17, 1, 0))


def _inproj_kernel(x_ref, sc_ref, sh_ref, wmid_ref, wqkv_ref, wng_ref, b_ref,
                   o_ref, okc_ref, oks_ref, okw_ref, h_scr):
    j = pl.program_id(1)

    @pl.when(j == 0)
    def _():
        tm = x_ref.shape[0]
        h_scr[...] = (x_ref[...] * (1.0 + _rows(sc_ref[...], tm)) + _rows(sh_ref[...], tm)).astype(bf16)

    def emit(w_ref):
        res = _dot(h_scr[...], w_ref[...]) + b_ref[...]
        o_ref[...] = res
        return res

    for w_ref, (_, j0, nj, _) in zip((wmid_ref, wqkv_ref, wng_ref), IN_PIECES):
        @pl.when(jnp.logical_and(j >= j0, j < j0 + nj))
        def _(w_ref=w_ref, j0=j0):
            res = emit(w_ref)
            if j0 == IN_PIECES[1][1]:
                for k, okt_ref in enumerate((okc_ref, oks_ref, okw_ref)):
                    @pl.when(j == KV_J0 + k)
                    def _(okt_ref=okt_ref):
                        okt_ref[...] = res.T


def _inproj(x, mod, w_pieces, b, layer, tm, seq_rows):
    m = x.shape[0]
    tn = IN_TN
    tpb = (m // tm) // mod.shape[0]
    tps = seq_rows // tm
    kt_spec = pl.BlockSpec((None, tn, tm), lambda i, j: (i // tps, 0, i % tps))
    kt_shape = jax.ShapeDtypeStruct((m // seq_rows, tn, seq_rows), f32)

    def w_spec(j0, nj, rot):
        return pl.BlockSpec((None, D_MODEL, tn), lambda i, j: (layer, 0, (jnp.clip(j - j0, 0, nj - 1) + rot) % nj))

    return pl.pallas_call(
        _inproj_kernel,
        grid=(m // tm, N_PACK // tn),
        in_specs=[pl.BlockSpec((tm, D_MODEL), lambda i, j: (i, 0)),
                  _mod_spec(mod, 1, tpb, 2), _mod_spec(mod, 0, tpb, 2)]
        + [w_spec(j0, nj, rot) for (_, j0, nj, rot) in IN_PIECES]
        + [pl.BlockSpec((None, 1, tn), lambda i, j: (layer, 0, j))],
        out_specs=[pl.BlockSpec((tm, tn), lambda i, j: (i, j)), kt_spec, kt_spec, kt_spec],
        out_shape=[jax.ShapeDtypeStruct((m, N_PACK), f32), kt_shape, kt_shape, kt_shape],
        scratch_shapes=[pltpu.VMEM((tm, D_MODEL), bf16)],
        compiler_params=_cparams(("parallel", "arbitrary")),
        name="inproj",
    )(x, mod, mod, *w_pieces, b)


def _cmp_bias_kernel(pe_ref, wlt_ref, o_ref):
    for kv in range(2):
        acc = jnp.zeros((8, 512), f32)
        for l in range(CMP_STRIDE):
            w = wlt_ref[kv, l]
            lead = _dot(jnp.broadcast_to(pe_ref[kv, 0, l:l + 1, :], (8, 256)).astype(bf16), w[:, 0:256])
            tail = _dot(jnp.broadcast_to(pe_ref[kv, 1, l:l + 1, :], (8, 256)).astype(bf16), w[:, 256:512])
            acc = acc + jnp.concatenate([lead, tail], axis=1)
        o_ref[kv] = acc


def _cmp_bias(pe, wlt):
    return pl.pallas_call(
        _cmp_bias_kernel,
        out_shape=jax.ShapeDtypeStruct((2, 8, 512), f32),
        compiler_params=pltpu.CompilerParams(vmem_limit_bytes=VMEM_LIMIT),
        name="cmp_bias",
    )(pe, wlt)


def _topk_mask_rows(imp, sidx, n_keep):
    n_rows = imp.shape[0]
    rank = jnp.zeros(imp.shape, i32)
    for j in range(n_rows):
        row = imp[j:j + 1, :]
        beats = jnp.where(row > imp, 1, jnp.where(row == imp, jnp.where(sidx > j, 1, 0), 0))
        rank = rank + beats
    return jnp.where(rank < n_keep, jnp.where(imp > -1e29, 1.0, 0.0), 0.0)


def _nsa_prompt_kernel(q_ref, ng_ref, ks_ref, vs_ref, kw_ref, vw_ref, kc_ref, vc_ref,
                       bc_ref, bt_ref, ct_ref, e_ref, o_ref, selk_scr, s_scr, mx_scr):
    i = pl.program_id(1)
    q0 = i * TQ
    rows = GROUP * TQ
    n_sel_blk = ks_ref.shape[0] // SEL_LEN

    d0 = lax.broadcasted_iota(i32, (TQ, TK), 0) - lax.broadcasted_iota(i32, (TQ, TK), 1)
    dist_c = (q0 + (lax.broadcasted_iota(i32, (rows, 128), 0) & (TQ - 1))
              - CMP_STRIDE * lax.broadcasted_iota(i32, (rows, 128), 1) - (CMP_LEN - 1))
    mask_c = dist_c >= 0
    sig = _sigmoid(ng_ref[...])

    sidx = lax.broadcasted_iota(i32, (n_sel_blk, TQ), 0)
    cur = (q0 + lax.broadcasted_iota(i32, (n_sel_blk, TQ), 1)) // SEL_LEN
    forced = jnp.where(sidx == 0, 1, jnp.where(sidx == cur, 1, jnp.where(sidx == cur - 1, 1, 0)))

    def branch(qs, g, k_ref, v_ref, kt_hi, window):
        n_tiles = k_ref.shape[0] // TK
        ones = jnp.ones((TK, HEAD_DIM), f32)

        def score_tile(kt, slot):
            k0 = pl.multiple_of(jnp.clip(kt, 0, n_tiles - 1) * TK, TK)
            k = k_ref[pl.ds(k0, TK), g * 64:(g + 1) * 64].astype(bf16)
            delta = q0 - kt * TK
            dist = d0 + delta
            if window:
                live = jnp.where(kt >= 0, 0.0, NEG)
                addm = jnp.where(dist >= 0, jnp.where(dist < WINDOW, live, NEG), NEG)
            else:
                addm = jnp.where(dist >= 0, selk_scr[:, pl.ds(k0, TK)], NEG)
            va = jnp.clip(delta, 0, 256) // 128
            vb = jnp.clip(delta - 128, 0, 256) // 128
            bias = jnp.concatenate([bt_ref[g * 3 + va], bt_ref[g * 3 + vb]], axis=1)
            s = _dot_nt(qs, k) + bias + jnp.concatenate([addm] * GROUP, axis=0)
            s_scr[:, pl.ds(slot * TK, TK)] = s
            return jnp.maximum(s[:, 0:128], s[:, 128:256])

        def value_tile(kt, slot, mb):
            k0 = pl.multiple_of(jnp.clip(kt, 0, n_tiles - 1) * TK, TK)
            v = jnp.concatenate([v_ref[pl.ds(k0, TK), g * 64:(g + 1) * 64], ones], axis=1).astype(bf16)
            p = jnp.exp2(s_scr[:, pl.ds(slot * TK, TK)] - jnp.concatenate([mb, mb], axis=1))
            return _dot(p.astype(bf16), v)

        if window:
            tiles = [(kt_hi - 2 + j, j) for j in range(3)]
            mx = jnp.full((rows, 128), NEG, f32)
            for kt, slot in tiles:
                mx = jnp.maximum(mx, score_tile(kt, slot))
            mb = jnp.broadcast_to(jnp.max(mx, axis=-1, keepdims=True), (rows, 128))
            acc = jnp.zeros((rows, 2 * HEAD_DIM), f32)
            for kt, slot in tiles:
                acc = acc + value_tile(kt, slot, mb)
        else:
            n_groups = kt_hi // SLC_UNROLL + 1

            def scores(qd, mx):
                for j in range(SLC_UNROLL):
                    mx = jnp.maximum(mx, score_tile(SLC_UNROLL * qd + j, SLC_UNROLL * qd + j))
                return mx

            mx = lax.fori_loop(0, n_groups, scores, jnp.full((rows, 128), NEG, f32))
            mx_scr[...] = jnp.broadcast_to(jnp.max(mx, axis=-1, keepdims=True), (rows, 128))

            def values(qd, acc):
                mb = mx_scr[...]
                for j in range(SLC_UNROLL):
                    acc = acc + value_tile(SLC_UNROLL * qd + j, SLC_UNROLL * qd + j, mb)
                return acc

            acc = lax.fori_loop(0, n_groups, values, jnp.zeros((rows, 2 * HEAD_DIM), f32))
        return acc[:, 0:HEAD_DIM] / jnp.maximum(acc[:, HEAD_DIM:HEAD_DIM + 1], 1e-30)

    for g in range(N_KV):
        qg = q_ref[:, g * 256:(g + 1) * 256]
        qst = jnp.concatenate([qg[:, r * 64:(r + 1) * 64] for r in range(GROUP)], axis=0)
        qs = (qst * SCALE).astype(bf16)
        qs2 = (qst * (SCALE * LOG2E)).astype(bf16)

        kcg = kc_ref[:, g * 64:(g + 1) * 64].astype(bf16)
        vcg = vc_ref[:, g * 64:(g + 1) * 64].astype(bf16)
        bias_c = jnp.concatenate([bc_ref[g * GROUP + r] for r in range(GROUP)], axis=0)
        s = jnp.where(mask_c, _dot_nt(qs, kcg) + bias_c, NEG)
        p = jnp.where(mask_c, jnp.exp(s - jnp.max(s, axis=-1, keepdims=True)), 0.0)
        p = p / jnp.maximum(jnp.sum(p, axis=-1, keepdims=True), 1e-30)
        o_c = _dot(p.astype(bf16), vcg)

        p_sum = p[0:TQ] + p[TQ:2 * TQ] + p[2 * TQ:3 * TQ] + p[3 * TQ:4 * TQ]
        ct = ct_ref[...]
        imp_t = sum(_dot_nt(ct, piece) for piece in _split3(p_sum))[0:n_sel_blk]
        imp_t = jnp.where(forced > 0, imp_t + FORCE_BONUS, imp_t)
        imp_t = jnp.where(sidx <= cur, imp_t, NEG)
        sel_t = _topk_mask_rows(imp_t, sidx, N_SEL)
        sel_pad = jnp.concatenate([sel_t, jnp.zeros((128 - n_sel_blk, TQ), f32)], axis=0)
        sel_q = sel_pad.T.astype(bf16)
        selk_scr[...] = jnp.where(_dot(sel_q, e_ref[...]) > 0.5, 0.0, NEG)

        kt_hi = (q0 + TQ - 1) // TK
        o_s = branch(qs2, g, ks_ref, vs_ref, kt_hi, False)
        o_w = branch(qs2, g, kw_ref, vw_ref, kt_hi, True)

        gate = lambda c: jnp.concatenate(
            [sig[:, g * 12 + r * 3 + c:g * 12 + r * 3 + c + 1] for r in range(GROUP)], axis=0)
        o = gate(0) * o_c + gate(1) * o_s + gate(2) * o_w
        o_ref[:, g * 256:(g + 1) * 256] = jnp.concatenate(
            [o[r * TQ:(r + 1) * TQ] for r in range(GROUP)], axis=1).astype(bf16)


def _nsa_prompt(proj, kvc, bias_c, bias_t, ct, e, nb, t):
    nq = t // TQ
    full = lambda a: pl.BlockSpec(a.shape, lambda b, i: (0,) * a.ndim)
    kv = lambda c: pl.BlockSpec((t, 256), lambda b, i: (b, c // 256))
    cmp_spec = lambda k: pl.BlockSpec((None, None, t // CMP_STRIDE, 256), lambda b, i: (k, b, 0, 0))
    return pl.pallas_call(
        _nsa_prompt_kernel,
        grid=(nb, nq),
        in_specs=[pl.BlockSpec((TQ, 1024), lambda b, i: (b * nq + i, C_Q // 1024)),
                  pl.BlockSpec((TQ, 128), lambda b, i: (b * nq + i, C_NG // 128)),
                  kv(C_KVS), kv(C_KVS + 256), kv(C_KVW), kv(C_KVW + 256),
                  cmp_spec(0), cmp_spec(1),
                  pl.BlockSpec((N_HEADS, TQ, 128), lambda b, i: (0, i, 0)),
                  full(bias_t), full(ct), full(e)],
        out_specs=pl.BlockSpec((TQ, 1024), lambda b, i: (b * nq + i, 0)),
        out_shape=jax.ShapeDtypeStruct((nb * t, 1024), bf16),
        scratch_shapes=[pltpu.VMEM((TQ, t), f32), pltpu.VMEM((GROUP * TQ, t), f32),
                        pltpu.VMEM((GROUP * TQ, 128), f32)],
        compiler_params=_cparams(("parallel", "arbitrary")),
        name="nsa_prompt",
    )(proj, proj, proj, proj, proj, proj, kvc, kvc, bias_c, bias_t, ct, e)


def _lru_gates(xc, ry, wa_ref, ba_ref, wi_ref, bi_ref, lam_ref):
    xb = xc.astype(bf16)
    ra = jnp.concatenate([_dot(xb[:, c * 256:(c + 1) * 256], wa_ref[c]) for c in range(4)], axis=1)
    ri = jnp.concatenate([_dot(xb[:, c * 256:(c + 1) * 256], wi_ref[c]) for c in range(4)], axis=1)
    r = _sigmoid(ra + ba_ref[...])
    ig = _sigmoid(ri + bi_ref[...])
    nl = -lam_ref[...]
    softplus = jnp.maximum(nl, 0.0) + jnp.log1p(jnp.exp(-jnp.abs(nl)))
    log_a = -LRU_C * r * softplus
    a = jnp.exp(log_a)
    u = jnp.sqrt(jnp.tanh(-log_a) * (a * a + 1.0)) * (ig * xc)
    return a, u, jax.nn.gelu(ry)


def _rglru_prompt_kernel(rx_ref, ry_ref, cw_ref, cb_ref, wa_ref, ba_ref, wi_ref, bi_ref, lam_ref,
                         o_ref, hl_ref, xp_scr, a_scr, u_scr, hs_scr, h_scr):
    tt = pl.program_id(1)
    tr = rx_ref.shape[0]

    @pl.when(tt == 0)
    def _():
        xp_scr[0:8, :] = jnp.zeros((8, D_RNN), f32)
        h_scr[...] = jnp.zeros((1, D_RNN), f32)

    x = rx_ref[...]
    xp_scr[8:8 + tr, :] = x
    y = x * cw_ref[CONV_W - 1:CONV_W, :] + cb_ref[...]
    for k in range(CONV_W - 1):
        y = y + xp_scr[5 + k:5 + k + tr, :] * cw_ref[k:k + 1, :]
    xp_scr[0:8, :] = xp_scr[tr:tr + 8, :]

    a, u, gate = _lru_gates(y, ry_ref[...], wa_ref, ba_ref, wi_ref, bi_ref, lam_ref)
    a_scr[...] = a
    u_scr[...] = u

    def step(t, h):
        h = a_scr[pl.ds(t, 1), :] * h + u_scr[pl.ds(t, 1), :]
        hs_scr[pl.ds(t, 1), :] = h
        return h

    h = lax.fori_loop(0, tr, step, h_scr[...], unroll=8)
    h_scr[...] = h
    hl_ref[...] = h
    o_ref[...] = (hs_scr[...] * gate).astype(bf16)


def _rnn_specs(rnn_w, layer, ngrid):
    zeros = lambda n: (0,) * n
    if ngrid == 2:
        return [pl.BlockSpec((None,) + a.shape[1:], lambda b, i, n=a.ndim - 1: (layer,) + zeros(n)) for a in rnn_w]
    return [pl.BlockSpec((None,) + a.shape[1:], lambda i, n=a.ndim - 1: (layer,) + zeros(n)) for a in rnn_w]


def _rglru_prompt(proj, nb, t, rnn_w, layer):
    tr = 256
    nt = t // tr
    return pl.pallas_call(
        _rglru_prompt_kernel,
        grid=(nb, nt),
        in_specs=[pl.BlockSpec((tr, D_RNN), lambda b, i: (b * nt + i, C_RX // D_RNN)),
                  pl.BlockSpec((tr, D_RNN), lambda b, i: (b * nt + i, C_RY // D_RNN))]
        + _rnn_specs(rnn_w, layer, 2),
        out_specs=[pl.BlockSpec((tr, D_RNN), lambda b, i: (b * nt + i, 0)),
                   pl.BlockSpec((None, 1, D_RNN), lambda b, i: (b, 0, 0))],
        out_shape=[jax.ShapeDtypeStruct((nb * t, D_RNN), bf16),
                   jax.ShapeDtypeStruct((nb, 1, D_RNN), f32)],
        scratch_shapes=[pltpu.VMEM((tr + 8, D_RNN), f32), pltpu.VMEM((tr, D_RNN), f32),
                        pltpu.VMEM((tr, D_RNN), f32), pltpu.VMEM((tr, D_RNN), f32),
                        pltpu.VMEM((1, D_RNN), f32)],
        compiler_params=_cparams(("parallel", "arbitrary")),
        name="rglru_prompt",
    )(proj, proj, *rnn_w)


def _rglru_sample_kernel(rx_ref, ry_ref, cbuf_ref, h0_ref, cw_ref, cb_ref, wa_ref, ba_ref, wi_ref, bi_ref,
                         lam_ref, o_ref, hl_ref):
    ns = h0_ref.shape[0]
    n_t = rx_ref.shape[0] // ns
    xp = [cbuf_ref[k] for k in range(CONV_W - 1)] + [rx_ref[s * ns:(s + 1) * ns, :] for s in range(n_t)]
    h = h0_ref[...]
    for s in range(n_t):
        y = xp[s + CONV_W - 1] * cw_ref[CONV_W - 1:CONV_W, :] + cb_ref[...]
        for k in range(CONV_W - 1):
            y = y + xp[s + k] * cw_ref[k:k + 1, :]
        a, u, gate = _lru_gates(y, ry_ref[s * ns:(s + 1) * ns, :], wa_ref, ba_ref, wi_ref, bi_ref, lam_ref)
        h = a * h + u
        o_ref[s * ns:(s + 1) * ns, :] = (h * gate).astype(bf16)
    hl_ref[...] = h


def _rglru_sample(proj, cbuf, h0, rnn_w, layer):
    ms = proj.shape[0]
    ns = h0.shape[1]
    return pl.pallas_call(
        _rglru_sample_kernel,
        grid=(1,),
        in_specs=[pl.BlockSpec((ms, D_RNN), lambda i: (0, C_RX // D_RNN)),
                  pl.BlockSpec((ms, D_RNN), lambda i: (0, C_RY // D_RNN)),
                  pl.BlockSpec((None, CONV_W - 1, ns, D_RNN), lambda i: (layer, 0, 0, 0)),
                  pl.BlockSpec((None, ns, D_RNN), lambda i: (layer, 0, 0))]
        + _rnn_specs(rnn_w, layer, 1),
        out_specs=[pl.BlockSpec((ms, D_RNN), lambda i: (0, 0)),
                   pl.BlockSpec((ns, D_RNN), lambda i: (0, 0))],
        out_shape=[jax.ShapeDtypeStruct((ms, D_RNN), bf16),
                   jax.ShapeDtypeStruct((ns, D_RNN), f32)],
        compiler_params=_cparams(("arbitrary",)),
        name="rglru_sample",
    )(proj, proj, cbuf, h0, *rnn_w)


def _mixout_kernel(oa_ref, orn_ref, ga_ref, gr_ref, x_ref, gt_ref, wa_ref, wr_ref, wo_ref, lg_ref, lb_ref, o_ref):
    a1 = _dot(oa_ref[...], wa_ref[...])
    a2 = _dot(orn_ref[...], wr_ref[...])
    merged = (_sigmoid(ga_ref[...]) * a1 + _sigmoid(gr_ref[...]) * a2).astype(bf16)
    mix = _dot(merged, wo_ref[...])
    z = ALPHA * x_ref[...] + (1.0 + _rows(gt_ref[...], x_ref.shape[0])) * mix
    o_ref[...] = _layer_norm(z, lg_ref[...], lb_ref[...])


def _mixout(o_attn, o_rnn, proj, x, mod, w_ao, w_ro, w_o, ln_g, ln_b, layer, tm):
    m = o_attn.shape[0]
    tpb = (m // tm) // mod.shape[0]
    ln_spec = pl.BlockSpec((None, None, 1, D_MODEL), lambda i: (layer, 0, 0, 0))
    w_spec = lambda k: pl.BlockSpec((None, k, D_MODEL), lambda i: (layer, 0, 0), pipeline_mode=pl.Buffered(1))
    return pl.pallas_call(
        _mixout_kernel,
        grid=(m // tm,),
        in_specs=[pl.BlockSpec((tm, 1024), lambda i: (i, 0)),
                  pl.BlockSpec((tm, 1024), lambda i: (i, 0)),
                  pl.BlockSpec((tm, D_MODEL), lambda i: (i, C_MG // D_MODEL)),
                  pl.BlockSpec((tm, D_MODEL), lambda i: (i, C_MG // D_MODEL + 1)),
                  pl.BlockSpec((tm, D_MODEL), lambda i: (i, 0)),
                  _mod_spec(mod, 2, tpb, 1),
                  w_spec(1024), w_spec(1024), w_spec(D_MODEL),
                  ln_spec, ln_spec],
        out_specs=pl.BlockSpec((tm, D_MODEL), lambda i: (i, 0)),
        out_shape=jax.ShapeDtypeStruct((m, D_MODEL), f32),
        compiler_params=_cparams(("parallel",)),
        name="mixout",
    )(o_attn, o_rnn, proj, proj, x, mod, w_ao, w_ro, w_o, ln_g, ln_b)


def _ffn_kernel(x_ref, sc_ref, sh_ref, gt_ref, wg_ref, wu_ref, wo_ref, lg_ref, lb_ref, o_ref, h_scr):
    j = pl.program_id(1)
    tm = x_ref.shape[0]

    @pl.when(j == 0)
    def _():
        h_scr[...] = (x_ref[...] * (1.0 + _rows(sc_ref[...], tm)) + _rows(sh_ref[...], tm)).astype(bf16)
        o_ref[...] = jnp.zeros(o_ref.shape, f32)

    h = h_scr[...]
    act = (_silu(_dot(h, wg_ref[...])) * _dot(h, wu_ref[...])).astype(bf16)
    o_ref[...] += _dot(act, wo_ref[...])

    @pl.when(j == pl.num_programs(1) - 1)
    def _():
        z = ALPHA * x_ref[...] + (1.0 + _rows(gt_ref[...], tm)) * o_ref[...]
        o_ref[...] = _layer_norm(z, lg_ref[...], lb_ref[...])


def _ffn(x, mod, w_in, w_out, ln_g, ln_b, layer, tm):
    m = x.shape[0]
    tf = 512
    nf = D_FF // tf
    tpb = (m // tm) // mod.shape[0]
    ln_spec = pl.BlockSpec((None, None, 1, D_MODEL), lambda i, j: (layer, 1, 0, 0))
    return pl.pallas_call(
        _ffn_kernel,
        grid=(m // tm, nf),
        in_specs=[pl.BlockSpec((tm, D_MODEL), lambda i, j: (i, 0)),
                  _mod_spec(mod, 4, tpb, 2), _mod_spec(mod, 3, tpb, 2), _mod_spec(mod, 5, tpb, 2),
                  pl.BlockSpec((None, D_MODEL, tf), lambda i, j: (layer, 0, j)),
                  pl.BlockSpec((None, D_MODEL, tf), lambda i, j: (layer, 0, nf + j)),
                  pl.BlockSpec((None, tf, D_MODEL), lambda i, j: (layer, j, 0)),
                  ln_spec, ln_spec],
        out_specs=pl.BlockSpec((tm, D_MODEL), lambda i, j: (i, 0)),
        out_shape=jax.ShapeDtypeStruct((m, D_MODEL), f32),
        scratch_shapes=[pltpu.VMEM((tm, D_MODEL), bf16)],
        compiler_params=_cparams(("parallel", "arbitrary")),
        name="ffn",
    )(x, mod, mod, mod, w_in, w_in, w_out, ln_g, ln_b)


CMP_BATCH = 4


def _compress_sample_kernel(pt_ref, newc_ref, *rest):
    batch = newc_ref.shape[0]
    n_pg = batch * N_PAGES
    pages = rest[0:n_pg]
    wlt_ref, cb_ref, w2_ref, selp_ref, o_ref, x_scr = rest[n_pg:]
    n_new = newc_ref.shape[1]
    per = N_PAGES * (PAGE // CMP_STRIDE) + 8
    row8 = lax.broadcasted_iota(i32, (8, 256), 0)
    selp = selp_ref[...]
    for s in range(batch):
        for p in range(N_PAGES):
            y = _dot_nt(selp, pages[s * N_PAGES + p][...].astype(bf16))
            for l in range(CMP_STRIDE):
                x_scr[l, s * per + p * 8:s * per + (p + 1) * 8, :] = y[l * 8:(l + 1) * 8, :]
        for l in range(CMP_STRIDE):
            if l < n_new:
                extra = jnp.where(row8 == 0, jnp.broadcast_to(newc_ref[s, l:l + 1, :], (8, 256)), 0.0)
            else:
                extra = jnp.zeros((8, 256), f32)
            x_scr[l, s * per + per - 8:s * per + per, :] = extra
    acc = jnp.zeros((batch * per, 512), f32)
    for l in range(CMP_STRIDE):
        acc = acc + _dot(x_scr[l].astype(bf16), wlt_ref[l])
    acc = acc + cb_ref[0:1, :]
    for s in range(batch):
        a = acc[s * per:(s + 1) * per]
        z = a[:, 0:256] + pltpu.roll(a[:, 256:512], per - 1, 0)
        o_ref[s] = _dot(_silu(z).astype(bf16), w2_ref[...])[0:per - 8].astype(bf16)


def _compress_sample(page_table, newc, rows_t, page_loc, layer, wlt, cb, w2, selp):
    ns, n_new, _ = newc.shape
    per = N_PAGES * (PAGE // CMP_STRIDE) + 8
    cbt = CMP_BATCH if ns % CMP_BATCH == 0 else 1

    def page_spec(s, p):
        def index(i, kv, pt):
            major, blk = page_loc(i * cbt + s, p, pt)
            return (major, kv, blk)
        return pl.BlockSpec((None, 256, PAGE), index)

    in_specs = ([pl.BlockSpec((cbt, n_new, 256), lambda i, kv, pt: (i, 0, kv))]
                + [page_spec(s, p) for s in range(cbt) for p in range(N_PAGES)]
                + [pl.BlockSpec((None, None, CMP_STRIDE, 256, 512), lambda i, kv, pt: (layer, kv, 0, 0, 0)),
                   pl.BlockSpec((None, 8, 512), lambda i, kv, pt: (kv, 0, 0)),
                   pl.BlockSpec((None, None, 256, 256), lambda i, kv, pt: (layer, kv, 0, 0)),
                   pl.BlockSpec(selp.shape, lambda i, kv, pt: (0, 0))])
    grid_spec = pltpu.PrefetchScalarGridSpec(
        num_scalar_prefetch=1,
        grid=(ns // cbt, 2),
        in_specs=in_specs,
        out_specs=pl.BlockSpec((None, cbt, per - 8, 256), lambda i, kv, pt: (kv, i, 0, 0)),
        scratch_shapes=[pltpu.VMEM((CMP_STRIDE, cbt * per, 256), f32)],
    )
    return pl.pallas_call(
        _compress_sample_kernel,
        grid_spec=grid_spec,
        out_shape=jax.ShapeDtypeStruct((2, ns, per - 8, 256), bf16),
        compiler_params=_cparams(("arbitrary", "arbitrary")),
        name="compress_sample",
    )(page_table, newc, *([rows_t] * (cbt * N_PAGES)), wlt, cb, w2, selp)


ATT_BATCH = 4


def _nsa_sample_kernel(pt_ref, qrow_ref, ng_ref, kc_ref, vc_ref, news_ref, neww_ref, *rest):
    n_pg = ATT_BATCH * N_PAGES
    cs = rest[0:n_pg]
    (win_ref, bc_ref, mc_ref, bs_ref, ms_ref, bw_ref, mw_ref,
     smat_ref, smt_ref, c_ref, e_ref, dsel_ref, o_ref, new_scr) = rest[n_pg:]
    n_new = news_ref.shape[1]
    dsel = dsel_ref[...]
    mc = mc_ref[...]
    smat = smat_ref[...]
    cmat = c_ref[...]

    @pl.when(pl.program_id(0) == 0)
    def _():
        new_scr[...] = jnp.zeros(new_scr.shape, f32)

    def diag(x):
        y = x * dsel
        return y[:, 0:64] + y[:, 64:128] + y[:, 128:192] + y[:, 192:256]

    def one_sample(b):
        qs = (qrow_ref[b] * SCALE).astype(bf16)
        kc = kc_ref[b]
        vc = vc_ref[b]

        s = jnp.where(mc > 0.5, _dot_nt(qs, kc) + bc_ref[...], NEG)
        p = jnp.exp(s - jnp.max(s, axis=-1, keepdims=True)) * mc
        p = p / jnp.maximum(jnp.sum(p, axis=-1, keepdims=True), 1e-30)
        o_c = diag(_dot(p.astype(bf16), vc))

        p_sum = sum(_dot(smat, piece_) for piece_ in _split3(p))
        imp = sum(_dot(piece_, cmat) for piece_ in _split3(p_sum))
        n_blk = (N_PAGES * PAGE + n_new + SEL_LEN - 1) // SEL_LEN
        cur = (N_PAGES * PAGE) // SEL_LEN
        sidx = lax.broadcasted_iota(i32, imp.shape, 1)
        forced = jnp.where(sidx == 0, 1, jnp.where(sidx == cur, 1, jnp.where(sidx == cur - 1, 1, 0)))
        imp = jnp.where(forced > 0, imp + FORCE_BONUS, imp)
        imp = jnp.where(sidx <= cur, imp, NEG)
        rank = jnp.zeros(imp.shape, i32)
        for j in range(n_blk):
            col = imp[:, j:j + 1]
            rank = rank + jnp.where(col > imp, 1, jnp.where(col == imp, jnp.where(sidx > j, 1, 0), 0))
        sel = jnp.where(rank < N_SEL, jnp.where(imp > -1e29, jnp.where(sidx < n_blk, 1.0, 0.0), 0.0), 0.0)
        sel64 = _dot(smt_ref[...], sel.astype(bf16)).astype(bf16)
        selk = _dot(sel64, e_ref[...])

        def attend(kt_tiles, vt_tiles, new_rows, bias, mask):
            k_new, v_new = new_rows[:, 0:256].astype(bf16), new_rows[:, 256:512].astype(bf16)
            k_past = jnp.concatenate(kt_tiles, axis=1)
            v_past = jnp.concatenate(vt_tiles, axis=1)
            n_past = k_past.shape[1]
            s = jnp.concatenate([_dot(qs, k_past), _dot_nt(qs, k_new)], axis=1) + bias
            s = jnp.where(mask > 0.5, s, NEG)
            p = jnp.exp(s - jnp.max(s, axis=-1, keepdims=True)) * mask
            p = (p / jnp.maximum(jnp.sum(p, axis=-1, keepdims=True), 1e-30)).astype(bf16)
            acc = _dot_nt(p[:, 0:n_past], v_past) + _dot(p[:, n_past:n_past + 128], v_new)
            return diag(acc)

        new_scr[b, 0, 0:n_new, :] = news_ref[b]
        pages = cs[b * N_PAGES:(b + 1) * N_PAGES]
        k_tiles = [pg[0:256, :].astype(bf16) for pg in pages]
        v_tiles = [pg[256:512, :].astype(bf16) for pg in pages]
        o_s = attend(k_tiles, v_tiles, new_scr[b, 0], bs_ref[...], ms_ref[...] * selk)

        new_scr[b, 1, 0:n_new, :] = neww_ref[b]
        o_w = attend([win_ref[b, 0:256, :].astype(bf16)], [win_ref[b, 256:512, :].astype(bf16)],
                     new_scr[b, 1], bw_ref[...], mw_ref[...])

        sig = _sigmoid(ng_ref[b])
        o_ref[b] = sig[:, 0:1] * o_c + sig[:, 1:2] * o_s + sig[:, 2:3] * o_w

    for b in range(ATT_BATCH):
        one_sample(b)


def _nsa_sample(page_table, qrow, ng, kvc, news, neww, cache_s, win, layer, n_phys, consts):
    nb = qrow.shape[0]
    n_new = news.shape[1]
    ab = ATT_BATCH
    full = lambda a: pl.BlockSpec(a.shape, lambda i, pt: (0,) * a.ndim)
    per_b = lambda shape: pl.BlockSpec((ab,) + shape, lambda i, pt: (i, 0, 0))
    cmp_spec = lambda kv: pl.BlockSpec((None, ab) + kvc.shape[2:], lambda i, pt: (kv, i, 0, 0))

    def page_spec(s, p):
        return pl.BlockSpec((None, KV_W, PAGE), lambda i, pt: (layer * n_phys + pt[i * ab + s, p], 0, 0))

    in_specs = ([per_b((64, 256)), per_b((64, 3)), cmp_spec(0), cmp_spec(1),
                 per_b((n_new, KV_W)), per_b((n_new, KV_W))]
                + [page_spec(s, p) for s in range(ab) for p in range(N_PAGES)]
                + [pl.BlockSpec((ab, KV_W, WINDOW), lambda i, pt: ((layer * nb) // ab + i, 0, 0))]
                + [full(c) for c in consts])
    grid_spec = pltpu.PrefetchScalarGridSpec(
        num_scalar_prefetch=1,
        grid=(nb // ab,),
        in_specs=in_specs,
        out_specs=pl.BlockSpec((ab, 64, 64), lambda i, pt: (i, 0, 0)),
        scratch_shapes=[pltpu.VMEM((ab, 2, 128, KV_W), f32)],
    )
    return pl.pallas_call(
        _nsa_sample_kernel,
        grid_spec=grid_spec,
        out_shape=jax.ShapeDtypeStruct((nb, 64, 64), f32),
        compiler_params=_cparams(("arbitrary",)),
        name="nsa_sample",
    )(page_table, qrow, ng, kvc, kvc, news, neww, *([cache_s] * (ab * N_PAGES)), win, *consts)


def _static_tables(t, past, n_new):
    ar = np.arange
    tiles = np.stack([d + ar(128)[:, None] - ar(128)[None, :] for d in (0, 128, 256)])
    cmp_p = ar(t)[:, None] - (CMP_STRIDE * ar(128)[None, :] + CMP_LEN - 1)
    qpos = past + ar(n_new)
    cmp_s = qpos[:, None] - (CMP_STRIDE * ar(128)[None, :] + CMP_LEN - 1)
    kpos_s = np.concatenate([ar(past), past + ar(128)])
    slc_s = qpos[:, None] - kpos_s[None, :]
    kpos_w = np.concatenate([past - WINDOW + ar(WINDOW), past + ar(128)])
    win_s = qpos[:, None] - kpos_w[None, :]
    parts = [tiles.reshape(-1, 128), cmp_p, cmp_s.reshape(-1, 128), slc_s.reshape(-1, 128), win_s.reshape(-1, 128)]
    sizes = [p.shape[0] for p in parts]
    flat = np.concatenate(parts, axis=0)
    pad = (-flat.shape[0]) % 256
    flat = np.concatenate([flat, np.zeros((pad, 128), flat.dtype)], axis=0)
    buckets = _t5_bucket_np(flat)
    real_s = np.concatenate([np.ones(past, bool), ar(128) < n_new])
    real_w = np.concatenate([np.ones(WINDOW, bool), ar(128) < n_new])
    mask_c = (cmp_s >= 0)
    mask_s = (slc_s >= 0) & real_s[None, :]
    mask_w = (win_s >= 0) & (win_s < WINDOW) & real_w[None, :]
    rep = lambda mk: np.tile(mk[None].astype(np.float32), (N_HEADS, 1, 1)).reshape(N_HEADS * n_new, -1)
    return buckets, sizes, rep(mask_c), rep(mask_s), rep(mask_w)


def _cmp_to_sel_np(n_cmp, n_sel):
    j = np.arange(n_cmp)[:, None]
    s = np.arange(n_sel)[None, :]
    lo = np.maximum(j * CMP_STRIDE, s * SEL_LEN)
    hi = np.minimum(j * CMP_STRIDE + CMP_LEN, (s + 1) * SEL_LEN)
    return (np.maximum(hi - lo, 0) / CMP_LEN).astype(np.float32)


def _kron4(w):
    lead = [(0, 0)] * (w.ndim - 2)
    span = (N_KV - 1) * HEAD_DIM
    return sum(jnp.pad(w, lead + [(g * HEAD_DIM, span - g * HEAD_DIM)] * 2) for g in range(N_KV))


def _block_diag_rnn(w):
    w4 = w.reshape(DEPTH, 4, 4, RNN_BLOCK, RNN_BLOCK)
    eye = jnp.eye(4, dtype=w.dtype)
    return jnp.einsum('jk,zcjde->zcjdke', eye, w4).reshape(DEPTH, 4, 256, 256)


def _split_in(w, b):
    n_qkv = 1024 + 3 * KV_W
    n_ng = 3 * N_HEADS
    o_rxy = n_qkv + n_ng
    o_mg = o_rxy + 2 * D_RNN
    n_pad = N_PACK - (C_NG + n_ng)
    w_mid = w[..., o_rxy:].astype(bf16)
    w_qkv = w[..., :n_qkv].astype(bf16)
    w_ng = jnp.pad(w[..., n_qkv:o_rxy].astype(bf16), ((0, 0), (0, 0), (0, IN_TN - n_ng)))
    bp = jnp.concatenate([b[..., o_mg:], b[..., o_rxy:o_mg], b[..., :n_qkv], b[..., n_qkv:o_rxy],
                          jnp.zeros(b.shape[:-1] + (n_pad,), b.dtype)], axis=-1)
    return (w_mid, w_qkv, w_ng), bp.reshape(DEPTH, 1, N_PACK)


def _feature_major(cache):
    d, n, rows = cache.shape[:3]
    return cache.transpose(0, 1, 3, 4, 5, 2).reshape(d * n, KV_W, rows)


def kernel(x_prompt, x_sample, c_prompt, c_sample, cache_cmp_kv, cache_slc_kv, cache_win_kv, state_conv, state_rnn_h, page_table, rel_bias, w_ada, b_ada, w_in, b_in, cmp_pos, cmp_w1, cmp_w2, w_attn_o, conv_w, conv_b, lru_wa, lru_ba, lru_wi, lru_bi, lru_lambda, w_rnn_o, w_out, w_ffn_in, w_ffn_out, ln_g, ln_b):
    nb, t, _ = x_prompt.shape
    ns, n_new, _ = x_sample.shape
    n_phys = cache_cmp_kv.shape[1]
    past = page_table.shape[1] * PAGE
    assert page_table.shape[1] == N_PAGES and cache_win_kv.shape[2] == WINDOW and t % (SLC_UNROLL * TK) == 0
    assert t == N_PAGES * PAGE
    assert n_new <= CMP_STRIDE and ns % 8 == 0 and ns % CMP_BATCH == 0 and ns % ATT_BATCH == 0
    mp, ms = nb * t, ns * n_new

    buckets, sizes, mask_c, mask_s, mask_w = _static_tables(t, past, n_new)
    bias_all = _bias_lookup(rel_bias, jnp.asarray(buckets))
    offs = np.cumsum([0] + sizes)
    seg = lambda k: bias_all[:, offs[k]:offs[k + 1]]
    bias_t = (seg(0) * LOG2E).reshape(N_KV, GROUP, 3, 128, 128).transpose(0, 2, 1, 3, 4).reshape(
        N_KV * 3, GROUP * 128, 128)
    bias_cp = seg(1)
    bias_cs = seg(2).reshape(N_HEADS * n_new, 128)
    bias_ss = seg(3).reshape(N_HEADS * n_new, past + 128)
    bias_ws = seg(4).reshape(N_HEADS * n_new, WINDOW + 128)

    n_sel_p = t // SEL_LEN
    ct_p = np.zeros((128, 128), np.float32)
    ct_p[:n_sel_p, :t // CMP_STRIDE - 1] = _cmp_to_sel_np(t // CMP_STRIDE - 1, n_sel_p).T
    e_p = (np.arange(128)[:, None] == (np.arange(t)[None, :] // SEL_LEN)).astype(np.float32)
    n_cmp_s = (past + n_new + CMP_STRIDE - 1) // CMP_STRIDE - 1
    n_sel_s = (past + n_new + SEL_LEN - 1) // SEL_LEN
    c_s = np.zeros((128, 128), np.float32)
    c_s[:n_cmp_s, :n_sel_s] = _cmp_to_sel_np(n_cmp_s, n_sel_s)
    e_s = (np.arange(128)[:, None] == (np.arange(past + 128)[None, :] // SEL_LEN)).astype(np.float32)
    hq = np.arange(N_HEADS * n_new)
    smat = ((hq[None, :] // (GROUP * n_new)) * n_new + hq[None, :] % n_new
            == np.arange(N_KV * n_new)[:, None]).astype(np.float32)
    dsel = (hq[:, None] // (GROUP * n_new) == np.arange(256)[None, :] // HEAD_DIM).astype(np.float32)
    pos = np.arange(PAGE)
    selp = ((pos % CMP_STRIDE) * (PAGE // CMP_STRIDE) + pos // CMP_STRIDE)[None, :] == np.arange(PAGE)[:, None]
    selp = jnp.asarray(selp, bf16)
    consts_s = (bias_cs, jnp.asarray(mask_c), bias_ss, jnp.asarray(mask_s), bias_ws,
                jnp.asarray(mask_w), jnp.asarray(smat, bf16), jnp.asarray(smat.T, bf16), jnp.asarray(c_s, bf16),
                jnp.asarray(e_s, bf16), jnp.asarray(dsel))

    w_in_p, b_in_p = _split_in(w_in, b_in)
    w1 = cmp_w1.reshape(DEPTH, 2, 2, CMP_STRIDE, HEAD_DIM, HEAD_DIM)
    wlt = jnp.concatenate([_kron4(w1[:, :, 0]), _kron4(w1[:, :, 1])], axis=-1).astype(bf16)
    w2k = _kron4(cmp_w2).astype(bf16)
    pe = jnp.tile(cmp_pos.reshape(DEPTH, 2, 2, CMP_STRIDE, HEAD_DIM), (1, 1, 1, 1, N_KV))
    rnn_w = (conv_w, conv_b.reshape(DEPTH, 1, D_RNN), _block_diag_rnn(lru_wa).astype(bf16),
             lru_ba.reshape(DEPTH, 1, D_RNN), _block_diag_rnn(lru_wi).astype(bf16),
             lru_bi.reshape(DEPTH, 1, D_RNN), lru_lambda.reshape(DEPTH, 1, D_RNN))
    w_ao, w_ro, w_o = w_attn_o.astype(bf16), w_rnn_o.astype(bf16), w_out.astype(bf16)
    w_f1, w_f2 = w_ffn_in.astype(bf16), w_ffn_out.astype(bf16)
    ln_g4, ln_b4 = ln_g.reshape(DEPTH, 2, 1, D_MODEL), ln_b.reshape(DEPTH, 2, 1, D_MODEL)
    b_ada3 = b_ada.reshape(DEPTH, 1, 6 * D_MODEL)

    n_cpad = -(nb + ns) % 8
    c_all = jnp.concatenate([c_prompt, c_sample, jnp.zeros((n_cpad, D_MODEL), f32)], axis=0)
    cache_ct, cache_st, win_t = _feature_major(cache_cmp_kv), _feature_major(cache_slc_kv), _feature_major(cache_win_kv)
    cbuf_t = state_conv.transpose(0, 2, 1, 3)

    y_p = x_prompt.reshape(mp, D_MODEL)
    y_s = x_sample.transpose(1, 0, 2).reshape(ms, D_MODEL)
    st_p, st_s = [], []
    for l in range(DEPTH):
        ada = _ada(c_all, w_ada, b_ada3, l)
        mod_p = ada[:nb].reshape(nb, 1, 6 * D_MODEL)
        mod_s = ada[nb:nb + ns].reshape(1, ns, 6 * D_MODEL)
        cb = _cmp_bias(pe[l], wlt[l])

        proj, *kvt = _inproj(y_p, mod_p, w_in_p, b_in_p, l, tm=TM_INPROJ, seq_rows=t)
        kvc_p = _compress_sample(jnp.zeros((nb, 1), i32), jnp.zeros((nb, 1, KV_W), f32), kvt[0],
                                 lambda b, p, pt: (b, p), l, wlt, cb, w2k, selp)
        o_attn = _nsa_prompt(proj, kvc_p, bias_cp, bias_t, jnp.asarray(ct_p, bf16), jnp.asarray(e_p, bf16), nb, t)
        o_rnn, h_last = _rglru_prompt(proj, nb, t, rnn_w, l)
        x1 = _mixout(o_attn, o_rnn, proj, y_p, mod_p, w_ao, w_ro, w_o, ln_g4, ln_b4, l, tm=TM_MIXOUT)
        y_p = _ffn(x1, mod_p, w_f1, w_f2, ln_g4, ln_b4, l, tm=TM_FFN)
        kv = lambda k, kvt=kvt: kvt[k].reshape(nb, 2, N_KV, HEAD_DIM, t).transpose(0, 4, 1, 2, 3)
        st_p.append((kv(0), kv(1), kv(2)[:, t - min(WINDOW, t):],
                     proj.reshape(nb, t, N_PACK)[:, t - (CONV_W - 1):, C_RX:C_RX + D_RNN],
                     h_last.reshape(nb, D_RNN)))

        proj, *kvt = _inproj(y_s, mod_s, w_in_p, b_in_p, l, tm=ms, seq_rows=ms)
        proj3 = proj.reshape(n_new, ns, N_PACK)
        q5 = proj3[:, :, C_Q:C_Q + 1024].reshape(n_new, ns, N_KV, GROUP, HEAD_DIM).transpose(1, 2, 3, 0, 4)
        qrow = (q5[:, :, :, :, None, :] * jnp.eye(N_KV, dtype=f32)[None, :, None, None, :, None]
                ).reshape(ns, N_HEADS * n_new, 256)
        ng = proj3[:, :, C_NG:C_NG + 48].reshape(n_new, ns, N_HEADS, 3).transpose(1, 2, 0, 3).reshape(
            ns, N_HEADS * n_new, 3)
        new_rows = lambda c: proj3[:, :, c:c + KV_W].transpose(1, 0, 2)
        kvc_s = _compress_sample(page_table, new_rows(C_KVC), cache_ct,
                                 lambda b, p, pt: (l * n_phys + pt[b, p], 0), l, wlt, cb, w2k, selp)
        o64 = _nsa_sample(page_table, qrow, ng, kvc_s, new_rows(C_KVS), new_rows(C_KVW),
                          cache_st, win_t, l, n_phys, consts_s)
        o_attn = o64.reshape(ns, N_HEADS, n_new, HEAD_DIM).transpose(2, 0, 1, 3).reshape(ms, 1024).astype(bf16)
        o_rnn, h_last = _rglru_sample(proj, cbuf_t, state_rnn_h, rnn_w, l)
        x1 = _mixout(o_attn, o_rnn, proj, y_s, mod_s, w_ao, w_ro, w_o, ln_g4, ln_b4, l, tm=min(TM_MIXOUT, ms))
        y_s = _ffn(x1, mod_s, w_f1, w_f2, ln_g4, ln_b4, l, tm=ms)
        kv = lambda k, kvt=kvt: kvt[k].reshape(2, N_KV, HEAD_DIM, n_new, ns).transpose(4, 3, 0, 1, 2)
        xp = jnp.concatenate([state_conv[l], proj3[:, :, C_RX:C_RX + D_RNN].transpose(1, 0, 2)], axis=1)
        st_s.append((kv(0), kv(1), kv(2), xp[:, n_new:], h_last))

    stack = lambda sts, k: jnp.stack([s[k] for s in sts])
    return (y_p.reshape(nb, t, D_MODEL), y_s.reshape(n_new, ns, D_MODEL).transpose(1, 0, 2),
            stack(st_p, 0), stack(st_p, 1), stack(st_p, 2), stack(st_p, 3), stack(st_p, 4),
            stack(st_s, 0), stack(st_s, 1), stack(st_s, 2), stack(st_s, 3), stack(st_s, 4))
```

```python
import math

import numpy as np
import jax
import jax.numpy as jnp
from jax import lax
from jax.experimental import pallas as pl
from jax.experimental.pallas import tpu as pltpu

f32 = jnp.float32
bf16 = jnp.bfloat16
i32 = jnp.int32

D_MODEL = 2048
N_HEADS = 16
N_KV = 4
GROUP = 4
HEAD_DIM = 64
CMP_STRIDE = 16
CMP_LEN = 32
SEL_LEN = 64
N_SEL = 16
WINDOW = 512
PAGE = 128
FORCE_BONUS = 1e3
N_BUCKETS = 32
MAX_DISTANCE = 128
D_RNN = 1024
RNN_BLOCK = 64
CONV_W = 4
LRU_C = 8.0
D_FF = 5632
DEPTH = 2
ALPHA = (2.0 * DEPTH) ** 0.25
SCALE = HEAD_DIM ** -0.5
LOG2E = math.log2(math.e)
NEG = -1e30
KV_W = 2 * N_KV * HEAD_DIM

C_MG = 0
C_RX = 4096
C_RY = 5120
C_Q = 6144
C_KVC = 7168
C_KVS = 7680
C_KVW = 8192
C_NG = 8704
N_PACK = 9216

VMEM_LIMIT = 56 * 1024 * 1024

TQ = 128
TK = 256
SLC_UNROLL = 4
N_PAGES = 16
TM_INPROJ = 1024
TM_MIXOUT = 256
TM_FFN = 512


def _cparams(sem):
    return pltpu.CompilerParams(dimension_semantics=sem, vmem_limit_bytes=VMEM_LIMIT)


def _sigmoid(x):
    return 1.0 / (1.0 + jnp.exp(-x))


def _silu(x):
    return x * _sigmoid(x)


def _layer_norm(z, g, b):
    mu = jnp.mean(z, axis=-1, keepdims=True)
    zc = z - mu
    var = jnp.mean(zc * zc, axis=-1, keepdims=True)
    return zc * lax.rsqrt(var + 1e-5) * g + b


def _dot(a, b):
    return jnp.dot(a, b, preferred_element_type=f32)


def _dot_nt(a, b):
    return lax.dot_general(a, b, (((1,), (1,)), ((), ())), preferred_element_type=f32)


def _split3(x):
    hi = x.astype(bf16)
    r1 = x - hi.astype(f32)
    mid = r1.astype(bf16)
    lo = (r1 - mid.astype(f32)).astype(bf16)
    return hi, mid, lo


def _rows(mod, tm):
    mr = mod.shape[0]
    if mr == 1 or mr == tm:
        return mod
    return jnp.concatenate([mod] * (tm // mr), axis=0)


def _mod_spec(mod, k, tiles_per_block, ngrid):
    mr = mod.shape[1]
    if ngrid == 1:
        return pl.BlockSpec((None, mr, D_MODEL), lambda i: (i // tiles_per_block, 0, k))
    return pl.BlockSpec((None, mr, D_MODEL), lambda i, j: (i // tiles_per_block, 0, k))


def _ada_kernel(c_ref, w_ref, b_ref, o_ref):
    h = _silu(c_ref[...]).astype(bf16)
    o_ref[...] = _dot(h, w_ref[...].astype(bf16)) + b_ref[...]


def _ada(c_all, w, b, layer):
    m = c_all.shape[0]
    n = w.shape[2]
    tn = 1024
    return pl.pallas_call(
        _ada_kernel,
        grid=(n // tn,),
        in_specs=[pl.BlockSpec((m, D_MODEL), lambda j: (0, 0)),
                  pl.BlockSpec((None, D_MODEL, tn), lambda j: (layer, 0, j)),
                  pl.BlockSpec((None, 1, tn), lambda j: (layer, 0, j))],
        out_specs=pl.BlockSpec((m, tn), lambda j: (0, j)),
        out_shape=jax.ShapeDtypeStruct((m, n), f32),
        compiler_params=_cparams(("arbitrary",)),
        name="ada",
    )(c_all, w, b)


def _bias_lookup_kernel(tbl_ref, bk_ref, o_ref):
    bk = bk_ref[...]
    for h in range(N_HEADS):
        acc = jnp.zeros(bk.shape, f32)
        for k in range(N_BUCKETS):
            acc = jnp.where(bk == k, tbl_ref[k, h], acc)
        o_ref[h] = acc


def _bias_lookup(tbl, buckets):
    n = buckets.shape[0]
    tr = 256
    return pl.pallas_call(
        _bias_lookup_kernel,
        grid=(n // tr,),
        in_specs=[pl.BlockSpec(memory_space=pltpu.SMEM),
                  pl.BlockSpec((tr, 128), lambda i: (i, 0))],
        out_specs=pl.BlockSpec((N_HEADS, tr, 128), lambda i: (0, i, 0)),
        out_shape=jax.ShapeDtypeStruct((N_HEADS, n, 128), f32),
        compiler_params=_cparams(("arbitrary",)),
        name="bias_lookup",
    )(tbl, buckets)


def _t5_bucket_np(dist):
    n = np.maximum(dist, 0)
    exact = N_BUCKETS // 2
    ratio = np.maximum(n, 1).astype(np.float32) / np.float32(exact)
    log_ratio = np.log(ratio).astype(np.float32) / np.float32(math.log(MAX_DISTANCE / exact))
    large = np.minimum(exact + (log_ratio * np.float32(N_BUCKETS - exact)).astype(np.int32), N_BUCKETS - 1)
    return np.where(n < exact, n, large).astype(np.int32)


IN_TN = 512
KV_J0 = C_KVC // IN_TN


IN_PIECES = (("rxy_mg", 0, 12, 4), ("qkv_ng", 12, 6, 0))


def _inproj_kernel(x_ref, sc_ref, sh_ref, wmid_ref, wqkv_ref, b_ref,
                   o_ref, okc_ref, oks_ref, okw_ref, h_scr):
    j = pl.program_id(1)

    @pl.when(j == 0)
    def _():
        tm = x_ref.shape[0]
        h_scr[...] = (x_ref[...] * (1.0 + _rows(sc_ref[...], tm)) + _rows(sh_ref[...], tm)).astype(bf16)

    def emit(w_ref):
        res = _dot(h_scr[...], w_ref[...]) + b_ref[...]
        o_ref[...] = res
        return res

    for w_ref, (_, j0, nj, _) in zip((wmid_ref, wqkv_ref), IN_PIECES):
        @pl.when(jnp.logical_and(j >= j0, j < j0 + nj))
        def _(w_ref=w_ref, j0=j0):
            res = emit(w_ref)
            if j0 == IN_PIECES[1][1]:
                for k, okt_ref in enumerate((okc_ref, oks_ref, okw_ref)):
                    @pl.when(j == KV_J0 + k)
                    def _(okt_ref=okt_ref):
                        okt_ref[...] = res.T


def _inproj(x, mod, w_pieces, b, layer, tm, seq_rows):
    m = x.shape[0]
    tn = IN_TN
    tpb = (m // tm) // mod.shape[0]
    tps = seq_rows // tm
    kt_spec = pl.BlockSpec((None, tn, tm), lambda i, j: (i // tps, 0, i % tps))
    kt_shape = jax.ShapeDtypeStruct((m // seq_rows, tn, seq_rows), f32)

    def w_spec(j0, nj, rot):
        return pl.BlockSpec((None, D_MODEL, tn), lambda i, j: (layer, 0, (jnp.clip(j - j0, 0, nj - 1) + rot) % nj))

    return pl.pallas_call(
        _inproj_kernel,
        grid=(m // tm, N_PACK // tn),
        in_specs=[pl.BlockSpec((tm, D_MODEL), lambda i, j: (i, 0)),
                  _mod_spec(mod, 1, tpb, 2), _mod_spec(mod, 0, tpb, 2)]
        + [w_spec(j0, nj, rot) for (_, j0, nj, rot) in IN_PIECES]
        + [pl.BlockSpec((None, 1, tn), lambda i, j: (layer, 0, j))],
        out_specs=[pl.BlockSpec((tm, tn), lambda i, j: (i, j)), kt_spec, kt_spec, kt_spec],
        out_shape=[jax.ShapeDtypeStruct((m, N_PACK), f32), kt_shape, kt_shape, kt_shape],
        scratch_shapes=[pltpu.VMEM((tm, D_MODEL), bf16)],
        compiler_params=_cparams(("parallel", "arbitrary")),
        name="inproj",
    )(x, mod, mod, *w_pieces, b)


def _cmp_bias_kernel(pe_ref, wlt_ref, o_ref):
    for kv in range(2):
        acc = jnp.zeros((8, 512), f32)
        for l in range(CMP_STRIDE):
            w = wlt_ref[kv, l]
            lead = _dot(jnp.broadcast_to(pe_ref[kv, 0, l:l + 1, :], (8, 256)).astype(bf16), w[:, 0:256])
            tail = _dot(jnp.broadcast_to(pe_ref[kv, 1, l:l + 1, :], (8, 256)).astype(bf16), w[:, 256:512])
            acc = acc + jnp.concatenate([lead, tail], axis=1)
        o_ref[kv] = acc


def _cmp_bias(pe, wlt):
    return pl.pallas_call(
        _cmp_bias_kernel,
        out_shape=jax.ShapeDtypeStruct((2, 8, 512), f32),
        compiler_params=pltpu.CompilerParams(vmem_limit_bytes=VMEM_LIMIT),
        name="cmp_bias",
    )(pe, wlt)


def _topk_mask_rows(imp, sidx, n_keep):
    n_rows = imp.shape[0]
    rank = jnp.zeros(imp.shape, i32)
    for j in range(n_rows):
        row = imp[j:j + 1, :]
        beats = jnp.where(row > imp, 1, jnp.where(row == imp, jnp.where(sidx > j, 1, 0), 0))
        rank = rank + beats
    return jnp.where(rank < n_keep, jnp.where(imp > -1e29, 1.0, 0.0), 0.0)


def _nsa_prompt_kernel(q_ref, ng_ref, ks_ref, vs_ref, kw_ref, vw_ref, kc_ref, vc_ref,
                       bc_ref, bt_ref, ct_ref, e_ref, o_ref, selk_scr, s_scr, mx_scr):
    i = pl.program_id(1)
    q0 = i * TQ
    rows = GROUP * TQ
    n_sel_blk = ks_ref.shape[0] // SEL_LEN

    d0 = lax.broadcasted_iota(i32, (TQ, TK), 0) - lax.broadcasted_iota(i32, (TQ, TK), 1)
    dist_c = (q0 + (lax.broadcasted_iota(i32, (rows, 128), 0) & (TQ - 1))
              - CMP_STRIDE * lax.broadcasted_iota(i32, (rows, 128), 1) - (CMP_LEN - 1))
    mask_c = dist_c >= 0
    sig = _sigmoid(ng_ref[...])

    sidx = lax.broadcasted_iota(i32, (n_sel_blk, TQ), 0)
    cur = (q0 + lax.broadcasted_iota(i32, (n_sel_blk, TQ), 1)) // SEL_LEN
    forced = jnp.where(sidx == 0, 1, jnp.where(sidx == cur, 1, jnp.where(sidx == cur - 1, 1, 0)))

    def branch(qs, g, k_ref, v_ref, kt_hi, window):
        n_tiles = k_ref.shape[0] // TK
        ones = jnp.ones((TK, HEAD_DIM), f32)

        def score_tile(kt, slot):
            k0 = pl.multiple_of(jnp.clip(kt, 0, n_tiles - 1) * TK, TK)
            k = k_ref[pl.ds(k0, TK), g * 64:(g + 1) * 64].astype(bf16)
            delta = q0 - kt * TK
            dist = d0 + delta
            if window:
                live = jnp.where(kt >= 0, 0.0, NEG)
                addm = jnp.where(dist >= 0, jnp.where(dist < WINDOW, live, NEG), NEG)
            else:
                addm = jnp.where(dist >= 0, selk_scr[:, pl.ds(k0, TK)], NEG)
            va = jnp.clip(delta, 0, 256) // 128
            vb = jnp.clip(delta - 128, 0, 256) // 128
            bias = jnp.concatenate([bt_ref[g * 3 + va], bt_ref[g * 3 + vb]], axis=1)
            s = _dot_nt(qs, k) + bias + jnp.concatenate([addm] * GROUP, axis=0)
            s_scr[:, pl.ds(slot * TK, TK)] = s
            return jnp.maximum(s[:, 0:128], s[:, 128:256])

        def value_tile(kt, slot, mb):
            k0 = pl.multiple_of(jnp.clip(kt, 0, n_tiles - 1) * TK, TK)
            v = jnp.concatenate([v_ref[pl.ds(k0, TK), g * 64:(g + 1) * 64], ones], axis=1).astype(bf16)
            p = jnp.exp2(s_scr[:, pl.ds(slot * TK, TK)] - jnp.concatenate([mb, mb], axis=1))
            return _dot(p.astype(bf16), v)

        if window:
            tiles = [(kt_hi - 2 + j, j) for j in range(3)]
            mx = jnp.full((rows, 128), NEG, f32)
            for kt, slot in tiles:
                mx = jnp.maximum(mx, score_tile(kt, slot))
            mb = jnp.broadcast_to(jnp.max(mx, axis=-1, keepdims=True), (rows, 128))
            acc = jnp.zeros((rows, 2 * HEAD_DIM), f32)
            for kt, slot in tiles:
                acc = acc + value_tile(kt, slot, mb)
        else:
            n_groups = kt_hi // SLC_UNROLL + 1

            def scores(qd, mx):
                for j in range(SLC_UNROLL):
                    mx = jnp.maximum(mx, score_tile(SLC_UNROLL * qd + j, SLC_UNROLL * qd + j))
                return mx

            mx = lax.fori_loop(0, n_groups, scores, jnp.full((rows, 128), NEG, f32))
            mx_scr[...] = jnp.broadcast_to(jnp.max(mx, axis=-1, keepdims=True), (rows, 128))

            def values(qd, acc):
                mb = mx_scr[...]
                for j in range(SLC_UNROLL):
                    acc = acc + value_tile(SLC_UNROLL * qd + j, SLC_UNROLL * qd + j, mb)
                return acc

            acc = lax.fori_loop(0, n_groups, values, jnp.zeros((rows, 2 * HEAD_DIM), f32))
        return acc[:, 0:HEAD_DIM] / jnp.maximum(acc[:, HEAD_DIM:HEAD_DIM + 1], 1e-30)

    for g in range(N_KV):
        qg = q_ref[:, g * 256:(g + 1) * 256]
        qst = jnp.concatenate([qg[:, r * 64:(r + 1) * 64] for r in range(GROUP)], axis=0)
        qs = (qst * SCALE).astype(bf16)
        qs2 = (qst * (SCALE * LOG2E)).astype(bf16)

        kcg = kc_ref[:, g * 64:(g + 1) * 64].astype(bf16)
        vcg = vc_ref[:, g * 64:(g + 1) * 64].astype(bf16)
        bias_c = jnp.concatenate([bc_ref[g * GROUP + r] for r in range(GROUP)], axis=0)
        s = jnp.where(mask_c, _dot_nt(qs, kcg) + bias_c, NEG)
        p = jnp.where(mask_c, jnp.exp(s - jnp.max(s, axis=-1, keepdims=True)), 0.0)
        p = p / jnp.maximum(jnp.sum(p, axis=-1, keepdims=True), 1e-30)
        o_c = _dot(p.astype(bf16), vcg)

        p_sum = p[0:TQ] + p[TQ:2 * TQ] + p[2 * TQ:3 * TQ] + p[3 * TQ:4 * TQ]
        ct = ct_ref[...]
        imp_t = sum(_dot_nt(ct, piece) for piece in _split3(p_sum))[0:n_sel_blk]
        imp_t = jnp.where(forced > 0, imp_t + FORCE_BONUS, imp_t)
        imp_t = jnp.where(sidx <= cur, imp_t, NEG)
        sel_t = _topk_mask_rows(imp_t, sidx, N_SEL)
        sel_pad = jnp.concatenate([sel_t, jnp.zeros((128 - n_sel_blk, TQ), f32)], axis=0)
        sel_q = sel_pad.T.astype(bf16)
        selk_scr[...] = jnp.where(_dot(sel_q, e_ref[...]) > 0.5, 0.0, NEG)

        kt_hi = (q0 + TQ - 1) // TK
        o_s = branch(qs2, g, ks_ref, vs_ref, kt_hi, False)
        o_w = branch(qs2, g, kw_ref, vw_ref, kt_hi, True)

        gate = lambda c: jnp.concatenate(
            [sig[:, g * 12 + r * 3 + c:g * 12 + r * 3 + c + 1] for r in range(GROUP)], axis=0)
        o = gate(0) * o_c + gate(1) * o_s + gate(2) * o_w
        o_ref[:, g * 256:(g + 1) * 256] = jnp.concatenate(
            [o[r * TQ:(r + 1) * TQ] for r in range(GROUP)], axis=1).astype(bf16)


def _nsa_prompt(proj, kvc, bias_c, bias_t, ct, e, nb, t):
    nq = t // TQ
    full = lambda a: pl.BlockSpec(a.shape, lambda b, i: (0,) * a.ndim)
    kv = lambda c: pl.BlockSpec((t, 256), lambda b, i: (b, c // 256))
    cmp_spec = lambda k: pl.BlockSpec((None, None, t // CMP_STRIDE, 256), lambda b, i: (k, b, 0, 0))
    return pl.pallas_call(
        _nsa_prompt_kernel,
        grid=(nb, nq),
        in_specs=[pl.BlockSpec((TQ, 1024), lambda b, i: (b * nq + i, C_Q // 1024)),
                  pl.BlockSpec((TQ, 128), lambda b, i: (b * nq + i, C_NG // 128)),
                  kv(C_KVS), kv(C_KVS + 256), kv(C_KVW), kv(C_KVW + 256),
                  cmp_spec(0), cmp_spec(1),
                  pl.BlockSpec((N_HEADS, TQ, 128), lambda b, i: (0, i, 0)),
                  full(bias_t), full(ct), full(e)],
        out_specs=pl.BlockSpec((TQ, 1024), lambda b, i: (b * nq + i, 0)),
        out_shape=jax.ShapeDtypeStruct((nb * t, 1024), bf16),
        scratch_shapes=[pltpu.VMEM((TQ, t), f32), pltpu.VMEM((GROUP * TQ, t), f32),
                        pltpu.VMEM((GROUP * TQ, 128), f32)],
        compiler_params=_cparams(("parallel", "arbitrary")),
        name="nsa_prompt",
    )(proj, proj, proj, proj, proj, proj, kvc, kvc, bias_c, bias_t, ct, e)


def _lru_gates(xc, ry, wa_ref, ba_ref, wi_ref, bi_ref, lam_ref):
    xb = xc.astype(bf16)
    ra = jnp.concatenate([_dot(xb[:, c * 256:(c + 1) * 256], wa_ref[c]) for c in range(4)], axis=1)
    ri = jnp.concatenate([_dot(xb[:, c * 256:(c + 1) * 256], wi_ref[c]) for c in range(4)], axis=1)
    r = _sigmoid(ra + ba_ref[...])
    ig = _sigmoid(ri + bi_ref[...])
    nl = -lam_ref[...]
    softplus = jnp.maximum(nl, 0.0) + jnp.log1p(jnp.exp(-jnp.abs(nl)))
    log_a = -LRU_C * r * softplus
    a = jnp.exp(log_a)
    u = jnp.sqrt(jnp.tanh(-log_a) * (a * a + 1.0)) * (ig * xc)
    return a, u, jax.nn.gelu(ry)


def _rglru_prompt_kernel(rx_ref, ry_ref, cw_ref, cb_ref, wa_ref, ba_ref, wi_ref, bi_ref, lam_ref,
                         o_ref, hl_ref, xp_scr, a_scr, u_scr, hs_scr, h_scr):
    tt = pl.program_id(1)
    tr = rx_ref.shape[0]

    @pl.when(tt == 0)
    def _():
        xp_scr[0:8, :] = jnp.zeros((8, D_RNN), f32)
        h_scr[...] = jnp.zeros((1, D_RNN), f32)

    x = rx_ref[...]
    xp_scr[8:8 + tr, :] = x
    y = x * cw_ref[CONV_W - 1:CONV_W, :] + cb_ref[...]
    for k in range(CONV_W - 1):
        y = y + xp_scr[5 + k:5 + k + tr, :] * cw_ref[k:k + 1, :]
    xp_scr[0:8, :] = xp_scr[tr:tr + 8, :]

    a, u, gate = _lru_gates(y, ry_ref[...], wa_ref, ba_ref, wi_ref, bi_ref, lam_ref)
    a_scr[...] = a
    u_scr[...] = u

    def step(t, h):
        h = a_scr[pl.ds(t, 1), :] * h + u_scr[pl.ds(t, 1), :]
        hs_scr[pl.ds(t, 1), :] = h
        return h

    h = lax.fori_loop(0, tr, step, h_scr[...], unroll=8)
    h_scr[...] = h
    hl_ref[...] = h
    o_ref[...] = (hs_scr[...] * gate).astype(bf16)


def _rnn_specs(rnn_w, layer, ngrid):
    zeros = lambda n: (0,) * n
    if ngrid == 2:
        return [pl.BlockSpec((None,) + a.shape[1:], lambda b, i, n=a.ndim - 1: (layer,) + zeros(n)) for a in rnn_w]
    return [pl.BlockSpec((None,) + a.shape[1:], lambda i, n=a.ndim - 1: (layer,) + zeros(n)) for a in rnn_w]


def _rglru_prompt(proj, nb, t, rnn_w, layer):
    tr = 256
    nt = t // tr
    return pl.pallas_call(
        _rglru_prompt_kernel,
        grid=(nb, nt),
        in_specs=[pl.BlockSpec((tr, D_RNN), lambda b, i: (b * nt + i, C_RX // D_RNN)),
                  pl.BlockSpec((tr, D_RNN), lambda b, i: (b * nt + i, C_RY // D_RNN))]
        + _rnn_specs(rnn_w, layer, 2),
        out_specs=[pl.BlockSpec((tr, D_RNN), lambda b, i: (b * nt + i, 0)),
                   pl.BlockSpec((None, 1, D_RNN), lambda b, i: (b, 0, 0))],
        out_shape=[jax.ShapeDtypeStruct((nb * t, D_RNN), bf16),
                   jax.ShapeDtypeStruct((nb, 1, D_RNN), f32)],
        scratch_shapes=[pltpu.VMEM((tr + 8, D_RNN), f32), pltpu.VMEM((tr, D_RNN), f32),
                        pltpu.VMEM((tr, D_RNN), f32), pltpu.VMEM((tr, D_RNN), f32),
                        pltpu.VMEM((1, D_RNN), f32)],
        compiler_params=_cparams(("parallel", "arbitrary")),
        name="rglru_prompt",
    )(proj, proj, *rnn_w)


def _rglru_sample_kernel(rx_ref, ry_ref, cbuf_ref, h0_ref, cw_ref, cb_ref, wa_ref, ba_ref, wi_ref, bi_ref,
                         lam_ref, o_ref, hl_ref):
    ns = h0_ref.shape[0]
    n_t = rx_ref.shape[0] // ns
    xp = [cbuf_ref[k] for k in range(CONV_W - 1)] + [rx_ref[s * ns:(s + 1) * ns, :] for s in range(n_t)]
    h = h0_ref[...]
    for s in range(n_t):
        y = xp[s + CONV_W - 1] * cw_ref[CONV_W - 1:CONV_W, :] + cb_ref[...]
        for k in range(CONV_W - 1):
            y = y + xp[s + k] * cw_ref[k:k + 1, :]
        a, u, gate = _lru_gates(y, ry_ref[s * ns:(s + 1) * ns, :], wa_ref, ba_ref, wi_ref, bi_ref, lam_ref)
        h = a * h + u
        o_ref[s * ns:(s + 1) * ns, :] = (h * gate).astype(bf16)
    hl_ref[...] = h


def _rglru_sample(proj, cbuf, h0, rnn_w, layer):
    ms = proj.shape[0]
    ns = h0.shape[1]
    return pl.pallas_call(
        _rglru_sample_kernel,
        grid=(1,),
        in_specs=[pl.BlockSpec((ms, D_RNN), lambda i: (0, C_RX // D_RNN)),
                  pl.BlockSpec((ms, D_RNN), lambda i: (0, C_RY // D_RNN)),
                  pl.BlockSpec((None, CONV_W - 1, ns, D_RNN), lambda i: (layer, 0, 0, 0)),
                  pl.BlockSpec((None, ns, D_RNN), lambda i: (layer, 0, 0))]
        + _rnn_specs(rnn_w, layer, 1),
        out_specs=[pl.BlockSpec((ms, D_RNN), lambda i: (0, 0)),
                   pl.BlockSpec((ns, D_RNN), lambda i: (0, 0))],
        out_shape=[jax.ShapeDtypeStruct((ms, D_RNN), bf16),
                   jax.ShapeDtypeStruct((ns, D_RNN), f32)],
        compiler_params=_cparams(("arbitrary",)),
        name="rglru_sample",
    )(proj, proj, cbuf, h0, *rnn_w)


def _mixout_kernel(oa_ref, orn_ref, ga_ref, gr_ref, x_ref, gt_ref, wa_ref, wr_ref, wo_ref, lg_ref, lb_ref, o_ref):
    a1 = _dot(oa_ref[...], wa_ref[...])
    a2 = _dot(orn_ref[...], wr_ref[...])
    merged = (_sigmoid(ga_ref[...]) * a1 + _sigmoid(gr_ref[...]) * a2).astype(bf16)
    mix = _dot(merged, wo_ref[...])
    z = ALPHA * x_ref[...] + (1.0 + _rows(gt_ref[...], x_ref.shape[0])) * mix
    o_ref[...] = _layer_norm(z, lg_ref[...], lb_ref[...])


def _mixout(o_attn, o_rnn, proj, x, mod, w_ao, w_ro, w_o, ln_g, ln_b, layer, tm):
    m = o_attn.shape[0]
    tpb = (m // tm) // mod.shape[0]
    ln_spec = pl.BlockSpec((None, None, 1, D_MODEL), lambda i: (layer, 0, 0, 0))
    w_spec = lambda k: pl.BlockSpec((None, k, D_MODEL), lambda i: (layer, 0, 0), pipeline_mode=pl.Buffered(1))
    return pl.pallas_call(
        _mixout_kernel,
        grid=(m // tm,),
        in_specs=[pl.BlockSpec((tm, 1024), lambda i: (i, 0)),
                  pl.BlockSpec((tm, 1024), lambda i: (i, 0)),
                  pl.BlockSpec((tm, D_MODEL), lambda i: (i, C_MG // D_MODEL)),
                  pl.BlockSpec((tm, D_MODEL), lambda i: (i, C_MG // D_MODEL + 1)),
                  pl.BlockSpec((tm, D_MODEL), lambda i: (i, 0)),
                  _mod_spec(mod, 2, tpb, 1),
                  w_spec(1024), w_spec(1024), w_spec(D_MODEL),
                  ln_spec, ln_spec],
        out_specs=pl.BlockSpec((tm, D_MODEL), lambda i: (i, 0)),
        out_shape=jax.ShapeDtypeStruct((m, D_MODEL), f32),
        compiler_params=_cparams(("parallel",)),
        name="mixout",
    )(o_attn, o_rnn, proj, proj, x, mod, w_ao, w_ro, w_o, ln_g, ln_b)


def _ffn_kernel(x_ref, sc_ref, sh_ref, gt_ref, wg_ref, wu_ref, wo_ref, lg_ref, lb_ref, o_ref, h_scr):
    j = pl.program_id(1)
    tm = x_ref.shape[0]

    @pl.when(j == 0)
    def _():
        h_scr[...] = (x_ref[...] * (1.0 + _rows(sc_ref[...], tm)) + _rows(sh_ref[...], tm)).astype(bf16)
        o_ref[...] = jnp.zeros(o_ref.shape, f32)

    h = h_scr[...]
    act = (_silu(_dot(h, wg_ref[...])) * _dot(h, wu_ref[...])).astype(bf16)
    o_ref[...] += _dot(act, wo_ref[...])

    @pl.when(j == pl.num_programs(1) - 1)
    def _():
        z = ALPHA * x_ref[...] + (1.0 + _rows(gt_ref[...], tm)) * o_ref[...]
        o_ref[...] = _layer_norm(z, lg_ref[...], lb_ref[...])


def _ffn(x, mod, w_in, w_out, ln_g, ln_b, layer, tm):
    m = x.shape[0]
    tf = 512
    nf = D_FF // tf
    tpb = (m // tm) // mod.shape[0]
    ln_spec = pl.BlockSpec((None, None, 1, D_MODEL), lambda i, j: (layer, 1, 0, 0))
    return pl.pallas_call(
        _ffn_kernel,
        grid=(m // tm, nf),
        in_specs=[pl.BlockSpec((tm, D_MODEL), lambda i, j: (i, 0)),
                  _mod_spec(mod, 4, tpb, 2), _mod_spec(mod, 3, tpb, 2), _mod_spec(mod, 5, tpb, 2),
                  pl.BlockSpec((None, D_MODEL, tf), lambda i, j: (layer, 0, j)),
                  pl.BlockSpec((None, D_MODEL, tf), lambda i, j: (layer, 0, nf + j)),
                  pl.BlockSpec((None, tf, D_MODEL), lambda i, j: (layer, j, 0)),
                  ln_spec, ln_spec],
        out_specs=pl.BlockSpec((tm, D_MODEL), lambda i, j: (i, 0)),
        out_shape=jax.ShapeDtypeStruct((m, D_MODEL), f32),
        scratch_shapes=[pltpu.VMEM((tm, D_MODEL), bf16)],
        compiler_params=_cparams(("parallel", "arbitrary")),
        name="ffn",
    )(x, mod, mod, mod, w_in, w_in, w_out, ln_g, ln_b)


CMP_BATCH = 4


def _compress_sample_kernel(pt_ref, newc_ref, *rest):
    batch = newc_ref.shape[0]
    n_pg = batch * N_PAGES
    pages = rest[0:n_pg]
    wlt_ref, cb_ref, w2_ref, selp_ref, o_ref, x_scr = rest[n_pg:]
    n_new = newc_ref.shape[1]
    per = N_PAGES * (PAGE // CMP_STRIDE) + 8
    row8 = lax.broadcasted_iota(i32, (8, 256), 0)
    selp = selp_ref[...]
    for s in range(batch):
        for p in range(N_PAGES):
            y = _dot_nt(selp, pages[s * N_PAGES + p][...].astype(bf16))
            for l in range(CMP_STRIDE):
                x_scr[l, s * per + p * 8:s * per + (p + 1) * 8, :] = y[l * 8:(l + 1) * 8, :]
        for l in range(CMP_STRIDE):
            if l < n_new:
                extra = jnp.where(row8 == 0, jnp.broadcast_to(newc_ref[s, l:l + 1, :], (8, 256)), 0.0)
            else:
                extra = jnp.zeros((8, 256), f32)
            x_scr[l, s * per + per - 8:s * per + per, :] = extra
    acc = jnp.zeros((batch * per, 512), f32)
    for l in range(CMP_STRIDE):
        acc = acc + _dot(x_scr[l].astype(bf16), wlt_ref[l])
    acc = acc + cb_ref[0:1, :]
    for s in range(batch):
        a = acc[s * per:(s + 1) * per]
        z = a[:, 0:256] + pltpu.roll(a[:, 256:512], per - 1, 0)
        o_ref[s] = _dot(_silu(z).astype(bf16), w2_ref[...])[0:per - 8].astype(bf16)


def _compress_sample(page_table, newc, rows_t, page_loc, layer, wlt, cb, w2, selp):
    ns, n_new, _ = newc.shape
    per = N_PAGES * (PAGE // CMP_STRIDE) + 8
    cbt = CMP_BATCH if ns % CMP_BATCH == 0 else 1

    def page_spec(s, p):
        def index(i, kv, pt):
            major, blk = page_loc(i * cbt + s, p, pt)
            return (major, kv, blk)
        return pl.BlockSpec((None, 256, PAGE), index)

    in_specs = ([pl.BlockSpec((cbt, n_new, 256), lambda i, kv, pt: (i, 0, kv))]
                + [page_spec(s, p) for s in range(cbt) for p in range(N_PAGES)]
                + [pl.BlockSpec((None, None, CMP_STRIDE, 256, 512), lambda i, kv, pt: (layer, kv, 0, 0, 0)),
                   pl.BlockSpec((None, 8, 512), lambda i, kv, pt: (kv, 0, 0)),
                   pl.BlockSpec((None, None, 256, 256), lambda i, kv, pt: (layer, kv, 0, 0)),
                   pl.BlockSpec(selp.shape, lambda i, kv, pt: (0, 0))])
    grid_spec = pltpu.PrefetchScalarGridSpec(
        num_scalar_prefetch=1,
        grid=(ns // cbt, 2),
        in_specs=in_specs,
        out_specs=pl.BlockSpec((None, cbt, per - 8, 256), lambda i, kv, pt: (kv, i, 0, 0)),
        scratch_shapes=[pltpu.VMEM((CMP_STRIDE, cbt * per, 256), f32)],
    )
    return pl.pallas_call(
        _compress_sample_kernel,
        grid_spec=grid_spec,
        out_shape=jax.ShapeDtypeStruct((2, ns, per - 8, 256), bf16),
        compiler_params=_cparams(("arbitrary", "arbitrary")),
        name="compress_sample",
    )(page_table, newc, *([rows_t] * (cbt * N_PAGES)), wlt, cb, w2, selp)


ATT_BATCH = 4


def _nsa_sample_kernel(pt_ref, qrow_ref, ng_ref, kc_ref, vc_ref, news_ref, neww_ref, *rest):
    n_pg = ATT_BATCH * N_PAGES
    cs = rest[0:n_pg]
    (win_ref, bc_ref, mc_ref, bs_ref, ms_ref, bw_ref, mw_ref,
     smat_ref, smt_ref, c_ref, e_ref, dsel_ref, o_ref, new_scr) = rest[n_pg:]
    n_new = news_ref.shape[1]
    dsel = dsel_ref[...]
    mc = mc_ref[...]
    smat = smat_ref[...]
    cmat = c_ref[...]

    @pl.when(pl.program_id(0) == 0)
    def _():
        new_scr[...] = jnp.zeros(new_scr.shape, f32)

    def diag(x):
        y = x * dsel
        return y[:, 0:64] + y[:, 64:128] + y[:, 128:192] + y[:, 192:256]

    def one_sample(b):
        qs = (qrow_ref[b] * SCALE).astype(bf16)
        kc = kc_ref[b]
        vc = vc_ref[b]

        s = jnp.where(mc > 0.5, _dot_nt(qs, kc) + bc_ref[...], NEG)
        p = jnp.exp(s - jnp.max(s, axis=-1, keepdims=True)) * mc
        p = p / jnp.maximum(jnp.sum(p, axis=-1, keepdims=True), 1e-30)
        o_c = diag(_dot(p.astype(bf16), vc))

        p_sum = sum(_dot(smat, piece_) for piece_ in _split3(p))
        imp = sum(_dot(piece_, cmat) for piece_ in _split3(p_sum))
        n_blk = (N_PAGES * PAGE + n_new + SEL_LEN - 1) // SEL_LEN
        cur = (N_PAGES * PAGE) // SEL_LEN
        sidx = lax.broadcasted_iota(i32, imp.shape, 1)
        forced = jnp.where(sidx == 0, 1, jnp.where(sidx == cur, 1, jnp.where(sidx == cur - 1, 1, 0)))
        imp = jnp.where(forced > 0, imp + FORCE_BONUS, imp)
        imp = jnp.where(sidx <= cur, imp, NEG)
        rank = jnp.zeros(imp.shape, i32)
        for j in range(n_blk):
            col = imp[:, j:j + 1]
            rank = rank + jnp.where(col > imp, 1, jnp.where(col == imp, jnp.where(sidx > j, 1, 0), 0))
        sel = jnp.where(rank < N_SEL, jnp.where(imp > -1e29, jnp.where(sidx < n_blk, 1.0, 0.0), 0.0), 0.0)
        sel64 = _dot(smt_ref[...], sel.astype(bf16)).astype(bf16)
        selk = _dot(sel64, e_ref[...])

        def attend(kt_tiles, vt_tiles, new_rows, bias, mask):
            k_new, v_new = new_rows[:, 0:256].astype(bf16), new_rows[:, 256:512].astype(bf16)
            k_past = jnp.concatenate(kt_tiles, axis=1)
            v_past = jnp.concatenate(vt_tiles, axis=1)
            n_past = k_past.shape[1]
            s = jnp.concatenate([_dot(qs, k_past), _dot_nt(qs, k_new)], axis=1) + bias
            s = jnp.where(mask > 0.5, s, NEG)
            p = jnp.exp(s - jnp.max(s, axis=-1, keepdims=True)) * mask
            p = (p / jnp.maximum(jnp.sum(p, axis=-1, keepdims=True), 1e-30)).astype(bf16)
            acc = _dot_nt(p[:, 0:n_past], v_past) + _dot(p[:, n_past:n_past + 128], v_new)
            return diag(acc)

        new_scr[b, 0, 0:n_new, :] = news_ref[b]
        pages = cs[b * N_PAGES:(b + 1) * N_PAGES]
        k_tiles = [pg[0:256, :].astype(bf16) for pg in pages]
        v_tiles = [pg[256:512, :].astype(bf16) for pg in pages]
        o_s = attend(k_tiles, v_tiles, new_scr[b, 0], bs_ref[...], ms_ref[...] * selk)

        new_scr[b, 1, 0:n_new, :] = neww_ref[b]
        o_w = attend([win_ref[b, 0:256, :].astype(bf16)], [win_ref[b, 256:512, :].astype(bf16)],
                     new_scr[b, 1], bw_ref[...], mw_ref[...])

        sig = _sigmoid(ng_ref[b])
        o_ref[b] = sig[:, 0:1] * o_c + sig[:, 1:2] * o_s + sig[:, 2:3] * o_w

    for b in range(ATT_BATCH):
        one_sample(b)


def _nsa_sample(page_table, qrow, ng, kvc, news, neww, cache_s, win, layer, n_phys, consts):
    nb = qrow.shape[0]
    n_new = news.shape[1]
    ab = ATT_BATCH
    full = lambda a: pl.BlockSpec(a.shape, lambda i, pt: (0,) * a.ndim)
    per_b = lambda shape: pl.BlockSpec((ab,) + shape, lambda i, pt: (i, 0, 0))
    cmp_spec = lambda kv: pl.BlockSpec((None, ab) + kvc.shape[2:], lambda i, pt: (kv, i, 0, 0))

    def page_spec(s, p):
        return pl.BlockSpec((None, KV_W, PAGE), lambda i, pt: (layer * n_phys + pt[i * ab + s, p], 0, 0))

    in_specs = ([per_b((64, 256)), per_b((64, 3)), cmp_spec(0), cmp_spec(1),
                 per_b((n_new, KV_W)), per_b((n_new, KV_W))]
                + [page_spec(s, p) for s in range(ab) for p in range(N_PAGES)]
                + [pl.BlockSpec((ab, KV_W, WINDOW), lambda i, pt: ((layer * nb) // ab + i, 0, 0))]
                + [full(c) for c in consts])
    grid_spec = pltpu.PrefetchScalarGridSpec(
        num_scalar_prefetch=1,
        grid=(nb // ab,),
        in_specs=in_specs,
        out_specs=pl.BlockSpec((ab, 64, 64), lambda i, pt: (i, 0, 0)),
        scratch_shapes=[pltpu.VMEM((ab, 2, 128, KV_W), f32)],
    )
    return pl.pallas_call(
        _nsa_sample_kernel,
        grid_spec=grid_spec,
        out_shape=jax.ShapeDtypeStruct((nb, 64, 64), f32),
        compiler_params=_cparams(("arbitrary",)),
        name="nsa_sample",
    )(page_table, qrow, ng, kvc, kvc, news, neww, *([cache_s] * (ab * N_PAGES)), win, *consts)


def _static_tables(t, past, n_new):
    ar = np.arange
    tiles = np.stack([d + ar(128)[:, None] - ar(128)[None, :] for d in (0, 128, 256)])
    cmp_p = ar(t)[:, None] - (CMP_STRIDE * ar(128)[None, :] + CMP_LEN - 1)
    qpos = past + ar(n_new)
    cmp_s = qpos[:, None] - (CMP_STRIDE * ar(128)[None, :] + CMP_LEN - 1)
    kpos_s = np.concatenate([ar(past), past + ar(128)])
    slc_s = qpos[:, None] - kpos_s[None, :]
    kpos_w = np.concatenate([past - WINDOW + ar(WINDOW), past + ar(128)])
    win_s = qpos[:, None] - kpos_w[None, :]
    parts = [tiles.reshape(-1, 128), cmp_p, cmp_s.reshape(-1, 128), slc_s.reshape(-1, 128), win_s.reshape(-1, 128)]
    sizes = [p.shape[0] for p in parts]
    flat = np.concatenate(parts, axis=0)
    pad = (-flat.shape[0]) % 256
    flat = np.concatenate([flat, np.zeros((pad, 128), flat.dtype)], axis=0)
    buckets = _t5_bucket_np(flat)
    real_s = np.concatenate([np.ones(past, bool), ar(128) < n_new])
    real_w = np.concatenate([np.ones(WINDOW, bool), ar(128) < n_new])
    mask_c = (cmp_s >= 0)
    mask_s = (slc_s >= 0) & real_s[None, :]
    mask_w = (win_s >= 0) & (win_s < WINDOW) & real_w[None, :]
    rep = lambda mk: np.tile(mk[None].astype(np.float32), (N_HEADS, 1, 1)).reshape(N_HEADS * n_new, -1)
    return buckets, sizes, rep(mask_c), rep(mask_s), rep(mask_w)


def _cmp_to_sel_np(n_cmp, n_sel):
    j = np.arange(n_cmp)[:, None]
    s = np.arange(n_sel)[None, :]
    lo = np.maximum(j * CMP_STRIDE, s * SEL_LEN)
    hi = np.minimum(j * CMP_STRIDE + CMP_LEN, (s + 1) * SEL_LEN)
    return (np.maximum(hi - lo, 0) / CMP_LEN).astype(np.float32)


def _kron4(w):
    lead = [(0, 0)] * (w.ndim - 2)
    span = (N_KV - 1) * HEAD_DIM
    return sum(jnp.pad(w, lead + [(g * HEAD_DIM, span - g * HEAD_DIM)] * 2) for g in range(N_KV))


def _block_diag_rnn(w):
    w4 = w.reshape(DEPTH, 4, 4, RNN_BLOCK, RNN_BLOCK)
    eye = jnp.eye(4, dtype=w.dtype)
    return jnp.einsum('jk,zcjde->zcjdke', eye, w4).reshape(DEPTH, 4, 256, 256)


def _split_in(w, b):
    n_qkv = 1024 + 3 * KV_W
    n_ng = 3 * N_HEADS
    o_rxy = n_qkv + n_ng
    o_mg = o_rxy + 2 * D_RNN
    n_pad = N_PACK - (C_NG + n_ng)
    w_mid = w[..., o_rxy:].astype(bf16)
    w_qkv = jnp.pad(w[..., :o_rxy].astype(bf16), ((0, 0), (0, 0), (0, IN_TN - n_ng)))
    bp = jnp.concatenate([b[..., o_mg:], b[..., o_rxy:o_mg], b[..., :n_qkv], b[..., n_qkv:o_rxy],
                          jnp.zeros(b.shape[:-1] + (n_pad,), b.dtype)], axis=-1)
    return (w_mid, w_qkv), bp.reshape(DEPTH, 1, N_PACK)


def _feature_major(cache):
    d, n, rows = cache.shape[:3]
    return cache.transpose(0, 1, 3, 4, 5, 2).reshape(d * n, KV_W, rows)


def kernel(x_prompt, x_sample, c_prompt, c_sample, cache_cmp_kv, cache_slc_kv, cache_win_kv, state_conv, state_rnn_h, page_table, rel_bias, w_ada, b_ada, w_in, b_in, cmp_pos, cmp_w1, cmp_w2, w_attn_o, conv_w, conv_b, lru_wa, lru_ba, lru_wi, lru_bi, lru_lambda, w_rnn_o, w_out, w_ffn_in, w_ffn_out, ln_g, ln_b):
    nb, t, _ = x_prompt.shape
    ns, n_new, _ = x_sample.shape
    n_phys = cache_cmp_kv.shape[1]
    past = page_table.shape[1] * PAGE
    assert page_table.shape[1] == N_PAGES and cache_win_kv.shape[2] == WINDOW and t % (SLC_UNROLL * TK) == 0
    assert t == N_PAGES * PAGE
    assert n_new <= CMP_STRIDE and ns % 8 == 0 and ns % CMP_BATCH == 0 and ns % ATT_BATCH == 0
    mp, ms = nb * t, ns * n_new

    buckets, sizes, mask_c, mask_s, mask_w = _static_tables(t, past, n_new)
    bias_all = _bias_lookup(rel_bias, jnp.asarray(buckets))
    offs = np.cumsum([0] + sizes)
    seg = lambda k: bias_all[:, offs[k]:offs[k + 1]]
    bias_t = (seg(0) * LOG2E).reshape(N_KV, GROUP, 3, 128, 128).transpose(0, 2, 1, 3, 4).reshape(
        N_KV * 3, GROUP * 128, 128)
    bias_cp = seg(1)
    bias_cs = seg(2).reshape(N_HEADS * n_new, 128)
    bias_ss = seg(3).reshape(N_HEADS * n_new, past + 128)
    bias_ws = seg(4).reshape(N_HEADS * n_new, WINDOW + 128)

    n_sel_p = t // SEL_LEN
    ct_p = np.zeros((128, 128), np.float32)
    ct_p[:n_sel_p, :t // CMP_STRIDE - 1] = _cmp_to_sel_np(t // CMP_STRIDE - 1, n_sel_p).T
    e_p = (np.arange(128)[:, None] == (np.arange(t)[None, :] // SEL_LEN)).astype(np.float32)
    n_cmp_s = (past + n_new + CMP_STRIDE - 1) // CMP_STRIDE - 1
    n_sel_s = (past + n_new + SEL_LEN - 1) // SEL_LEN
    c_s = np.zeros((128, 128), np.float32)
    c_s[:n_cmp_s, :n_sel_s] = _cmp_to_sel_np(n_cmp_s, n_sel_s)
    e_s = (np.arange(128)[:, None] == (np.arange(past + 128)[None, :] // SEL_LEN)).astype(np.float32)
    hq = np.arange(N_HEADS * n_new)
    smat = ((hq[None, :] // (GROUP * n_new)) * n_new + hq[None, :] % n_new
            == np.arange(N_KV * n_new)[:, None]).astype(np.float32)
    dsel = (hq[:, None] // (GROUP * n_new) == np.arange(256)[None, :] // HEAD_DIM).astype(np.float32)
    pos = np.arange(PAGE)
    selp = ((pos % CMP_STRIDE) * (PAGE // CMP_STRIDE) + pos // CMP_STRIDE)[None, :] == np.arange(PAGE)[:, None]
    selp = jnp.asarray(selp, bf16)
    consts_s = (bias_cs, jnp.asarray(mask_c), bias_ss, jnp.asarray(mask_s), bias_ws,
                jnp.asarray(mask_w), jnp.asarray(smat, bf16), jnp.asarray(smat.T, bf16), jnp.asarray(c_s, bf16),
                jnp.asarray(e_s, bf16), jnp.asarray(dsel))

    w_in_p, b_in_p = _split_in(w_in, b_in)
    w1 = cmp_w1.reshape(DEPTH, 2, 2, CMP_STRIDE, HEAD_DIM, HEAD_DIM)
    wlt = jnp.concatenate([_kron4(w1[:, :, 0]), _kron4(w1[:, :, 1])], axis=-1).astype(bf16)
    w2k = _kron4(cmp_w2).astype(bf16)
    pe = jnp.tile(cmp_pos.reshape(DEPTH, 2, 2, CMP_STRIDE, HEAD_DIM), (1, 1, 1, 1, N_KV))
    rnn_w = (conv_w, conv_b.reshape(DEPTH, 1, D_RNN), _block_diag_rnn(lru_wa).astype(bf16),
             lru_ba.reshape(DEPTH, 1, D_RNN), _block_diag_rnn(lru_wi).astype(bf16),
             lru_bi.reshape(DEPTH, 1, D_RNN), lru_lambda.reshape(DEPTH, 1, D_RNN))
    w_ao, w_ro, w_o = w_attn_o.astype(bf16), w_rnn_o.astype(bf16), w_out.astype(bf16)
    w_f1, w_f2 = w_ffn_in.astype(bf16), w_ffn_out.astype(bf16)
    ln_g4, ln_b4 = ln_g.reshape(DEPTH, 2, 1, D_MODEL), ln_b.reshape(DEPTH, 2, 1, D_MODEL)
    b_ada3 = b_ada.reshape(DEPTH, 1, 6 * D_MODEL)

    n_cpad = -(nb + ns) % 8
    c_all = jnp.concatenate([c_prompt, c_sample, jnp.zeros((n_cpad, D_MODEL), f32)], axis=0)
    cache_ct, cache_st, win_t = _feature_major(cache_cmp_kv), _feature_major(cache_slc_kv), _feature_major(cache_win_kv)
    cbuf_t = state_conv.transpose(0, 2, 1, 3)

    y_p = x_prompt.reshape(mp, D_MODEL)
    y_s = x_sample.transpose(1, 0, 2).reshape(ms, D_MODEL)
    st_p, st_s = [], []
    for l in range(DEPTH):
        ada = _ada(c_all, w_ada, b_ada3, l)
        mod_p = ada[:nb].reshape(nb, 1, 6 * D_MODEL)
        mod_s = ada[nb:nb + ns].reshape(1, ns, 6 * D_MODEL)
        cb = _cmp_bias(pe[l], wlt[l])

        proj, *kvt = _inproj(y_p, mod_p, w_in_p, b_in_p, l, tm=TM_INPROJ, seq_rows=t)
        kvc_p = _compress_sample(jnp.zeros((nb, 1), i32), jnp.zeros((nb, 1, KV_W), f32), kvt[0],
                                 lambda b, p, pt: (b, p), l, wlt, cb, w2k, selp)
        o_attn = _nsa_prompt(proj, kvc_p, bias_cp, bias_t, jnp.asarray(ct_p, bf16), jnp.asarray(e_p, bf16), nb, t)
        o_rnn, h_last = _rglru_prompt(proj, nb, t, rnn_w, l)
        x1 = _mixout(o_attn, o_rnn, proj, y_p, mod_p, w_ao, w_ro, w_o, ln_g4, ln_b4, l, tm=TM_MIXOUT)
        y_p = _ffn(x1, mod_p, w_f1, w_f2, ln_g4, ln_b4, l, tm=TM_FFN)
        kv = lambda k, kvt=kvt: kvt[k].reshape(nb, 2, N_KV, HEAD_DIM, t).transpose(0, 4, 1, 2, 3)
        st_p.append((kv(0), kv(1), kv(2)[:, t - min(WINDOW, t):],
                     proj.reshape(nb, t, N_PACK)[:, t - (CONV_W - 1):, C_RX:C_RX + D_RNN],
                     h_last.reshape(nb, D_RNN)))

        proj, *kvt = _inproj(y_s, mod_s, w_in_p, b_in_p, l, tm=ms, seq_rows=ms)
        proj3 = proj.reshape(n_new, ns, N_PACK)
        q5 = proj3[:, :, C_Q:C_Q + 1024].reshape(n_new, ns, N_KV, GROUP, HEAD_DIM).transpose(1, 2, 3, 0, 4)
        qrow = (q5[:, :, :, :, None, :] * jnp.eye(N_KV, dtype=f32)[None, :, None, None, :, None]
                ).reshape(ns, N_HEADS * n_new, 256)
        ng = proj3[:, :, C_NG:C_NG + 48].reshape(n_new, ns, N_HEADS, 3).transpose(1, 2, 0, 3).reshape(
            ns, N_HEADS * n_new, 3)
        new_rows = lambda c: proj3[:, :, c:c + KV_W].transpose(1, 0, 2)
        kvc_s = _compress_sample(page_table, new_rows(C_KVC), cache_ct,
                                 lambda b, p, pt: (l * n_phys + pt[b, p], 0), l, wlt, cb, w2k, selp)
        o64 = _nsa_sample(page_table, qrow, ng, kvc_s, new_rows(C_KVS), new_rows(C_KVW),
                          cache_st, win_t, l, n_phys, consts_s)
        o_attn = o64.reshape(ns, N_HEADS, n_new, HEAD_DIM).transpose(2, 0, 1, 3).reshape(ms, 1024).astype(bf16)
        o_rnn, h_last = _rglru_sample(proj, cbuf_t, state_rnn_h, rnn_w, l)
        x1 = _mixout(o_attn, o_rnn, proj, y_s, mod_s, w_ao, w_ro, w_o, ln_g4, ln_b4, l, tm=min(TM_MIXOUT, ms))
        y_s = _ffn(x1, mod_s, w_f1, w_f2, ln_g4, ln_b4, l, tm=ms)
        kv = lambda k, kvt=kvt: kvt[k].reshape(2, N_KV, HEAD_DIM, n_new, ns).transpose(4, 3, 0, 1, 2)
        xp = jnp.concatenate([state_conv[l], proj3[:, :, C_RX:C_RX + D_RNN].transpose(1, 0, 2)], axis=1)
        st_s.append((kv(0), kv(1), kv(2), xp[:, n_new:], h_last))

    stack = lambda sts, k: jnp.stack([s[k] for s in sts])
    return (y_p.reshape(nb, t, D_MODEL), y_s.reshape(n_new, ns, D_MODEL).transpose(1, 0, 2),
            stack(st_p, 0), stack(st_p, 1), stack(st_p, 2), stack(st_p, 3), stack(st_p, 4),
            stack(st_s, 0), stack(st_s, 1), stack(st_s, 2), stack(st_s, 3), stack(st_s, 4))
```
